```python
import math
import jax, jax.numpy as jnp
from jax import lax
import numpy as np

D_MODEL = 1024
BATCH = 8
SEQ = 4096
DEPTH = 2

N_MIXERS = 2
D_FF = ((8 * D_MODEL // 3 + 127) // 128) * 128
HALF_STEP = 0.5
RMS_EPS = 1e-6
MASK_VALUE = -1e30

NSA_HEAD_DIM = 64
NSA_HEADS = D_MODEL // NSA_HEAD_DIM
NSA_KV_HEADS = NSA_HEADS // 4
NSA_GROUP = NSA_HEADS // NSA_KV_HEADS
CMP_BLOCK = 32
CMP_STRIDE = 16
CMP_HIDDEN = NSA_HEAD_DIM
SEL_BLOCK = 64
N_SELECT = 16
WINDOW = 512
Q_BLOCK = 32
FORCED_SCORE = 1e4
NSA_IN = NSA_HEADS * NSA_HEAD_DIM + 6 * NSA_KV_HEADS * NSA_HEAD_DIM + 3 * NSA_HEADS

RWKV_HEAD_DIM = 64
RWKV_HEADS = D_MODEL // RWKV_HEAD_DIM
DECAY_LORA = max(32, int(round(1.8 * D_MODEL ** 0.5 / 32)) * 32)
AAA_LORA = max(32, int(round(1.8 * D_MODEL ** 0.5 / 32)) * 32)
GATE_LORA = max(32, int(round(0.6 * D_MODEL ** 0.8 / 32)) * 32)
GN_EPS = 64e-5
RWKV_IN = 3 * D_MODEL + DECAY_LORA + AAA_LORA + GATE_LORA

kernel_name = 'hybrid_nsa_rwkv7_macaron_sandwich'


def rms_norm(x, g):
    xf = x.astype(jnp.float32)
    y = xf * lax.rsqrt(jnp.mean(xf * xf, axis=-1, keepdims=True) + RMS_EPS)
    return (y * g.astype(jnp.float32)).astype(x.dtype)


def swiglu(x, w_gu, w_down):
    gate, up = jnp.split(x @ w_gu, 2, axis=-1)
    return (jax.nn.silu(gate) * up) @ w_down


def alibi_slopes(n_heads):
    return 2.0 ** (-8.0 * jnp.arange(1, n_heads + 1, dtype=jnp.float32) / n_heads)


def masked_softmax(scores, mask):
    p = jax.nn.softmax(jnp.where(mask, scores, MASK_VALUE), axis=-1)
    return jnp.where(mask, p, 0.0)


def nsa_mixer(u, w_in, pe_k, w_ck1, w_ck2, pe_v, w_cv1, w_cv2, w_out):
    B, S, _ = u.shape
    H, HK, G, HD = NSA_HEADS, NSA_KV_HEADS, NSA_GROUP, NSA_HEAD_DIM
    KV = HK * HD
    f32 = jnp.float32
    proj = u @ w_in
    cuts = [H * HD + i * KV for i in range(7)]
    q, kc, vc, ks, vs, kw, vw, gl = jnp.split(proj, cuts, axis=-1)
    q = (q * HD ** -0.5).reshape(B, S, HK, G, HD)
    kc, vc, ks, vs, kw, vw = [t.reshape(B, S, HK, HD) for t in (kc, vc, ks, vs, kw, vw)]
    gates = jax.nn.sigmoid(gl.astype(f32)).astype(u.dtype).reshape(B, S, HK, G, 3)

    n_cmp = (S - CMP_BLOCK) // CMP_STRIDE + 1
    cmp_idx = np.arange(n_cmp)[:, None] * CMP_STRIDE + np.arange(CMP_BLOCK)[None, :]

    def compress(t, pe, w1, w2):
        blocks = t[:, cmp_idx] + pe[None, None, :, None, :]
        hid = jax.nn.silu(jnp.einsum('bnlgd,ldc->bngc', blocks, w1))
        return jnp.einsum('bngc,cd->bngd', hid, w2)

    k_cmp = compress(kc, pe_k, w_ck1, w_ck2)
    v_cmp = compress(vc, pe_v, w_cv1, w_cv2)
    cmp_end = jnp.asarray(cmp_idx[:, -1], jnp.int32)

    n_sel = S // SEL_BLOCK
    k_top = min(N_SELECT, n_sel)
    cs = np.arange(n_cmp) * CMP_STRIDE
    bs = np.arange(n_sel) * SEL_BLOCK
    overlap = np.clip(np.minimum(cs[:, None] + CMP_BLOCK, bs[None, :] + SEL_BLOCK)
                      - np.maximum(cs[:, None], bs[None, :]), 0, None)
    cmp_to_sel = jnp.asarray(overlap / CMP_BLOCK, jnp.float32)
    k_blocks = ks.reshape(B, n_sel, SEL_BLOCK, HK, HD).transpose(0, 3, 1, 2, 4)
    v_blocks = vs.reshape(B, n_sel, SEL_BLOCK, HK, HD).transpose(0, 3, 1, 2, 4)
    gather_blocks = jax.vmap(jax.vmap(lambda blk, idx: blk[idx]))
    sel_ids = jnp.arange(n_sel)

    kw_pad = jnp.pad(kw, ((0, 0), (WINDOW, 0), (0, 0), (0, 0)))
    vw_pad = jnp.pad(vw, ((0, 0), (WINDOW, 0), (0, 0), (0, 0)))

    slopes = alibi_slopes(H).reshape(HK, G)[None, :, :, None, None]

    def query_block(qi):
        t0 = qi * Q_BLOCK
        tpos = t0 + jnp.arange(Q_BLOCK)
        qb = lax.dynamic_slice_in_dim(q, t0, Q_BLOCK, axis=1)
        gb = lax.dynamic_slice_in_dim(gates, t0, Q_BLOCK, axis=1)

        dist_c = tpos[:, None] - cmp_end[None, :]
        s_c = jnp.einsum('bqgrd,bngd->bgrqn', qb, k_cmp).astype(f32) - slopes * dist_c.astype(f32)
        p_c = masked_softmax(s_c, dist_c >= 0)
        o_c = jnp.einsum('bgrqn,bngd->bqgrd', p_c.astype(u.dtype), v_cmp)

        imp = jnp.einsum('bgrqn,ns->bgqs', p_c, cmp_to_sel)
        cur = (tpos // SEL_BLOCK)[:, None]
        forced = (sel_ids == 0) | (sel_ids == cur) | (sel_ids == cur - 1)
        visible = sel_ids * SEL_BLOCK <= tpos[:, None]
        imp = jnp.where(forced, FORCED_SCORE, jnp.where(visible, imp, -1.0))
        _, top = lax.top_k(imp, k_top)
        M = k_top * SEL_BLOCK
        k_g = gather_blocks(k_blocks, top).reshape(B, HK, Q_BLOCK, M, HD)
        v_g = gather_blocks(v_blocks, top).reshape(B, HK, Q_BLOCK, M, HD)
        spos = (top[..., None] * SEL_BLOCK + jnp.arange(SEL_BLOCK)).reshape(B, HK, Q_BLOCK, M)
        dist_s = (tpos[:, None] - spos)[:, :, None]
        s_s = jnp.einsum('bqgrd,bgqmd->bgrqm', qb, k_g).astype(f32) - slopes * dist_s.astype(f32)
        p_s = masked_softmax(s_s, dist_s >= 0)
        o_s = jnp.einsum('bgrqm,bgqmd->bqgrd', p_s.astype(u.dtype), v_g)

        kwb = lax.dynamic_slice_in_dim(kw_pad, t0, Q_BLOCK + WINDOW, axis=1)
        vwb = lax.dynamic_slice_in_dim(vw_pad, t0, Q_BLOCK + WINDOW, axis=1)
        wpos = t0 - WINDOW + jnp.arange(Q_BLOCK + WINDOW)
        dist_w = tpos[:, None] - wpos[None, :]
        mask_w = (dist_w >= 0) & (dist_w < WINDOW) & (wpos >= 0)[None, :]
        s_w = jnp.einsum('bqgrd,bkgd->bgrqk', qb, kwb).astype(f32) - slopes * dist_w.astype(f32)
        p_w = masked_softmax(s_w, mask_w)
        o_w = jnp.einsum('bgrqk,bkgd->bqgrd', p_w.astype(u.dtype), vwb)

        return gb[..., 0:1] * o_c + gb[..., 1:2] * o_s + gb[..., 2:3] * o_w

    out = lax.map(query_block, jnp.arange(S // Q_BLOCK))
    out = jnp.moveaxis(out, 0, 1).reshape(B, S, H * HD)
    return out @ w_out


def rwkv7_mixer(u, mu, w_in, w0, w_w2, a0, w_a2, w_g2, k_k, k_a, r_k, gn_w, gn_b, w_out):
    B, S, D = u.shape
    H, N = RWKV_HEADS, RWKV_HEAD_DIM
    f32 = jnp.float32
    xx = jnp.pad(u, ((0, 0), (1, 0), (0, 0)))[:, :-1] - u
    offs = np.cumsum((0, D, D, D, DECAY_LORA, AAA_LORA, GATE_LORA))

    def proj(i):
        return (u + xx * mu[i]) @ w_in[:, int(offs[i]):int(offs[i + 1])]

    r, k, v = proj(0), proj(1), proj(2)
    w = -jax.nn.softplus(-(w0 + jnp.tanh(proj(3)) @ w_w2).astype(f32)) - 0.5
    a = jax.nn.sigmoid((a0 + proj(4) @ w_a2).astype(f32))
    g = jax.nn.sigmoid(proj(5)) @ w_g2

    heads = lambda t: t.reshape(B, S, H, N)
    kk = heads((k * k_k).astype(f32))
    kk = kk / jnp.maximum(jnp.linalg.norm(kk, axis=-1, keepdims=True), 1e-12)
    k_mod = k.astype(f32) * (1.0 + (a - 1.0) * k_a.astype(f32))
    decay = jnp.exp(-jnp.exp(w))
    r_h, k_h, v_h = heads(r.astype(f32)), heads(k_mod), heads(v.astype(f32))
    a_h, w_h = heads(a), heads(decay)
    tm = lambda t: jnp.moveaxis(t, 1, 0)

    def step(state, inp):
        r_t, w_t, k_t, v_t, kk_t, a_t = inp
        sa = jnp.einsum('bhvk,bhk->bhv', state, -kk_t)
        state = (state * w_t[:, :, None, :] + sa[..., None] * (kk_t * a_t)[:, :, None, :]
                 + v_t[..., None] * k_t[:, :, None, :])
        return state, jnp.einsum('bhvk,bhk->bhv', state, r_t)

    state0 = jnp.zeros((B, H, N, N), f32)
    _, y = lax.scan(step, state0, (tm(r_h), tm(w_h), tm(k_h), tm(v_h), tm(kk), tm(a_h)))
    y = jnp.moveaxis(y, 0, 1)
    mean = jnp.mean(y, axis=-1, keepdims=True)
    var = jnp.mean(jnp.square(y - mean), axis=-1, keepdims=True)
    y = ((y - mean) * lax.rsqrt(var + GN_EPS)).reshape(B, S, D) * gn_w.astype(f32) + gn_b.astype(f32)
    bonus = jnp.sum(r_h * k_h * r_k.astype(f32), axis=-1, keepdims=True) * v_h
    y = (y + bonus.reshape(B, S, D)).astype(u.dtype) * g
    return y @ w_out


def setup_inputs(seed: int = 0) -> dict:
    key = jax.random.key(seed)
    keys = iter(jax.random.split(key, 48))
    f32 = jnp.float32
    L, D, HD = DEPTH, D_MODEL, NSA_HEAD_DIM
    n_a = len(range(0, DEPTH, N_MIXERS))
    n_b = len(range(1, DEPTH, N_MIXERS))

    def dense(shape, fan_in, scale=1.0):
        return scale * fan_in ** -0.5 * jax.random.normal(next(keys), shape, f32)

    def gain(shape):
        return 1.0 + 0.02 * jax.random.normal(next(keys), shape, f32)

    def small(shape, scale):
        return scale * jax.random.normal(next(keys), shape, f32)

    def unif(shape, lo, hi):
        return jax.random.uniform(next(keys), shape, f32, lo, hi)

    return {
        'x': jax.random.normal(next(keys), (BATCH, SEQ, D), f32),
        'ffn1_norm_pre': gain((L, D)),
        'ffn1_w_gu': dense((L, D, 2 * D_FF), D),
        'ffn1_w_down': dense((L, D_FF, D), D_FF),
        'ffn1_norm_post': gain((L, D)),
        'mix_norm_pre': gain((L, D)),
        'nsa_w_in': dense((n_a, D, NSA_IN), D),
        'nsa_pe_k': small((n_a, CMP_BLOCK, HD), 0.1),
        'nsa_w_ck1': dense((n_a, CMP_BLOCK, HD, CMP_HIDDEN), CMP_BLOCK * HD),
        'nsa_w_ck2': dense((n_a, CMP_HIDDEN, HD), CMP_HIDDEN),
        'nsa_pe_v': small((n_a, CMP_BLOCK, HD), 0.1),
        'nsa_w_cv1': dense((n_a, CMP_BLOCK, HD, CMP_HIDDEN), CMP_BLOCK * HD),
        'nsa_w_cv2': dense((n_a, CMP_HIDDEN, HD), CMP_HIDDEN),
        'nsa_w_out': dense((n_a, D, D), D),
        'rwkv_mu': unif((n_b, 6, D), 0.0, 1.0),
        'rwkv_w_in': dense((n_b, D, RWKV_IN), D),
        'rwkv_w0': unif((n_b, D), -6.0, 0.0),
        'rwkv_w_w2': dense((n_b, DECAY_LORA, D), DECAY_LORA, 0.1),
        'rwkv_a0': small((n_b, D), 0.1),
        'rwkv_w_a2': dense((n_b, AAA_LORA, D), AAA_LORA, 0.1),
        'rwkv_w_g2': dense((n_b, GATE_LORA, D), GATE_LORA),
        'rwkv_k_k': 0.85 + small((n_b, D), 0.02),
        'rwkv_k_a': gain((n_b, D)),
        'rwkv_r_k': small((n_b, RWKV_HEADS, RWKV_HEAD_DIM), 0.3),
        'rwkv_gn_w': gain((n_b, D)),
        'rwkv_gn_b': small((n_b, D), 0.02),
        'rwkv_w_out': dense((n_b, D, D), D),
        'mix_norm_post': gain((L, D)),
        'ffn2_norm_pre': gain((L, D)),
        'ffn2_w_gu': dense((L, D, 2 * D_FF), D),
        'ffn2_w_down': dense((L, D_FF, D), D_FF),
        'ffn2_norm_post': gain((L, D)),
    }


def reference(x, ffn1_norm_pre, ffn1_w_gu, ffn1_w_down, ffn1_norm_post, mix_norm_pre,
              nsa_w_in, nsa_pe_k, nsa_w_ck1, nsa_w_ck2, nsa_pe_v, nsa_w_cv1, nsa_w_cv2, nsa_w_out,
              rwkv_mu, rwkv_w_in, rwkv_w0, rwkv_w_w2, rwkv_a0, rwkv_w_a2, rwkv_w_g2, rwkv_k_k,
              rwkv_k_a, rwkv_r_k, rwkv_gn_w, rwkv_gn_b, rwkv_w_out, mix_norm_post,
              ffn2_norm_pre, ffn2_w_gu, ffn2_w_down, ffn2_norm_post):
    h = x
    for i in range(DEPTH):
        f = swiglu(rms_norm(h, ffn1_norm_pre[i]), ffn1_w_gu[i], ffn1_w_down[i])
        h = h + HALF_STEP * rms_norm(f, ffn1_norm_post[i])
        u = rms_norm(h, mix_norm_pre[i])
        j = i // N_MIXERS
        if i % N_MIXERS == 0:
            m = nsa_mixer(u, nsa_w_in[j], nsa_pe_k[j], nsa_w_ck1[j], nsa_w_ck2[j],
                          nsa_pe_v[j], nsa_w_cv1[j], nsa_w_cv2[j], nsa_w_out[j])
        else:
            m = rwkv7_mixer(u, rwkv_mu[j], rwkv_w_in[j], rwkv_w0[j], rwkv_w_w2[j], rwkv_a0[j],
                            rwkv_w_a2[j], rwkv_w_g2[j], rwkv_k_k[j], rwkv_k_a[j], rwkv_r_k[j],
                            rwkv_gn_w[j], rwkv_gn_b[j], rwkv_w_out[j])
        h = h + rms_norm(m, mix_norm_post[i])
        f = swiglu(rms_norm(h, ffn2_norm_pre[i]), ffn2_w_gu[i], ffn2_w_down[i])
        h = h + HALF_STEP * rms_norm(f, ffn2_norm_post[i])
    return h
```

```python
import functools

import numpy as np
import jax
import jax.numpy as jnp
from jax import lax
from jax.experimental import pallas as pl
from jax.experimental.pallas import tpu as pltpu

F32 = jnp.float32
BF16 = jnp.bfloat16

D_MODEL = 1024
D_FF = 2816
HALF_STEP = 0.5
RMS_EPS = 1e-6
MASK_VALUE = -1e30

HD = 64
N_HEADS = 16
N_KV = 4
N_GRP = 4
KV_W = N_KV * HD
CMP_BLOCK = 32
CMP_STRIDE = 16
SEL_BLOCK = 64
N_SELECT = 16
WINDOW = 512
FORCED_SCORE = 1e4
N_GATE = 3 * N_HEADS

DECAY_LORA = 64
AAA_LORA = 64
GATE_LORA = 160
GN_EPS = 64e-5

LANES = 128
VMEM_LIMIT_BYTES = 56 * 1024 * 1024

FFN_TM = 512
FFN_CK = 1408
PROJ_TM = 512
ATT_TQ = 128
SEL_TQ = 256
RW_C = 64
RW_Q = 256
RW_NCH = 4


def _cparams(sem):
    return pltpu.CompilerParams(dimension_semantics=sem, vmem_limit_bytes=VMEM_LIMIT_BYTES)


def _rms(x, g):
    ms = jnp.mean(x * x, axis=-1, keepdims=True)
    return x * lax.rsqrt(ms + RMS_EPS) * g


def _const_spec(shape):
    nd = len(shape)
    return pl.BlockSpec(shape, lambda *_: (0,) * nd, pipeline_mode=pl.Buffered(1))


def _dot(a, b):
    return jnp.dot(a, b, preferred_element_type=F32)


def _dot_nt(a, b):
    return lax.dot_general(a, b, (((1,), (1,)), ((), ())), preferred_element_type=F32)


def _dot_tn(a, b):
    return lax.dot_general(a, b, (((0,), (0,)), ((), ())), preferred_element_type=F32)


def _split2(x):
    hi = x.astype(BF16)
    lo = (x - hi.astype(F32)).astype(BF16)
    return hi, lo


def _split3(x):
    hi = x.astype(BF16)
    r1 = x - hi.astype(F32)
    mid = r1.astype(BF16)
    lo = (r1 - mid.astype(F32)).astype(BF16)
    return hi, mid, lo


def _ffn_body(x_ref, gpre_ref, wgu_ref, wd_ref, gpost_ref, o_ref):
    x = x_ref[...]
    xn = _rms(x, gpre_ref[...]).astype(BF16)
    acc = None
    for c in range(D_FF // FFN_CK):
        lo = c * FFN_CK
        gate = _dot(xn, wgu_ref[:, lo:lo + FFN_CK])
        up = _dot(xn, wgu_ref[:, D_FF + lo:D_FF + lo + FFN_CK])
        act = (gate * jax.nn.sigmoid(gate) * up).astype(BF16)
        part = _dot(act, wd_ref[lo:lo + FFN_CK, :])
        acc = part if acc is None else acc + part
    o_ref[...] = x + HALF_STEP * _rms(acc, gpost_ref[...])


def _ffn(h2, g_pre, w_gu, w_down, g_post):
    t = h2.shape[0]
    tm = min(FFN_TM, t)
    return pl.pallas_call(
        _ffn_body,
        out_shape=jax.ShapeDtypeStruct((t, D_MODEL), F32),
        grid=(t // tm,),
        in_specs=[
            pl.BlockSpec((tm, D_MODEL), lambda i: (i, 0)),
            _const_spec((1, D_MODEL)),
            _const_spec((D_MODEL, 2 * D_FF)),
            _const_spec((D_FF, D_MODEL)),
            _const_spec((1, D_MODEL)),
        ],
        out_specs=pl.BlockSpec((tm, D_MODEL), lambda i: (i, 0)),
        compiler_params=_cparams(("parallel",)),
        name="ffn",
    )(h2, g_pre.reshape(1, D_MODEL), w_gu.astype(BF16), w_down.astype(BF16), g_post.reshape(1, D_MODEL))


def _outproj_body(y_ref, h_ref, w_ref, g_ref, o_ref):
    m = _dot(y_ref[...].astype(BF16), w_ref[...])
    o_ref[...] = h_ref[...] + _rms(m, g_ref[...])


def _outproj(y2, h2, w_out, g_post):
    t = h2.shape[0]
    tm = min(PROJ_TM, t)
    return pl.pallas_call(
        _outproj_body,
        out_shape=jax.ShapeDtypeStruct((t, D_MODEL), F32),
        grid=(t // tm,),
        in_specs=[
            pl.BlockSpec((tm, D_MODEL), lambda i: (i, 0)),
            pl.BlockSpec((tm, D_MODEL), lambda i: (i, 0)),
            _const_spec((D_MODEL, D_MODEL)),
            _const_spec((1, D_MODEL)),
        ],
        out_specs=pl.BlockSpec((tm, D_MODEL), lambda i: (i, 0)),
        compiler_params=_cparams(("parallel",)),
        name="outproj",
    )(y2, h2, w_out.astype(BF16), g_post.reshape(1, D_MODEL))


def _nsa_proj_body(h_ref, g_ref, wq_ref, wkv_ref, wg_ref,
                   q_ref, kc_ref, vc_ref, ks_ref, vs_ref, kw_ref, vw_ref, gate_ref):
    u = _rms(h_ref[0], g_ref[...]).astype(BF16)
    q_ref[0] = (_dot(u, wq_ref[...]) * (HD ** -0.5)).astype(BF16)
    kv = _dot(u, wkv_ref[...])
    kc_ref[0] = kv[:, 0 * KV_W:1 * KV_W]
    vc_ref[0] = kv[:, 1 * KV_W:2 * KV_W]
    for i, ref in enumerate((ks_ref, vs_ref, kw_ref, vw_ref)):
        blk = kv[:, (2 + i) * KV_W:(3 + i) * KV_W].astype(BF16)
        for g in range(N_KV):
            ref[0, g] = blk[:, g * HD:(g + 1) * HD]
    gate_ref[0] = jax.nn.sigmoid(_dot(u, wg_ref[...]))


def _nsa_proj(h, g_pre, w_in):
    b, s, _ = h.shape
    tm = min(PROJ_TM, s)
    nq = N_HEADS * HD
    w_q = w_in[:, :nq].astype(BF16)
    w_kv = w_in[:, nq:nq + 6 * KV_W].astype(BF16)
    w_gl = w_in[:, nq + 6 * KV_W:].reshape(D_MODEL, N_KV, N_GRP * 3)
    w_gl = jnp.pad(w_gl, ((0, 0), (0, 0), (0, LANES - N_GRP * 3))).reshape(D_MODEL, N_KV * LANES).astype(BF16)
    hm = jax.ShapeDtypeStruct((b, N_KV, s, HD), BF16)
    hm_spec = pl.BlockSpec((1, N_KV, tm, HD), lambda bi, i: (bi, 0, i, 0))
    tok = lambda w: pl.BlockSpec((1, tm, w), lambda bi, i: (bi, i, 0))
    return pl.pallas_call(
        _nsa_proj_body,
        out_shape=(
            jax.ShapeDtypeStruct((b, s, nq), BF16),
            jax.ShapeDtypeStruct((b, s, KV_W), F32),
            jax.ShapeDtypeStruct((b, s, KV_W), F32),
            hm, hm, hm, hm,
            jax.ShapeDtypeStruct((b, s, N_KV * LANES), F32),
        ),
        grid=(b, s // tm),
        in_specs=[
            tok(D_MODEL),
            _const_spec((1, D_MODEL)),
            _const_spec((D_MODEL, nq)),
            _const_spec((D_MODEL, 6 * KV_W)),
            _const_spec((D_MODEL, N_KV * LANES)),
        ],
        out_specs=(tok(nq), tok(KV_W), tok(KV_W), hm_spec, hm_spec, hm_spec, hm_spec, tok(N_KV * LANES)),
        compiler_params=_cparams(("parallel", "parallel")),
        name="nsa_proj",
    )(h, g_pre.reshape(1, D_MODEL), w_q, w_kv, w_gl)


def _compress_body(x_ref, pea_ref, peb_ref, wa_ref, wb_ref, w2_ref, o_ref):
    x = x_ref[0]
    ha = _dot((x + pea_ref[...]).astype(BF16), wa_ref[...])
    hb = _dot((x + peb_ref[...]).astype(BF16), wb_ref[...])
    n = ha.shape[0]
    hid = ha + pltpu.roll(hb, n - 1, 0)
    hid = hid * jax.nn.sigmoid(hid)
    out = _dot(hid.astype(BF16), w2_ref[...]).astype(BF16)
    for g in range(N_KV):
        o_ref[0, g] = out[:, g * HD:(g + 1) * HD]


def _compress(t, pe, w1, w2):
    b, s, _ = t.shape
    half = CMP_BLOCK // 2
    rows = s // half
    x = t.reshape(b, rows, half * KV_W)
    eye = jnp.eye(N_KV, dtype=F32)
    wbd = jnp.einsum("ldc,gh->lgdhc", w1, eye).reshape(CMP_BLOCK, KV_W, KV_W)
    wa = wbd[:half].reshape(half * KV_W, KV_W).astype(BF16)
    wb = wbd[half:].reshape(half * KV_W, KV_W).astype(BF16)
    pe_t = jnp.broadcast_to(pe[:, None, :], (CMP_BLOCK, N_KV, HD))
    pea = pe_t[:half].reshape(1, half * KV_W)
    peb = pe_t[half:].reshape(1, half * KV_W)
    w2bd = jnp.einsum("cd,gh->gchd", w2, eye).reshape(KV_W, KV_W).astype(BF16)
    return pl.pallas_call(
        _compress_body,
        out_shape=jax.ShapeDtypeStruct((b, N_KV, rows, HD), BF16),
        grid=(b,),
        in_specs=[
            pl.BlockSpec((1, rows, half * KV_W), lambda bi: (bi, 0, 0)),
            _const_spec((1, half * KV_W)),
            _const_spec((1, half * KV_W)),
            _const_spec((half * KV_W, KV_W)),
            _const_spec((half * KV_W, KV_W)),
            _const_spec((KV_W, KV_W)),
        ],
        out_specs=pl.BlockSpec((1, N_KV, rows, HD), lambda bi: (bi, 0, 0, 0)),
        compiler_params=_cparams(("parallel",)),
        name="nsa_compress",
    )(x, pea, peb, wa, wb, w2bd)


def _alibi_slope(h):
    return float(2.0 ** (-8.0 * (h + 1) / N_HEADS))


def _select_body(q_ref, kc_ref, c2s_ref, eye_ref, sel_ref, *, n_sel, n_cmp_rows):
    tq = q_ref.shape[1]
    t0 = pl.program_id(1) * tq
    n_io = lax.broadcasted_iota(jnp.int32, (n_cmp_rows, tq), 0)
    t_io = t0 + lax.broadcasted_iota(jnp.int32, (n_cmp_rows, tq), 1)
    dist = t_io - (n_io * CMP_STRIDE + (CMP_BLOCK - 1))
    valid = (dist >= 0) & (n_io < n_cmp_rows - 1)
    distf = dist.astype(F32)

    s_io = lax.broadcasted_iota(jnp.int32, (n_sel, tq), 0)
    tt = t0 + lax.broadcasted_iota(jnp.int32, (n_sel, tq), 1)
    cur = lax.shift_right_logical(tt, 6)
    forced = (s_io == 0) | (s_io == cur) | (s_io == cur - 1)
    visible = s_io * SEL_BLOCK <= tt

    for g in range(N_KV):
        kc = kc_ref[0, g]
        imp = jnp.zeros((n_sel, tq), F32)
        for r in range(N_GRP):
            hh = g * N_GRP + r
            qh = q_ref[0, :, hh * HD:(hh + 1) * HD]
            s = _dot_nt(kc, qh) - _alibi_slope(hh) * distf
            s = jnp.where(valid, s, MASK_VALUE)
            m = jnp.max(s, axis=0, keepdims=True)
            p = jnp.where(valid, jnp.exp(s - m), 0.0)
            l = jnp.sum(p, axis=0, keepdims=True)
            p = p / jnp.where(l > 0.0, l, 1.0)
            p_hi, p_lo = _split2(p)
            imp = imp + _dot(c2s_ref[...], p_hi) + _dot(c2s_ref[...], p_lo)
        x = jnp.where(forced, FORCED_SCORE, jnp.where(visible, imp, -1.0))
        cnt = jnp.zeros((n_sel, tq), F32)
        for j in range(n_sel):
            xj = x[j:j + 1, :]
            ge = jnp.where(xj >= x, 1.0, 0.0)
            gt = jnp.where(xj > x, 1.0, 0.0)
            cnt = cnt + jnp.where(s_io > j, ge, gt)
        sel_t = jnp.where(cnt < float(N_SELECT), 1.0, 0.0).astype(BF16)
        sel_ref[0, g] = _dot_nt(eye_ref[...], sel_t).astype(BF16)


def _select(q, k_cmp):
    b, s, nq = q.shape
    tq = min(SEL_TQ, s)
    n_sel = s // SEL_BLOCK
    rows = k_cmp.shape[2]
    n_cmp = (s - CMP_BLOCK) // CMP_STRIDE + 1
    assert rows == n_cmp + 1
    cs = np.arange(rows) * CMP_STRIDE
    bs = np.arange(n_sel) * SEL_BLOCK
    overlap = np.clip(np.minimum(cs[:, None] + CMP_BLOCK, bs[None, :] + SEL_BLOCK)
                      - np.maximum(cs[:, None], bs[None, :]), 0, None)
    c2s_t = jnp.asarray((overlap / CMP_BLOCK).T, BF16)
    eye = jnp.eye(tq, dtype=BF16)
    return pl.pallas_call(
        functools.partial(_select_body, n_sel=n_sel, n_cmp_rows=rows),
        out_shape=jax.ShapeDtypeStruct((b, N_KV, s, n_sel), BF16),
        grid=(b, s // tq),
        in_specs=[
            pl.BlockSpec((1, tq, nq), lambda bi, i: (bi, i, 0)),
            pl.BlockSpec((1, N_KV, rows, HD), lambda bi, i: (bi, 0, 0, 0)),
            _const_spec((n_sel, rows)),
            _const_spec((tq, tq)),
        ],
        out_specs=pl.BlockSpec((1, N_KV, tq, n_sel), lambda bi, i: (bi, 0, i, 0)),
        compiler_params=_cparams(("parallel", "parallel")),
        name="nsa_select",
    )(q, k_cmp, c2s_t, eye)


def _attn_body(slope_ref, q_ref, ks_ref, vs_ref, kw_ref, vw_ref, kc_ref, vc_ref, sel_ref, gate_ref,
               o_ref, *, n_cmp_rows):
    tq = q_ref.shape[1]
    tk = tq
    g = pl.program_id(1)
    qi = pl.program_id(2)
    t0 = qi * tq
    slopes = [slope_ref[g * N_GRP + r] for r in range(N_GRP)]
    qs = [q_ref[0, :, r * HD:(r + 1) * HD] for r in range(N_GRP)]

    rel = (lax.broadcasted_iota(jnp.int32, (tq, tk), 1) - lax.broadcasted_iota(jnp.int32, (tq, tk), 0))

    def online(c, carry, k_ref, v_ref, valid, relc):
        k = k_ref[0, 0, pl.ds(pl.multiple_of(c * tk, tk), tk), :]
        v = v_ref[0, 0, pl.ds(pl.multiple_of(c * tk, tk), tk), :]
        relf = relc.astype(F32)
        out = []
        for r in range(N_GRP):
            m, l, acc = carry[r]
            s = _dot_nt(qs[r], k) + slopes[r] * relf
            s = jnp.where(valid, s, MASK_VALUE)
            m_new = jnp.maximum(m, jnp.max(s, axis=-1, keepdims=True))
            alpha = jnp.exp(m - m_new)
            p = jnp.exp(s - m_new)
            l_new = alpha * l + jnp.sum(p, axis=-1, keepdims=True)
            acc_new = alpha * acc + _dot(p.astype(BF16), v)
            out.append((m_new, l_new, acc_new))
        return tuple(out)

    def init():
        return tuple((jnp.full((tq, 1), MASK_VALUE, F32), jnp.zeros((tq, 1), F32), jnp.zeros((tq, HD), F32))
                     for _ in range(N_GRP))

    sel = sel_ref[0, 0]
    n_sel = sel.shape[1]
    blk_io = lax.broadcasted_iota(jnp.int32, (n_sel, tk), 0)
    col_blk = lax.shift_right_logical(lax.broadcasted_iota(jnp.int32, (n_sel, tk), 1), 6)

    def sel_step(c, carry):
        expand = jnp.where(blk_io == c * (tk // SEL_BLOCK) + col_blk, 1.0, 0.0).astype(BF16)
        chosen = _dot(sel, expand) > 0.5
        relc = rel + (c - qi) * tk
        return online(c, carry, ks_ref, vs_ref, chosen & (relc <= 0), relc)

    res_s = lax.fori_loop(0, qi + 1, sel_step, init())

    def win_step(c, carry):
        relc = rel + (c - qi) * tk
        return online(c, carry, kw_ref, vw_ref, (relc <= 0) & (relc > -WINDOW), relc)

    res_w = lax.fori_loop(jnp.maximum(qi - WINDOW // tk, 0), qi + 1, win_step, init())

    n_io = lax.broadcasted_iota(jnp.int32, (tq, n_cmp_rows), 1)
    t_io = t0 + lax.broadcasted_iota(jnp.int32, (tq, n_cmp_rows), 0)
    dist = t_io - (n_io * CMP_STRIDE + (CMP_BLOCK - 1))
    valid_c = (dist >= 0) & (n_io < n_cmp_rows - 1)
    distf = dist.astype(F32)
    kc = kc_ref[0, 0]
    vc = vc_ref[0, 0]

    gate = gate_ref[0]
    outs = []
    for r in range(N_GRP):
        s = jnp.where(valid_c, _dot_nt(qs[r], kc) - slopes[r] * distf, MASK_VALUE)
        m = jnp.max(s, axis=-1, keepdims=True)
        p = jnp.where(valid_c, jnp.exp(s - m), 0.0)
        l = jnp.sum(p, axis=-1, keepdims=True)
        o_c = _dot(p.astype(BF16), vc) / jnp.where(l > 0.0, l, 1.0)
        o_s = res_s[r][2] / res_s[r][1]
        o_w = res_w[r][2] / res_w[r][1]
        outs.append(gate[:, 3 * r:3 * r + 1] * o_c + gate[:, 3 * r + 1:3 * r + 2] * o_s
                    + gate[:, 3 * r + 2:3 * r + 3] * o_w)
    o_ref[0] = jnp.concatenate(outs, axis=-1).astype(BF16)


def _attention(q, ks, vs, kw, vw, k_cmp, v_cmp, sel, gates):
    b, s, nq = q.shape
    tq = min(ATT_TQ, s)
    rows = k_cmp.shape[2]
    n_sel = s // SEL_BLOCK
    gw = N_GRP * HD
    slopes = jnp.asarray([_alibi_slope(h) for h in range(N_HEADS)], F32)
    full = pl.BlockSpec((1, 1, s, HD), lambda bi, g, i: (bi, g, 0, 0))
    cmp_spec = pl.BlockSpec((1, 1, rows, HD), lambda bi, g, i: (bi, g, 0, 0))
    return pl.pallas_call(
        functools.partial(_attn_body, n_cmp_rows=rows),
        out_shape=jax.ShapeDtypeStruct((b, s, nq), BF16),
        grid=(b, N_KV, s // tq),
        in_specs=[
            pl.BlockSpec(memory_space=pltpu.SMEM),
            pl.BlockSpec((1, tq, gw), lambda bi, g, i: (bi, i, g)),
            full, full, full, full, cmp_spec, cmp_spec,
            pl.BlockSpec((1, 1, tq, n_sel), lambda bi, g, i: (bi, g, i, 0)),
            pl.BlockSpec((1, tq, LANES), lambda bi, g, i: (bi, i, g)),
        ],
        out_specs=pl.BlockSpec((1, tq, gw), lambda bi, g, i: (bi, i, g)),
        compiler_params=_cparams(("parallel", "parallel", "arbitrary")),
        name="nsa_attention",
    )(slopes, q, ks, vs, kw, vw, k_cmp, v_cmp, sel, gates)


def _nsa_mixer(h, g_pre, w_in, pe_k, w_ck1, w_ck2, pe_v, w_cv1, w_cv2, w_out, g_post):
    b, s, d = h.shape
    q, kc, vc, ks, vs, kw, vw, gates = _nsa_proj(h, g_pre, w_in)
    k_cmp = _compress(kc, pe_k, w_ck1, w_ck2)
    v_cmp = _compress(vc, pe_v, w_cv1, w_cv2)
    sel = _select(q, k_cmp)
    out = _attention(q, ks, vs, kw, vw, k_cmp, v_cmp, sel, gates)
    return _outproj(out.reshape(b * s, d), h.reshape(b * s, d), w_out, g_post).reshape(b, s, d)


def _head_sum(x, sel_ref):
    hi, lo = _split2(x)
    return _dot(hi, sel_ref[...]) + _dot(lo, sel_ref[...])


def _head_expand(x, selt_ref):
    hi, lo = _split2(x)
    return _dot(hi, selt_ref[...]) + _dot(lo, selt_ref[...])


def _softplus(x):
    return jnp.maximum(x, 0.0) + jnp.log(1.0 + jnp.exp(-jnp.abs(x)))


def _rw_proj_body(h_ref, hp_ref, g_ref, mu_ref, vec_ref, wr_ref, wk_ref, wv_ref, wd_ref, wa_ref, wg_ref,
                  w2d_ref, w2a_ref, w2g_ref, sel_ref, selt_ref,
                  r_ref, lw_ref, k_ref, v_ref, a_ref, b_ref, gg_ref):
    tm = h_ref.shape[1]
    u = _rms(h_ref[0], g_ref[...])
    prev = _rms(hp_ref[0], g_ref[...])[7:8, :]
    prev = jnp.where(pl.program_id(1) == 0, 0.0, prev)
    row = lax.broadcasted_iota(jnp.int32, (tm, D_MODEL), 0)
    u_prev = jnp.where(row == 0, prev, pltpu.roll(u, 1, 0))
    xx = u_prev - u

    def mix(i):
        return (u + xx * mu_ref[i:i + 1, :]).astype(BF16)

    w0, a0, k_k, k_a = (vec_ref[i:i + 1, :] for i in range(4))
    r = _dot(mix(0), wr_ref[...])
    k = _dot(mix(1), wk_ref[...])
    v = _dot(mix(2), wv_ref[...])
    d1 = jnp.tanh(_dot(mix(3), wd_ref[...])).astype(BF16)
    a1 = _dot(mix(4), wa_ref[...]).astype(BF16)
    g1 = jax.nn.sigmoid(_dot(mix(5), wg_ref[...])).astype(BF16)
    w = -_softplus(-(w0 + _dot(d1, w2d_ref[...]))) - 0.5
    alpha = jax.nn.sigmoid(a0 + _dot(a1, w2a_ref[...]))
    kk = k * k_k
    norm = jnp.sqrt(_head_sum(kk * kk, sel_ref))
    kk = kk * _head_expand(1.0 / jnp.maximum(norm, 1e-12), selt_ref)
    r_ref[0] = r
    lw_ref[0] = -jnp.exp(w)
    k_ref[0] = k * (1.0 + (alpha - 1.0) * k_a)
    v_ref[0] = v
    a_ref[0] = -kk
    b_ref[0] = kk * alpha
    gg_ref[0] = _dot(g1, w2g_ref[...])


def _head_selectors():
    lane_head = np.arange(D_MODEL) // HD
    sel = (lane_head[:, None] == np.arange(LANES)[None, :]).astype(np.float32)
    return jnp.asarray(sel, BF16), jnp.asarray(sel.T, BF16)


def _pad_cols(w, n):
    return jnp.pad(w, ((0, 0), (0, n - w.shape[1])))


def _pad_rows(w, n):
    return jnp.pad(w, ((0, n - w.shape[0]), (0, 0)))


def _rw_proj(h, g_pre, mu, w_in, w0, w_w2, a0, w_a2, w_g2, k_k, k_a):
    b, s, d = h.shape
    tm = min(PROJ_TM, s)
    offs = np.cumsum((0, d, d, d, DECAY_LORA, AAA_LORA, GATE_LORA))
    cols = [w_in[:, int(offs[i]):int(offs[i + 1])] for i in range(6)]
    ld, lg = LANES, 2 * LANES
    wr, wk, wv = (c.astype(BF16) for c in cols[:3])
    wd = _pad_cols(cols[3], ld).astype(BF16)
    wa = _pad_cols(cols[4], ld).astype(BF16)
    wg = _pad_cols(cols[5], lg).astype(BF16)
    w2d = _pad_rows(w_w2, ld).astype(BF16)
    w2a = _pad_rows(w_a2, ld).astype(BF16)
    w2g = _pad_rows(w_g2, lg).astype(BF16)
    mu8 = _pad_rows(mu, 8)
    vecs = _pad_rows(jnp.stack([w0, a0, k_k, k_a]), 8)
    sel, selt = _head_selectors()
    tok = pl.BlockSpec((1, tm, d), lambda bi, i: (bi, i, 0))
    out = jax.ShapeDtypeStruct((b, s, d), F32)
    return pl.pallas_call(
        _rw_proj_body,
        out_shape=(out,) * 7,
        grid=(b, s // tm),
        in_specs=[
            tok,
            pl.BlockSpec((1, 8, d), lambda bi, i: (bi, jnp.maximum(i * (tm // 8) - 1, 0), 0)),
            _const_spec((1, d)), _const_spec((8, d)), _const_spec((8, d)),
            _const_spec((d, d)), _const_spec((d, d)), _const_spec((d, d)),
            _const_spec((d, ld)), _const_spec((d, ld)), _const_spec((d, lg)),
            _const_spec((ld, d)), _const_spec((ld, d)), _const_spec((lg, d)),
            _const_spec((d, LANES)), _const_spec((LANES, d)),
        ],
        out_specs=(tok,) * 7,
        compiler_params=_cparams(("parallel", "parallel")),
        name="rwkv_proj",
    )(h, h, g_pre.reshape(1, d), mu8, vecs, wr, wk, wv, wd, wa, wg, w2d, w2a, w2g, sel, selt)


RW_HIGH_PRECISION_INVERSE = True


def _mm(a, b, nt=False, high=False):
    f = _dot_nt if nt else _dot
    if not high:
        return f(a.astype(BF16), b.astype(BF16))
    a_hi, a_lo = _split2(a)
    b_hi, b_lo = _split2(b)
    return f(a_hi, b_hi) + f(a_hi, b_lo) + f(a_lo, b_hi)


def _rw_prep_body(ltri_ref, r_ref, lw_ref, k_ref, v_ref, a_ref, b_ref,
                  rhat_ref, y1_ref, g_ref, n_ref):
    c = RW_C
    q = RW_Q
    nh = q // HD
    lane_head = lax.shift_right_logical(lax.broadcasted_iota(jnp.int32, (c, q), 1), 6)
    ri = lax.broadcasted_iota(jnp.int32, (q, q), 0)
    ci = lax.broadcasted_iota(jnp.int32, (q, q), 1)
    same_head = lax.shift_right_logical(ri, 6) == lax.shift_right_logical(ci, 6)
    strict_bd = same_head & ((ci & (c - 1)) < (ri & (c - 1)))
    eye = ri == ci
    t_io = lax.broadcasted_iota(jnp.int32, (c, q), 0)
    j_io = lax.broadcasted_iota(jnp.int32, (c, q), 1) & (c - 1)
    strict_ls = j_io < t_io
    incl_ls = j_io <= t_io
    high = RW_HIGH_PRECISION_INVERSE

    def expand4(x):
        return jnp.concatenate([jnp.where(lane_head == hh, x, 0.0) for hh in range(nh)], axis=0)

    def collapse4(x):
        out = x[0:c]
        for hh in range(1, nh):
            out = out + x[hh * c:(hh + 1) * c]
        return out

    for ch in range(r_ref.shape[1] // c):
        sl = slice(ch * c, (ch + 1) * c)
        lw = lw_ref[0, sl, :]
        cum = None
        for part in _split3(lw):
            t = _dot(ltri_ref[...], part)
            cum = t if cum is None else cum + t
        cum_c = cum[c - 1:c, :]
        a = a_ref[0, sl, :]
        b = b_ref[0, sl, :]
        k = k_ref[0, sl, :]
        v = v_ref[0, sl, :]
        e_inv = jnp.exp(-cum)
        e_rem = jnp.exp(cum_c - cum)
        at = a * jnp.exp(cum - lw)
        rt = r_ref[0, sl, :] * jnp.exp(cum)
        bt = b * e_inv
        kt = k * e_inv
        bbar = b * e_rem
        kbar = k * e_rem

        x4 = expand4(at)
        b4 = expand4(bt)
        k4 = expand4(kt)
        v4 = expand4(v).astype(BF16)

        lbd = jnp.where(strict_bd, _mm(x4, b4, nt=True, high=high), 0.0)
        tbd = jnp.where(eye, 1.0, lbd)
        p = _mm(lbd, lbd, high=high)
        n_lvl = int(np.log2(c)) - 1
        for lvl in range(n_lvl):
            tbd = tbd + _mm(p, tbd, high=high)
            if lvl + 1 < n_lvl:
                p = _mm(p, p, high=high)
        t_ls = collapse4(tbd).astype(BF16)

        ar = _dot_nt(jnp.concatenate([at, rt], axis=0).astype(BF16), k4.astype(BF16))
        a_ak = jnp.where(strict_ls, ar[0:c], 0.0).astype(BF16)
        a_rk = jnp.where(incl_ls, ar[c:2 * c], 0.0).astype(BF16)
        a_rb = jnp.where(incl_ls, _dot_nt(rt.astype(BF16), b4.astype(BF16)), 0.0).astype(BF16)

        g1 = _dot(a_ak, v4)
        y0 = _dot(a_rk, v4)
        u0 = _dot(t_ls, expand4(g1).astype(BF16))
        ahat = _dot(t_ls, x4.astype(BF16))
        rhat = rt + _dot(a_rb, expand4(ahat).astype(BF16))
        y1 = y0 + _dot(a_rb, expand4(u0).astype(BF16))

        gm = _dot_tn(bbar.astype(BF16), ahat.astype(BF16))
        gm = jnp.where(same_head, gm, 0.0) + jnp.where(eye, jnp.exp(cum_c), 0.0)
        nm = _dot_tn(jnp.concatenate([bbar, kbar], axis=0).astype(BF16),
                     jnp.concatenate([u0, v], axis=0).astype(BF16))
        nm = jnp.where(same_head, nm, 0.0)

        rhat_ref[0, sl, :] = rhat.astype(BF16)
        y1_ref[0, sl, :] = y1
        g_ref[0, sl, :] = collapse4(gm).astype(BF16)
        n_ref[0, sl, :] = collapse4(nm)


def _rw_prep(r, lw, k, v, a, b):
    bsz, s, d = r.shape
    rows = min(RW_C * RW_NCH, s)
    ltri = jnp.asarray(np.tril(np.ones((RW_C, RW_C), np.float32)), BF16)
    blk = pl.BlockSpec((1, rows, RW_Q), lambda bi, qi, j: (bi, j, qi))
    return pl.pallas_call(
        _rw_prep_body,
        out_shape=(
            jax.ShapeDtypeStruct((bsz, s, d), BF16),
            jax.ShapeDtypeStruct((bsz, s, d), F32),
            jax.ShapeDtypeStruct((bsz, s, d), BF16),
            jax.ShapeDtypeStruct((bsz, s, d), F32),
        ),
        grid=(bsz, d // RW_Q, s // rows),
        in_specs=[_const_spec((RW_C, RW_C))] + [blk] * 6,
        out_specs=(blk,) * 4,
        compiler_params=_cparams(("parallel", "parallel", "parallel")),
        name="rwkv_prep",
    )(ltri, r, lw, k, v, a, b)


def _rw_scan_body(rhat_ref, y1_ref, g_ref, n_ref, y_ref, h_sc):
    c = RW_C
    q = RW_Q
    nq = h_sc.shape[0]

    @pl.when(pl.program_id(1) == 0)
    def _():
        h_sc[...] = jnp.zeros(h_sc.shape, F32)

    ri = lax.broadcasted_iota(jnp.int32, (q, q), 0)
    ci = lax.broadcasted_iota(jnp.int32, (q, q), 1)
    same_head = lax.shift_right_logical(ri, 6) == lax.shift_right_logical(ci, 6)

    def step(ch, carry):
        rows = pl.ds(pl.multiple_of(ch * c, c), c)
        for qi in range(nq):
            lanes = slice(qi * q, (qi + 1) * q)
            hb = h_sc[qi].astype(BF16)
            y_ref[0, rows, lanes] = _dot(rhat_ref[0, rows, lanes], hb) + y1_ref[0, rows, lanes]
            g_ls = g_ref[0, rows, lanes]
            n_ls = n_ref[0, rows, lanes]
            gbd = jnp.where(same_head, jnp.concatenate([g_ls] * (q // c), axis=0), 0.0)
            nbd = jnp.where(same_head, jnp.concatenate([n_ls] * (q // c), axis=0), 0.0)
            h_sc[qi] = _dot(gbd.astype(BF16), hb) + nbd
        return carry

    lax.fori_loop(0, rhat_ref.shape[1] // c, step, 0)


def _rw_scan(rhat, y1, g, n):
    bsz, s, d = rhat.shape
    rows = min(512, s)
    blk = pl.BlockSpec((1, rows, d), lambda bi, j: (bi, j, 0))
    return pl.pallas_call(
        _rw_scan_body,
        out_shape=jax.ShapeDtypeStruct((bsz, s, d), F32),
        grid=(bsz, s // rows),
        in_specs=[blk] * 4,
        out_specs=blk,
        scratch_shapes=[pltpu.VMEM((d // RW_Q, RW_Q, RW_Q), F32)],
        compiler_params=_cparams(("parallel", "arbitrary")),
        name="rwkv_scan",
    )(rhat, y1, g, n)


def _rw_post_body(y_ref, r_ref, k_ref, v_ref, gg_ref, h_ref, vec_ref, w_ref, gpost_ref, sel_ref, selt_ref, o_ref):
    gn_w, gn_b, r_k = (vec_ref[i:i + 1, :] for i in range(3))
    y = y_ref[...]
    inv_n = 1.0 / HD
    mean = _head_expand(_head_sum(y, sel_ref) * inv_n, selt_ref)
    yc = y - mean
    var = _head_sum(yc * yc, sel_ref) * inv_n
    yn = yc * _head_expand(lax.rsqrt(var + GN_EPS), selt_ref) * gn_w + gn_b
    v = v_ref[...]
    bonus = _head_expand(_head_sum(r_ref[...] * k_ref[...] * r_k, sel_ref), selt_ref) * v
    z = ((yn + bonus) * gg_ref[...]).astype(BF16)
    o_ref[...] = h_ref[...] + _rms(_dot(z, w_ref[...]), gpost_ref[...])


def _rw_post(y, r, k, v, gg, h, gn_w, gn_b, r_k, w_out, g_post):
    t, d = h.shape
    tm = min(PROJ_TM, t)
    vecs = _pad_rows(jnp.stack([gn_w, gn_b, r_k.reshape(d)]), 8)
    sel, selt = _head_selectors()
    tok = pl.BlockSpec((tm, d), lambda i: (i, 0))
    return pl.pallas_call(
        _rw_post_body,
        out_shape=jax.ShapeDtypeStruct((t, d), F32),
        grid=(t // tm,),
        in_specs=[tok] * 6 + [_const_spec((8, d)), _const_spec((d, d)), _const_spec((1, d)),
                              _const_spec((d, LANES)), _const_spec((LANES, d))],
        out_specs=tok,
        compiler_params=_cparams(("parallel",)),
        name="rwkv_post",
    )(y, r, k, v, gg, h, vecs, w_out.astype(BF16), g_post.reshape(1, d), sel, selt)


def _rwkv_mixer(h, g_pre, mu, w_in, w0, w_w2, a0, w_a2, w_g2, k_k, k_a, r_k, gn_w, gn_b, w_out, g_post):
    b, s, d = h.shape
    r, lw, k, v, a, bb, gg = _rw_proj(h, g_pre, mu, w_in, w0, w_w2, a0, w_a2, w_g2, k_k, k_a)
    rhat, y1, g, n = _rw_prep(r, lw, k, v, a, bb)
    y = _rw_scan(rhat, y1, g, n)
    f2 = lambda x: x.reshape(b * s, d)
    return _rw_post(f2(y), f2(r), f2(k), f2(v), f2(gg), f2(h), gn_w, gn_b, r_k, w_out, g_post).reshape(b, s, d)


def kernel(x, ffn1_norm_pre, ffn1_w_gu, ffn1_w_down, ffn1_norm_post, mix_norm_pre, nsa_w_in, nsa_pe_k,
           nsa_w_ck1, nsa_w_ck2, nsa_pe_v, nsa_w_cv1, nsa_w_cv2, nsa_w_out, rwkv_mu, rwkv_w_in, rwkv_w0,
           rwkv_w_w2, rwkv_a0, rwkv_w_a2, rwkv_w_g2, rwkv_k_k, rwkv_k_a, rwkv_r_k, rwkv_gn_w, rwkv_gn_b,
           rwkv_w_out, mix_norm_post, ffn2_norm_pre, ffn2_w_gu, ffn2_w_down, ffn2_norm_post):
    b, s, d = x.shape
    flat = lambda t: t.reshape(b * s, d)
    cube = lambda t: t.reshape(b, s, d)
    h = x
    depth = ffn1_norm_pre.shape[0]
    for i in range(depth):
        h = cube(_ffn(flat(h), ffn1_norm_pre[i], ffn1_w_gu[i], ffn1_w_down[i], ffn1_norm_post[i]))
        j = i // 2
        if i % 2 == 0:
            h = _nsa_mixer(h, mix_norm_pre[i], nsa_w_in[j], nsa_pe_k[j], nsa_w_ck1[j], nsa_w_ck2[j],
                           nsa_pe_v[j], nsa_w_cv1[j], nsa_w_cv2[j], nsa_w_out[j], mix_norm_post[i])
        else:
            h = _rwkv_mixer(h, mix_norm_pre[i], rwkv_mu[j], rwkv_w_in[j], rwkv_w0[j], rwkv_w_w2[j],
                            rwkv_a0[j], rwkv_w_a2[j], rwkv_w_g2[j], rwkv_k_k[j], rwkv_k_a[j], rwkv_r_k[j],
                            rwkv_gn_w[j], rwkv_gn_b[j], rwkv_w_out[j], mix_norm_post[i])
        h = cube(_ffn(flat(h), ffn2_norm_pre[i], ffn2_w_gu[i], ffn2_w_down[i], ffn2_norm_post[i]))
    return h
```

```python
import functools

import numpy as np
import jax
import jax.numpy as jnp
from jax import lax
from jax.experimental import pallas as pl
from jax.experimental.pallas import tpu as pltpu

F32 = jnp.float32
BF16 = jnp.bfloat16

D_MODEL = 1024
D_FF = 2816
HALF_STEP = 0.5
RMS_EPS = 1e-6
MASK_VALUE = -1e30

HD = 64
N_HEADS = 16
N_KV = 4
N_GRP = 4
KV_W = N_KV * HD
CMP_BLOCK = 32
CMP_STRIDE = 16
SEL_BLOCK = 64
N_SELECT = 16
WINDOW = 512
FORCED_SCORE = 1e4
N_GATE = 3 * N_HEADS

DECAY_LORA = 64
AAA_LORA = 64
GATE_LORA = 160
GN_EPS = 64e-5

LANES = 128
VMEM_LIMIT_BYTES = 56 * 1024 * 1024

FFN_TM = 512
FFN_CK = 1408
PROJ_TM = 512
ATT_T = 256
SEL_TQ = 256
GATE_ROWS = 16
LOG2E = 1.4426950408889634
RW_C = 64
RW_Q = 256
RW_NCH = 4


def _cparams(sem):
    return pltpu.CompilerParams(dimension_semantics=sem, vmem_limit_bytes=VMEM_LIMIT_BYTES)


def _rms(x, g):
    ms = jnp.mean(x * x, axis=-1, keepdims=True)
    return x * lax.rsqrt(ms + RMS_EPS) * g


def _const_spec(shape):
    nd = len(shape)
    return pl.BlockSpec(shape, lambda *_: (0,) * nd, pipeline_mode=pl.Buffered(1))


def _dot(a, b):
    return jnp.dot(a, b, preferred_element_type=F32)


def _dot_nt(a, b):
    return lax.dot_general(a, b, (((1,), (1,)), ((), ())), preferred_element_type=F32)


def _dot_tn(a, b):
    return lax.dot_general(a, b, (((0,), (0,)), ((), ())), preferred_element_type=F32)


def _split2(x):
    hi = x.astype(BF16)
    lo = (x - hi.astype(F32)).astype(BF16)
    return hi, lo


def _split3(x):
    hi = x.astype(BF16)
    r1 = x - hi.astype(F32)
    mid = r1.astype(BF16)
    lo = (r1 - mid.astype(F32)).astype(BF16)
    return hi, mid, lo


def _ffn_body(x_ref, gpre_ref, wgu_ref, wd_ref, gpost_ref, o_ref):
    x = x_ref[...]
    xn = _rms(x, gpre_ref[...]).astype(BF16)
    acc = None
    for c in range(D_FF // FFN_CK):
        lo = c * FFN_CK
        gate = _dot(xn, wgu_ref[:, lo:lo + FFN_CK])
        up = _dot(xn, wgu_ref[:, D_FF + lo:D_FF + lo + FFN_CK])
        act = (gate * jax.nn.sigmoid(gate) * up).astype(BF16)
        part = _dot(act, wd_ref[lo:lo + FFN_CK, :])
        acc = part if acc is None else acc + part
    o_ref[...] = x + HALF_STEP * _rms(acc, gpost_ref[...])


def _ffn(h2, g_pre, w_gu, w_down, g_post):
    t = h2.shape[0]
    tm = min(FFN_TM, t)
    return pl.pallas_call(
        _ffn_body,
        out_shape=jax.ShapeDtypeStruct((t, D_MODEL), F32),
        grid=(t // tm,),
        in_specs=[
            pl.BlockSpec((tm, D_MODEL), lambda i: (i, 0)),
            _const_spec((1, D_MODEL)),
            _const_spec((D_MODEL, 2 * D_FF)),
            _const_spec((D_FF, D_MODEL)),
            _const_spec((1, D_MODEL)),
        ],
        out_specs=pl.BlockSpec((tm, D_MODEL), lambda i: (i, 0)),
        compiler_params=_cparams(("parallel",)),
        name="ffn",
    )(h2, g_pre.reshape(1, D_MODEL), w_gu.astype(BF16), w_down.astype(BF16), g_post.reshape(1, D_MODEL))


def _nsa_proj_body(h_ref, g_ref, wqt_ref, wc_ref, wk_ref, wvt_ref, wgt_ref,
                   qt_ref, kc_ref, vc_ref, ks_ref, kw_ref, vst_ref, vwt_ref, gt_ref):
    tm = h_ref.shape[1]
    u = _rms(h_ref[0], g_ref[...]).astype(BF16)
    qt_ref[0] = (_dot_nt(wqt_ref[...], u) * (HD ** -0.5 * LOG2E)).astype(BF16)
    c = _dot(u, wc_ref[...])
    kc_ref[0] = c[:, :KV_W]
    vc_ref[0] = c[:, KV_W:]
    kk = _dot(u, wk_ref[...])
    t0 = pl.program_id(1) * tm
    lane = lax.broadcasted_iota(jnp.int32, (tm, LANES), 1)
    blk = lax.shift_right_logical(t0 + lax.broadcasted_iota(jnp.int32, (tm, LANES), 0), 6)
    onehot = jnp.where(lane - HD == blk, 1.0, 0.0)
    for g in range(N_KV):
        ks_ref[0, g] = (kk[:, g * LANES:(g + 1) * LANES] + onehot).astype(BF16)
        kw_ref[0, g] = kk[:, (N_KV + g) * LANES:(N_KV + g + 1) * LANES].astype(BF16)
    vt = _dot_nt(wvt_ref[...], u)
    tk = vst_ref.shape[4]
    for j in range(tm // tk):
        cols = slice(j * tk, (j + 1) * tk)
        vst_ref[0, :, j] = vt[:KV_W, cols].reshape(N_KV, HD, tk).astype(BF16)
        vwt_ref[0, :, j] = vt[KV_W:, cols].reshape(N_KV, HD, tk).astype(BF16)
    gt_ref[0] = jax.nn.sigmoid(_dot_nt(wgt_ref[...], u)).reshape(N_KV, GATE_ROWS, tm)


def _nsa_proj(h, g_pre, w_in):
    b, s, _ = h.shape
    tm = min(PROJ_TM, s)
    tk = min(ATT_T, s)
    nq = N_HEADS * HD
    assert s // SEL_BLOCK <= HD
    col = lambda i: w_in[:, nq + i * KV_W:nq + (i + 1) * KV_W]
    w_qt = w_in[:, :nq].T.astype(BF16)
    w_c = jnp.concatenate([col(0), col(1)], axis=1).astype(BF16)
    pad_heads = lambda w: jnp.pad(w.reshape(D_MODEL, N_KV, HD), ((0, 0), (0, 0), (0, LANES - HD))).reshape(D_MODEL, N_KV * LANES)
    w_k = jnp.concatenate([pad_heads(col(2)), pad_heads(col(4))], axis=1).astype(BF16)
    w_vt = jnp.concatenate([col(3), col(5)], axis=1).T.astype(BF16)
    w_gl = w_in[:, nq + 6 * KV_W:].reshape(D_MODEL, N_KV, N_GRP * 3)
    w_gt = jnp.pad(w_gl, ((0, 0), (0, 0), (0, GATE_ROWS - N_GRP * 3))).reshape(D_MODEL, N_KV * GATE_ROWS).T.astype(BF16)
    tok = lambda w: pl.BlockSpec((1, tm, w), lambda bi, i: (bi, i, 0))
    key = jax.ShapeDtypeStruct((b, N_KV, s, LANES), BF16)
    key_spec = pl.BlockSpec((1, N_KV, tm, LANES), lambda bi, i: (bi, 0, i, 0))
    valt = jax.ShapeDtypeStruct((b, N_KV, s // tk, HD, tk), BF16)
    valt_spec = pl.BlockSpec((1, N_KV, tm // tk, HD, tk), lambda bi, i: (bi, 0, i, 0, 0))
    return pl.pallas_call(
        _nsa_proj_body,
        out_shape=(
            jax.ShapeDtypeStruct((b, nq, s), BF16),
            jax.ShapeDtypeStruct((b, s, KV_W), F32),
            jax.ShapeDtypeStruct((b, s, KV_W), F32),
            key, key, valt, valt,
            jax.ShapeDtypeStruct((b, N_KV, GATE_ROWS, s), F32),
        ),
        grid=(b, s // tm),
        in_specs=[
            tok(D_MODEL),
            _const_spec((1, D_MODEL)),
            _const_spec((nq, D_MODEL)),
            _const_spec((D_MODEL, 2 * KV_W)),
            _const_spec((D_MODEL, 2 * N_KV * LANES)),
            _const_spec((2 * KV_W, D_MODEL)),
            _const_spec((N_KV * GATE_ROWS, D_MODEL)),
        ],
        out_specs=(
            pl.BlockSpec((1, nq, tm), lambda bi, i: (bi, 0, i)),
            tok(KV_W), tok(KV_W), key_spec, key_spec, valt_spec, valt_spec,
            pl.BlockSpec((1, N_KV, GATE_ROWS, tm), lambda bi, i: (bi, 0, 0, i)),
        ),
        compiler_params=_cparams(("parallel", "parallel")),
        name="nsa_proj",
    )(h, g_pre.reshape(1, D_MODEL), w_qt, w_c, w_k, w_vt, w_gt)


def _compress_body(x_ref, pea_ref, peb_ref, wa_ref, wb_ref, w2_ref, o_ref, *, transposed):
    x = x_ref[0]
    ha = _dot((x + pea_ref[...]).astype(BF16), wa_ref[...])
    hb = _dot((x + peb_ref[...]).astype(BF16), wb_ref[...])
    n = ha.shape[0]
    hid = ha + pltpu.roll(hb, n - 1, 0)
    hid = (hid * jax.nn.sigmoid(hid)).astype(BF16)
    if transposed:
        o_ref[0] = _dot_nt(w2_ref[...], hid).reshape(N_KV, HD, n).astype(BF16)
    else:
        out = _dot(hid, w2_ref[...]).astype(BF16)
        for g in range(N_KV):
            o_ref[0, g] = out[:, g * HD:(g + 1) * HD]


def _compress(t, pe, w1, w2, transposed):
    b, s, _ = t.shape
    half = CMP_BLOCK // 2
    rows = s // half
    x = t.reshape(b, rows, half * KV_W)
    eye = jnp.eye(N_KV, dtype=F32)
    wbd = jnp.einsum("ldc,gh->lgdhc", w1, eye).reshape(CMP_BLOCK, KV_W, KV_W)
    wa = wbd[:half].reshape(half * KV_W, KV_W).astype(BF16)
    wb = wbd[half:].reshape(half * KV_W, KV_W).astype(BF16)
    pe_t = jnp.broadcast_to(pe[:, None, :], (CMP_BLOCK, N_KV, HD))
    pea = pe_t[:half].reshape(1, half * KV_W)
    peb = pe_t[half:].reshape(1, half * KV_W)
    w2bd = jnp.einsum("cd,gh->gchd", w2, eye).reshape(KV_W, KV_W)
    if transposed:
        w2bd = w2bd.T
        out_shape, out_block = (b, N_KV, HD, rows), (1, N_KV, HD, rows)
    else:
        out_shape, out_block = (b, N_KV, rows, HD), (1, N_KV, rows, HD)
    return pl.pallas_call(
        functools.partial(_compress_body, transposed=transposed),
        out_shape=jax.ShapeDtypeStruct(out_shape, BF16),
        grid=(b,),
        in_specs=[
            pl.BlockSpec((1, rows, half * KV_W), lambda bi: (bi, 0, 0)),
            _const_spec((1, half * KV_W)),
            _const_spec((1, half * KV_W)),
            _const_spec((half * KV_W, KV_W)),
            _const_spec((half * KV_W, KV_W)),
            _const_spec((KV_W, KV_W)),
        ],
        out_specs=pl.BlockSpec(out_block, lambda bi: (bi, 0, 0, 0)),
        compiler_params=_cparams(("parallel",)),
        name="nsa_compress",
    )(x, pea, peb, wa, wb, w2bd.astype(BF16))


def _alibi_slope(h):
    return float(2.0 ** (-8.0 * (h + 1) / N_HEADS))


def _cmp_scores(kc, qh, slope2, distf, valid):
    s = jnp.where(valid, _dot(kc, qh) - slope2 * distf, MASK_VALUE)
    m = jnp.max(s, axis=0, keepdims=True)
    p = jnp.where(valid, jnp.exp2(s - m), 0.0)
    return p, jnp.sum(p, axis=0, keepdims=True)


def _cmp_geometry(n_cmp_rows, tq, t0):
    n_io = lax.broadcasted_iota(jnp.int32, (n_cmp_rows, tq), 0)
    t_io = t0 + lax.broadcasted_iota(jnp.int32, (n_cmp_rows, tq), 1)
    dist = t_io - (n_io * CMP_STRIDE + (CMP_BLOCK - 1))
    return dist.astype(F32), (dist >= 0) & (n_io < n_cmp_rows - 1)


def _select_body(qt_ref, kc_ref, c2s_ref, mneg_ref, *, n_sel, n_cmp_rows):
    tq = qt_ref.shape[2]
    t0 = pl.program_id(1) * tq
    distf, valid = _cmp_geometry(n_cmp_rows, tq, t0)
    s_io = lax.broadcasted_iota(jnp.int32, (n_sel, tq), 0)
    tt = t0 + lax.broadcasted_iota(jnp.int32, (n_sel, tq), 1)
    cur = lax.shift_right_logical(tt, 6)
    forced = (s_io == 0) | (s_io == cur) | (s_io == cur - 1)
    visible = s_io * SEL_BLOCK <= tt

    for g in range(N_KV):
        kc = kc_ref[0, g]
        imp = jnp.zeros((n_sel, tq), F32)
        for r in range(N_GRP):
            hh = g * N_GRP + r
            p, l = _cmp_scores(kc, qt_ref[0, hh * HD:(hh + 1) * HD, :], _alibi_slope(hh) * LOG2E, distf, valid)
            p = p / jnp.where(l > 0.0, l, 1.0)
            p_hi, p_lo = _split2(p)
            imp = imp + _dot(c2s_ref[...], p_hi) + _dot(c2s_ref[...], p_lo)
        x = jnp.where(forced, FORCED_SCORE, jnp.where(visible, imp, -1.0))
        cnt = jnp.zeros((n_sel, tq), F32)
        for j in range(n_sel):
            xj = x[j:j + 1, :]
            ge = jnp.where(xj >= x, 1.0, 0.0)
            gt = jnp.where(xj > x, 1.0, 0.0)
            cnt = cnt + jnp.where(s_io > j, ge, gt)
        mneg_ref[0, g, 0:n_sel, :] = jnp.where(cnt < float(N_SELECT), 0.0, MASK_VALUE).astype(BF16)
        if n_sel < HD:
            mneg_ref[0, g, n_sel:HD, :] = jnp.zeros((HD - n_sel, tq), BF16)


def _select(qt, k_cmp):
    b, nq, s = qt.shape
    tq = min(SEL_TQ, s)
    n_sel = s // SEL_BLOCK
    rows = k_cmp.shape[2]
    n_cmp = (s - CMP_BLOCK) // CMP_STRIDE + 1
    assert rows == n_cmp + 1
    cs = np.arange(rows) * CMP_STRIDE
    bs = np.arange(n_sel) * SEL_BLOCK
    overlap = np.clip(np.minimum(cs[:, None] + CMP_BLOCK, bs[None, :] + SEL_BLOCK)
                      - np.maximum(cs[:, None], bs[None, :]), 0, None)
    c2s_t = jnp.asarray((overlap / CMP_BLOCK).T, BF16)
    return pl.pallas_call(
        functools.partial(_select_body, n_sel=n_sel, n_cmp_rows=rows),
        out_shape=jax.ShapeDtypeStruct((b, N_KV, HD, s), BF16),
        grid=(b, s // tq),
        in_specs=[
            pl.BlockSpec((1, nq, tq), lambda bi, i: (bi, 0, i)),
            pl.BlockSpec((1, N_KV, rows, HD), lambda bi, i: (bi, 0, 0, 0)),
            _const_spec((n_sel, rows)),
        ],
        out_specs=pl.BlockSpec((1, N_KV, HD, tq), lambda bi, i: (bi, 0, 0, i)),
        compiler_params=_cparams(("parallel", "parallel")),
        name="nsa_select",
    )(qt, k_cmp, c2s_t)


def _attn_body(slope_ref, qt_ref, ks_ref, kw_ref, vst_ref, vwt_ref, kc_ref, vct_ref, mneg_ref, gt_ref,
               o_ref, s_sc, *, n_cmp_rows):
    tq = qt_ref.shape[2]
    tk = tq
    g = pl.program_id(1)
    qi = pl.program_id(2)
    t0 = qi * tq
    slopes = [slope_ref[g * N_GRP + r] for r in range(N_GRP)]
    q_rows = [qt_ref[0, r * HD:(r + 1) * HD, :] for r in range(N_GRP)]
    qk = [jnp.concatenate([q_rows[r], mneg_ref[0, 0]], axis=0) for r in range(N_GRP)]

    key_io = lax.broadcasted_iota(jnp.int32, (tk, tq), 0)
    rel = key_io - lax.broadcasted_iota(jnp.int32, (tk, tq), 1)
    key_f = key_io.astype(F32)
    base = [slopes[r] * key_f for r in range(N_GRP)]

    def chunk(c, carry, k_ref, vt_ref, valid):
        kblk = k_ref[0, 0, pl.ds(pl.multiple_of(c * tk, tk), tk), :]
        vt = vt_ref[0, 0, c]
        cf = ((c - qi) * tk).astype(F32)
        offs = [slopes[r] * cf for r in range(N_GRP)]
        m_news = []
        for r in range(N_GRP):
            s = _dot(kblk, qk[r]) + base[r]
            if valid is not None:
                s = jnp.where(valid, s, MASK_VALUE)
            s_sc[r] = s
            m_news.append(jnp.maximum(carry[r][0], jnp.max(s, axis=0, keepdims=True) + offs[r]))
        out = []
        for r in range(N_GRP):
            m, l, acc = carry[r]
            alpha = jnp.exp2(m - m_news[r])
            p = jnp.exp2(s_sc[r] - (m_news[r] - offs[r]))
            l_new = alpha * l + jnp.sum(p, axis=0, keepdims=True)
            acc_new = alpha * acc + _dot(vt, p.astype(BF16))
            out.append((m_news[r], l_new, acc_new))
        return tuple(out)

    def init():
        return tuple((jnp.full((1, tq), MASK_VALUE, F32), jnp.zeros((1, tq), F32), jnp.zeros((HD, tq), F32))
                     for _ in range(N_GRP))

    causal = rel <= 0
    res_s = lax.fori_loop(0, qi, lambda c, cr: chunk(c, cr, ks_ref, vst_ref, None), init())
    res_s = chunk(qi, res_s, ks_ref, vst_ref, causal)

    def win_step(c, cr):
        return chunk(c, cr, kw_ref, vwt_ref, rel > (qi - c) * tk - WINDOW)

    res_w = lax.fori_loop(jnp.maximum(qi - WINDOW // tk, 0), qi, win_step, init())
    res_w = chunk(qi, res_w, kw_ref, vwt_ref, causal)

    distf, valid_c = _cmp_geometry(n_cmp_rows, tq, t0)
    kc = kc_ref[0, 0]
    vct = vct_ref[0, 0]
    gate = gt_ref[0, 0]
    for r in range(N_GRP):
        p, l = _cmp_scores(kc, q_rows[r], slopes[r], distf, valid_c)
        o_c = _dot(vct, p.astype(BF16)) / jnp.where(l > 0.0, l, 1.0)
        o_s = res_s[r][2] / res_s[r][1]
        o_w = res_w[r][2] / res_w[r][1]
        out = (gate[3 * r:3 * r + 1, :] * o_c + gate[3 * r + 1:3 * r + 2, :] * o_s
               + gate[3 * r + 2:3 * r + 3, :] * o_w)
        o_ref[0, r * HD:(r + 1) * HD, :] = out.astype(BF16)


def _attention(qt, ks, kw, vst, vwt, k_cmp, v_cmpt, mneg, gates_t):
    b, nq, s = qt.shape
    tq = vst.shape[4]
    rows = k_cmp.shape[2]
    gw = N_GRP * HD
    slopes = jnp.asarray([_alibi_slope(h) * LOG2E for h in range(N_HEADS)], F32)
    keys = pl.BlockSpec((1, 1, s, LANES), lambda bi, g, i: (bi, g, 0, 0))
    vals = pl.BlockSpec((1, 1, s // tq, HD, tq), lambda bi, g, i: (bi, g, 0, 0, 0))
    return pl.pallas_call(
        functools.partial(_attn_body, n_cmp_rows=rows),
        out_shape=jax.ShapeDtypeStruct((b, nq, s), BF16),
        grid=(b, N_KV, s // tq),
        in_specs=[
            pl.BlockSpec(memory_space=pltpu.SMEM),
            pl.BlockSpec((1, gw, tq), lambda bi, g, i: (bi, g, i)),
            keys, keys, vals, vals,
            pl.BlockSpec((1, 1, rows, HD), lambda bi, g, i: (bi, g, 0, 0)),
            pl.BlockSpec((1, 1, HD, rows), lambda bi, g, i: (bi, g, 0, 0)),
            pl.BlockSpec((1, 1, HD, tq), lambda bi, g, i: (bi, g, 0, i)),
            pl.BlockSpec((1, 1, GATE_ROWS, tq), lambda bi, g, i: (bi, g, 0, i)),
        ],
        out_specs=pl.BlockSpec((1, gw, tq), lambda bi, g, i: (bi, g, i)),
        scratch_shapes=[pltpu.VMEM((N_GRP, tq, tq), F32)],
        compiler_params=_cparams(("parallel", "parallel", "arbitrary")),
        name="nsa_attention",
    )(slopes, qt, ks, kw, vst, vwt, k_cmp, v_cmpt, mneg, gates_t)


def _outproj_t_body(yt_ref, h_ref, w_ref, g_ref, o_ref):
    o_ref[0] = h_ref[0] + _rms(_dot_tn(yt_ref[0], w_ref[...]), g_ref[...])


def _outproj_t(yt, h, w_out, g_post):
    b, s, d = h.shape
    tm = min(PROJ_TM, s)
    return pl.pallas_call(
        _outproj_t_body,
        out_shape=jax.ShapeDtypeStruct((b, s, d), F32),
        grid=(b, s // tm),
        in_specs=[
            pl.BlockSpec((1, d, tm), lambda bi, i: (bi, 0, i)),
            pl.BlockSpec((1, tm, d), lambda bi, i: (bi, i, 0)),
            _const_spec((d, d)),
            _const_spec((1, d)),
        ],
        out_specs=pl.BlockSpec((1, tm, d), lambda bi, i: (bi, i, 0)),
        compiler_params=_cparams(("parallel", "parallel")),
        name="outproj_t",
    )(yt, h, w_out.astype(BF16), g_post.reshape(1, d))


def _nsa_mixer(h, g_pre, w_in, pe_k, w_ck1, w_ck2, pe_v, w_cv1, w_cv2, w_out, g_post):
    qt, kc, vc, ks, kw, vst, vwt, gates_t = _nsa_proj(h, g_pre, w_in)
    k_cmp = _compress(kc, pe_k, w_ck1, w_ck2, transposed=False)
    v_cmpt = _compress(vc, pe_v, w_cv1, w_cv2, transposed=True)
    mneg = _select(qt, k_cmp)
    out_t = _attention(qt, ks, kw, vst, vwt, k_cmp, v_cmpt, mneg, gates_t)
    return _outproj_t(out_t, h, w_out, g_post)


def _head_sum(x, sel_ref):
    hi, lo = _split2(x)
    return _dot(hi, sel_ref[...]) + _dot(lo, sel_ref[...])


def _head_expand(x, selt_ref):
    hi, lo = _split2(x)
    return _dot(hi, selt_ref[...]) + _dot(lo, selt_ref[...])


def _softplus(x):
    return jnp.maximum(x, 0.0) + jnp.log(1.0 + jnp.exp(-jnp.abs(x)))


def _rw_proj_body(h_ref, hp_ref, g_ref, mu_ref, vec_ref, wr_ref, wk_ref, wv_ref, wd_ref, wa_ref, wg_ref,
                  w2d_ref, w2a_ref, w2g_ref, sel_ref, selt_ref,
                  r_ref, lw_ref, k_ref, v_ref, a_ref, b_ref, gg_ref):
    tm = h_ref.shape[1]
    u = _rms(h_ref[0], g_ref[...])
    prev = _rms(hp_ref[0], g_ref[...])[7:8, :]
    prev = jnp.where(pl.program_id(1) == 0, 0.0, prev)
    row = lax.broadcasted_iota(jnp.int32, (tm, D_MODEL), 0)
    u_prev = jnp.where(row == 0, prev, pltpu.roll(u, 1, 0))
    xx = u_prev - u

    def mix(i):
        return (u + xx * mu_ref[i:i + 1, :]).astype(BF16)

    w0, a0, k_k, k_a = (vec_ref[i:i + 1, :] for i in range(4))
    r = _dot(mix(0), wr_ref[...])
    k = _dot(mix(1), wk_ref[...])
    v = _dot(mix(2), wv_ref[...])
    d1 = jnp.tanh(_dot(mix(3), wd_ref[...])).astype(BF16)
    a1 = _dot(mix(4), wa_ref[...]).astype(BF16)
    g1 = jax.nn.sigmoid(_dot(mix(5), wg_ref[...])).astype(BF16)
    w = -_softplus(-(w0 + _dot(d1, w2d_ref[...]))) - 0.5
    alpha = jax.nn.sigmoid(a0 + _dot(a1, w2a_ref[...]))
    kk = k * k_k
    norm = jnp.sqrt(_head_sum(kk * kk, sel_ref))
    kk = kk * _head_expand(1.0 / jnp.maximum(norm, 1e-12), selt_ref)
    r_ref[0] = r
    lw_ref[0] = -jnp.exp(w)
    k_ref[0] = k * (1.0 + (alpha - 1.0) * k_a)
    v_ref[0] = v
    a_ref[0] = -kk
    b_ref[0] = kk * alpha
    gg_ref[0] = _dot(g1, w2g_ref[...])


def _head_selectors():
    lane_head = np.arange(D_MODEL) // HD
    sel = (lane_head[:, None] == np.arange(LANES)[None, :]).astype(np.float32)
    return jnp.asarray(sel, BF16), jnp.asarray(sel.T, BF16)


def _pad_cols(w, n):
    return jnp.pad(w, ((0, 0), (0, n - w.shape[1])))


def _pad_rows(w, n):
    return jnp.pad(w, ((0, n - w.shape[0]), (0, 0)))


def _rw_proj(h, g_pre, mu, w_in, w0, w_w2, a0, w_a2, w_g2, k_k, k_a):
    b, s, d = h.shape
    tm = min(PROJ_TM, s)
    offs = np.cumsum((0, d, d, d, DECAY_LORA, AAA_LORA, GATE_LORA))
    cols = [w_in[:, int(offs[i]):int(offs[i + 1])] for i in range(6)]
    ld, lg = LANES, 2 * LANES
    wr, wk, wv = (c.astype(BF16) for c in cols[:3])
    wd = _pad_cols(cols[3], ld).astype(BF16)
    wa = _pad_cols(cols[4], ld).astype(BF16)
    wg = _pad_cols(cols[5], lg).astype(BF16)
    w2d = _pad_rows(w_w2, ld).astype(BF16)
    w2a = _pad_rows(w_a2, ld).astype(BF16)
    w2g = _pad_rows(w_g2, lg).astype(BF16)
    mu8 = _pad_rows(mu, 8)
    vecs = _pad_rows(jnp.stack([w0, a0, k_k, k_a]), 8)
    sel, selt = _head_selectors()
    tok = pl.BlockSpec((1, tm, d), lambda bi, i: (bi, i, 0))
    out = jax.ShapeDtypeStruct((b, s, d), F32)
    return pl.pallas_call(
        _rw_proj_body,
        out_shape=(out,) * 7,
        grid=(b, s // tm),
        in_specs=[
            tok,
            pl.BlockSpec((1, 8, d), lambda bi, i: (bi, jnp.maximum(i * (tm // 8) - 1, 0), 0)),
            _const_spec((1, d)), _const_spec((8, d)), _const_spec((8, d)),
            _const_spec((d, d)), _const_spec((d, d)), _const_spec((d, d)),
            _const_spec((d, ld)), _const_spec((d, ld)), _const_spec((d, lg)),
            _const_spec((ld, d)), _const_spec((ld, d)), _const_spec((lg, d)),
            _const_spec((d, LANES)), _const_spec((LANES, d)),
        ],
        out_specs=(tok,) * 7,
        compiler_params=_cparams(("parallel", "parallel")),
        name="rwkv_proj",
    )(h, h, g_pre.reshape(1, d), mu8, vecs, wr, wk, wv, wd, wa, wg, w2d, w2a, w2g, sel, selt)


RW_HIGH_PRECISION_INVERSE = False


def _mm(a, b, nt=False, high=False):
    f = _dot_nt if nt else _dot
    if not high:
        return f(a.astype(BF16), b.astype(BF16))
    a_hi, a_lo = _split2(a)
    b_hi, b_lo = _split2(b)
    return f(a_hi, b_hi) + f(a_hi, b_lo) + f(a_lo, b_hi)


def _rw_prep_body(ltri_ref, r_ref, lw_ref, k_ref, v_ref, a_ref, b_ref,
                  rhat_ref, y1_ref, g_ref, n_ref):
    c = RW_C
    q = RW_Q
    nh = q // HD
    lane_head = lax.shift_right_logical(lax.broadcasted_iota(jnp.int32, (c, q), 1), 6)
    ri = lax.broadcasted_iota(jnp.int32, (q, q), 0)
    ci = lax.broadcasted_iota(jnp.int32, (q, q), 1)
    same_head = lax.shift_right_logical(ri, 6) == lax.shift_right_logical(ci, 6)
    strict_bd = same_head & ((ci & (c - 1)) < (ri & (c - 1)))
    eye = ri == ci
    t_io = lax.broadcasted_iota(jnp.int32, (c, q), 0)
    j_io = lax.broadcasted_iota(jnp.int32, (c, q), 1) & (c - 1)
    strict_ls = j_io < t_io
    incl_ls = j_io <= t_io
    high = RW_HIGH_PRECISION_INVERSE

    def expand4(x):
        return jnp.concatenate([jnp.where(lane_head == hh, x, 0.0) for hh in range(nh)], axis=0)

    def collapse4(x):
        out = x[0:c]
        for hh in range(1, nh):
            out = out + x[hh * c:(hh + 1) * c]
        return out

    for ch in range(r_ref.shape[1] // c):
        sl = slice(ch * c, (ch + 1) * c)
        lw = lw_ref[0, sl, :]
        cum = None
        for part in _split3(lw):
            t = _dot(ltri_ref[...], part)
            cum = t if cum is None else cum + t
        cum_c = cum[c - 1:c, :]
        a = a_ref[0, sl, :]
        b = b_ref[0, sl, :]
        k = k_ref[0, sl, :]
        v = v_ref[0, sl, :]
        e_inv = jnp.exp(-cum)
        e_rem = jnp.exp(cum_c - cum)
        at = a * jnp.exp(cum - lw)
        rt = r_ref[0, sl, :] * jnp.exp(cum)
        bt = b * e_inv
        kt = k * e_inv
        bbar = b * e_rem
        kbar = k * e_rem

        x4 = expand4(at)
        b4 = expand4(bt)
        k4 = expand4(kt)
        v4 = expand4(v).astype(BF16)

        lbd = jnp.where(strict_bd, _mm(x4, b4, nt=True, high=high), 0.0)
        tbd = jnp.where(eye, 1.0, lbd)
        p = _mm(lbd, lbd, high=high)
        n_lvl = int(np.log2(c)) - 1
        for lvl in range(n_lvl):
            tbd = tbd + _mm(p, tbd, high=high)
            if lvl + 1 < n_lvl:
                p = _mm(p, p, high=high)
        t_ls = collapse4(tbd).astype(BF16)

        ar = _dot_nt(jnp.concatenate([at, rt], axis=0).astype(BF16), k4.astype(BF16))
        a_ak = jnp.where(strict_ls, ar[0:c], 0.0).astype(BF16)
        a_rk = jnp.where(incl_ls, ar[c:2 * c], 0.0).astype(BF16)
        a_rb = jnp.where(incl_ls, _dot_nt(rt.astype(BF16), b4.astype(BF16)), 0.0).astype(BF16)

        g1 = _dot(a_ak, v4)
        y0 = _dot(a_rk, v4)
        u0 = _dot(t_ls, expand4(g1).astype(BF16))
        ahat = _dot(t_ls, x4.astype(BF16))
        rhat = rt + _dot(a_rb, expand4(ahat).astype(BF16))
        y1 = y0 + _dot(a_rb, expand4(u0).astype(BF16))

        gm = _dot_tn(bbar.astype(BF16), ahat.astype(BF16))
        gm = jnp.where(same_head, gm, 0.0) + jnp.where(eye, jnp.exp(cum_c), 0.0)
        nm = _dot_tn(jnp.concatenate([bbar, kbar], axis=0).astype(BF16),
                     jnp.concatenate([u0, v], axis=0).astype(BF16))
        nm = jnp.where(same_head, nm, 0.0)

        rhat_ref[0, sl, :] = rhat.astype(BF16)
        y1_ref[0, sl, :] = y1
        g_ref[0, sl, :] = collapse4(gm).astype(BF16)
        n_ref[0, sl, :] = collapse4(nm)


def _rw_prep(r, lw, k, v, a, b):
    bsz, s, d = r.shape
    rows = min(RW_C * RW_NCH, s)
    ltri = jnp.asarray(np.tril(np.ones((RW_C, RW_C), np.float32)), BF16)
    blk = pl.BlockSpec((1, rows, RW_Q), lambda bi, qi, j: (bi, j, qi))
    return pl.pallas_call(
        _rw_prep_body,
        out_shape=(
            jax.ShapeDtypeStruct((bsz, s, d), BF16),
            jax.ShapeDtypeStruct((bsz, s, d), F32),
            jax.ShapeDtypeStruct((bsz, s, d), BF16),
            jax.ShapeDtypeStruct((bsz, s, d), F32),
        ),
        grid=(bsz, d // RW_Q, s // rows),
        in_specs=[_const_spec((RW_C, RW_C))] + [blk] * 6,
        out_specs=(blk,) * 4,
        compiler_params=_cparams(("parallel", "parallel", "parallel")),
        name="rwkv_prep",
    )(ltri, r, lw, k, v, a, b)


def _rw_scan_body(rhat_ref, y1_ref, g_ref, n_ref, y_ref, h_sc):
    c = RW_C
    q = RW_Q
    nq = h_sc.shape[0]

    @pl.when(pl.program_id(1) == 0)
    def _():
        h_sc[...] = jnp.zeros(h_sc.shape, F32)

    ri = lax.broadcasted_iota(jnp.int32, (q, q), 0)
    ci = lax.broadcasted_iota(jnp.int32, (q, q), 1)
    same_head = lax.shift_right_logical(ri, 6) == lax.shift_right_logical(ci, 6)

    def step(ch, carry):
        rows = pl.ds(pl.multiple_of(ch * c, c), c)
        for qi in range(nq):
            lanes = slice(qi * q, (qi + 1) * q)
            hb = h_sc[qi].astype(BF16)
            y_ref[0, rows, lanes] = _dot(rhat_ref[0, rows, lanes], hb) + y1_ref[0, rows, lanes]
            g_ls = g_ref[0, rows, lanes]
            n_ls = n_ref[0, rows, lanes]
            gbd = jnp.where(same_head, jnp.concatenate([g_ls] * (q // c), axis=0), 0.0)
            nbd = jnp.where(same_head, jnp.concatenate([n_ls] * (q // c), axis=0), 0.0)
            h_sc[qi] = _dot(gbd.astype(BF16), hb) + nbd
        return carry

    lax.fori_loop(0, rhat_ref.shape[1] // c, step, 0)


def _rw_scan(rhat, y1, g, n):
    bsz, s, d = rhat.shape
    rows = min(512, s)
    blk = pl.BlockSpec((1, rows, d), lambda bi, j: (bi, j, 0))
    return pl.pallas_call(
        _rw_scan_body,
        out_shape=jax.ShapeDtypeStruct((bsz, s, d), F32),
        grid=(bsz, s // rows),
        in_specs=[blk] * 4,
        out_specs=blk,
        scratch_shapes=[pltpu.VMEM((d // RW_Q, RW_Q, RW_Q), F32)],
        compiler_params=_cparams(("parallel", "arbitrary")),
        name="rwkv_scan",
    )(rhat, y1, g, n)


def _rw_post_body(y_ref, r_ref, k_ref, v_ref, gg_ref, h_ref, vec_ref, w_ref, gpost_ref, sel_ref, selt_ref, o_ref):
    gn_w, gn_b, r_k = (vec_ref[i:i + 1, :] for i in range(3))
    y = y_ref[...]
    inv_n = 1.0 / HD
    mean = _head_expand(_head_sum(y, sel_ref) * inv_n, selt_ref)
    yc = y - mean
    var = _head_sum(yc * yc, sel_ref) * inv_n
    yn = yc * _head_expand(lax.rsqrt(var + GN_EPS), selt_ref) * gn_w + gn_b
    v = v_ref[...]
    bonus = _head_expand(_head_sum(r_ref[...] * k_ref[...] * r_k, sel_ref), selt_ref) * v
    z = ((yn + bonus) * gg_ref[...]).astype(BF16)
    o_ref[...] = h_ref[...] + _rms(_dot(z, w_ref[...]), gpost_ref[...])


def _rw_post(y, r, k, v, gg, h, gn_w, gn_b, r_k, w_out, g_post):
    t, d = h.shape
    tm = min(PROJ_TM, t)
    vecs = _pad_rows(jnp.stack([gn_w, gn_b, r_k.reshape(d)]), 8)
    sel, selt = _head_selectors()
    tok = pl.BlockSpec((tm, d), lambda i: (i, 0))
    return pl.pallas_call(
        _rw_post_body,
        out_shape=jax.ShapeDtypeStruct((t, d), F32),
        grid=(t // tm,),
        in_specs=[tok] * 6 + [_const_spec((8, d)), _const_spec((d, d)), _const_spec((1, d)),
                              _const_spec((d, LANES)), _const_spec((LANES, d))],
        out_specs=tok,
        compiler_params=_cparams(("parallel",)),
        name="rwkv_post",
    )(y, r, k, v, gg, h, vecs, w_out.astype(BF16), g_post.reshape(1, d), sel, selt)


def _rwkv_mixer(h, g_pre, mu, w_in, w0, w_w2, a0, w_a2, w_g2, k_k, k_a, r_k, gn_w, gn_b, w_out, g_post):
    b, s, d = h.shape
    r, lw, k, v, a, bb, gg = _rw_proj(h, g_pre, mu, w_in, w0, w_w2, a0, w_a2, w_g2, k_k, k_a)
    rhat, y1, g, n = _rw_prep(r, lw, k, v, a, bb)
    y = _rw_scan(rhat, y1, g, n)
    f2 = lambda x: x.reshape(b * s, d)
    return _rw_post(f2(y), f2(r), f2(k), f2(v), f2(gg), f2(h), gn_w, gn_b, r_k, w_out, g_post).reshape(b, s, d)


def kernel(x, ffn1_norm_pre, ffn1_w_gu, ffn1_w_down, ffn1_norm_post, mix_norm_pre, nsa_w_in, nsa_pe_k,
           nsa_w_ck1, nsa_w_ck2, nsa_pe_v, nsa_w_cv1, nsa_w_cv2, nsa_w_out, rwkv_mu, rwkv_w_in, rwkv_w0,
           rwkv_w_w2, rwkv_a0, rwkv_w_a2, rwkv_w_g2, rwkv_k_k, rwkv_k_a, rwkv_r_k, rwkv_gn_w, rwkv_gn_b,
           rwkv_w_out, mix_norm_post, ffn2_norm_pre, ffn2_w_gu, ffn2_w_down, ffn2_norm_post):
    b, s, d = x.shape
    flat = lambda t: t.reshape(b * s, d)
    cube = lambda t: t.reshape(b, s, d)
    h = x
    depth = ffn1_norm_pre.shape[0]
    for i in range(depth):
        h = cube(_ffn(flat(h), ffn1_norm_pre[i], ffn1_w_gu[i], ffn1_w_down[i], ffn1_norm_post[i]))
        j = i // 2
        if i % 2 == 0:
            h = _nsa_mixer(h, mix_norm_pre[i], nsa_w_in[j], nsa_pe_k[j], nsa_w_ck1[j], nsa_w_ck2[j],
                           nsa_pe_v[j], nsa_w_cv1[j], nsa_w_cv2[j], nsa_w_out[j], mix_norm_post[i])
        else:
            h = _rwkv_mixer(h, mix_norm_pre[i], rwkv_mu[j], rwkv_w_in[j], rwkv_w0[j], rwkv_w_w2[j],
                            rwkv_a0[j], rwkv_w_a2[j], rwkv_w_g2[j], rwkv_k_k[j], rwkv_k_a[j], rwkv_r_k[j],
                            rwkv_gn_w[j], rwkv_gn_b[j], rwkv_w_out[j], mix_norm_post[i])
        h = cube(_ffn(flat(h), ffn2_norm_pre[i], ffn2_w_gu[i], ffn2_w_down[i], ffn2_norm_post[i]))
    return h
```

```python
import functools

import numpy as np
import jax
import jax.numpy as jnp
from jax import lax
from jax.experimental import pallas as pl
from jax.experimental.pallas import tpu as pltpu

F32 = jnp.float32
BF16 = jnp.bfloat16

D_MODEL = 1024
D_FF = 2816
HALF_STEP = 0.5
RMS_EPS = 1e-6
MASK_VALUE = -1e30

HD = 64
N_HEADS = 16
N_KV = 4
N_GRP = 4
KV_W = N_KV * HD
CMP_BLOCK = 32
CMP_STRIDE = 16
SEL_BLOCK = 64
N_SELECT = 16
WINDOW = 512
FORCED_SCORE = 1e4
N_GATE = 3 * N_HEADS

DECAY_LORA = 64
AAA_LORA = 64
GATE_LORA = 160
GN_EPS = 64e-5

LANES = 128
VMEM_LIMIT_BYTES = 56 * 1024 * 1024

FFN_TM = 512
FFN_CK = 1408
PROJ_TM = 512
ATT_T = 256
SEL_TQ = 256
GATE_ROWS = 16
SUM_ROWS = 16
LOG2E = 1.4426950408889634
RW_C = 64
RW_Q = 128
RW_NCH = 8


def _cparams(sem):
    return pltpu.CompilerParams(dimension_semantics=sem, vmem_limit_bytes=VMEM_LIMIT_BYTES)


def _rms(x, g):
    ms = jnp.mean(x * x, axis=-1, keepdims=True)
    return x * lax.rsqrt(ms + RMS_EPS) * g


def _const_spec(shape):
    nd = len(shape)
    return pl.BlockSpec(shape, lambda *_: (0,) * nd, pipeline_mode=pl.Buffered(1))


def _dot(a, b):
    return jnp.dot(a, b, preferred_element_type=F32)


def _dot_nt(a, b):
    return lax.dot_general(a, b, (((1,), (1,)), ((), ())), preferred_element_type=F32)


def _dot_tn(a, b):
    return lax.dot_general(a, b, (((0,), (0,)), ((), ())), preferred_element_type=F32)


def _split2(x):
    hi = x.astype(BF16)
    lo = (x - hi.astype(F32)).astype(BF16)
    return hi, lo


def _split3(x):
    hi = x.astype(BF16)
    r1 = x - hi.astype(F32)
    mid = r1.astype(BF16)
    lo = (r1 - mid.astype(F32)).astype(BF16)
    return hi, mid, lo


def _ffn_body(x_ref, gpre_ref, wgu_ref, wd_ref, gpost_ref, o_ref):
    x = x_ref[...]
    xn = _rms(x, gpre_ref[...]).astype(BF16)
    acc = None
    for c in range(D_FF // FFN_CK):
        lo = c * FFN_CK
        gate = _dot(xn, wgu_ref[:, lo:lo + FFN_CK])
        up = _dot(xn, wgu_ref[:, D_FF + lo:D_FF + lo + FFN_CK])
        act = (gate * jax.nn.sigmoid(gate) * up).astype(BF16)
        part = _dot(act, wd_ref[lo:lo + FFN_CK, :])
        acc = part if acc is None else acc + part
    o_ref[...] = x + HALF_STEP * _rms(acc, gpost_ref[...])


def _ffn(h2, g_pre, w_gu, w_down, g_post):
    t = h2.shape[0]
    tm = min(FFN_TM, t)
    return pl.pallas_call(
        _ffn_body,
        out_shape=jax.ShapeDtypeStruct((t, D_MODEL), F32),
        grid=(t // tm,),
        in_specs=[
            pl.BlockSpec((tm, D_MODEL), lambda i: (i, 0)),
            _const_spec((1, D_MODEL)),
            _const_spec((D_MODEL, 2 * D_FF)),
            _const_spec((D_FF, D_MODEL)),
            _const_spec((1, D_MODEL)),
        ],
        out_specs=pl.BlockSpec((tm, D_MODEL), lambda i: (i, 0)),
        compiler_params=_cparams(("parallel",)),
        name="ffn",
    )(h2, g_pre.reshape(1, D_MODEL), w_gu.astype(BF16), w_down.astype(BF16), g_post.reshape(1, D_MODEL))


def _nsa_proj_body(h_ref, g_ref, wqt_ref, wc_ref, wk_ref, wvt_ref, wgt_ref,
                   qt_ref, kc_ref, vc_ref, ks_ref, kw_ref, vst_ref, vwt_ref, gt_ref):
    tm = h_ref.shape[1]
    u = _rms(h_ref[0], g_ref[...]).astype(BF16)
    qt_ref[0] = (_dot_nt(wqt_ref[...], u) * (HD ** -0.5 * LOG2E)).astype(BF16)
    c = _dot(u, wc_ref[...])
    kc_ref[0] = c[:, :KV_W]
    vc_ref[0] = c[:, KV_W:]
    kk = _dot(u, wk_ref[...])
    t0 = pl.program_id(1) * tm
    lane = lax.broadcasted_iota(jnp.int32, (tm, LANES), 1)
    blk = lax.shift_right_logical(t0 + lax.broadcasted_iota(jnp.int32, (tm, LANES), 0), 6)
    onehot = jnp.where(lane - HD == blk, 1.0, 0.0)
    for g in range(N_KV):
        ks_ref[0, g] = (kk[:, g * LANES:(g + 1) * LANES] + onehot).astype(BF16)
        kw_ref[0, g] = kk[:, (N_KV + g) * LANES:(N_KV + g + 1) * LANES].astype(BF16)
    vt = _dot_nt(wvt_ref[...], u)
    tk = vst_ref.shape[4]
    for j in range(tm // tk):
        cols = slice(j * tk, (j + 1) * tk)
        vst_ref[0, :, j] = vt[:KV_W, cols].reshape(N_KV, HD, tk).astype(BF16)
        vwt_ref[0, :, j] = vt[KV_W:, cols].reshape(N_KV, HD, tk).astype(BF16)
    gt_ref[0] = jax.nn.sigmoid(_dot_nt(wgt_ref[...], u)).reshape(N_KV, GATE_ROWS, tm)


def _nsa_proj(h, g_pre, w_in):
    b, s, _ = h.shape
    tm = min(PROJ_TM, s)
    tk = min(ATT_T, s)
    nq = N_HEADS * HD
    assert s // SEL_BLOCK <= HD
    col = lambda i: w_in[:, nq + i * KV_W:nq + (i + 1) * KV_W]
    w_qt = w_in[:, :nq].T.astype(BF16)
    w_c = jnp.concatenate([col(0), col(1)], axis=1).astype(BF16)
    pad_heads = lambda w: jnp.pad(w.reshape(D_MODEL, N_KV, HD), ((0, 0), (0, 0), (0, LANES - HD))).reshape(D_MODEL, N_KV * LANES)
    w_k = jnp.concatenate([pad_heads(col(2)), pad_heads(col(4))], axis=1).astype(BF16)
    w_vt = jnp.concatenate([col(3), col(5)], axis=1).T.astype(BF16)
    w_gl = w_in[:, nq + 6 * KV_W:].reshape(D_MODEL, N_KV, N_GRP * 3)
    w_gt = jnp.pad(w_gl, ((0, 0), (0, 0), (0, GATE_ROWS - N_GRP * 3))).reshape(D_MODEL, N_KV * GATE_ROWS).T.astype(BF16)
    tok = lambda w: pl.BlockSpec((1, tm, w), lambda bi, i: (bi, i, 0))
    key = jax.ShapeDtypeStruct((b, N_KV, s, LANES), BF16)
    key_spec = pl.BlockSpec((1, N_KV, tm, LANES), lambda bi, i: (bi, 0, i, 0))
    valt = jax.ShapeDtypeStruct((b, N_KV, s // tk, HD, tk), BF16)
    valt_spec = pl.BlockSpec((1, N_KV, tm // tk, HD, tk), lambda bi, i: (bi, 0, i, 0, 0))
    return pl.pallas_call(
        _nsa_proj_body,
        out_shape=(
            jax.ShapeDtypeStruct((b, nq, s), BF16),
            jax.ShapeDtypeStruct((b, s, KV_W), F32),
            jax.ShapeDtypeStruct((b, s, KV_W), F32),
            key, key, valt, valt,
            jax.ShapeDtypeStruct((b, N_KV, GATE_ROWS, s), F32),
        ),
        grid=(b, s // tm),
        in_specs=[
            tok(D_MODEL),
            _const_spec((1, D_MODEL)),
            _const_spec((nq, D_MODEL)),
            _const_spec((D_MODEL, 2 * KV_W)),
            _const_spec((D_MODEL, 2 * N_KV * LANES)),
            _const_spec((2 * KV_W, D_MODEL)),
            _const_spec((N_KV * GATE_ROWS, D_MODEL)),
        ],
        out_specs=(
            pl.BlockSpec((1, nq, tm), lambda bi, i: (bi, 0, i)),
            tok(KV_W), tok(KV_W), key_spec, key_spec, valt_spec, valt_spec,
            pl.BlockSpec((1, N_KV, GATE_ROWS, tm), lambda bi, i: (bi, 0, 0, i)),
        ),
        compiler_params=_cparams(("parallel", "parallel")),
        name="nsa_proj",
    )(h, g_pre.reshape(1, D_MODEL), w_qt, w_c, w_k, w_vt, w_gt)


def _compress_body(x_ref, pea_ref, peb_ref, wa_ref, wb_ref, w2_ref, o_ref, *, transposed):
    x = x_ref[0]
    ha = _dot((x + pea_ref[...]).astype(BF16), wa_ref[...])
    hb = _dot((x + peb_ref[...]).astype(BF16), wb_ref[...])
    n = ha.shape[0]
    hid = ha + pltpu.roll(hb, n - 1, 0)
    hid = (hid * jax.nn.sigmoid(hid)).astype(BF16)
    if transposed:
        o_ref[0] = _dot_nt(w2_ref[...], hid).reshape(N_KV, HD, n).astype(BF16)
    else:
        out = _dot(hid, w2_ref[...]).astype(BF16)
        for g in range(N_KV):
            o_ref[0, g] = out[:, g * HD:(g + 1) * HD]


def _compress(t, pe, w1, w2, transposed):
    b, s, _ = t.shape
    half = CMP_BLOCK // 2
    rows = s // half
    x = t.reshape(b, rows, half * KV_W)
    eye = jnp.eye(N_KV, dtype=F32)
    wbd = jnp.einsum("ldc,gh->lgdhc", w1, eye).reshape(CMP_BLOCK, KV_W, KV_W)
    wa = wbd[:half].reshape(half * KV_W, KV_W).astype(BF16)
    wb = wbd[half:].reshape(half * KV_W, KV_W).astype(BF16)
    pe_t = jnp.broadcast_to(pe[:, None, :], (CMP_BLOCK, N_KV, HD))
    pea = pe_t[:half].reshape(1, half * KV_W)
    peb = pe_t[half:].reshape(1, half * KV_W)
    w2bd = jnp.einsum("cd,gh->gchd", w2, eye).reshape(KV_W, KV_W)
    if transposed:
        w2bd = w2bd.T
        out_shape, out_block = (b, N_KV, HD, rows), (1, N_KV, HD, rows)
    else:
        out_shape, out_block = (b, N_KV, rows, HD), (1, N_KV, rows, HD)
    return pl.pallas_call(
        functools.partial(_compress_body, transposed=transposed),
        out_shape=jax.ShapeDtypeStruct(out_shape, BF16),
        grid=(b,),
        in_specs=[
            pl.BlockSpec((1, rows, half * KV_W), lambda bi: (bi, 0, 0)),
            _const_spec((1, half * KV_W)),
            _const_spec((1, half * KV_W)),
            _const_spec((half * KV_W, KV_W)),
            _const_spec((half * KV_W, KV_W)),
            _const_spec((KV_W, KV_W)),
        ],
        out_specs=pl.BlockSpec(out_block, lambda bi: (bi, 0, 0, 0)),
        compiler_params=_cparams(("parallel",)),
        name="nsa_compress",
    )(x, pea, peb, wa, wb, w2bd.astype(BF16))


def _alibi_slope(h):
    return float(2.0 ** (-8.0 * (h + 1) / N_HEADS))


def _cmp_scores(kc, qh, slope2, distf, valid):
    s = jnp.where(valid, _dot(kc, qh) - slope2 * distf, MASK_VALUE)
    m = jnp.max(s, axis=0, keepdims=True)
    p = jnp.where(valid, jnp.exp2(s - m), 0.0)
    return p, jnp.sum(p, axis=0, keepdims=True)


def _cmp_geometry(n_cmp_rows, tq, t0):
    n_io = lax.broadcasted_iota(jnp.int32, (n_cmp_rows, tq), 0)
    t_io = t0 + lax.broadcasted_iota(jnp.int32, (n_cmp_rows, tq), 1)
    dist = t_io - (n_io * CMP_STRIDE + (CMP_BLOCK - 1))
    return dist.astype(F32), (dist >= 0) & (n_io < n_cmp_rows - 1)


def _select_body(qt_ref, kc_ref, c2s_ref, mneg_ref, *, n_sel, n_cmp_rows):
    tq = qt_ref.shape[2]
    t0 = pl.program_id(1) * tq
    distf, valid = _cmp_geometry(n_cmp_rows, tq, t0)
    s_io = lax.broadcasted_iota(jnp.int32, (n_sel, tq), 0)
    tt = t0 + lax.broadcasted_iota(jnp.int32, (n_sel, tq), 1)
    cur = lax.shift_right_logical(tt, 6)
    forced = (s_io == 0) | (s_io == cur) | (s_io == cur - 1)
    visible = s_io * SEL_BLOCK <= tt

    for g in range(N_KV):
        kc = kc_ref[0, g]
        imp = jnp.zeros((n_sel, tq), F32)
        for r in range(N_GRP):
            hh = g * N_GRP + r
            p, l = _cmp_scores(kc, qt_ref[0, hh * HD:(hh + 1) * HD, :], _alibi_slope(hh) * LOG2E, distf, valid)
            p = p / jnp.where(l > 0.0, l, 1.0)
            p_hi, p_lo = _split2(p)
            imp = imp + _dot(c2s_ref[...], p_hi) + _dot(c2s_ref[...], p_lo)
        x = jnp.where(forced, FORCED_SCORE, jnp.where(visible, imp, -1.0))
        cnt = jnp.zeros((n_sel, tq), F32)
        for j in range(n_sel):
            xj = x[j:j + 1, :]
            ge = jnp.where(xj >= x, 1.0, 0.0)
            gt = jnp.where(xj > x, 1.0, 0.0)
            cnt = cnt + jnp.where(s_io > j, ge, gt)
        mneg_ref[0, g, 0:n_sel, :] = jnp.where(cnt < float(N_SELECT), 0.0, MASK_VALUE).astype(BF16)
        if n_sel < HD:
            mneg_ref[0, g, n_sel:HD, :] = jnp.zeros((HD - n_sel, tq), BF16)


def _select(qt, k_cmp):
    b, nq, s = qt.shape
    tq = min(SEL_TQ, s)
    n_sel = s // SEL_BLOCK
    rows = k_cmp.shape[2]
    n_cmp = (s - CMP_BLOCK) // CMP_STRIDE + 1
    assert rows == n_cmp + 1
    cs = np.arange(rows) * CMP_STRIDE
    bs = np.arange(n_sel) * SEL_BLOCK
    overlap = np.clip(np.minimum(cs[:, None] + CMP_BLOCK, bs[None, :] + SEL_BLOCK)
                      - np.maximum(cs[:, None], bs[None, :]), 0, None)
    c2s_t = jnp.asarray((overlap / CMP_BLOCK).T, BF16)
    return pl.pallas_call(
        functools.partial(_select_body, n_sel=n_sel, n_cmp_rows=rows),
        out_shape=jax.ShapeDtypeStruct((b, N_KV, HD, s), BF16),
        grid=(b, s // tq),
        in_specs=[
            pl.BlockSpec((1, nq, tq), lambda bi, i: (bi, 0, i)),
            pl.BlockSpec((1, N_KV, rows, HD), lambda bi, i: (bi, 0, 0, 0)),
            _const_spec((n_sel, rows)),
        ],
        out_specs=pl.BlockSpec((1, N_KV, HD, tq), lambda bi, i: (bi, 0, 0, i)),
        compiler_params=_cparams(("parallel", "parallel")),
        name="nsa_select",
    )(qt, k_cmp, c2s_t)


def _attn_body(slope_ref, qt_ref, ks_ref, kw_ref, vst_ref, vwt_ref, kc_ref, vct_ref, mneg_ref, gt_ref,
               o_ref, s_sc, bias_sc, *, n_cmp_rows):
    tq = qt_ref.shape[2]
    tk = tq
    g = pl.program_id(1)
    qi = pl.program_id(2)
    t0 = qi * tq
    slopes = [slope_ref[g * N_GRP + r] for r in range(N_GRP)]
    q_rows = [qt_ref[0, r * HD:(r + 1) * HD, :] for r in range(N_GRP)]
    qk = [jnp.concatenate([q_rows[r], mneg_ref[0, 0]], axis=0) for r in range(N_GRP)]

    @pl.when(qi == 0)
    def _():
        key_io = lax.broadcasted_iota(jnp.int32, (tk, tq), 0)
        rel = key_io - lax.broadcasted_iota(jnp.int32, (tk, tq), 1)
        key_f = key_io.astype(F32)
        causal = jnp.where(rel <= 0, 0.0, MASK_VALUE)
        oldest = jnp.where(rel > 0, 0.0, MASK_VALUE)
        for r in range(N_GRP):
            base = slopes[r] * key_f
            bias_sc[0, r] = base
            bias_sc[1, r] = base + causal
            bias_sc[2, r] = base + oldest

    ones_rows = jnp.ones((SUM_ROWS, tk), BF16)

    def offsets(c):
        cf = ((c - qi) * tk).astype(F32)
        return [slopes[r] * cf for r in range(N_GRP)]

    def score_stage(c, k_ref, window, slot):
        kblk = k_ref[0, 0, pl.ds(pl.multiple_of(c * tk, tk), tk), :]
        variant = jnp.where(c == qi, 1, 0)
        if window:
            variant = jnp.where(c == qi - WINDOW // tk, 2, variant)
        offs = offsets(c)
        mcs = []
        for r in range(N_GRP):
            s = _dot(kblk, qk[r]) + bias_sc[variant, r]
            s_sc[slot, r] = s
            mcs.append(jnp.max(s, axis=0, keepdims=True) + offs[r])
        return tuple(mcs)

    def value_stage(c, carry, mcs, vt_ref, slot):
        vt = jnp.concatenate([vt_ref[0, 0, c], ones_rows], axis=0)
        offs = offsets(c)
        out = []
        for r in range(N_GRP):
            m, acc = carry[r]
            m_new = jnp.maximum(m, mcs[r])
            p = jnp.exp2(s_sc[slot, r] - (m_new - offs[r]))
            out.append((m_new, jnp.exp2(m - m_new) * acc + _dot(vt, p.astype(BF16))))
        return tuple(out)

    def branch(k_ref, vt_ref, c_lo, window):
        init = tuple((jnp.full((1, tq), MASK_VALUE, F32), jnp.zeros((HD + SUM_ROWS, tq), F32))
                     for _ in range(N_GRP))
        steps = qi - c_lo

        def pair(k, state):
            carry, mcs0 = state
            c = c_lo + 2 * k
            mcs1 = score_stage(c + 1, k_ref, window, 1)
            carry = value_stage(c, carry, mcs0, vt_ref, 0)
            mcs0 = score_stage(c + 2, k_ref, window, 0)
            return value_stage(c + 1, carry, mcs1, vt_ref, 1), mcs0

        carry, mcs0 = lax.fori_loop(0, steps // 2, pair, (init, score_stage(c_lo, k_ref, window, 0)))

        def tail_odd():
            mcs1 = score_stage(qi, k_ref, window, 1)
            return value_stage(qi, value_stage(qi - 1, carry, mcs0, vt_ref, 0), mcs1, vt_ref, 1)

        def tail_even():
            return value_stage(qi, carry, mcs0, vt_ref, 0)

        return lax.cond((steps & 1) == 1, tail_odd, tail_even)

    res_s = branch(ks_ref, vst_ref, jnp.int32(0), False)
    res_w = branch(kw_ref, vwt_ref, jnp.maximum(qi - WINDOW // tk, 0), True)

    distf, valid_c = _cmp_geometry(n_cmp_rows, tq, t0)
    kc = kc_ref[0, 0]
    vct = vct_ref[0, 0]
    gate = gt_ref[0, 0]
    for r in range(N_GRP):
        p, l = _cmp_scores(kc, q_rows[r], slopes[r], distf, valid_c)
        o_c = _dot(vct, p.astype(BF16)) / jnp.where(l > 0.0, l, 1.0)
        o_s = res_s[r][1][0:HD] / res_s[r][1][HD:HD + 1]
        o_w = res_w[r][1][0:HD] / res_w[r][1][HD:HD + 1]
        out = (gate[3 * r:3 * r + 1, :] * o_c + gate[3 * r + 1:3 * r + 2, :] * o_s
               + gate[3 * r + 2:3 * r + 3, :] * o_w)
        o_ref[0, r * HD:(r + 1) * HD, :] = out.astype(BF16)


def _attention(qt, ks, kw, vst, vwt, k_cmp, v_cmpt, mneg, gates_t):
    b, nq, s = qt.shape
    tq = vst.shape[4]
    assert WINDOW % tq == 0
    rows = k_cmp.shape[2]
    gw = N_GRP * HD
    slopes = jnp.asarray([_alibi_slope(h) * LOG2E for h in range(N_HEADS)], F32)
    keys = pl.BlockSpec((1, 1, s, LANES), lambda bi, g, i: (bi, g, 0, 0))
    vals = pl.BlockSpec((1, 1, s // tq, HD, tq), lambda bi, g, i: (bi, g, 0, 0, 0))
    return pl.pallas_call(
        functools.partial(_attn_body, n_cmp_rows=rows),
        out_shape=jax.ShapeDtypeStruct((b, nq, s), BF16),
        grid=(b, N_KV, s // tq),
        in_specs=[
            pl.BlockSpec(memory_space=pltpu.SMEM),
            pl.BlockSpec((1, gw, tq), lambda bi, g, i: (bi, g, i)),
            keys, keys, vals, vals,
            pl.BlockSpec((1, 1, rows, HD), lambda bi, g, i: (bi, g, 0, 0)),
            pl.BlockSpec((1, 1, HD, rows), lambda bi, g, i: (bi, g, 0, 0)),
            pl.BlockSpec((1, 1, HD, tq), lambda bi, g, i: (bi, g, 0, i)),
            pl.BlockSpec((1, 1, GATE_ROWS, tq), lambda bi, g, i: (bi, g, 0, i)),
        ],
        out_specs=pl.BlockSpec((1, gw, tq), lambda bi, g, i: (bi, g, i)),
        scratch_shapes=[pltpu.VMEM((2, N_GRP, tq, tq), F32), pltpu.VMEM((3, N_GRP, tq, tq), F32)],
        compiler_params=_cparams(("parallel", "parallel", "arbitrary")),
        name="nsa_attention",
    )(slopes, qt, ks, kw, vst, vwt, k_cmp, v_cmpt, mneg, gates_t)


def _outproj_t_body(yt_ref, h_ref, w_ref, g_ref, o_ref):
    o_ref[0] = h_ref[0] + _rms(_dot_tn(yt_ref[0], w_ref[...]), g_ref[...])


def _outproj_t(yt, h, w_out, g_post):
    b, s, d = h.shape
    tm = min(PROJ_TM, s)
    return pl.pallas_call(
        _outproj_t_body,
        out_shape=jax.ShapeDtypeStruct((b, s, d), F32),
        grid=(b, s // tm),
        in_specs=[
            pl.BlockSpec((1, d, tm), lambda bi, i: (bi, 0, i)),
            pl.BlockSpec((1, tm, d), lambda bi, i: (bi, i, 0)),
            _const_spec((d, d)),
            _const_spec((1, d)),
        ],
        out_specs=pl.BlockSpec((1, tm, d), lambda bi, i: (bi, i, 0)),
        compiler_params=_cparams(("parallel", "parallel")),
        name="outproj_t",
    )(yt, h, w_out.astype(BF16), g_post.reshape(1, d))


def _nsa_mixer(h, g_pre, w_in, pe_k, w_ck1, w_ck2, pe_v, w_cv1, w_cv2, w_out, g_post):
    qt, kc, vc, ks, kw, vst, vwt, gates_t = _nsa_proj(h, g_pre, w_in)
    k_cmp = _compress(kc, pe_k, w_ck1, w_ck2, transposed=False)
    v_cmpt = _compress(vc, pe_v, w_cv1, w_cv2, transposed=True)
    mneg = _select(qt, k_cmp)
    out_t = _attention(qt, ks, kw, vst, vwt, k_cmp, v_cmpt, mneg, gates_t)
    return _outproj_t(out_t, h, w_out, g_post)


def _head_sum(x, sel_ref):
    hi, lo = _split2(x)
    return _dot(hi, sel_ref[...]) + _dot(lo, sel_ref[...])


def _head_expand(x, selt_ref):
    hi, lo = _split2(x)
    return _dot(hi, selt_ref[...]) + _dot(lo, selt_ref[...])


def _softplus(x):
    return jnp.maximum(x, 0.0) + jnp.log(1.0 + jnp.exp(-jnp.abs(x)))


def _rw_proj_body(h_ref, hp_ref, g_ref, mu_ref, vec_ref, wr_ref, wk_ref, wv_ref, wd_ref, wa_ref, wg_ref,
                  w2d_ref, w2a_ref, w2g_ref, sel_ref, selt_ref,
                  r_ref, lw_ref, k_ref, v_ref, a_ref, b_ref, gg_ref):
    tm = h_ref.shape[1]
    u = _rms(h_ref[0], g_ref[...])
    prev = _rms(hp_ref[0], g_ref[...])[7:8, :]
    prev = jnp.where(pl.program_id(1) == 0, 0.0, prev)
    row = lax.broadcasted_iota(jnp.int32, (tm, D_MODEL), 0)
    u_prev = jnp.where(row == 0, prev, pltpu.roll(u, 1, 0))
    xx = u_prev - u

    def mix(i):
        return (u + xx * mu_ref[i:i + 1, :]).astype(BF16)

    w0, a0, k_k, k_a = (vec_ref[i:i + 1, :] for i in range(4))
    r = _dot(mix(0), wr_ref[...])
    k = _dot(mix(1), wk_ref[...])
    v = _dot(mix(2), wv_ref[...])
    d1 = jnp.tanh(_dot(mix(3), wd_ref[...])).astype(BF16)
    a1 = _dot(mix(4), wa_ref[...]).astype(BF16)
    g1 = jax.nn.sigmoid(_dot(mix(5), wg_ref[...])).astype(BF16)
    w = -_softplus(-(w0 + _dot(d1, w2d_ref[...]))) - 0.5
    alpha = jax.nn.sigmoid(a0 + _dot(a1, w2a_ref[...]))
    kk = k * k_k
    norm = jnp.sqrt(_head_sum(kk * kk, sel_ref))
    kk = kk * _head_expand(1.0 / jnp.maximum(norm, 1e-12), selt_ref)
    r_ref[0] = r
    lw_ref[0] = -jnp.exp(w)
    k_ref[0] = k * (1.0 + (alpha - 1.0) * k_a)
    v_ref[0] = v
    a_ref[0] = -kk
    b_ref[0] = kk * alpha
    gg_ref[0] = _dot(g1, w2g_ref[...])


def _head_selectors():
    lane_head = np.arange(D_MODEL) // HD
    sel = (lane_head[:, None] == np.arange(LANES)[None, :]).astype(np.float32)
    return jnp.asarray(sel, BF16), jnp.asarray(sel.T, BF16)


def _pad_cols(w, n):
    return jnp.pad(w, ((0, 0), (0, n - w.shape[1])))


def _pad_rows(w, n):
    return jnp.pad(w, ((0, n - w.shape[0]), (0, 0)))


def _rw_proj(h, g_pre, mu, w_in, w0, w_w2, a0, w_a2, w_g2, k_k, k_a):
    b, s, d = h.shape
    tm = min(PROJ_TM, s)
    offs = np.cumsum((0, d, d, d, DECAY_LORA, AAA_LORA, GATE_LORA))
    cols = [w_in[:, int(offs[i]):int(offs[i + 1])] for i in range(6)]
    ld, lg = LANES, 2 * LANES
    wr, wk, wv = (c.astype(BF16) for c in cols[:3])
    wd = _pad_cols(cols[3], ld).astype(BF16)
    wa = _pad_cols(cols[4], ld).astype(BF16)
    wg = _pad_cols(cols[5], lg).astype(BF16)
    w2d = _pad_rows(w_w2, ld).astype(BF16)
    w2a = _pad_rows(w_a2, ld).astype(BF16)
    w2g = _pad_rows(w_g2, lg).astype(BF16)
    mu8 = _pad_rows(mu, 8)
    vecs = _pad_rows(jnp.stack([w0, a0, k_k, k_a]), 8)
    sel, selt = _head_selectors()
    tok = pl.BlockSpec((1, tm, d), lambda bi, i: (bi, i, 0))
    out = jax.ShapeDtypeStruct((b, s, d), F32)
    return pl.pallas_call(
        _rw_proj_body,
        out_shape=(out,) * 7,
        grid=(b, s // tm),
        in_specs=[
            tok,
            pl.BlockSpec((1, 8, d), lambda bi, i: (bi, jnp.maximum(i * (tm // 8) - 1, 0), 0)),
            _const_spec((1, d)), _const_spec((8, d)), _const_spec((8, d)),
            _const_spec((d, d)), _const_spec((d, d)), _const_spec((d, d)),
            _const_spec((d, ld)), _const_spec((d, ld)), _const_spec((d, lg)),
            _const_spec((ld, d)), _const_spec((ld, d)), _const_spec((lg, d)),
            _const_spec((d, LANES)), _const_spec((LANES, d)),
        ],
        out_specs=(tok,) * 7,
        compiler_params=_cparams(("parallel", "parallel")),
        name="rwkv_proj",
    )(h, h, g_pre.reshape(1, d), mu8, vecs, wr, wk, wv, wd, wa, wg, w2d, w2a, w2g, sel, selt)


def _rw_prep_body(ltri_ref, r_ref, lw_ref, k_ref, v_ref, a_ref, b_ref,
                  rhat_ref, y1_ref, g_ref, n_ref):
    c = RW_C
    q = RW_Q
    nh = q // HD
    nch = r_ref.shape[1] // c
    lane_head = lax.shift_right_logical(lax.broadcasted_iota(jnp.int32, (c, q), 1), 6)
    ri = lax.broadcasted_iota(jnp.int32, (q, q), 0)
    ci = lax.broadcasted_iota(jnp.int32, (q, q), 1)
    same_head = lax.shift_right_logical(ri, 6) == lax.shift_right_logical(ci, 6)
    strict_bd = same_head & ((ci & (c - 1)) < (ri & (c - 1)))
    eye = ri == ci
    t_io = lax.broadcasted_iota(jnp.int32, (c, q), 0)
    j_io = lax.broadcasted_iota(jnp.int32, (c, q), 1) & (c - 1)
    strict_ls = j_io < t_io
    incl_ls = j_io <= t_io
    rows = [slice(i * c, (i + 1) * c) for i in range(nch)]

    def each(f):
        return [f(i) for i in range(nch)]

    def expand(x):
        return jnp.concatenate([jnp.where(lane_head == hh, x, 0.0) for hh in range(nh)], axis=0).astype(BF16)

    def collapse(x):
        out = x[0:c]
        for hh in range(1, nh):
            out = out + x[hh * c:(hh + 1) * c]
        return out

    lw = each(lambda i: lw_ref[0, rows[i], :])
    parts = each(lambda i: _split3(lw[i]))
    cum = each(lambda i: _dot(ltri_ref[...], parts[i][0]) + _dot(ltri_ref[...], parts[i][1])
               + _dot(ltri_ref[...], parts[i][2]))
    cum_c = each(lambda i: cum[i][c - 1:c, :])
    e_inv = each(lambda i: jnp.exp(-cum[i]))
    e_rem = each(lambda i: jnp.exp(cum_c[i] - cum[i]))
    at = each(lambda i: a_ref[0, rows[i], :] * jnp.exp(cum[i] - lw[i]))
    rt = each(lambda i: r_ref[0, rows[i], :] * jnp.exp(cum[i]))
    x4 = each(lambda i: expand(at[i]))
    b4 = each(lambda i: expand(b_ref[0, rows[i], :] * e_inv[i]))
    k4 = each(lambda i: expand(k_ref[0, rows[i], :] * e_inv[i]))
    v4 = each(lambda i: expand(v_ref[0, rows[i], :]))
    bbar = each(lambda i: (b_ref[0, rows[i], :] * e_rem[i]).astype(BF16))
    kbar = each(lambda i: (k_ref[0, rows[i], :] * e_rem[i]).astype(BF16))

    lbd = each(lambda i: jnp.where(strict_bd, _dot_nt(x4[i], b4[i]), 0.0))
    tbd = each(lambda i: jnp.where(eye, 1.0, lbd[i]))
    lb = each(lambda i: lbd[i].astype(BF16))
    p = each(lambda i: _dot(lb[i], lb[i]))
    n_lvl = int(np.log2(c)) - 1
    for lvl in range(n_lvl):
        pb = each(lambda i: p[i].astype(BF16))
        tbd = each(lambda i: tbd[i] + _dot(pb[i], tbd[i].astype(BF16)))
        if lvl + 1 < n_lvl:
            p = each(lambda i: _dot(pb[i], pb[i]))
    t_ls = each(lambda i: collapse(tbd[i]).astype(BF16))

    ar = each(lambda i: _dot_nt(jnp.concatenate([at[i], rt[i]], axis=0).astype(BF16), k4[i]))
    a_ak = each(lambda i: jnp.where(strict_ls, ar[i][0:c], 0.0).astype(BF16))
    a_rk = each(lambda i: jnp.where(incl_ls, ar[i][c:2 * c], 0.0).astype(BF16))
    a_rb = each(lambda i: jnp.where(incl_ls, _dot_nt(rt[i].astype(BF16), b4[i]), 0.0).astype(BF16))

    g1 = each(lambda i: _dot(a_ak[i], v4[i]))
    y0 = each(lambda i: _dot(a_rk[i], v4[i]))
    u0 = each(lambda i: _dot(t_ls[i], expand(g1[i])))
    ahat = each(lambda i: _dot(t_ls[i], x4[i]))
    rhat = each(lambda i: rt[i] + _dot(a_rb[i], expand(ahat[i])))
    y1 = each(lambda i: y0[i] + _dot(a_rb[i], expand(u0[i])))

    gm = each(lambda i: _dot_tn(bbar[i], ahat[i].astype(BF16)))
    nm = each(lambda i: _dot_tn(jnp.concatenate([bbar[i], kbar[i]], axis=0),
                                jnp.concatenate([u0[i], v_ref[0, rows[i], :]], axis=0).astype(BF16)))
    for i in range(nch):
        gmi = jnp.where(same_head, gm[i], 0.0) + jnp.where(eye, jnp.exp(cum_c[i]), 0.0)
        rhat_ref[0, rows[i], :] = rhat[i].astype(BF16)
        y1_ref[0, rows[i], :] = y1[i]
        g_ref[0, rows[i], :] = collapse(gmi).astype(BF16)
        n_ref[0, rows[i], :] = collapse(jnp.where(same_head, nm[i], 0.0))


def _rw_prep(r, lw, k, v, a, b):
    bsz, s, d = r.shape
    rows = min(RW_C * RW_NCH, s)
    ltri = jnp.asarray(np.tril(np.ones((RW_C, RW_C), np.float32)), BF16)
    blk = pl.BlockSpec((1, rows, RW_Q), lambda bi, qi, j: (bi, j, qi))
    return pl.pallas_call(
        _rw_prep_body,
        out_shape=(
            jax.ShapeDtypeStruct((bsz, s, d), BF16),
            jax.ShapeDtypeStruct((bsz, s, d), F32),
            jax.ShapeDtypeStruct((bsz, s, d), BF16),
            jax.ShapeDtypeStruct((bsz, s, d), F32),
        ),
        grid=(bsz, d // RW_Q, s // rows),
        in_specs=[_const_spec((RW_C, RW_C))] + [blk] * 6,
        out_specs=(blk,) * 4,
        compiler_params=_cparams(("parallel", "parallel", "parallel")),
        name="rwkv_prep",
    )(ltri, r, lw, k, v, a, b)


def _rw_scan_body(rhat_ref, y1_ref, g_ref, n_ref, y_ref, h_sc):
    c = RW_C
    q = RW_Q
    nq = h_sc.shape[0]

    @pl.when(pl.program_id(1) == 0)
    def _():
        h_sc[...] = jnp.zeros(h_sc.shape, F32)

    ri = lax.broadcasted_iota(jnp.int32, (q, q), 0)
    ci = lax.broadcasted_iota(jnp.int32, (q, q), 1)
    same_head = lax.shift_right_logical(ri, 6) == lax.shift_right_logical(ci, 6)

    def step(ch, carry):
        rows = pl.ds(pl.multiple_of(ch * c, c), c)
        for qi in range(nq):
            lanes = slice(qi * q, (qi + 1) * q)
            hb = h_sc[qi].astype(BF16)
            y_ref[0, rows, lanes] = _dot(rhat_ref[0, rows, lanes], hb) + y1_ref[0, rows, lanes]
            g_ls = g_ref[0, rows, lanes]
            n_ls = n_ref[0, rows, lanes]
            gbd = jnp.where(same_head, jnp.concatenate([g_ls] * (q // c), axis=0), 0.0)
            nbd = jnp.where(same_head, jnp.concatenate([n_ls] * (q // c), axis=0), 0.0)
            h_sc[qi] = _dot(gbd.astype(BF16), hb) + nbd
        return carry

    lax.fori_loop(0, rhat_ref.shape[1] // c, step, 0)


def _rw_scan(rhat, y1, g, n):
    bsz, s, d = rhat.shape
    rows = min(512, s)
    blk = pl.BlockSpec((1, rows, d), lambda bi, j: (bi, j, 0))
    return pl.pallas_call(
        _rw_scan_body,
        out_shape=jax.ShapeDtypeStruct((bsz, s, d), F32),
        grid=(bsz, s // rows),
        in_specs=[blk] * 4,
        out_specs=blk,
        scratch_shapes=[pltpu.VMEM((d // RW_Q, RW_Q, RW_Q), F32)],
        compiler_params=_cparams(("parallel", "arbitrary")),
        name="rwkv_scan",
    )(rhat, y1, g, n)


def _rw_post_body(y_ref, r_ref, k_ref, v_ref, gg_ref, h_ref, vec_ref, w_ref, gpost_ref, sel_ref, selt_ref, o_ref):
    gn_w, gn_b, r_k = (vec_ref[i:i + 1, :] for i in range(3))
    y = y_ref[...]
    inv_n = 1.0 / HD
    mean = _head_expand(_head_sum(y, sel_ref) * inv_n, selt_ref)
    yc = y - mean
    var = _head_sum(yc * yc, sel_ref) * inv_n
    yn = yc * _head_expand(lax.rsqrt(var + GN_EPS), selt_ref) * gn_w + gn_b
    v = v_ref[...]
    bonus = _head_expand(_head_sum(r_ref[...] * k_ref[...] * r_k, sel_ref), selt_ref) * v
    z = ((yn + bonus) * gg_ref[...]).astype(BF16)
    o_ref[...] = h_ref[...] + _rms(_dot(z, w_ref[...]), gpost_ref[...])


def _rw_post(y, r, k, v, gg, h, gn_w, gn_b, r_k, w_out, g_post):
    t, d = h.shape
    tm = min(PROJ_TM, t)
    vecs = _pad_rows(jnp.stack([gn_w, gn_b, r_k.reshape(d)]), 8)
    sel, selt = _head_selectors()
    tok = pl.BlockSpec((tm, d), lambda i: (i, 0))
    return pl.pallas_call(
        _rw_post_body,
        out_shape=jax.ShapeDtypeStruct((t, d), F32),
        grid=(t // tm,),
        in_specs=[tok] * 6 + [_const_spec((8, d)), _const_spec((d, d)), _const_spec((1, d)),
                              _const_spec((d, LANES)), _const_spec((LANES, d))],
        out_specs=tok,
        compiler_params=_cparams(("parallel",)),
        name="rwkv_post",
    )(y, r, k, v, gg, h, vecs, w_out.astype(BF16), g_post.reshape(1, d), sel, selt)


def _rwkv_mixer(h, g_pre, mu, w_in, w0, w_w2, a0, w_a2, w_g2, k_k, k_a, r_k, gn_w, gn_b, w_out, g_post):
    b, s, d = h.shape
    r, lw, k, v, a, bb, gg = _rw_proj(h, g_pre, mu, w_in, w0, w_w2, a0, w_a2, w_g2, k_k, k_a)
    rhat, y1, g, n = _rw_prep(r, lw, k, v, a, bb)
    y = _rw_scan(rhat, y1, g, n)
    f2 = lambda x: x.reshape(b * s, d)
    return _rw_post(f2(y), f2(r), f2(k), f2(v), f2(gg), f2(h), gn_w, gn_b, r_k, w_out, g_post).reshape(b, s, d)


def kernel(x, ffn1_norm_pre, ffn1_w_gu, ffn1_w_down, ffn1_norm_post, mix_norm_pre, nsa_w_in, nsa_pe_k,
           nsa_w_ck1, nsa_w_ck2, nsa_pe_v, nsa_w_cv1, nsa_w_cv2, nsa_w_out, rwkv_mu, rwkv_w_in, rwkv_w0,
           rwkv_w_w2, rwkv_a0, rwkv_w_a2, rwkv_w_g2, rwkv_k_k, rwkv_k_a, rwkv_r_k, rwkv_gn_w, rwkv_gn_b,
           rwkv_w_out, mix_norm_post, ffn2_norm_pre, ffn2_w_gu, ffn2_w_down, ffn2_norm_post):
    b, s, d = x.shape
    flat = lambda t: t.reshape(b * s, d)
    cube = lambda t: t.reshape(b, s, d)
    h = x
    depth = ffn1_norm_pre.shape[0]
    for i in range(depth):
        h = cube(_ffn(flat(h), ffn1_norm_pre[i], ffn1_w_gu[i], ffn1_w_down[i], ffn1_norm_post[i]))
        j = i // 2
        if i % 2 == 0:
            h = _nsa_mixer(h, mix_norm_pre[i], nsa_w_in[j], nsa_pe_k[j], nsa_w_ck1[j], nsa_w_ck2[j],
                           nsa_pe_v[j], nsa_w_cv1[j], nsa_w_cv2[j], nsa_w_out[j], mix_norm_post[i])
        else:
            h = _rwkv_mixer(h, mix_norm_pre[i], rwkv_mu[j], rwkv_w_in[j], rwkv_w0[j], rwkv_w_w2[j],
                            rwkv_a0[j], rwkv_w_a2[j], rwkv_w_g2[j], rwkv_k_k[j], rwkv_k_a[j], rwkv_r_k[j],
                            rwkv_gn_w[j], rwkv_gn_b[j], rwkv_w_out[j], mix_norm_post[i])
        h = cube(_ffn(flat(h), ffn2_norm_pre[i], ffn2_w_gu[i], ffn2_w_down[i], ffn2_norm_post[i]))
    return h
```

```python
import functools

import numpy as np
import jax
import jax.numpy as jnp
from jax import lax
from jax.experimental import pallas as pl
from jax.experimental.pallas import tpu as pltpu

F32 = jnp.float32
BF16 = jnp.bfloat16

D_MODEL = 1024
D_FF = 2816
HALF_STEP = 0.5
RMS_EPS = 1e-6
MASK_VALUE = -1e30

HD = 64
N_HEADS = 16
N_KV = 4
N_GRP = 4
KV_W = N_KV * HD
CMP_BLOCK = 32
CMP_STRIDE = 16
SEL_BLOCK = 64
N_SELECT = 16
WINDOW = 512
FORCED_SCORE = 1e4
N_GATE = 3 * N_HEADS

DECAY_LORA = 64
AAA_LORA = 64
GATE_LORA = 160
GN_EPS = 64e-5

LANES = 128
VMEM_LIMIT_BYTES = 56 * 1024 * 1024

FFN_TM = 512
FFN_CK = 1408
PROJ_TM = 512
ATT_T = 256
SEL_TQ = 256
GATE_ROWS = 16
SUM_ROWS = 16
LOG2E = 1.4426950408889634
RW_C = 64
RW_Q = 128
RW_NCH = 8


def _cparams(sem):
    return pltpu.CompilerParams(dimension_semantics=sem, vmem_limit_bytes=VMEM_LIMIT_BYTES)


def _rms(x, g):
    ms = jnp.mean(x * x, axis=-1, keepdims=True)
    return x * lax.rsqrt(ms + RMS_EPS) * g


def _const_spec(shape):
    nd = len(shape)
    return pl.BlockSpec(shape, lambda *_: (0,) * nd, pipeline_mode=pl.Buffered(1))


def _dot(a, b):
    return jnp.dot(a, b, preferred_element_type=F32)


def _dot_nt(a, b):
    return lax.dot_general(a, b, (((1,), (1,)), ((), ())), preferred_element_type=F32)


def _dot_tn(a, b):
    return lax.dot_general(a, b, (((0,), (0,)), ((), ())), preferred_element_type=F32)


def _split2(x):
    hi = x.astype(BF16)
    lo = (x - hi.astype(F32)).astype(BF16)
    return hi, lo


def _split3(x):
    hi = x.astype(BF16)
    r1 = x - hi.astype(F32)
    mid = r1.astype(BF16)
    lo = (r1 - mid.astype(F32)).astype(BF16)
    return hi, mid, lo


def _ffn_body(x_ref, gpre_ref, wgu_ref, wd_ref, gpost_ref, o_ref):
    x = x_ref[...]
    xn = _rms(x, gpre_ref[...]).astype(BF16)
    acc = None
    for c in range(D_FF // FFN_CK):
        lo = c * FFN_CK
        gate = _dot(xn, wgu_ref[:, lo:lo + FFN_CK])
        up = _dot(xn, wgu_ref[:, D_FF + lo:D_FF + lo + FFN_CK])
        act = (gate * jax.nn.sigmoid(gate) * up).astype(BF16)
        part = _dot(act, wd_ref[lo:lo + FFN_CK, :])
        acc = part if acc is None else acc + part
    o_ref[...] = x + HALF_STEP * _rms(acc, gpost_ref[...])


def _ffn(h2, g_pre, w_gu, w_down, g_post):
    t = h2.shape[0]
    tm = min(FFN_TM, t)
    return pl.pallas_call(
        _ffn_body,
        out_shape=jax.ShapeDtypeStruct((t, D_MODEL), F32),
        grid=(t // tm,),
        in_specs=[
            pl.BlockSpec((tm, D_MODEL), lambda i: (i, 0)),
            _const_spec((1, D_MODEL)),
            _const_spec((D_MODEL, 2 * D_FF)),
            _const_spec((D_FF, D_MODEL)),
            _const_spec((1, D_MODEL)),
        ],
        out_specs=pl.BlockSpec((tm, D_MODEL), lambda i: (i, 0)),
        compiler_params=_cparams(("parallel",)),
        name="ffn",
    )(h2, g_pre.reshape(1, D_MODEL), w_gu.astype(BF16), w_down.astype(BF16), g_post.reshape(1, D_MODEL))


def _nsa_proj_body(h_ref, g_ref, wqt_ref, wc_ref, wk_ref, wvt_ref, wgt_ref,
                   qt_ref, kc_ref, vc_ref, ks_ref, kw_ref, vst_ref, vwt_ref, gt_ref):
    tm = h_ref.shape[1]
    u = _rms(h_ref[0], g_ref[...]).astype(BF16)
    qt_ref[0] = (_dot_nt(wqt_ref[...], u) * (HD ** -0.5 * LOG2E)).astype(BF16)
    c = _dot(u, wc_ref[...])
    kc_ref[0] = c[:, :KV_W]
    vc_ref[0] = c[:, KV_W:]
    kk = _dot(u, wk_ref[...])
    t0 = pl.program_id(1) * tm
    lane = lax.broadcasted_iota(jnp.int32, (tm, LANES), 1)
    blk = lax.shift_right_logical(t0 + lax.broadcasted_iota(jnp.int32, (tm, LANES), 0), 6)
    onehot = jnp.where(lane - HD == blk, 1.0, 0.0)
    for g in range(N_KV):
        ks_ref[0, g] = (kk[:, g * LANES:(g + 1) * LANES] + onehot).astype(BF16)
        kw_ref[0, g] = kk[:, (N_KV + g) * LANES:(N_KV + g + 1) * LANES].astype(BF16)
    vt = _dot_nt(wvt_ref[...], u)
    tk = vst_ref.shape[4]
    for j in range(tm // tk):
        cols = slice(j * tk, (j + 1) * tk)
        vst_ref[0, :, j] = vt[:KV_W, cols].reshape(N_KV, HD, tk).astype(BF16)
        vwt_ref[0, :, j] = vt[KV_W:, cols].reshape(N_KV, HD, tk).astype(BF16)
    gt_ref[0] = jax.nn.sigmoid(_dot_nt(wgt_ref[...], u)).reshape(N_KV, GATE_ROWS, tm)


def _nsa_proj(h, g_pre, w_in):
    b, s, _ = h.shape
    tm = min(PROJ_TM, s)
    tk = min(ATT_T, s)
    nq = N_HEADS * HD
    assert s // SEL_BLOCK <= HD
    col = lambda i: w_in[:, nq + i * KV_W:nq + (i + 1) * KV_W]
    w_qt = w_in[:, :nq].T.astype(BF16)
    w_c = jnp.concatenate([col(0), col(1)], axis=1).astype(BF16)
    pad_heads = lambda w: jnp.pad(w.reshape(D_MODEL, N_KV, HD), ((0, 0), (0, 0), (0, LANES - HD))).reshape(D_MODEL, N_KV * LANES)
    w_k = jnp.concatenate([pad_heads(col(2)), pad_heads(col(4))], axis=1).astype(BF16)
    w_vt = jnp.concatenate([col(3), col(5)], axis=1).T.astype(BF16)
    w_gl = w_in[:, nq + 6 * KV_W:].reshape(D_MODEL, N_KV, N_GRP * 3)
    w_gt = jnp.pad(w_gl, ((0, 0), (0, 0), (0, GATE_ROWS - N_GRP * 3))).reshape(D_MODEL, N_KV * GATE_ROWS).T.astype(BF16)
    tok = lambda w: pl.BlockSpec((1, tm, w), lambda bi, i: (bi, i, 0))
    key = jax.ShapeDtypeStruct((b, N_KV, s, LANES), BF16)
    key_spec = pl.BlockSpec((1, N_KV, tm, LANES), lambda bi, i: (bi, 0, i, 0))
    valt = jax.ShapeDtypeStruct((b, N_KV, s // tk, HD, tk), BF16)
    valt_spec = pl.BlockSpec((1, N_KV, tm // tk, HD, tk), lambda bi, i: (bi, 0, i, 0, 0))
    return pl.pallas_call(
        _nsa_proj_body,
        out_shape=(
            jax.ShapeDtypeStruct((b, nq, s), BF16),
            jax.ShapeDtypeStruct((b, s, KV_W), F32),
            jax.ShapeDtypeStruct((b, s, KV_W), F32),
            key, key, valt, valt,
            jax.ShapeDtypeStruct((b, N_KV, GATE_ROWS, s), F32),
        ),
        grid=(b, s // tm),
        in_specs=[
            tok(D_MODEL),
            _const_spec((1, D_MODEL)),
            _const_spec((nq, D_MODEL)),
            _const_spec((D_MODEL, 2 * KV_W)),
            _const_spec((D_MODEL, 2 * N_KV * LANES)),
            _const_spec((2 * KV_W, D_MODEL)),
            _const_spec((N_KV * GATE_ROWS, D_MODEL)),
        ],
        out_specs=(
            pl.BlockSpec((1, nq, tm), lambda bi, i: (bi, 0, i)),
            tok(KV_W), tok(KV_W), key_spec, key_spec, valt_spec, valt_spec,
            pl.BlockSpec((1, N_KV, GATE_ROWS, tm), lambda bi, i: (bi, 0, 0, i)),
        ),
        compiler_params=_cparams(("parallel", "parallel")),
        name="nsa_proj",
    )(h, g_pre.reshape(1, D_MODEL), w_qt, w_c, w_k, w_vt, w_gt)


def _compress_body(x_ref, pea_ref, peb_ref, wa_ref, wb_ref, w2_ref, o_ref, *, transposed):
    x = x_ref[0]
    ha = _dot((x + pea_ref[...]).astype(BF16), wa_ref[...])
    hb = _dot((x + peb_ref[...]).astype(BF16), wb_ref[...])
    n = ha.shape[0]
    hid = ha + pltpu.roll(hb, n - 1, 0)
    hid = (hid * jax.nn.sigmoid(hid)).astype(BF16)
    if transposed:
        o_ref[0] = _dot_nt(w2_ref[...], hid).reshape(N_KV, HD, n).astype(BF16)
    else:
        out = _dot(hid, w2_ref[...]).astype(BF16)
        for g in range(N_KV):
            o_ref[0, g] = out[:, g * HD:(g + 1) * HD]


def _compress(t, pe, w1, w2, transposed):
    b, s, _ = t.shape
    half = CMP_BLOCK // 2
    rows = s // half
    x = t.reshape(b, rows, half * KV_W)
    eye = jnp.eye(N_KV, dtype=F32)
    wbd = jnp.einsum("ldc,gh->lgdhc", w1, eye).reshape(CMP_BLOCK, KV_W, KV_W)
    wa = wbd[:half].reshape(half * KV_W, KV_W).astype(BF16)
    wb = wbd[half:].reshape(half * KV_W, KV_W).astype(BF16)
    pe_t = jnp.broadcast_to(pe[:, None, :], (CMP_BLOCK, N_KV, HD))
    pea = pe_t[:half].reshape(1, half * KV_W)
    peb = pe_t[half:].reshape(1, half * KV_W)
    w2bd = jnp.einsum("cd,gh->gchd", w2, eye).reshape(KV_W, KV_W)
    if transposed:
        w2bd = w2bd.T
        out_shape, out_block = (b, N_KV, HD, rows), (1, N_KV, HD, rows)
    else:
        out_shape, out_block = (b, N_KV, rows, HD), (1, N_KV, rows, HD)
    return pl.pallas_call(
        functools.partial(_compress_body, transposed=transposed),
        out_shape=jax.ShapeDtypeStruct(out_shape, BF16),
        grid=(b,),
        in_specs=[
            pl.BlockSpec((1, rows, half * KV_W), lambda bi: (bi, 0, 0)),
            _const_spec((1, half * KV_W)),
            _const_spec((1, half * KV_W)),
            _const_spec((half * KV_W, KV_W)),
            _const_spec((half * KV_W, KV_W)),
            _const_spec((KV_W, KV_W)),
        ],
        out_specs=pl.BlockSpec(out_block, lambda bi: (bi, 0, 0, 0)),
        compiler_params=_cparams(("parallel",)),
        name="nsa_compress",
    )(x, pea, peb, wa, wb, w2bd.astype(BF16))


def _alibi_slope(h):
    return float(2.0 ** (-8.0 * (h + 1) / N_HEADS))


def _cmp_scores(kc, qh, slope2, distf, valid):
    s = jnp.where(valid, _dot(kc, qh) - slope2 * distf, MASK_VALUE)
    m = jnp.max(s, axis=0, keepdims=True)
    p = jnp.where(valid, jnp.exp2(s - m), 0.0)
    return p, jnp.sum(p, axis=0, keepdims=True)


def _cmp_geometry(n_cmp_rows, tq, t0):
    n_io = lax.broadcasted_iota(jnp.int32, (n_cmp_rows, tq), 0)
    t_io = t0 + lax.broadcasted_iota(jnp.int32, (n_cmp_rows, tq), 1)
    dist = t_io - (n_io * CMP_STRIDE + (CMP_BLOCK - 1))
    return dist.astype(F32), (dist >= 0) & (n_io < n_cmp_rows - 1)


def _select_body(qt_ref, kc_ref, c2s_ref, mneg_ref, *, n_sel, n_cmp_rows):
    tq = qt_ref.shape[2]
    t0 = pl.program_id(1) * tq
    distf, valid = _cmp_geometry(n_cmp_rows, tq, t0)
    s_io = lax.broadcasted_iota(jnp.int32, (n_sel, tq), 0)
    tt = t0 + lax.broadcasted_iota(jnp.int32, (n_sel, tq), 1)
    cur = lax.shift_right_logical(tt, 6)
    forced = (s_io == 0) | (s_io == cur) | (s_io == cur - 1)
    visible = s_io * SEL_BLOCK <= tt

    for g in range(N_KV):
        kc = kc_ref[0, g]
        imp = jnp.zeros((n_sel, tq), F32)
        for r in range(N_GRP):
            hh = g * N_GRP + r
            p, l = _cmp_scores(kc, qt_ref[0, hh * HD:(hh + 1) * HD, :], _alibi_slope(hh) * LOG2E, distf, valid)
            p = p * (1.0 / jnp.where(l > 0.0, l, 1.0))
            imp = imp + _dot(c2s_ref[...], p.astype(BF16))
        x = jnp.where(forced, FORCED_SCORE, jnp.where(visible, imp, -1.0))
        cnt = jnp.zeros((n_sel, tq), F32)
        for j in range(n_sel):
            xj = x[j:j + 1, :]
            ge = jnp.where(xj >= x, 1.0, 0.0)
            gt = jnp.where(xj > x, 1.0, 0.0)
            cnt = cnt + jnp.where(s_io > j, ge, gt)
        mneg_ref[0, g, 0:n_sel, :] = jnp.where(cnt < float(N_SELECT), 0.0, MASK_VALUE).astype(BF16)
        if n_sel < HD:
            mneg_ref[0, g, n_sel:HD, :] = jnp.zeros((HD - n_sel, tq), BF16)


def _select(qt, k_cmp):
    b, nq, s = qt.shape
    tq = min(SEL_TQ, s)
    n_sel = s // SEL_BLOCK
    rows = k_cmp.shape[2]
    n_cmp = (s - CMP_BLOCK) // CMP_STRIDE + 1
    assert rows == n_cmp + 1
    cs = np.arange(rows) * CMP_STRIDE
    bs = np.arange(n_sel) * SEL_BLOCK
    overlap = np.clip(np.minimum(cs[:, None] + CMP_BLOCK, bs[None, :] + SEL_BLOCK)
                      - np.maximum(cs[:, None], bs[None, :]), 0, None)
    c2s_t = jnp.asarray((overlap / CMP_BLOCK).T, BF16)
    return pl.pallas_call(
        functools.partial(_select_body, n_sel=n_sel, n_cmp_rows=rows),
        out_shape=jax.ShapeDtypeStruct((b, N_KV, HD, s), BF16),
        grid=(b, s // tq),
        in_specs=[
            pl.BlockSpec((1, nq, tq), lambda bi, i: (bi, 0, i)),
            pl.BlockSpec((1, N_KV, rows, HD), lambda bi, i: (bi, 0, 0, 0)),
            _const_spec((n_sel, rows)),
        ],
        out_specs=pl.BlockSpec((1, N_KV, HD, tq), lambda bi, i: (bi, 0, 0, i)),
        compiler_params=_cparams(("parallel", "parallel")),
        name="nsa_select",
    )(qt, k_cmp, c2s_t)


def _attn_body(slope_ref, qt_ref, ks_ref, kw_ref, vst_ref, vwt_ref, kc_ref, vct_ref, mneg_ref, gt_ref,
               o_ref, s_sc, bias_sc, *, n_cmp_rows):
    tq = qt_ref.shape[2]
    tk = tq
    g = pl.program_id(1)
    qi = pl.program_id(2)
    t0 = qi * tq
    slopes = [slope_ref[g * N_GRP + r] for r in range(N_GRP)]
    q_rows = [qt_ref[0, r * HD:(r + 1) * HD, :] for r in range(N_GRP)]
    qk = [jnp.concatenate([q_rows[r], mneg_ref[0, 0]], axis=0) for r in range(N_GRP)]

    @pl.when(qi == 0)
    def _():
        key_io = lax.broadcasted_iota(jnp.int32, (tk, tq), 0)
        rel = key_io - lax.broadcasted_iota(jnp.int32, (tk, tq), 1)
        key_f = key_io.astype(F32)
        causal = jnp.where(rel <= 0, 0.0, MASK_VALUE)
        oldest = jnp.where(rel > 0, 0.0, MASK_VALUE)
        for r in range(N_GRP):
            base = slopes[r] * key_f
            bias_sc[0, r] = base
            bias_sc[1, r] = base + causal
            bias_sc[2, r] = base + oldest

    ones_rows = jnp.ones((SUM_ROWS, tk), BF16)

    def offsets(c):
        cf = ((c - qi) * tk).astype(F32)
        return [slopes[r] * cf for r in range(N_GRP)]

    def score_stage(c, k_ref, window, slot):
        kblk = k_ref[0, 0, pl.ds(pl.multiple_of(c * tk, tk), tk), :]
        variant = jnp.where(c == qi, 1, 0)
        if window:
            variant = jnp.where(c == qi - WINDOW // tk, 2, variant)
        offs = offsets(c)
        mcs = []
        for r in range(N_GRP):
            s = _dot(kblk, qk[r]) + bias_sc[variant, r]
            s_sc[slot, r] = s
            mcs.append(jnp.max(s, axis=0, keepdims=True) + offs[r])
        return tuple(mcs)

    def value_stage(c, carry, mcs, vt_ref, slot):
        vt = jnp.concatenate([vt_ref[0, 0, c], ones_rows], axis=0)
        offs = offsets(c)
        out = []
        for r in range(N_GRP):
            m, acc = carry[r]
            m_new = jnp.maximum(m, mcs[r])
            p = jnp.exp2(s_sc[slot, r] - (m_new - offs[r]))
            out.append((m_new, jnp.exp2(m - m_new) * acc + _dot(vt, p.astype(BF16))))
        return tuple(out)

    def branch(k_ref, vt_ref, c_lo, window):
        init = tuple((jnp.full((1, tq), MASK_VALUE, F32), jnp.zeros((HD + SUM_ROWS, tq), F32))
                     for _ in range(N_GRP))
        steps = qi - c_lo

        def pair(k, state):
            carry, mcs0 = state
            c = c_lo + 2 * k
            mcs1 = score_stage(c + 1, k_ref, window, 1)
            carry = value_stage(c, carry, mcs0, vt_ref, 0)
            mcs0 = score_stage(c + 2, k_ref, window, 0)
            return value_stage(c + 1, carry, mcs1, vt_ref, 1), mcs0

        carry, mcs0 = lax.fori_loop(0, steps // 2, pair, (init, score_stage(c_lo, k_ref, window, 0)))

        def tail_odd():
            mcs1 = score_stage(qi, k_ref, window, 1)
            return value_stage(qi, value_stage(qi - 1, carry, mcs0, vt_ref, 0), mcs1, vt_ref, 1)

        def tail_even():
            return value_stage(qi, carry, mcs0, vt_ref, 0)

        return lax.cond((steps & 1) == 1, tail_odd, tail_even)

    res_s = branch(ks_ref, vst_ref, jnp.int32(0), False)
    res_w = branch(kw_ref, vwt_ref, jnp.maximum(qi - WINDOW // tk, 0), True)

    distf, valid_c = _cmp_geometry(n_cmp_rows, tq, t0)
    kc = kc_ref[0, 0]
    vct = vct_ref[0, 0]
    gate = gt_ref[0, 0]
    for r in range(N_GRP):
        p, l = _cmp_scores(kc, q_rows[r], slopes[r], distf, valid_c)
        o_c = _dot(vct, p.astype(BF16)) / jnp.where(l > 0.0, l, 1.0)
        o_s = res_s[r][1][0:HD] / res_s[r][1][HD:HD + 1]
        o_w = res_w[r][1][0:HD] / res_w[r][1][HD:HD + 1]
        out = (gate[3 * r:3 * r + 1, :] * o_c + gate[3 * r + 1:3 * r + 2, :] * o_s
               + gate[3 * r + 2:3 * r + 3, :] * o_w)
        o_ref[0, r * HD:(r + 1) * HD, :] = out.astype(BF16)


def _attention(qt, ks, kw, vst, vwt, k_cmp, v_cmpt, mneg, gates_t):
    b, nq, s = qt.shape
    tq = vst.shape[4]
    assert WINDOW % tq == 0
    rows = k_cmp.shape[2]
    gw = N_GRP * HD
    slopes = jnp.asarray([_alibi_slope(h) * LOG2E for h in range(N_HEADS)], F32)
    keys = pl.BlockSpec((1, 1, s, LANES), lambda bi, g, i: (bi, g, 0, 0))
    vals = pl.BlockSpec((1, 1, s // tq, HD, tq), lambda bi, g, i: (bi, g, 0, 0, 0))
    return pl.pallas_call(
        functools.partial(_attn_body, n_cmp_rows=rows),
        out_shape=jax.ShapeDtypeStruct((b, nq, s), BF16),
        grid=(b, N_KV, s // tq),
        in_specs=[
            pl.BlockSpec(memory_space=pltpu.SMEM),
            pl.BlockSpec((1, gw, tq), lambda bi, g, i: (bi, g, i)),
            keys, keys, vals, vals,
            pl.BlockSpec((1, 1, rows, HD), lambda bi, g, i: (bi, g, 0, 0)),
            pl.BlockSpec((1, 1, HD, rows), lambda bi, g, i: (bi, g, 0, 0)),
            pl.BlockSpec((1, 1, HD, tq), lambda bi, g, i: (bi, g, 0, i)),
            pl.BlockSpec((1, 1, GATE_ROWS, tq), lambda bi, g, i: (bi, g, 0, i)),
        ],
        out_specs=pl.BlockSpec((1, gw, tq), lambda bi, g, i: (bi, g, i)),
        scratch_shapes=[pltpu.VMEM((2, N_GRP, tq, tq), F32), pltpu.VMEM((3, N_GRP, tq, tq), F32)],
        compiler_params=_cparams(("parallel", "parallel", "arbitrary")),
        name="nsa_attention",
    )(slopes, qt, ks, kw, vst, vwt, k_cmp, v_cmpt, mneg, gates_t)


def _outproj_t_body(yt_ref, h_ref, w_ref, g_ref, o_ref):
    o_ref[0] = h_ref[0] + _rms(_dot_tn(yt_ref[0], w_ref[...]), g_ref[...])


def _outproj_t(yt, h, w_out, g_post):
    b, s, d = h.shape
    tm = min(PROJ_TM, s)
    return pl.pallas_call(
        _outproj_t_body,
        out_shape=jax.ShapeDtypeStruct((b, s, d), F32),
        grid=(b, s // tm),
        in_specs=[
            pl.BlockSpec((1, d, tm), lambda bi, i: (bi, 0, i)),
            pl.BlockSpec((1, tm, d), lambda bi, i: (bi, i, 0)),
            _const_spec((d, d)),
            _const_spec((1, d)),
        ],
        out_specs=pl.BlockSpec((1, tm, d), lambda bi, i: (bi, i, 0)),
        compiler_params=_cparams(("parallel", "parallel")),
        name="outproj_t",
    )(yt, h, w_out.astype(BF16), g_post.reshape(1, d))


def _nsa_mixer(h, g_pre, w_in, pe_k, w_ck1, w_ck2, pe_v, w_cv1, w_cv2, w_out, g_post):
    qt, kc, vc, ks, kw, vst, vwt, gates_t = _nsa_proj(h, g_pre, w_in)
    k_cmp = _compress(kc, pe_k, w_ck1, w_ck2, transposed=False)
    v_cmpt = _compress(vc, pe_v, w_cv1, w_cv2, transposed=True)
    mneg = _select(qt, k_cmp)
    out_t = _attention(qt, ks, kw, vst, vwt, k_cmp, v_cmpt, mneg, gates_t)
    return _outproj_t(out_t, h, w_out, g_post)


def _head_sum(x, sel_ref):
    return _dot(x.astype(BF16), sel_ref[...])


def _head_expand(x, selt_ref):
    hi, lo = _split2(x)
    return _dot(hi, selt_ref[...]) + _dot(lo, selt_ref[...])


def _softplus(x):
    return jnp.maximum(x, 0.0) + jnp.log(1.0 + jnp.exp(-jnp.abs(x)))


def _rw_proj_body(h_ref, hp_ref, g_ref, mu_ref, vec_ref, wr_ref, wk_ref, wv_ref, wd_ref, wa_ref, wg_ref,
                  w2d_ref, w2a_ref, w2g_ref, sel_ref, selt_ref,
                  r_ref, lw_ref, k_ref, v_ref, a_ref, b_ref, gg_ref):
    tm = h_ref.shape[1]
    u = _rms(h_ref[0], g_ref[...])
    prev = _rms(hp_ref[0], g_ref[...])[7:8, :]
    prev = jnp.where(pl.program_id(1) == 0, 0.0, prev)
    row = lax.broadcasted_iota(jnp.int32, (tm, D_MODEL), 0)
    u_prev = jnp.where(row == 0, prev, pltpu.roll(u, 1, 0))
    xx = u_prev - u

    def mix(i):
        return (u + xx * mu_ref[i:i + 1, :]).astype(BF16)

    w0, a0, k_k, k_a = (vec_ref[i:i + 1, :] for i in range(4))
    r = _dot(mix(0), wr_ref[...])
    k = _dot(mix(1), wk_ref[...])
    v = _dot(mix(2), wv_ref[...])
    d1 = jnp.tanh(_dot(mix(3), wd_ref[...])).astype(BF16)
    a1 = _dot(mix(4), wa_ref[...]).astype(BF16)
    g1 = jax.nn.sigmoid(_dot(mix(5), wg_ref[...])).astype(BF16)
    w = -_softplus(-(w0 + _dot(d1, w2d_ref[...]))) - 0.5
    alpha = jax.nn.sigmoid(a0 + _dot(a1, w2a_ref[...]))
    kk = k * k_k
    norm = jnp.sqrt(_head_sum(kk * kk, sel_ref))
    kk = kk * _head_expand(1.0 / jnp.maximum(norm, 1e-12), selt_ref)
    r_ref[0] = r.astype(BF16)
    lw_ref[0] = -jnp.exp(w)
    k_ref[0] = (k * (1.0 + (alpha - 1.0) * k_a)).astype(BF16)
    v_ref[0] = v.astype(BF16)
    a_ref[0] = (-kk).astype(BF16)
    b_ref[0] = (kk * alpha).astype(BF16)
    gg_ref[0] = _dot(g1, w2g_ref[...]).astype(BF16)


def _head_selectors():
    lane_head = np.arange(D_MODEL) // HD
    sel = (lane_head[:, None] == np.arange(LANES)[None, :]).astype(np.float32)
    return jnp.asarray(sel, BF16), jnp.asarray(sel.T, BF16)


def _pad_cols(w, n):
    return jnp.pad(w, ((0, 0), (0, n - w.shape[1])))


def _pad_rows(w, n):
    return jnp.pad(w, ((0, n - w.shape[0]), (0, 0)))


def _rw_proj(h, g_pre, mu, w_in, w0, w_w2, a0, w_a2, w_g2, k_k, k_a):
    b, s, d = h.shape
    tm = min(PROJ_TM, s)
    offs = np.cumsum((0, d, d, d, DECAY_LORA, AAA_LORA, GATE_LORA))
    cols = [w_in[:, int(offs[i]):int(offs[i + 1])] for i in range(6)]
    ld, lg = LANES, 2 * LANES
    wr, wk, wv = (c.astype(BF16) for c in cols[:3])
    wd = _pad_cols(cols[3], ld).astype(BF16)
    wa = _pad_cols(cols[4], ld).astype(BF16)
    wg = _pad_cols(cols[5], lg).astype(BF16)
    w2d = _pad_rows(w_w2, ld).astype(BF16)
    w2a = _pad_rows(w_a2, ld).astype(BF16)
    w2g = _pad_rows(w_g2, lg).astype(BF16)
    mu8 = _pad_rows(mu, 8)
    vecs = _pad_rows(jnp.stack([w0, a0, k_k, k_a]), 8)
    sel, selt = _head_selectors()
    tok = pl.BlockSpec((1, tm, d), lambda bi, i: (bi, i, 0))
    out = lambda dt: jax.ShapeDtypeStruct((b, s, d), dt)
    return pl.pallas_call(
        _rw_proj_body,
        out_shape=(out(BF16), out(F32), out(BF16), out(BF16), out(BF16), out(BF16), out(BF16)),
        grid=(b, s // tm),
        in_specs=[
            tok,
            pl.BlockSpec((1, 8, d), lambda bi, i: (bi, jnp.maximum(i * (tm // 8) - 1, 0), 0)),
            _const_spec((1, d)), _const_spec((8, d)), _const_spec((8, d)),
            _const_spec((d, d)), _const_spec((d, d)), _const_spec((d, d)),
            _const_spec((d, ld)), _const_spec((d, ld)), _const_spec((d, lg)),
            _const_spec((ld, d)), _const_spec((ld, d)), _const_spec((lg, d)),
            _const_spec((d, LANES)), _const_spec((LANES, d)),
        ],
        out_specs=(tok,) * 7,
        compiler_params=_cparams(("parallel", "parallel")),
        name="rwkv_proj",
    )(h, h, g_pre.reshape(1, d), mu8, vecs, wr, wk, wv, wd, wa, wg, w2d, w2a, w2g, sel, selt)


def _rw_prep_body(ltri_ref, r_ref, lw_ref, k_ref, v_ref, a_ref, b_ref,
                  rhat_ref, y1_ref, g_ref, n_ref):
    c = RW_C
    q = RW_Q
    nh = q // HD
    nch = r_ref.shape[1] // c
    lane_head = lax.shift_right_logical(lax.broadcasted_iota(jnp.int32, (c, q), 1), 6)
    ri = lax.broadcasted_iota(jnp.int32, (q, q), 0)
    ci = lax.broadcasted_iota(jnp.int32, (q, q), 1)
    same_head = lax.shift_right_logical(ri, 6) == lax.shift_right_logical(ci, 6)
    strict_bd = same_head & ((ci & (c - 1)) < (ri & (c - 1)))
    eye = ri == ci
    t_io = lax.broadcasted_iota(jnp.int32, (c, q), 0)
    j_io = lax.broadcasted_iota(jnp.int32, (c, q), 1) & (c - 1)
    strict_ls = j_io < t_io
    incl_ls = j_io <= t_io
    rows = [slice(i * c, (i + 1) * c) for i in range(nch)]

    def each(f):
        return [f(i) for i in range(nch)]

    def expand(x):
        return jnp.concatenate([jnp.where(lane_head == hh, x, 0.0) for hh in range(nh)], axis=0).astype(BF16)

    def collapse(x):
        out = x[0:c]
        for hh in range(1, nh):
            out = out + x[hh * c:(hh + 1) * c]
        return out

    lw = each(lambda i: lw_ref[0, rows[i], :])
    parts = each(lambda i: _split3(lw[i]))
    cum = each(lambda i: _dot(ltri_ref[...], parts[i][0]) + _dot(ltri_ref[...], parts[i][1])
               + _dot(ltri_ref[...], parts[i][2]))
    cum_c = each(lambda i: cum[i][c - 1:c, :])
    e_inv = each(lambda i: jnp.exp(-cum[i]))
    e_rem = each(lambda i: jnp.exp(cum_c[i] - cum[i]))
    at = each(lambda i: a_ref[0, rows[i], :] * jnp.exp(cum[i] - lw[i]))
    rt = each(lambda i: r_ref[0, rows[i], :] * jnp.exp(cum[i]))
    x4 = each(lambda i: expand(at[i]))
    b4 = each(lambda i: expand(b_ref[0, rows[i], :] * e_inv[i]))
    k4 = each(lambda i: expand(k_ref[0, rows[i], :] * e_inv[i]))
    v4 = each(lambda i: expand(v_ref[0, rows[i], :]))
    bbar = each(lambda i: (b_ref[0, rows[i], :] * e_rem[i]).astype(BF16))
    kbar = each(lambda i: (k_ref[0, rows[i], :] * e_rem[i]).astype(BF16))

    lbd = each(lambda i: jnp.where(strict_bd, _dot_nt(x4[i], b4[i]), 0.0))
    tbd = each(lambda i: jnp.where(eye, 1.0, lbd[i]))
    lb = each(lambda i: lbd[i].astype(BF16))
    p = each(lambda i: _dot(lb[i], lb[i]))
    n_lvl = int(np.log2(c)) - 1
    for lvl in range(n_lvl):
        pb = each(lambda i: p[i].astype(BF16))
        tbd = each(lambda i: tbd[i] + _dot(pb[i], tbd[i].astype(BF16)))
        if lvl + 1 < n_lvl:
            p = each(lambda i: _dot(pb[i], pb[i]))
    t_ls = each(lambda i: collapse(tbd[i]).astype(BF16))

    ar = each(lambda i: _dot_nt(jnp.concatenate([at[i], rt[i]], axis=0).astype(BF16), k4[i]))
    a_ak = each(lambda i: jnp.where(strict_ls, ar[i][0:c], 0.0).astype(BF16))
    a_rk = each(lambda i: jnp.where(incl_ls, ar[i][c:2 * c], 0.0).astype(BF16))
    a_rb = each(lambda i: jnp.where(incl_ls, _dot_nt(rt[i].astype(BF16), b4[i]), 0.0).astype(BF16))

    g1 = each(lambda i: _dot(a_ak[i], v4[i]))
    y0 = each(lambda i: _dot(a_rk[i], v4[i]))
    u0 = each(lambda i: _dot(t_ls[i], expand(g1[i])))
    ahat = each(lambda i: _dot(t_ls[i], x4[i]))
    rhat = each(lambda i: rt[i] + _dot(a_rb[i], expand(ahat[i])))
    y1 = each(lambda i: y0[i] + _dot(a_rb[i], expand(u0[i])))

    gm = each(lambda i: _dot_tn(bbar[i], ahat[i].astype(BF16)))
    nm = each(lambda i: _dot_tn(jnp.concatenate([bbar[i], kbar[i]], axis=0),
                                jnp.concatenate([u0[i].astype(BF16), v_ref[0, rows[i], :]], axis=0)))
    for i in range(nch):
        gmi = jnp.where(same_head, gm[i], 0.0) + jnp.where(eye, jnp.exp(cum_c[i]), 0.0)
        rhat_ref[0, rows[i], :] = rhat[i].astype(BF16)
        y1_ref[0, rows[i], :] = y1[i].astype(BF16)
        g_ref[0, rows[i], :] = collapse(gmi).astype(BF16)
        n_ref[0, rows[i], :] = collapse(jnp.where(same_head, nm[i], 0.0)).astype(BF16)


def _rw_prep(r, lw, k, v, a, b):
    bsz, s, d = r.shape
    rows = min(RW_C * RW_NCH, s)
    ltri = jnp.asarray(np.tril(np.ones((RW_C, RW_C), np.float32)), BF16)
    blk = pl.BlockSpec((1, rows, RW_Q), lambda bi, qi, j: (bi, j, qi))
    return pl.pallas_call(
        _rw_prep_body,
        out_shape=(jax.ShapeDtypeStruct((bsz, s, d), BF16),) * 4,
        grid=(bsz, d // RW_Q, s // rows),
        in_specs=[_const_spec((RW_C, RW_C))] + [blk] * 6,
        out_specs=(blk,) * 4,
        compiler_params=_cparams(("parallel", "parallel", "parallel")),
        name="rwkv_prep",
    )(ltri, r, lw, k, v, a, b)


def _rw_scan_body(rhat_ref, y1_ref, g_ref, n_ref, y_ref, h_sc):
    c = RW_C
    q = RW_Q
    nq = h_sc.shape[0]

    @pl.when(pl.program_id(1) == 0)
    def _():
        h_sc[...] = jnp.zeros(h_sc.shape, F32)

    ri = lax.broadcasted_iota(jnp.int32, (q, q), 0)
    ci = lax.broadcasted_iota(jnp.int32, (q, q), 1)
    same_head = lax.shift_right_logical(ri, 6) == lax.shift_right_logical(ci, 6)

    def step(ch, carry):
        rows = pl.ds(pl.multiple_of(ch * c, c), c)
        for qi in range(nq):
            lanes = slice(qi * q, (qi + 1) * q)
            hb = h_sc[qi].astype(BF16)
            y_ref[0, rows, lanes] = (_dot(rhat_ref[0, rows, lanes], hb) + y1_ref[0, rows, lanes]).astype(BF16)
            g_ls = g_ref[0, rows, lanes]
            n_ls = n_ref[0, rows, lanes]
            gbd = jnp.where(same_head, jnp.concatenate([g_ls] * (q // c), axis=0), 0.0)
            nbd = jnp.where(same_head, jnp.concatenate([n_ls] * (q // c), axis=0), 0.0)
            h_sc[qi] = _dot(gbd.astype(BF16), hb) + nbd
        return carry

    lax.fori_loop(0, rhat_ref.shape[1] // c, step, 0)


def _rw_scan(rhat, y1, g, n):
    bsz, s, d = rhat.shape
    rows = min(512, s)
    blk = pl.BlockSpec((1, rows, d), lambda bi, j: (bi, j, 0))
    return pl.pallas_call(
        _rw_scan_body,
        out_shape=jax.ShapeDtypeStruct((bsz, s, d), BF16),
        grid=(bsz, s // rows),
        in_specs=[blk] * 4,
        out_specs=blk,
        scratch_shapes=[pltpu.VMEM((d // RW_Q, RW_Q, RW_Q), F32)],
        compiler_params=_cparams(("parallel", "arbitrary")),
        name="rwkv_scan",
    )(rhat, y1, g, n)


def _rw_post_body(y_ref, r_ref, k_ref, v_ref, gg_ref, h_ref, vec_ref, w_ref, gpost_ref, sel_ref, selt_ref, o_ref):
    gn_w, gn_b, r_k = (vec_ref[i:i + 1, :] for i in range(3))
    f32 = lambda ref: ref[...].astype(F32)
    y = f32(y_ref)
    inv_n = 1.0 / HD
    mean = _head_expand(_head_sum(y, sel_ref) * inv_n, selt_ref)
    yc = y - mean
    var = _head_sum(yc * yc, sel_ref) * inv_n
    yn = yc * _head_expand(lax.rsqrt(var + GN_EPS), selt_ref) * gn_w + gn_b
    bonus = _head_expand(_head_sum(f32(r_ref) * f32(k_ref) * r_k, sel_ref), selt_ref) * f32(v_ref)
    z = ((yn + bonus) * f32(gg_ref)).astype(BF16)
    o_ref[...] = h_ref[...] + _rms(_dot(z, w_ref[...]), gpost_ref[...])


def _rw_post(y, r, k, v, gg, h, gn_w, gn_b, r_k, w_out, g_post):
    t, d = h.shape
    tm = min(PROJ_TM, t)
    vecs = _pad_rows(jnp.stack([gn_w, gn_b, r_k.reshape(d)]), 8)
    sel, selt = _head_selectors()
    tok = pl.BlockSpec((tm, d), lambda i: (i, 0))
    return pl.pallas_call(
        _rw_post_body,
        out_shape=jax.ShapeDtypeStruct((t, d), F32),
        grid=(t // tm,),
        in_specs=[tok] * 6 + [_const_spec((8, d)), _const_spec((d, d)), _const_spec((1, d)),
                              _const_spec((d, LANES)), _const_spec((LANES, d))],
        out_specs=tok,
        compiler_params=_cparams(("parallel",)),
        name="rwkv_post",
    )(y, r, k, v, gg, h, vecs, w_out.astype(BF16), g_post.reshape(1, d), sel, selt)


def _rwkv_mixer(h, g_pre, mu, w_in, w0, w_w2, a0, w_a2, w_g2, k_k, k_a, r_k, gn_w, gn_b, w_out, g_post):
    b, s, d = h.shape
    r, lw, k, v, a, bb, gg = _rw_proj(h, g_pre, mu, w_in, w0, w_w2, a0, w_a2, w_g2, k_k, k_a)
    rhat, y1, g, n = _rw_prep(r, lw, k, v, a, bb)
    y = _rw_scan(rhat, y1, g, n)
    f2 = lambda x: x.reshape(b * s, d)
    return _rw_post(f2(y), f2(r), f2(k), f2(v), f2(gg), f2(h), gn_w, gn_b, r_k, w_out, g_post).reshape(b, s, d)


def kernel(x, ffn1_norm_pre, ffn1_w_gu, ffn1_w_down, ffn1_norm_post, mix_norm_pre, nsa_w_in, nsa_pe_k,
           nsa_w_ck1, nsa_w_ck2, nsa_pe_v, nsa_w_cv1, nsa_w_cv2, nsa_w_out, rwkv_mu, rwkv_w_in, rwkv_w0,
           rwkv_w_w2, rwkv_a0, rwkv_w_a2, rwkv_w_g2, rwkv_k_k, rwkv_k_a, rwkv_r_k, rwkv_gn_w, rwkv_gn_b,
           rwkv_w_out, mix_norm_post, ffn2_norm_pre, ffn2_w_gu, ffn2_w_down, ffn2_norm_post):
    b, s, d = x.shape
    flat = lambda t: t.reshape(b * s, d)
    cube = lambda t: t.reshape(b, s, d)
    h = x
    depth = ffn1_norm_pre.shape[0]
    for i in range(depth):
        h = cube(_ffn(flat(h), ffn1_norm_pre[i], ffn1_w_gu[i], ffn1_w_down[i], ffn1_norm_post[i]))
        j = i // 2
        if i % 2 == 0:
            h = _nsa_mixer(h, mix_norm_pre[i], nsa_w_in[j], nsa_pe_k[j], nsa_w_ck1[j], nsa_w_ck2[j],
                           nsa_pe_v[j], nsa_w_cv1[j], nsa_w_cv2[j], nsa_w_out[j], mix_norm_post[i])
        else:
            h = _rwkv_mixer(h, mix_norm_pre[i], rwkv_mu[j], rwkv_w_in[j], rwkv_w0[j], rwkv_w_w2[j],
                            rwkv_a0[j], rwkv_w_a2[j], rwkv_w_g2[j], rwkv_k_k[j], rwkv_k_a[j], rwkv_r_k[j],
                            rwkv_gn_w[j], rwkv_gn_b[j], rwkv_w_out[j], mix_norm_post[i])
        h = cube(_ffn(flat(h), ffn2_norm_pre[i], ffn2_w_gu[i], ffn2_w_down[i], ffn2_norm_post[i]))
    return h
```

```python
import functools

import numpy as np
import jax
import jax.numpy as jnp
from jax import lax
from jax.experimental import pallas as pl
from jax.experimental.pallas import tpu as pltpu

F32 = jnp.float32
BF16 = jnp.bfloat16

D_MODEL = 1024
D_FF = 2816
HALF_STEP = 0.5
RMS_EPS = 1e-6
MASK_VALUE = -1e30

HD = 64
N_HEADS = 16
N_KV = 4
N_GRP = 4
KV_W = N_KV * HD
CMP_BLOCK = 32
CMP_STRIDE = 16
SEL_BLOCK = 64
N_SELECT = 16
WINDOW = 512
FORCED_SCORE = 1e4
N_GATE = 3 * N_HEADS

DECAY_LORA = 64
AAA_LORA = 64
GATE_LORA = 160
GN_EPS = 64e-5

LANES = 128
VMEM_LIMIT_BYTES = 56 * 1024 * 1024

FFN_TM = 512
FFN_CK = 1408
PROJ_TM = 512
ATT_T = 256
SEL_TQ = 256
GATE_ROWS = 16
SUM_ROWS = 16
LOG2E = 1.4426950408889634
RW_C = 64
RW_Q = 128
RW_NCH = 16
RW_GROUP = 16


def _cparams(sem):
    return pltpu.CompilerParams(dimension_semantics=sem, vmem_limit_bytes=VMEM_LIMIT_BYTES)


def _rms(x, g):
    ms = jnp.mean(x * x, axis=-1, keepdims=True)
    return x * lax.rsqrt(ms + RMS_EPS) * g


def _const_spec(shape):
    nd = len(shape)
    return pl.BlockSpec(shape, lambda *_: (0,) * nd, pipeline_mode=pl.Buffered(1))


def _dot(a, b):
    return jnp.dot(a, b, preferred_element_type=F32)


def _dot_nt(a, b):
    return lax.dot_general(a, b, (((1,), (1,)), ((), ())), preferred_element_type=F32)


def _dot_tn(a, b):
    return lax.dot_general(a, b, (((0,), (0,)), ((), ())), preferred_element_type=F32)


def _split2(x):
    hi = x.astype(BF16)
    lo = (x - hi.astype(F32)).astype(BF16)
    return hi, lo


def _split3(x):
    hi = x.astype(BF16)
    r1 = x - hi.astype(F32)
    mid = r1.astype(BF16)
    lo = (r1 - mid.astype(F32)).astype(BF16)
    return hi, mid, lo


def _ffn_body(x_ref, gpre_ref, wgu_ref, wd_ref, gpost_ref, o_ref):
    x = x_ref[...]
    xn = _rms(x, gpre_ref[...]).astype(BF16)
    acc = None
    for c in range(D_FF // FFN_CK):
        lo = c * FFN_CK
        gate = _dot(xn, wgu_ref[:, lo:lo + FFN_CK])
        up = _dot(xn, wgu_ref[:, D_FF + lo:D_FF + lo + FFN_CK])
        act = (gate * jax.nn.sigmoid(gate) * up).astype(BF16)
        part = _dot(act, wd_ref[lo:lo + FFN_CK, :])
        acc = part if acc is None else acc + part
    o_ref[...] = x + HALF_STEP * _rms(acc, gpost_ref[...])


def _ffn(h2, g_pre, w_gu, w_down, g_post):
    t = h2.shape[0]
    tm = min(FFN_TM, t)
    return pl.pallas_call(
        _ffn_body,
        out_shape=jax.ShapeDtypeStruct((t, D_MODEL), F32),
        grid=(t // tm,),
        in_specs=[
            pl.BlockSpec((tm, D_MODEL), lambda i: (i, 0)),
            _const_spec((1, D_MODEL)),
            _const_spec((D_MODEL, 2 * D_FF)),
            _const_spec((D_FF, D_MODEL)),
            _const_spec((1, D_MODEL)),
        ],
        out_specs=pl.BlockSpec((tm, D_MODEL), lambda i: (i, 0)),
        compiler_params=_cparams(("parallel",)),
        name="ffn",
    )(h2, g_pre.reshape(1, D_MODEL), w_gu.astype(BF16), w_down.astype(BF16), g_post.reshape(1, D_MODEL))


def _nsa_proj_body(h_ref, g_ref, wqt_ref, wc_ref, wk_ref, wvt_ref, wgt_ref,
                   qt_ref, kc_ref, vc_ref, ks_ref, kw_ref, vst_ref, vwt_ref, gt_ref):
    tm = h_ref.shape[1]
    u = _rms(h_ref[0], g_ref[...]).astype(BF16)
    qt_ref[0] = (_dot_nt(wqt_ref[...], u) * (HD ** -0.5 * LOG2E)).astype(BF16)
    c = _dot(u, wc_ref[...])
    kc_ref[0] = c[:, :KV_W]
    vc_ref[0] = c[:, KV_W:]
    kk = _dot(u, wk_ref[...])
    t0 = pl.program_id(1) * tm
    lane = lax.broadcasted_iota(jnp.int32, (tm, LANES), 1)
    blk = lax.shift_right_logical(t0 + lax.broadcasted_iota(jnp.int32, (tm, LANES), 0), 6)
    onehot = jnp.where(lane - HD == blk, 1.0, 0.0)
    for g in range(N_KV):
        ks_ref[0, g] = (kk[:, g * LANES:(g + 1) * LANES] + onehot).astype(BF16)
        kw_ref[0, g] = kk[:, (N_KV + g) * LANES:(N_KV + g + 1) * LANES].astype(BF16)
    vt = _dot_nt(wvt_ref[...], u)
    tk = vst_ref.shape[4]
    for j in range(tm // tk):
        cols = slice(j * tk, (j + 1) * tk)
        vst_ref[0, :, j] = vt[:KV_W, cols].reshape(N_KV, HD, tk).astype(BF16)
        vwt_ref[0, :, j] = vt[KV_W:, cols].reshape(N_KV, HD, tk).astype(BF16)
    gt_ref[0] = jax.nn.sigmoid(_dot_nt(wgt_ref[...], u)).reshape(N_KV, GATE_ROWS, tm)


def _nsa_proj(h, g_pre, w_in):
    b, s, _ = h.shape
    tm = min(PROJ_TM, s)
    tk = min(ATT_T, s)
    nq = N_HEADS * HD
    assert s // SEL_BLOCK <= HD
    col = lambda i: w_in[:, nq + i * KV_W:nq + (i + 1) * KV_W]
    w_qt = w_in[:, :nq].T.astype(BF16)
    w_c = jnp.concatenate([col(0), col(1)], axis=1).astype(BF16)
    pad_heads = lambda w: jnp.pad(w.reshape(D_MODEL, N_KV, HD), ((0, 0), (0, 0), (0, LANES - HD))).reshape(D_MODEL, N_KV * LANES)
    w_k = jnp.concatenate([pad_heads(col(2)), pad_heads(col(4))], axis=1).astype(BF16)
    w_vt = jnp.concatenate([col(3), col(5)], axis=1).T.astype(BF16)
    w_gl = w_in[:, nq + 6 * KV_W:].reshape(D_MODEL, N_KV, N_GRP * 3)
    w_gt = jnp.pad(w_gl, ((0, 0), (0, 0), (0, GATE_ROWS - N_GRP * 3))).reshape(D_MODEL, N_KV * GATE_ROWS).T.astype(BF16)
    tok = lambda w: pl.BlockSpec((1, tm, w), lambda bi, i: (bi, i, 0))
    key = jax.ShapeDtypeStruct((b, N_KV, s, LANES), BF16)
    key_spec = pl.BlockSpec((1, N_KV, tm, LANES), lambda bi, i: (bi, 0, i, 0))
    valt = jax.ShapeDtypeStruct((b, N_KV, s // tk, HD, tk), BF16)
    valt_spec = pl.BlockSpec((1, N_KV, tm // tk, HD, tk), lambda bi, i: (bi, 0, i, 0, 0))
    return pl.pallas_call(
        _nsa_proj_body,
        out_shape=(
            jax.ShapeDtypeStruct((b, nq, s), BF16),
            jax.ShapeDtypeStruct((b, s, KV_W), F32),
            jax.ShapeDtypeStruct((b, s, KV_W), F32),
            key, key, valt, valt,
            jax.ShapeDtypeStruct((b, N_KV, GATE_ROWS, s), F32),
        ),
        grid=(b, s // tm),
        in_specs=[
            tok(D_MODEL),
            _const_spec((1, D_MODEL)),
            _const_spec((nq, D_MODEL)),
            _const_spec((D_MODEL, 2 * KV_W)),
            _const_spec((D_MODEL, 2 * N_KV * LANES)),
            _const_spec((2 * KV_W, D_MODEL)),
            _const_spec((N_KV * GATE_ROWS, D_MODEL)),
        ],
        out_specs=(
            pl.BlockSpec((1, nq, tm), lambda bi, i: (bi, 0, i)),
            tok(KV_W), tok(KV_W), key_spec, key_spec, valt_spec, valt_spec,
            pl.BlockSpec((1, N_KV, GATE_ROWS, tm), lambda bi, i: (bi, 0, 0, i)),
        ),
        compiler_params=_cparams(("parallel", "parallel")),
        name="nsa_proj",
    )(h, g_pre.reshape(1, D_MODEL), w_qt, w_c, w_k, w_vt, w_gt)


def _compress_body(x_ref, pea_ref, peb_ref, wa_ref, wb_ref, w2_ref, o_ref, *, transposed):
    x = x_ref[0]
    ha = _dot((x + pea_ref[...]).astype(BF16), wa_ref[...])
    hb = _dot((x + peb_ref[...]).astype(BF16), wb_ref[...])
    n = ha.shape[0]
    hid = ha + pltpu.roll(hb, n - 1, 0)
    hid = (hid * jax.nn.sigmoid(hid)).astype(BF16)
    if transposed:
        o_ref[0] = _dot_nt(w2_ref[...], hid).reshape(N_KV, HD, n).astype(BF16)
    else:
        out = _dot(hid, w2_ref[...]).astype(BF16)
        for g in range(N_KV):
            o_ref[0, g] = out[:, g * HD:(g + 1) * HD]


def _compress(t, pe, w1, w2, transposed):
    b, s, _ = t.shape
    half = CMP_BLOCK // 2
    rows = s // half
    x = t.reshape(b, rows, half * KV_W)
    eye = jnp.eye(N_KV, dtype=F32)
    wbd = jnp.einsum("ldc,gh->lgdhc", w1, eye).reshape(CMP_BLOCK, KV_W, KV_W)
    wa = wbd[:half].reshape(half * KV_W, KV_W).astype(BF16)
    wb = wbd[half:].reshape(half * KV_W, KV_W).astype(BF16)
    pe_t = jnp.broadcast_to(pe[:, None, :], (CMP_BLOCK, N_KV, HD))
    pea = pe_t[:half].reshape(1, half * KV_W)
    peb = pe_t[half:].reshape(1, half * KV_W)
    w2bd = jnp.einsum("cd,gh->gchd", w2, eye).reshape(KV_W, KV_W)
    if transposed:
        w2bd = w2bd.T
        out_shape, out_block = (b, N_KV, HD, rows), (1, N_KV, HD, rows)
    else:
        out_shape, out_block = (b, N_KV, rows, HD), (1, N_KV, rows, HD)
    return pl.pallas_call(
        functools.partial(_compress_body, transposed=transposed),
        out_shape=jax.ShapeDtypeStruct(out_shape, BF16),
        grid=(b,),
        in_specs=[
            pl.BlockSpec((1, rows, half * KV_W), lambda bi: (bi, 0, 0)),
            _const_spec((1, half * KV_W)),
            _const_spec((1, half * KV_W)),
            _const_spec((half * KV_W, KV_W)),
            _const_spec((half * KV_W, KV_W)),
            _const_spec((KV_W, KV_W)),
        ],
        out_specs=pl.BlockSpec(out_block, lambda bi: (bi, 0, 0, 0)),
        compiler_params=_cparams(("parallel",)),
        name="nsa_compress",
    )(x, pea, peb, wa, wb, w2bd.astype(BF16))


def _alibi_slope(h):
    return float(2.0 ** (-8.0 * (h + 1) / N_HEADS))


def _cmp_scores(kc, qh, slope2, distf, valid):
    s = jnp.where(valid, _dot(kc, qh) - slope2 * distf, MASK_VALUE)
    m = jnp.max(s, axis=0, keepdims=True)
    p = jnp.where(valid, jnp.exp2(s - m), 0.0)
    return p, jnp.sum(p, axis=0, keepdims=True)


def _cmp_geometry(n_cmp_rows, tq, t0):
    n_io = lax.broadcasted_iota(jnp.int32, (n_cmp_rows, tq), 0)
    t_io = t0 + lax.broadcasted_iota(jnp.int32, (n_cmp_rows, tq), 1)
    dist = t_io - (n_io * CMP_STRIDE + (CMP_BLOCK - 1))
    return dist.astype(F32), (dist >= 0) & (n_io < n_cmp_rows - 1)


def _select_body(qt_ref, kc_ref, c2s_ref, mneg_ref, *, n_sel, n_cmp_rows):
    tq = qt_ref.shape[2]
    t0 = pl.program_id(1) * tq
    distf, valid = _cmp_geometry(n_cmp_rows, tq, t0)
    s_io = lax.broadcasted_iota(jnp.int32, (n_sel, tq), 0)
    tt = t0 + lax.broadcasted_iota(jnp.int32, (n_sel, tq), 1)
    cur = lax.shift_right_logical(tt, 6)
    forced = (s_io == 0) | (s_io == cur) | (s_io == cur - 1)
    visible = s_io * SEL_BLOCK <= tt

    for g in range(N_KV):
        kc = kc_ref[0, g]
        imp = jnp.zeros((n_sel, tq), F32)
        for r in range(N_GRP):
            hh = g * N_GRP + r
            p, l = _cmp_scores(kc, qt_ref[0, hh * HD:(hh + 1) * HD, :], _alibi_slope(hh) * LOG2E, distf, valid)
            p = p * (1.0 / jnp.where(l > 0.0, l, 1.0))
            imp = imp + _dot(c2s_ref[...], p.astype(BF16))
        x = jnp.where(forced, FORCED_SCORE, jnp.where(visible, imp, -1.0))
        cnt = jnp.zeros((n_sel, tq), F32)
        for j in range(n_sel):
            xj = x[j:j + 1, :]
            ge = jnp.where(xj >= x, 1.0, 0.0)
            gt = jnp.where(xj > x, 1.0, 0.0)
            cnt = cnt + jnp.where(s_io > j, ge, gt)
        mneg_ref[0, g, 0:n_sel, :] = jnp.where(cnt < float(N_SELECT), 0.0, MASK_VALUE).astype(BF16)
        if n_sel < HD:
            mneg_ref[0, g, n_sel:HD, :] = jnp.zeros((HD - n_sel, tq), BF16)


def _select(qt, k_cmp):
    b, nq, s = qt.shape
    tq = min(SEL_TQ, s)
    n_sel = s // SEL_BLOCK
    rows = k_cmp.shape[2]
    n_cmp = (s - CMP_BLOCK) // CMP_STRIDE + 1
    assert rows == n_cmp + 1
    cs = np.arange(rows) * CMP_STRIDE
    bs = np.arange(n_sel) * SEL_BLOCK
    overlap = np.clip(np.minimum(cs[:, None] + CMP_BLOCK, bs[None, :] + SEL_BLOCK)
                      - np.maximum(cs[:, None], bs[None, :]), 0, None)
    c2s_t = jnp.asarray((overlap / CMP_BLOCK).T, BF16)
    return pl.pallas_call(
        functools.partial(_select_body, n_sel=n_sel, n_cmp_rows=rows),
        out_shape=jax.ShapeDtypeStruct((b, N_KV, HD, s), BF16),
        grid=(b, s // tq),
        in_specs=[
            pl.BlockSpec((1, nq, tq), lambda bi, i: (bi, 0, i)),
            pl.BlockSpec((1, N_KV, rows, HD), lambda bi, i: (bi, 0, 0, 0)),
            _const_spec((n_sel, rows)),
        ],
        out_specs=pl.BlockSpec((1, N_KV, HD, tq), lambda bi, i: (bi, 0, 0, i)),
        compiler_params=_cparams(("parallel", "parallel")),
        name="nsa_select",
    )(qt, k_cmp, c2s_t)


def _attn_body(slope_ref, qt_ref, ks_ref, kw_ref, vst_ref, vwt_ref, kc_ref, vct_ref, mneg_ref, gt_ref,
               o_ref, s_sc, bias_sc, *, n_cmp_rows):
    tq = qt_ref.shape[2]
    tk = tq
    g = pl.program_id(1)
    qi = pl.program_id(2)
    t0 = qi * tq
    slopes = [slope_ref[g * N_GRP + r] for r in range(N_GRP)]
    q_rows = [qt_ref[0, r * HD:(r + 1) * HD, :] for r in range(N_GRP)]
    qk = [jnp.concatenate([q_rows[r], mneg_ref[0, 0]], axis=0) for r in range(N_GRP)]

    @pl.when(qi == 0)
    def _():
        key_io = lax.broadcasted_iota(jnp.int32, (tk, tq), 0)
        rel = key_io - lax.broadcasted_iota(jnp.int32, (tk, tq), 1)
        key_f = key_io.astype(F32)
        causal = jnp.where(rel <= 0, 0.0, MASK_VALUE)
        oldest = jnp.where(rel > 0, 0.0, MASK_VALUE)
        for r in range(N_GRP):
            base = slopes[r] * key_f
            bias_sc[0, r] = base
            bias_sc[1, r] = base + causal
            bias_sc[2, r] = base + oldest

    ones_rows = jnp.ones((SUM_ROWS, tk), BF16)

    def offsets(c):
        cf = ((c - qi) * tk).astype(F32)
        return [slopes[r] * cf for r in range(N_GRP)]

    def score_stage(c, k_ref, window, slot):
        kblk = k_ref[0, 0, pl.ds(pl.multiple_of(c * tk, tk), tk), :]
        variant = jnp.where(c == qi, 1, 0)
        if window:
            variant = jnp.where(c == qi - WINDOW // tk, 2, variant)
        offs = offsets(c)
        mcs = []
        for r in range(N_GRP):
            s = _dot(kblk, qk[r]) + bias_sc[variant, r]
            s_sc[slot, r] = s
            mcs.append(jnp.max(s, axis=0, keepdims=True) + offs[r])
        return tuple(mcs)

    def value_stage(c, carry, mcs, vt_ref, slot):
        vt = jnp.concatenate([vt_ref[0, 0, c], ones_rows], axis=0)
        offs = offsets(c)
        out = []
        for r in range(N_GRP):
            m, acc = carry[r]
            m_new = jnp.maximum(m, mcs[r])
            p = jnp.exp2(s_sc[slot, r] - (m_new - offs[r]))
            out.append((m_new, jnp.exp2(m - m_new) * acc + _dot(vt, p.astype(BF16))))
        return tuple(out)

    def branch(k_ref, vt_ref, c_lo, window):
        init = tuple((jnp.full((1, tq), MASK_VALUE, F32), jnp.zeros((HD + SUM_ROWS, tq), F32))
                     for _ in range(N_GRP))
        steps = qi - c_lo

        def trips(n_chunks, first, count, state):
            def trip(k, state):
                carry, mcs = state
                c = first + n_chunks * k
                for i in range(n_chunks):
                    mcs_next = score_stage(c + i + 1, k_ref, window, (i + 1) % 2)
                    carry = value_stage(c + i, carry, mcs, vt_ref, i % 2)
                    mcs = mcs_next
                return carry, mcs
            return lax.fori_loop(0, count, trip, state)

        state = (init, score_stage(c_lo, k_ref, window, 0))
        if not window:
            state = trips(4, c_lo, steps // 4, state)
        done = 0 if window else (steps // 4) * 4
        carry, mcs0 = trips(2, c_lo + done, (steps - done) // 2, state)

        def tail_odd():
            mcs1 = score_stage(qi, k_ref, window, 1)
            return value_stage(qi, value_stage(qi - 1, carry, mcs0, vt_ref, 0), mcs1, vt_ref, 1)

        def tail_even():
            return value_stage(qi, carry, mcs0, vt_ref, 0)

        return lax.cond((steps & 1) == 1, tail_odd, tail_even)

    res_s = branch(ks_ref, vst_ref, jnp.int32(0), False)
    res_w = branch(kw_ref, vwt_ref, jnp.maximum(qi - WINDOW // tk, 0), True)

    distf, valid_c = _cmp_geometry(n_cmp_rows, tq, t0)
    kc = kc_ref[0, 0]
    vct = vct_ref[0, 0]
    gate = gt_ref[0, 0]
    for r in range(N_GRP):
        p, l = _cmp_scores(kc, q_rows[r], slopes[r], distf, valid_c)
        o_c = _dot(vct, p.astype(BF16)) / jnp.where(l > 0.0, l, 1.0)
        o_s = res_s[r][1][0:HD] / res_s[r][1][HD:HD + 1]
        o_w = res_w[r][1][0:HD] / res_w[r][1][HD:HD + 1]
        out = (gate[3 * r:3 * r + 1, :] * o_c + gate[3 * r + 1:3 * r + 2, :] * o_s
               + gate[3 * r + 2:3 * r + 3, :] * o_w)
        o_ref[0, r * HD:(r + 1) * HD, :] = out.astype(BF16)


def _attention(qt, ks, kw, vst, vwt, k_cmp, v_cmpt, mneg, gates_t):
    b, nq, s = qt.shape
    tq = vst.shape[4]
    assert WINDOW % tq == 0
    rows = k_cmp.shape[2]
    gw = N_GRP * HD
    slopes = jnp.asarray([_alibi_slope(h) * LOG2E for h in range(N_HEADS)], F32)
    keys = pl.BlockSpec((1, 1, s, LANES), lambda bi, g, i: (bi, g, 0, 0))
    vals = pl.BlockSpec((1, 1, s // tq, HD, tq), lambda bi, g, i: (bi, g, 0, 0, 0))
    return pl.pallas_call(
        functools.partial(_attn_body, n_cmp_rows=rows),
        out_shape=jax.ShapeDtypeStruct((b, nq, s), BF16),
        grid=(b, N_KV, s // tq),
        in_specs=[
            pl.BlockSpec(memory_space=pltpu.SMEM),
            pl.BlockSpec((1, gw, tq), lambda bi, g, i: (bi, g, i)),
            keys, keys, vals, vals,
            pl.BlockSpec((1, 1, rows, HD), lambda bi, g, i: (bi, g, 0, 0)),
            pl.BlockSpec((1, 1, HD, rows), lambda bi, g, i: (bi, g, 0, 0)),
            pl.BlockSpec((1, 1, HD, tq), lambda bi, g, i: (bi, g, 0, i)),
            pl.BlockSpec((1, 1, GATE_ROWS, tq), lambda bi, g, i: (bi, g, 0, i)),
        ],
        out_specs=pl.BlockSpec((1, gw, tq), lambda bi, g, i: (bi, g, i)),
        scratch_shapes=[pltpu.VMEM((2, N_GRP, tq, tq), F32), pltpu.VMEM((3, N_GRP, tq, tq), F32)],
        compiler_params=_cparams(("parallel", "parallel", "arbitrary")),
        name="nsa_attention",
    )(slopes, qt, ks, kw, vst, vwt, k_cmp, v_cmpt, mneg, gates_t)


def _outproj_t_body(yt_ref, h_ref, w_ref, g_ref, o_ref):
    o_ref[0] = h_ref[0] + _rms(_dot_tn(yt_ref[0], w_ref[...]), g_ref[...])


def _outproj_t(yt, h, w_out, g_post):
    b, s, d = h.shape
    tm = min(PROJ_TM, s)
    return pl.pallas_call(
        _outproj_t_body,
        out_shape=jax.ShapeDtypeStruct((b, s, d), F32),
        grid=(b, s // tm),
        in_specs=[
            pl.BlockSpec((1, d, tm), lambda bi, i: (bi, 0, i)),
            pl.BlockSpec((1, tm, d), lambda bi, i: (bi, i, 0)),
            _const_spec((d, d)),
            _const_spec((1, d)),
        ],
        out_specs=pl.BlockSpec((1, tm, d), lambda bi, i: (bi, i, 0)),
        compiler_params=_cparams(("parallel", "parallel")),
        name="outproj_t",
    )(yt, h, w_out.astype(BF16), g_post.reshape(1, d))


def _nsa_mixer(h, g_pre, w_in, pe_k, w_ck1, w_ck2, pe_v, w_cv1, w_cv2, w_out, g_post):
    qt, kc, vc, ks, kw, vst, vwt, gates_t = _nsa_proj(h, g_pre, w_in)
    k_cmp = _compress(kc, pe_k, w_ck1, w_ck2, transposed=False)
    v_cmpt = _compress(vc, pe_v, w_cv1, w_cv2, transposed=True)
    mneg = _select(qt, k_cmp)
    out_t = _attention(qt, ks, kw, vst, vwt, k_cmp, v_cmpt, mneg, gates_t)
    return _outproj_t(out_t, h, w_out, g_post)


def _head_sum(x, sel_ref):
    return _dot(x.astype(BF16), sel_ref[...])


def _head_expand(x, selt_ref):
    hi, lo = _split2(x)
    return _dot(hi, selt_ref[...]) + _dot(lo, selt_ref[...])


def _softplus(x):
    return jnp.maximum(x, 0.0) + jnp.log(1.0 + jnp.exp(-jnp.abs(x)))


def _rw_proj_body(h_ref, hp_ref, g_ref, mu_ref, vec_ref, wr_ref, wk_ref, wv_ref, wd_ref, wa_ref, wg_ref,
                  w2d_ref, w2a_ref, w2g_ref, sel_ref, selt_ref,
                  r_ref, lw_ref, k_ref, v_ref, a_ref, b_ref, gg_ref):
    tm = h_ref.shape[1]
    u = _rms(h_ref[0], g_ref[...])
    prev = _rms(hp_ref[0], g_ref[...])[7:8, :]
    prev = jnp.where(pl.program_id(1) == 0, 0.0, prev)
    row = lax.broadcasted_iota(jnp.int32, (tm, D_MODEL), 0)
    u_prev = jnp.where(row == 0, prev, pltpu.roll(u, 1, 0))
    xx = u_prev - u

    def mix(i):
        return (u + xx * mu_ref[i:i + 1, :]).astype(BF16)

    w0, a0, k_k, k_a = (vec_ref[i:i + 1, :] for i in range(4))
    r = _dot(mix(0), wr_ref[...])
    k = _dot(mix(1), wk_ref[...])
    v = _dot(mix(2), wv_ref[...])
    d1 = jnp.tanh(_dot(mix(3), wd_ref[...])).astype(BF16)
    a1 = _dot(mix(4), wa_ref[...]).astype(BF16)
    g1 = jax.nn.sigmoid(_dot(mix(5), wg_ref[...])).astype(BF16)
    w = -_softplus(-(w0 + _dot(d1, w2d_ref[...]))) - 0.5
    alpha = jax.nn.sigmoid(a0 + _dot(a1, w2a_ref[...]))
    kk = k * k_k
    norm = jnp.sqrt(_head_sum(kk * kk, sel_ref))
    kk = kk * _head_expand(1.0 / jnp.maximum(norm, 1e-12), selt_ref)
    r_ref[0] = r.astype(BF16)
    lw_ref[0] = -jnp.exp(w)
    k_ref[0] = (k * (1.0 + (alpha - 1.0) * k_a)).astype(BF16)
    v_ref[0] = v.astype(BF16)
    a_ref[0] = (-kk).astype(BF16)
    b_ref[0] = (kk * alpha).astype(BF16)
    gg_ref[0] = _dot(g1, w2g_ref[...]).astype(BF16)


def _head_selectors():
    lane_head = np.arange(D_MODEL) // HD
    sel = (lane_head[:, None] == np.arange(LANES)[None, :]).astype(np.float32)
    return jnp.asarray(sel, BF16), jnp.asarray(sel.T, BF16)


def _pad_cols(w, n):
    return jnp.pad(w, ((0, 0), (0, n - w.shape[1])))


def _pad_rows(w, n):
    return jnp.pad(w, ((0, n - w.shape[0]), (0, 0)))


def _rw_proj(h, g_pre, mu, w_in, w0, w_w2, a0, w_a2, w_g2, k_k, k_a):
    b, s, d = h.shape
    tm = min(PROJ_TM, s)
    offs = np.cumsum((0, d, d, d, DECAY_LORA, AAA_LORA, GATE_LORA))
    cols = [w_in[:, int(offs[i]):int(offs[i + 1])] for i in range(6)]
    ld, lg = LANES, 2 * LANES
    wr, wk, wv = (c.astype(BF16) for c in cols[:3])
    wd = _pad_cols(cols[3], ld).astype(BF16)
    wa = _pad_cols(cols[4], ld).astype(BF16)
    wg = _pad_cols(cols[5], lg).astype(BF16)
    w2d = _pad_rows(w_w2, ld).astype(BF16)
    w2a = _pad_rows(w_a2, ld).astype(BF16)
    w2g = _pad_rows(w_g2, lg).astype(BF16)
    mu8 = _pad_rows(mu, 8)
    vecs = _pad_rows(jnp.stack([w0, a0, k_k, k_a]), 8)
    sel, selt = _head_selectors()
    tok = pl.BlockSpec((1, tm, d), lambda bi, i: (bi, i, 0))
    out = lambda dt: jax.ShapeDtypeStruct((b, s, d), dt)
    return pl.pallas_call(
        _rw_proj_body,
        out_shape=(out(BF16), out(F32), out(BF16), out(BF16), out(BF16), out(BF16), out(BF16)),
        grid=(b, s // tm),
        in_specs=[
            tok,
            pl.BlockSpec((1, 8, d), lambda bi, i: (bi, jnp.maximum(i * (tm // 8) - 1, 0), 0)),
            _const_spec((1, d)), _const_spec((8, d)), _const_spec((8, d)),
            _const_spec((d, d)), _const_spec((d, d)), _const_spec((d, d)),
            _const_spec((d, ld)), _const_spec((d, ld)), _const_spec((d, lg)),
            _const_spec((ld, d)), _const_spec((ld, d)), _const_spec((lg, d)),
            _const_spec((d, LANES)), _const_spec((LANES, d)),
        ],
        out_specs=(tok,) * 7,
        compiler_params=_cparams(("parallel", "parallel")),
        name="rwkv_proj",
    )(h, h, g_pre.reshape(1, d), mu8, vecs, wr, wk, wv, wd, wa, wg, w2d, w2a, w2g, sel, selt)


def _rw_prep_body(*refs):
    for first in range(0, refs[1].shape[1] // RW_C, RW_GROUP):
        _rw_prep_group(*refs, first=first)


def _rw_prep_group(ltri_ref, r_ref, lw_ref, k_ref, v_ref, a_ref, b_ref,
                   rhat_ref, y1_ref, g_ref, n_ref, *, first):
    c = RW_C
    q = RW_Q
    nh = q // HD
    nch = RW_GROUP
    lane_head = lax.shift_right_logical(lax.broadcasted_iota(jnp.int32, (c, q), 1), 6)
    ri = lax.broadcasted_iota(jnp.int32, (q, q), 0)
    ci = lax.broadcasted_iota(jnp.int32, (q, q), 1)
    same_head = lax.shift_right_logical(ri, 6) == lax.shift_right_logical(ci, 6)
    strict_bd = same_head & ((ci & (c - 1)) < (ri & (c - 1)))
    eye = ri == ci
    t_io = lax.broadcasted_iota(jnp.int32, (c, q), 0)
    j_io = lax.broadcasted_iota(jnp.int32, (c, q), 1) & (c - 1)
    strict_ls = j_io < t_io
    incl_ls = j_io <= t_io
    rows = [slice((first + i) * c, (first + i + 1) * c) for i in range(nch)]

    def each(f):
        return [f(i) for i in range(nch)]

    def expand(x):
        return jnp.concatenate([jnp.where(lane_head == hh, x, 0.0) for hh in range(nh)], axis=0).astype(BF16)

    def collapse(x):
        out = x[0:c]
        for hh in range(1, nh):
            out = out + x[hh * c:(hh + 1) * c]
        return out

    lw = each(lambda i: lw_ref[0, rows[i], :])
    parts = each(lambda i: _split3(lw[i]))
    cum = each(lambda i: _dot(ltri_ref[...], parts[i][0]) + _dot(ltri_ref[...], parts[i][1])
               + _dot(ltri_ref[...], parts[i][2]))
    cum_c = each(lambda i: cum[i][c - 1:c, :])
    e_inv = each(lambda i: jnp.exp(-cum[i]))
    e_rem = each(lambda i: jnp.exp(cum_c[i] - cum[i]))
    at = each(lambda i: a_ref[0, rows[i], :] * jnp.exp(cum[i] - lw[i]))
    rt = each(lambda i: r_ref[0, rows[i], :] * jnp.exp(cum[i]))
    x4 = each(lambda i: expand(at[i]))
    b4 = each(lambda i: expand(b_ref[0, rows[i], :] * e_inv[i]))
    k4 = each(lambda i: expand(k_ref[0, rows[i], :] * e_inv[i]))
    v4 = each(lambda i: expand(v_ref[0, rows[i], :]))
    bbar = each(lambda i: (b_ref[0, rows[i], :] * e_rem[i]).astype(BF16))
    kbar = each(lambda i: (k_ref[0, rows[i], :] * e_rem[i]).astype(BF16))

    lbd = each(lambda i: jnp.where(strict_bd, _dot_nt(x4[i], b4[i]), 0.0))
    tbd = each(lambda i: jnp.where(eye, 1.0, lbd[i]))
    lb = each(lambda i: lbd[i].astype(BF16))
    p = each(lambda i: _dot(lb[i], lb[i]))
    n_lvl = int(np.log2(c)) - 1
    for lvl in range(n_lvl):
        pb = each(lambda i: p[i].astype(BF16))
        tbd = each(lambda i: tbd[i] + _dot(pb[i], tbd[i].astype(BF16)))
        if lvl + 1 < n_lvl:
            p = each(lambda i: _dot(pb[i], pb[i]))
    t_ls = each(lambda i: collapse(tbd[i]).astype(BF16))

    ar = each(lambda i: _dot_nt(jnp.concatenate([at[i], rt[i]], axis=0).astype(BF16), k4[i]))
    a_ak = each(lambda i: jnp.where(strict_ls, ar[i][0:c], 0.0).astype(BF16))
    a_rk = each(lambda i: jnp.where(incl_ls, ar[i][c:2 * c], 0.0).astype(BF16))
    a_rb = each(lambda i: jnp.where(incl_ls, _dot_nt(rt[i].astype(BF16), b4[i]), 0.0).astype(BF16))

    g1 = each(lambda i: _dot(a_ak[i], v4[i]))
    y0 = each(lambda i: _dot(a_rk[i], v4[i]))
    u0 = each(lambda i: _dot(t_ls[i], expand(g1[i])))
    ahat = each(lambda i: _dot(t_ls[i], x4[i]))
    rhat = each(lambda i: rt[i] + _dot(a_rb[i], expand(ahat[i])))
    y1 = each(lambda i: y0[i] + _dot(a_rb[i], expand(u0[i])))

    gm = each(lambda i: _dot_tn(bbar[i], ahat[i].astype(BF16)))
    nm = each(lambda i: _dot_tn(jnp.concatenate([bbar[i], kbar[i]], axis=0),
                                jnp.concatenate([u0[i].astype(BF16), v_ref[0, rows[i], :]], axis=0)))
    for i in range(nch):
        gmi = jnp.where(same_head, gm[i], 0.0) + jnp.where(eye, jnp.exp(cum_c[i]), 0.0)
        rhat_ref[0, rows[i], :] = rhat[i].astype(BF16)
        y1_ref[0, rows[i], :] = y1[i].astype(BF16)
        g_ref[0, rows[i], :] = collapse(gmi).astype(BF16)
        n_ref[0, rows[i], :] = collapse(jnp.where(same_head, nm[i], 0.0)).astype(BF16)


def _rw_prep(r, lw, k, v, a, b):
    bsz, s, d = r.shape
    rows = min(RW_C * RW_NCH, s)
    ltri = jnp.asarray(np.tril(np.ones((RW_C, RW_C), np.float32)), BF16)
    blk = pl.BlockSpec((1, rows, RW_Q), lambda bi, qi, j: (bi, j, qi))
    return pl.pallas_call(
        _rw_prep_body,
        out_shape=(jax.ShapeDtypeStruct((bsz, s, d), BF16),) * 4,
        grid=(bsz, d // RW_Q, s // rows),
        in_specs=[_const_spec((RW_C, RW_C))] + [blk] * 6,
        out_specs=(blk,) * 4,
        compiler_params=_cparams(("parallel", "parallel", "parallel")),
        name="rwkv_prep",
    )(ltri, r, lw, k, v, a, b)


def _rw_scan_body(rhat_ref, y1_ref, g_ref, n_ref, y_ref, h_sc):
    c = RW_C
    q = RW_Q
    nq = h_sc.shape[0]

    @pl.when(pl.program_id(1) == 0)
    def _():
        h_sc[...] = jnp.zeros(h_sc.shape, F32)

    ri = lax.broadcasted_iota(jnp.int32, (q, q), 0)
    ci = lax.broadcasted_iota(jnp.int32, (q, q), 1)
    same_head = lax.shift_right_logical(ri, 6) == lax.shift_right_logical(ci, 6)

    def step(ch, carry):
        rows = pl.ds(pl.multiple_of(ch * c, c), c)
        for qi in range(nq):
            lanes = slice(qi * q, (qi + 1) * q)
            hb = h_sc[qi].astype(BF16)
            y_ref[0, rows, lanes] = (_dot(rhat_ref[0, rows, lanes], hb) + y1_ref[0, rows, lanes]).astype(BF16)
            g_ls = g_ref[0, rows, lanes]
            n_ls = n_ref[0, rows, lanes]
            gbd = jnp.where(same_head, jnp.concatenate([g_ls] * (q // c), axis=0), 0.0)
            nbd = jnp.where(same_head, jnp.concatenate([n_ls] * (q // c), axis=0), 0.0)
            h_sc[qi] = _dot(gbd.astype(BF16), hb) + nbd
        return carry

    lax.fori_loop(0, rhat_ref.shape[1] // c, step, 0)


def _rw_scan(rhat, y1, g, n):
    bsz, s, d = rhat.shape
    rows = min(512, s)
    blk = pl.BlockSpec((1, rows, d), lambda bi, j: (bi, j, 0))
    return pl.pallas_call(
        _rw_scan_body,
        out_shape=jax.ShapeDtypeStruct((bsz, s, d), BF16),
        grid=(bsz, s // rows),
        in_specs=[blk] * 4,
        out_specs=blk,
        scratch_shapes=[pltpu.VMEM((d // RW_Q, RW_Q, RW_Q), F32)],
        compiler_params=_cparams(("parallel", "arbitrary")),
        name="rwkv_scan",
    )(rhat, y1, g, n)


def _rw_post_body(y_ref, r_ref, k_ref, v_ref, gg_ref, h_ref, vec_ref, w_ref, gpost_ref, sel_ref, selt_ref, o_ref):
    gn_w, gn_b, r_k = (vec_ref[i:i + 1, :] for i in range(3))
    f32 = lambda ref: ref[...].astype(F32)
    y = f32(y_ref)
    inv_n = 1.0 / HD
    mean = _head_expand(_head_sum(y, sel_ref) * inv_n, selt_ref)
    yc = y - mean
    var = _head_sum(yc * yc, sel_ref) * inv_n
    yn = yc * _head_expand(lax.rsqrt(var + GN_EPS), selt_ref) * gn_w + gn_b
    bonus = _head_expand(_head_sum(f32(r_ref) * f32(k_ref) * r_k, sel_ref), selt_ref) * f32(v_ref)
    z = ((yn + bonus) * f32(gg_ref)).astype(BF16)
    o_ref[...] = h_ref[...] + _rms(_dot(z, w_ref[...]), gpost_ref[...])


def _rw_post(y, r, k, v, gg, h, gn_w, gn_b, r_k, w_out, g_post):
    t, d = h.shape
    tm = min(PROJ_TM, t)
    vecs = _pad_rows(jnp.stack([gn_w, gn_b, r_k.reshape(d)]), 8)
    sel, selt = _head_selectors()
    tok = pl.BlockSpec((tm, d), lambda i: (i, 0))
    return pl.pallas_call(
        _rw_post_body,
        out_shape=jax.ShapeDtypeStruct((t, d), F32),
        grid=(t // tm,),
        in_specs=[tok] * 6 + [_const_spec((8, d)), _const_spec((d, d)), _const_spec((1, d)),
                              _const_spec((d, LANES)), _const_spec((LANES, d))],
        out_specs=tok,
        compiler_params=_cparams(("parallel",)),
        name="rwkv_post",
    )(y, r, k, v, gg, h, vecs, w_out.astype(BF16), g_post.reshape(1, d), sel, selt)


def _rwkv_mixer(h, g_pre, mu, w_in, w0, w_w2, a0, w_a2, w_g2, k_k, k_a, r_k, gn_w, gn_b, w_out, g_post):
    b, s, d = h.shape
    r, lw, k, v, a, bb, gg = _rw_proj(h, g_pre, mu, w_in, w0, w_w2, a0, w_a2, w_g2, k_k, k_a)
    rhat, y1, g, n = _rw_prep(r, lw, k, v, a, bb)
    y = _rw_scan(rhat, y1, g, n)
    f2 = lambda x: x.reshape(b * s, d)
    return _rw_post(f2(y), f2(r), f2(k), f2(v), f2(gg), f2(h), gn_w, gn_b, r_k, w_out, g_post).reshape(b, s, d)


def kernel(x, ffn1_norm_pre, ffn1_w_gu, ffn1_w_down, ffn1_norm_post, mix_norm_pre, nsa_w_in, nsa_pe_k,
           nsa_w_ck1, nsa_w_ck2, nsa_pe_v, nsa_w_cv1, nsa_w_cv2, nsa_w_out, rwkv_mu, rwkv_w_in, rwkv_w0,
           rwkv_w_w2, rwkv_a0, rwkv_w_a2, rwkv_w_g2, rwkv_k_k, rwkv_k_a, rwkv_r_k, rwkv_gn_w, rwkv_gn_b,
           rwkv_w_out, mix_norm_post, ffn2_norm_pre, ffn2_w_gu, ffn2_w_down, ffn2_norm_post):
    b, s, d = x.shape
    flat = lambda t: t.reshape(b * s, d)
    cube = lambda t: t.reshape(b, s, d)
    h = x
    depth = ffn1_norm_pre.shape[0]
    for i in range(depth):
        h = cube(_ffn(flat(h), ffn1_norm_pre[i], ffn1_w_gu[i], ffn1_w_down[i], ffn1_norm_post[i]))
        j = i // 2
        if i % 2 == 0:
            h = _nsa_mixer(h, mix_norm_pre[i], nsa_w_in[j], nsa_pe_k[j], nsa_w_ck1[j], nsa_w_ck2[j],
                           nsa_pe_v[j], nsa_w_cv1[j], nsa_w_cv2[j], nsa_w_out[j], mix_norm_post[i])
        else:
            h = _rwkv_mixer(h, mix_norm_pre[i], rwkv_mu[j], rwkv_w_in[j], rwkv_w0[j], rwkv_w_w2[j],
                            rwkv_a0[j], rwkv_w_a2[j], rwkv_w_g2[j], rwkv_k_k[j], rwkv_k_a[j], rwkv_r_k[j],
                            rwkv_gn_w[j], rwkv_gn_b[j], rwkv_w_out[j], mix_norm_post[i])
        h = cube(_ffn(flat(h), ffn2_norm_pre[i], ffn2_w_gu[i], ffn2_w_down[i], ffn2_norm_post[i]))
    return h
```

```python
import functools

import numpy as np
import jax
import jax.numpy as jnp
from jax import lax
from jax.experimental import pallas as pl
from jax.experimental.pallas import tpu as pltpu

F32 = jnp.float32
BF16 = jnp.bfloat16

D_MODEL = 1024
D_FF = 2816
HALF_STEP = 0.5
RMS_EPS = 1e-6
MASK_VALUE = -1e30

HD = 64
N_HEADS = 16
N_KV = 4
N_GRP = 4
KV_W = N_KV * HD
CMP_BLOCK = 32
CMP_STRIDE = 16
SEL_BLOCK = 64
N_SELECT = 16
WINDOW = 512
FORCED_SCORE = 1e4
N_GATE = 3 * N_HEADS

DECAY_LORA = 64
AAA_LORA = 64
GATE_LORA = 160
GN_EPS = 64e-5

LANES = 128
VMEM_LIMIT_BYTES = 56 * 1024 * 1024

FFN_TM = 512
FFN_CUTS = (0, 1536, 2816)
PROJ_TM = 512
ATT_T = 256
SEL_TQ = 256
GATE_ROWS = 16
SUM_ROWS = 16
LOG2E = 1.4426950408889634
RW_C = 64
RW_Q = 128
RW_NCH = 16
RW_GROUP = 16


def _cparams(sem):
    return pltpu.CompilerParams(dimension_semantics=sem, vmem_limit_bytes=VMEM_LIMIT_BYTES)


def _rms(x, g):
    ms = jnp.mean(x * x, axis=-1, keepdims=True)
    return x * lax.rsqrt(ms + RMS_EPS) * g


def _const_spec(shape):
    nd = len(shape)
    return pl.BlockSpec(shape, lambda *_: (0,) * nd, pipeline_mode=pl.Buffered(1))


def _dot(a, b):
    return jnp.dot(a, b, preferred_element_type=F32)


def _dot_nt(a, b):
    return lax.dot_general(a, b, (((1,), (1,)), ((), ())), preferred_element_type=F32)


def _dot_tn(a, b):
    return lax.dot_general(a, b, (((0,), (0,)), ((), ())), preferred_element_type=F32)


def _split2(x):
    hi = x.astype(BF16)
    lo = (x - hi.astype(F32)).astype(BF16)
    return hi, lo


def _split3(x):
    hi = x.astype(BF16)
    r1 = x - hi.astype(F32)
    mid = r1.astype(BF16)
    lo = (r1 - mid.astype(F32)).astype(BF16)
    return hi, mid, lo


def _ffn_body(x_ref, gpre_ref, wgu_ref, wd_ref, gpost_ref, o_ref):
    x = x_ref[...]
    xn = _rms(x, gpre_ref[...]).astype(BF16)
    acc = None
    for lo, hi in zip(FFN_CUTS[:-1], FFN_CUTS[1:]):
        gate = _dot(xn, wgu_ref[:, lo:hi])
        up = _dot(xn, wgu_ref[:, D_FF + lo:D_FF + hi])
        act = (gate * jax.nn.sigmoid(gate) * up).astype(BF16)
        part = _dot(act, wd_ref[lo:hi, :])
        acc = part if acc is None else acc + part
    o_ref[...] = x + HALF_STEP * _rms(acc, gpost_ref[...])


def _ffn(h2, g_pre, w_gu, w_down, g_post):
    t = h2.shape[0]
    tm = min(FFN_TM, t)
    return pl.pallas_call(
        _ffn_body,
        out_shape=jax.ShapeDtypeStruct((t, D_MODEL), F32),
        grid=(t // tm,),
        in_specs=[
            pl.BlockSpec((tm, D_MODEL), lambda i: (i, 0)),
            _const_spec((1, D_MODEL)),
            _const_spec((D_MODEL, 2 * D_FF)),
            _const_spec((D_FF, D_MODEL)),
            _const_spec((1, D_MODEL)),
        ],
        out_specs=pl.BlockSpec((tm, D_MODEL), lambda i: (i, 0)),
        compiler_params=_cparams(("parallel",)),
        name="ffn",
    )(h2, g_pre.reshape(1, D_MODEL), w_gu.astype(BF16), w_down.astype(BF16), g_post.reshape(1, D_MODEL))


def _nsa_proj_body(h_ref, g_ref, wqt_ref, wc_ref, wk_ref, wvt_ref, wgt_ref,
                   qt_ref, kc_ref, vc_ref, ks_ref, kw_ref, vst_ref, vwt_ref, gt_ref):
    tm = h_ref.shape[1]
    u = _rms(h_ref[0], g_ref[...]).astype(BF16)
    qt_ref[0] = (_dot_nt(wqt_ref[...], u) * (HD ** -0.5 * LOG2E)).astype(BF16)
    c = _dot(u, wc_ref[...])
    for j in range(KV_W // LANES):
        kc_ref[0, j] = c[:, j * LANES:(j + 1) * LANES]
        vc_ref[0, j] = c[:, KV_W + j * LANES:KV_W + (j + 1) * LANES]
    kk = _dot(u, wk_ref[...])
    t0 = pl.program_id(1) * tm
    lane = lax.broadcasted_iota(jnp.int32, (tm, LANES), 1)
    blk = lax.shift_right_logical(t0 + lax.broadcasted_iota(jnp.int32, (tm, LANES), 0), 6)
    onehot = jnp.where(lane - HD == blk, 1.0, 0.0)
    for g in range(N_KV):
        ks_ref[0, g] = (kk[:, g * LANES:(g + 1) * LANES] + onehot).astype(BF16)
        kw_ref[0, g] = kk[:, (N_KV + g) * LANES:(N_KV + g + 1) * LANES].astype(BF16)
    vt = _dot_nt(wvt_ref[...], u)
    tk = vst_ref.shape[4]
    for j in range(tm // tk):
        cols = slice(j * tk, (j + 1) * tk)
        vst_ref[0, :, j] = vt[:KV_W, cols].reshape(N_KV, HD, tk).astype(BF16)
        vwt_ref[0, :, j] = vt[KV_W:, cols].reshape(N_KV, HD, tk).astype(BF16)
    gt_ref[0] = jax.nn.sigmoid(_dot_nt(wgt_ref[...], u)).reshape(N_KV, GATE_ROWS, tm)


def _nsa_proj(h, g_pre, w_in):
    b, s, _ = h.shape
    tm = min(PROJ_TM, s)
    tk = min(ATT_T, s)
    nq = N_HEADS * HD
    assert s // SEL_BLOCK <= HD
    col = lambda i: w_in[:, nq + i * KV_W:nq + (i + 1) * KV_W]
    w_qt = w_in[:, :nq].T.astype(BF16)
    w_c = jnp.concatenate([col(0), col(1)], axis=1).astype(BF16)
    pad_heads = lambda w: jnp.pad(w.reshape(D_MODEL, N_KV, HD), ((0, 0), (0, 0), (0, LANES - HD))).reshape(D_MODEL, N_KV * LANES)
    w_k = jnp.concatenate([pad_heads(col(2)), pad_heads(col(4))], axis=1).astype(BF16)
    w_vt = jnp.concatenate([col(3), col(5)], axis=1).T.astype(BF16)
    w_gl = w_in[:, nq + 6 * KV_W:].reshape(D_MODEL, N_KV, N_GRP * 3)
    w_gt = jnp.pad(w_gl, ((0, 0), (0, 0), (0, GATE_ROWS - N_GRP * 3))).reshape(D_MODEL, N_KV * GATE_ROWS).T.astype(BF16)
    tok = lambda w: pl.BlockSpec((1, tm, w), lambda bi, i: (bi, i, 0))
    cmp_spec = pl.BlockSpec((1, KV_W // LANES, tm, LANES), lambda bi, i: (bi, 0, i, 0))
    key = jax.ShapeDtypeStruct((b, N_KV, s, LANES), BF16)
    key_spec = pl.BlockSpec((1, N_KV, tm, LANES), lambda bi, i: (bi, 0, i, 0))
    valt = jax.ShapeDtypeStruct((b, N_KV, s // tk, HD, tk), BF16)
    valt_spec = pl.BlockSpec((1, N_KV, tm // tk, HD, tk), lambda bi, i: (bi, 0, i, 0, 0))
    return pl.pallas_call(
        _nsa_proj_body,
        out_shape=(
            jax.ShapeDtypeStruct((b, nq, s), BF16),
            jax.ShapeDtypeStruct((b, KV_W // LANES, s, LANES), F32),
            jax.ShapeDtypeStruct((b, KV_W // LANES, s, LANES), F32),
            key, key, valt, valt,
            jax.ShapeDtypeStruct((b, N_KV, GATE_ROWS, s), F32),
        ),
        grid=(b, s // tm),
        in_specs=[
            tok(D_MODEL),
            _const_spec((1, D_MODEL)),
            _const_spec((nq, D_MODEL)),
            _const_spec((D_MODEL, 2 * KV_W)),
            _const_spec((D_MODEL, 2 * N_KV * LANES)),
            _const_spec((2 * KV_W, D_MODEL)),
            _const_spec((N_KV * GATE_ROWS, D_MODEL)),
        ],
        out_specs=(
            pl.BlockSpec((1, nq, tm), lambda bi, i: (bi, 0, i)),
            cmp_spec, cmp_spec, key_spec, key_spec, valt_spec, valt_spec,
            pl.BlockSpec((1, N_KV, GATE_ROWS, tm), lambda bi, i: (bi, 0, 0, i)),
        ),
        compiler_params=_cparams(("parallel", "parallel")),
        name="nsa_proj",
    )(h, g_pre.reshape(1, D_MODEL), w_qt, w_c, w_k, w_vt, w_gt)


def _compress_body(x_ref, pe_ref, w1_ref, w2_ref, o_ref, *, transposed):
    half = CMP_BLOCK // 2
    n = x_ref.shape[2] // half
    ha = hb = None
    for l in range(half):
        xl = jnp.concatenate([x_ref[0, j, pl.ds(l, n, stride=half), :] for j in range(x_ref.shape[1])], axis=-1)
        a = _dot((xl + pe_ref[l:l + 1, :]).astype(BF16), w1_ref[l])
        b = _dot((xl + pe_ref[half + l:half + l + 1, :]).astype(BF16), w1_ref[half + l])
        ha = a if ha is None else ha + a
        hb = b if hb is None else hb + b
    hid = ha + pltpu.roll(hb, n - 1, 0)
    hid = (hid * jax.nn.sigmoid(hid)).astype(BF16)
    if transposed:
        o_ref[0] = _dot_nt(w2_ref[...], hid).reshape(N_KV, HD, n).astype(BF16)
    else:
        out = _dot(hid, w2_ref[...]).astype(BF16)
        for g in range(N_KV):
            o_ref[0, g] = out[:, g * HD:(g + 1) * HD]


def _compress(t, pe, w1, w2, transposed):
    b, _, s, _ = t.shape
    rows = s // (CMP_BLOCK // 2)
    eye = jnp.eye(N_KV, dtype=F32)
    wbd = jnp.einsum("ldc,gh->lgdhc", w1, eye).reshape(CMP_BLOCK, KV_W, KV_W).astype(BF16)
    pe_t = jnp.broadcast_to(pe[:, None, :], (CMP_BLOCK, N_KV, HD)).reshape(CMP_BLOCK, KV_W)
    w2bd = jnp.einsum("cd,gh->gchd", w2, eye).reshape(KV_W, KV_W)
    if transposed:
        w2bd = w2bd.T
        out_shape, out_block = (b, N_KV, HD, rows), (1, N_KV, HD, rows)
    else:
        out_shape, out_block = (b, N_KV, rows, HD), (1, N_KV, rows, HD)
    return pl.pallas_call(
        functools.partial(_compress_body, transposed=transposed),
        out_shape=jax.ShapeDtypeStruct(out_shape, BF16),
        grid=(b,),
        in_specs=[
            pl.BlockSpec((1, KV_W // LANES, s, LANES), lambda bi: (bi, 0, 0, 0)),
            _const_spec((CMP_BLOCK, KV_W)),
            _const_spec((CMP_BLOCK, KV_W, KV_W)),
            _const_spec((KV_W, KV_W)),
        ],
        out_specs=pl.BlockSpec(out_block, lambda bi: (bi, 0, 0, 0)),
        compiler_params=_cparams(("parallel",)),
        name="nsa_compress",
    )(t, pe_t, wbd, w2bd.astype(BF16))


def _alibi_slope(h):
    return float(2.0 ** (-8.0 * (h + 1) / N_HEADS))


def _cmp_probs(kc, qh, slope2, end_f, valid):
    s = _dot(kc, qh) + jnp.where(valid, slope2 * end_f, MASK_VALUE)
    return jnp.exp2(s - jnp.max(s, axis=0, keepdims=True)).astype(BF16)


def _cmp_geometry(n_cmp_rows, tq, t0):
    n_io = lax.broadcasted_iota(jnp.int32, (n_cmp_rows, tq), 0)
    t_io = t0 + lax.broadcasted_iota(jnp.int32, (n_cmp_rows, tq), 1)
    end = n_io * CMP_STRIDE + (CMP_BLOCK - 1)
    return end.astype(F32), (t_io >= end) & (n_io < n_cmp_rows - 1)


def _select_body(qt_ref, kc_ref, c2s_ref, mneg_ref, *, n_sel, n_cmp_rows):
    tq = qt_ref.shape[2]
    t0 = pl.program_id(1) * tq
    end_f, valid = _cmp_geometry(n_cmp_rows, tq, t0)
    s_io = lax.broadcasted_iota(jnp.int32, (n_sel, tq), 0)
    tt = t0 + lax.broadcasted_iota(jnp.int32, (n_sel, tq), 1)
    cur = lax.shift_right_logical(tt, 6)
    forced = (s_io == 0) | (s_io == cur) | (s_io == cur - 1)
    visible = s_io * SEL_BLOCK <= tt

    for g in range(N_KV):
        kc = kc_ref[0, g]
        heads = range(g * N_GRP, (g + 1) * N_GRP)
        probs = [_cmp_probs(kc, qt_ref[0, hh * HD:(hh + 1) * HD, :], _alibi_slope(hh) * LOG2E, end_f, valid)
                 for hh in heads]
        mass = [_dot(c2s_ref[...], p) for p in probs]
        imp = sum(m[0:n_sel] * (1.0 / m[n_sel:n_sel + 1]) for m in mass)
        x = jnp.where(forced, FORCED_SCORE, jnp.where(visible, imp, -1.0))
        cnt = jnp.zeros((n_sel, tq), F32)
        for j in range(n_sel):
            xj = x[j:j + 1, :]
            ge = jnp.where(xj >= x, 1.0, 0.0)
            gt = jnp.where(xj > x, 1.0, 0.0)
            cnt = cnt + jnp.where(s_io > j, ge, gt)
        mneg_ref[0, g, 0:n_sel, :] = jnp.where(cnt < float(N_SELECT), 0.0, MASK_VALUE).astype(BF16)
        if n_sel < HD:
            mneg_ref[0, g, n_sel:HD, :] = jnp.zeros((HD - n_sel, tq), BF16)


def _select(qt, k_cmp):
    b, nq, s = qt.shape
    tq = min(SEL_TQ, s)
    n_sel = s // SEL_BLOCK
    rows = k_cmp.shape[2]
    n_cmp = (s - CMP_BLOCK) // CMP_STRIDE + 1
    assert rows == n_cmp + 1
    cs = np.arange(rows) * CMP_STRIDE
    bs = np.arange(n_sel) * SEL_BLOCK
    overlap = np.clip(np.minimum(cs[:, None] + CMP_BLOCK, bs[None, :] + SEL_BLOCK)
                      - np.maximum(cs[:, None], bs[None, :]), 0, None)
    c2s_t = jnp.asarray(np.concatenate([(overlap / CMP_BLOCK).T, np.ones((SUM_ROWS, rows))], axis=0), BF16)
    return pl.pallas_call(
        functools.partial(_select_body, n_sel=n_sel, n_cmp_rows=rows),
        out_shape=jax.ShapeDtypeStruct((b, N_KV, HD, s), BF16),
        grid=(b, s // tq),
        in_specs=[
            pl.BlockSpec((1, nq, tq), lambda bi, i: (bi, 0, i)),
            pl.BlockSpec((1, N_KV, rows, HD), lambda bi, i: (bi, 0, 0, 0)),
            _const_spec((n_sel + SUM_ROWS, rows)),
        ],
        out_specs=pl.BlockSpec((1, N_KV, HD, tq), lambda bi, i: (bi, 0, 0, i)),
        compiler_params=_cparams(("parallel", "parallel")),
        name="nsa_select",
    )(qt, k_cmp, c2s_t)


def _attn_body(slope_ref, qt_ref, ks_ref, kw_ref, vst_ref, vwt_ref, kc_ref, vct_ref, mneg_ref, gt_ref,
               o_ref, s_sc, bias_sc, *, n_cmp_rows):
    tq = qt_ref.shape[2]
    tk = tq
    g = pl.program_id(1)
    qi = pl.program_id(2)
    t0 = qi * tq
    slopes = [slope_ref[g * N_GRP + r] for r in range(N_GRP)]
    q_rows = [qt_ref[0, r * HD:(r + 1) * HD, :] for r in range(N_GRP)]
    qk = [jnp.concatenate([q_rows[r], mneg_ref[0, 0]], axis=0) for r in range(N_GRP)]

    @pl.when(qi == 0)
    def _():
        key_io = lax.broadcasted_iota(jnp.int32, (tk, tq), 0)
        rel = key_io - lax.broadcasted_iota(jnp.int32, (tk, tq), 1)
        key_f = key_io.astype(F32)
        causal = jnp.where(rel <= 0, 0.0, MASK_VALUE)
        oldest = jnp.where(rel > 0, 0.0, MASK_VALUE)
        for r in range(N_GRP):
            base = slopes[r] * key_f
            bias_sc[0, r] = base
            bias_sc[1, r] = base + causal
            bias_sc[2, r] = base + oldest

    ones_rows = jnp.ones((SUM_ROWS, tk), BF16)

    def offsets(c):
        cf = ((c - qi) * tk).astype(F32)
        return [slopes[r] * cf for r in range(N_GRP)]

    def score_stage(c, k_ref, window, slot):
        kblk = k_ref[0, 0, pl.ds(pl.multiple_of(c * tk, tk), tk), :]
        variant = jnp.where(c == qi, 1, 0)
        if window:
            variant = jnp.where(c == qi - WINDOW // tk, 2, variant)
        offs = offsets(c)
        mcs = []
        for r in range(N_GRP):
            s = _dot(kblk, qk[r]) + bias_sc[variant, r]
            s_sc[slot, r] = s
            mcs.append(jnp.max(s, axis=0, keepdims=True) + offs[r])
        return tuple(mcs)

    def value_stage(c, carry, mcs, vt_ref, slot):
        vt = jnp.concatenate([vt_ref[0, 0, c], ones_rows], axis=0)
        offs = offsets(c)
        out = []
        for r in range(N_GRP):
            m, acc = carry[r]
            m_new = jnp.maximum(m, mcs[r])
            p = jnp.exp2(s_sc[slot, r] - (m_new - offs[r]))
            out.append((m_new, jnp.exp2(m - m_new) * acc + _dot(vt, p.astype(BF16))))
        return tuple(out)

    def branch(k_ref, vt_ref, c_lo, window):
        init = tuple((jnp.full((1, tq), MASK_VALUE, F32), jnp.zeros((HD + SUM_ROWS, tq), F32))
                     for _ in range(N_GRP))
        steps = qi - c_lo

        def trips(n_chunks, first, count, state):
            def trip(k, state):
                carry, mcs = state
                c = first + n_chunks * k
                for i in range(n_chunks):
                    mcs_next = score_stage(c + i + 1, k_ref, window, (i + 1) % 2)
                    carry = value_stage(c + i, carry, mcs, vt_ref, i % 2)
                    mcs = mcs_next
                return carry, mcs
            return lax.fori_loop(0, count, trip, state)

        state = (init, score_stage(c_lo, k_ref, window, 0))
        if not window:
            state = trips(4, c_lo, steps // 4, state)
        done = 0 if window else (steps // 4) * 4
        carry, mcs0 = trips(2, c_lo + done, (steps - done) // 2, state)

        def tail_odd():
            mcs1 = score_stage(qi, k_ref, window, 1)
            return value_stage(qi, value_stage(qi - 1, carry, mcs0, vt_ref, 0), mcs1, vt_ref, 1)

        def tail_even():
            return value_stage(qi, carry, mcs0, vt_ref, 0)

        return lax.cond((steps & 1) == 1, tail_odd, tail_even)

    res_s = branch(ks_ref, vst_ref, jnp.int32(0), False)
    res_w = branch(kw_ref, vwt_ref, jnp.maximum(qi - WINDOW // tk, 0), True)

    end_f, valid_c = _cmp_geometry(n_cmp_rows, tq, t0)
    has_block = t0 + lax.broadcasted_iota(jnp.int32, (1, tq), 1) >= CMP_BLOCK - 1
    kc = kc_ref[0, 0]
    vct = jnp.concatenate([vct_ref[0, 0], jnp.ones((SUM_ROWS, n_cmp_rows), BF16)], axis=0)
    gate = gt_ref[0, 0]
    normed = lambda acc: acc[0:HD] * (1.0 / acc[HD:HD + 1])
    probs = [_cmp_probs(kc, q_rows[r], slopes[r], end_f, valid_c) for r in range(N_GRP)]
    acc_c = [_dot(vct, p) for p in probs]
    for r in range(N_GRP):
        o_c = jnp.where(has_block, normed(acc_c[r]), 0.0)
        o_s = normed(res_s[r][1])
        o_w = normed(res_w[r][1])
        out = (gate[3 * r:3 * r + 1, :] * o_c + gate[3 * r + 1:3 * r + 2, :] * o_s
               + gate[3 * r + 2:3 * r + 3, :] * o_w)
        o_ref[0, r * HD:(r + 1) * HD, :] = out.astype(BF16)


def _attention(qt, ks, kw, vst, vwt, k_cmp, v_cmpt, mneg, gates_t):
    b, nq, s = qt.shape
    tq = vst.shape[4]
    assert WINDOW % tq == 0
    rows = k_cmp.shape[2]
    gw = N_GRP * HD
    slopes = jnp.asarray([_alibi_slope(h) * LOG2E for h in range(N_HEADS)], F32)
    keys = pl.BlockSpec((1, 1, s, LANES), lambda bi, g, i: (bi, g, 0, 0))
    vals = pl.BlockSpec((1, 1, s // tq, HD, tq), lambda bi, g, i: (bi, g, 0, 0, 0))
    return pl.pallas_call(
        functools.partial(_attn_body, n_cmp_rows=rows),
        out_shape=jax.ShapeDtypeStruct((b, nq, s), BF16),
        grid=(b, N_KV, s // tq),
        in_specs=[
            pl.BlockSpec(memory_space=pltpu.SMEM),
            pl.BlockSpec((1, gw, tq), lambda bi, g, i: (bi, g, i)),
            keys, keys, vals, vals,
            pl.BlockSpec((1, 1, rows, HD), lambda bi, g, i: (bi, g, 0, 0)),
            pl.BlockSpec((1, 1, HD, rows), lambda bi, g, i: (bi, g, 0, 0)),
            pl.BlockSpec((1, 1, HD, tq), lambda bi, g, i: (bi, g, 0, i)),
            pl.BlockSpec((1, 1, GATE_ROWS, tq), lambda bi, g, i: (bi, g, 0, i)),
        ],
        out_specs=pl.BlockSpec((1, gw, tq), lambda bi, g, i: (bi, g, i)),
        scratch_shapes=[pltpu.VMEM((2, N_GRP, tq, tq), F32), pltpu.VMEM((3, N_GRP, tq, tq), F32)],
        compiler_params=_cparams(("parallel", "parallel", "arbitrary")),
        name="nsa_attention",
    )(slopes, qt, ks, kw, vst, vwt, k_cmp, v_cmpt, mneg, gates_t)


def _outproj_t_body(yt_ref, h_ref, w_ref, g_ref, o_ref):
    o_ref[0] = h_ref[0] + _rms(_dot_tn(yt_ref[0], w_ref[...]), g_ref[...])


def _outproj_t(yt, h, w_out, g_post):
    b, s, d = h.shape
    tm = min(PROJ_TM, s)
    return pl.pallas_call(
        _outproj_t_body,
        out_shape=jax.ShapeDtypeStruct((b, s, d), F32),
        grid=(b, s // tm),
        in_specs=[
            pl.BlockSpec((1, d, tm), lambda bi, i: (bi, 0, i)),
            pl.BlockSpec((1, tm, d), lambda bi, i: (bi, i, 0)),
            _const_spec((d, d)),
            _const_spec((1, d)),
        ],
        out_specs=pl.BlockSpec((1, tm, d), lambda bi, i: (bi, i, 0)),
        compiler_params=_cparams(("parallel", "parallel")),
        name="outproj_t",
    )(yt, h, w_out.astype(BF16), g_post.reshape(1, d))


def _nsa_mixer(h, g_pre, w_in, pe_k, w_ck1, w_ck2, pe_v, w_cv1, w_cv2, w_out, g_post):
    qt, kc, vc, ks, kw, vst, vwt, gates_t = _nsa_proj(h, g_pre, w_in)
    k_cmp = _compress(kc, pe_k, w_ck1, w_ck2, transposed=False)
    v_cmpt = _compress(vc, pe_v, w_cv1, w_cv2, transposed=True)
    mneg = _select(qt, k_cmp)
    out_t = _attention(qt, ks, kw, vst, vwt, k_cmp, v_cmpt, mneg, gates_t)
    return _outproj_t(out_t, h, w_out, g_post)


def _head_sum(x, sel_ref):
    return _dot(x.astype(BF16), sel_ref[...])


def _head_expand(x, selt_ref):
    hi, lo = _split2(x)
    return _dot(hi, selt_ref[...]) + _dot(lo, selt_ref[...])


def _softplus(x):
    return jnp.maximum(x, 0.0) + jnp.log(1.0 + jnp.exp(-jnp.abs(x)))


def _rw_proj_body(h_ref, hp_ref, g_ref, mu_ref, vec_ref, wr_ref, wk_ref, wv_ref, wd_ref, wa_ref, wg_ref,
                  w2d_ref, w2a_ref, w2g_ref, sel_ref, selt_ref,
                  r_ref, lw_ref, k_ref, v_ref, a_ref, b_ref, gg_ref):
    tm = h_ref.shape[1]
    u = _rms(h_ref[0], g_ref[...])
    prev = _rms(hp_ref[0], g_ref[...])[7:8, :]
    prev = jnp.where(pl.program_id(1) == 0, 0.0, prev)
    row = lax.broadcasted_iota(jnp.int32, (tm, D_MODEL), 0)
    u_prev = jnp.where(row == 0, prev, pltpu.roll(u, 1, 0))
    xx = u_prev - u

    def mix(i):
        return (u + xx * mu_ref[i:i + 1, :]).astype(BF16)

    w0, a0, k_k, k_a = (vec_ref[i:i + 1, :] for i in range(4))
    r = _dot(mix(0), wr_ref[...])
    k = _dot(mix(1), wk_ref[...])
    v = _dot(mix(2), wv_ref[...])
    d1 = jnp.tanh(_dot(mix(3), wd_ref[...])).astype(BF16)
    a1 = _dot(mix(4), wa_ref[...]).astype(BF16)
    g1 = jax.nn.sigmoid(_dot(mix(5), wg_ref[...])).astype(BF16)
    w = -_softplus(-(w0 + _dot(d1, w2d_ref[...]))) - 0.5
    alpha = jax.nn.sigmoid(a0 + _dot(a1, w2a_ref[...]))
    kk = k * k_k
    norm = jnp.sqrt(_head_sum(kk * kk, sel_ref))
    kk = kk * _head_expand(1.0 / jnp.maximum(norm, 1e-12), selt_ref)
    r_ref[0] = r.astype(BF16)
    lw_ref[0] = -jnp.exp(w)
    k_ref[0] = (k * (1.0 + (alpha - 1.0) * k_a)).astype(BF16)
    v_ref[0] = v.astype(BF16)
    a_ref[0] = (-kk).astype(BF16)
    b_ref[0] = (kk * alpha).astype(BF16)
    gg_ref[0] = _dot(g1, w2g_ref[...]).astype(BF16)


def _head_selectors():
    lane_head = np.arange(D_MODEL) // HD
    sel = (lane_head[:, None] == np.arange(LANES)[None, :]).astype(np.float32)
    return jnp.asarray(sel, BF16), jnp.asarray(sel.T, BF16)


def _pad_cols(w, n):
    return jnp.pad(w, ((0, 0), (0, n - w.shape[1])))


def _pad_rows(w, n):
    return jnp.pad(w, ((0, n - w.shape[0]), (0, 0)))


def _rw_proj(h, g_pre, mu, w_in, w0, w_w2, a0, w_a2, w_g2, k_k, k_a):
    b, s, d = h.shape
    tm = min(PROJ_TM, s)
    offs = np.cumsum((0, d, d, d, DECAY_LORA, AAA_LORA, GATE_LORA))
    cols = [w_in[:, int(offs[i]):int(offs[i + 1])] for i in range(6)]
    ld, lg = LANES, 2 * LANES
    wr, wk, wv = (c.astype(BF16) for c in cols[:3])
    wd = _pad_cols(cols[3], ld).astype(BF16)
    wa = _pad_cols(cols[4], ld).astype(BF16)
    wg = _pad_cols(cols[5], lg).astype(BF16)
    w2d = _pad_rows(w_w2, ld).astype(BF16)
    w2a = _pad_rows(w_a2, ld).astype(BF16)
    w2g = _pad_rows(w_g2, lg).astype(BF16)
    mu8 = _pad_rows(mu, 8)
    vecs = _pad_rows(jnp.stack([w0, a0, k_k, k_a]), 8)
    sel, selt = _head_selectors()
    tok = pl.BlockSpec((1, tm, d), lambda bi, i: (bi, i, 0))
    out = lambda dt: jax.ShapeDtypeStruct((b, s, d), dt)
    return pl.pallas_call(
        _rw_proj_body,
        out_shape=(out(BF16), out(F32), out(BF16), out(BF16), out(BF16), out(BF16), out(BF16)),
        grid=(b, s // tm),
        in_specs=[
            tok,
            pl.BlockSpec((1, 8, d), lambda bi, i: (bi, jnp.maximum(i * (tm // 8) - 1, 0), 0)),
            _const_spec((1, d)), _const_spec((8, d)), _const_spec((8, d)),
            _const_spec((d, d)), _const_spec((d, d)), _const_spec((d, d)),
            _const_spec((d, ld)), _const_spec((d, ld)), _const_spec((d, lg)),
            _const_spec((ld, d)), _const_spec((ld, d)), _const_spec((lg, d)),
            _const_spec((d, LANES)), _const_spec((LANES, d)),
        ],
        out_specs=(tok,) * 7,
        compiler_params=_cparams(("parallel", "parallel")),
        name="rwkv_proj",
    )(h, h, g_pre.reshape(1, d), mu8, vecs, wr, wk, wv, wd, wa, wg, w2d, w2a, w2g, sel, selt)


def _rw_prep_body(*refs):
    for first in range(0, refs[1].shape[1] // RW_C, RW_GROUP):
        _rw_prep_group(*refs, first=first)


def _rw_prep_group(ltri_ref, r_ref, lw_ref, k_ref, v_ref, a_ref, b_ref,
                   rhat_ref, y1_ref, g_ref, n_ref, *, first):
    c = RW_C
    q = RW_Q
    nh = q // HD
    nch = RW_GROUP
    lane_head = lax.shift_right_logical(lax.broadcasted_iota(jnp.int32, (c, q), 1), 6)
    ri = lax.broadcasted_iota(jnp.int32, (q, q), 0)
    ci = lax.broadcasted_iota(jnp.int32, (q, q), 1)
    same_head = lax.shift_right_logical(ri, 6) == lax.shift_right_logical(ci, 6)
    strict_bd = same_head & ((ci & (c - 1)) < (ri & (c - 1)))
    eye = ri == ci
    t_io = lax.broadcasted_iota(jnp.int32, (c, q), 0)
    j_io = lax.broadcasted_iota(jnp.int32, (c, q), 1) & (c - 1)
    strict_ls = j_io < t_io
    incl_ls = j_io <= t_io
    rows = [slice((first + i) * c, (first + i + 1) * c) for i in range(nch)]

    def each(f):
        return [f(i) for i in range(nch)]

    def expand(x):
        return jnp.concatenate([jnp.where(lane_head == hh, x, 0.0) for hh in range(nh)], axis=0).astype(BF16)

    def collapse(x):
        out = x[0:c]
        for hh in range(1, nh):
            out = out + x[hh * c:(hh + 1) * c]
        return out

    lw = each(lambda i: lw_ref[0, rows[i], :])
    parts = each(lambda i: _split3(lw[i]))
    cum = each(lambda i: _dot(ltri_ref[...], parts[i][0]) + _dot(ltri_ref[...], parts[i][1])
               + _dot(ltri_ref[...], parts[i][2]))
    cum_c = each(lambda i: cum[i][c - 1:c, :])
    e_inv = each(lambda i: jnp.exp(-cum[i]))
    e_rem = each(lambda i: jnp.exp(cum_c[i] - cum[i]))
    at = each(lambda i: a_ref[0, rows[i], :] * jnp.exp(cum[i] - lw[i]))
    rt = each(lambda i: r_ref[0, rows[i], :] * jnp.exp(cum[i]))
    x4 = each(lambda i: expand(at[i]))
    b4 = each(lambda i: expand(b_ref[0, rows[i], :] * e_inv[i]))
    k4 = each(lambda i: expand(k_ref[0, rows[i], :] * e_inv[i]))
    v4 = each(lambda i: expand(v_ref[0, rows[i], :]))
    bbar = each(lambda i: (b_ref[0, rows[i], :] * e_rem[i]).astype(BF16))
    kbar = each(lambda i: (k_ref[0, rows[i], :] * e_rem[i]).astype(BF16))

    lbd = each(lambda i: jnp.where(strict_bd, _dot_nt(x4[i], b4[i]), 0.0))
    tbd = each(lambda i: jnp.where(eye, 1.0, lbd[i]))
    lb = each(lambda i: lbd[i].astype(BF16))
    p = each(lambda i: _dot(lb[i], lb[i]))
    n_lvl = int(np.log2(c)) - 1
    for lvl in range(n_lvl):
        pb = each(lambda i: p[i].astype(BF16))
        tbd = each(lambda i: tbd[i] + _dot(pb[i], tbd[i].astype(BF16)))
        if lvl + 1 < n_lvl:
            p = each(lambda i: _dot(pb[i], pb[i]))
    t_ls = each(lambda i: collapse(tbd[i]).astype(BF16))

    ar = each(lambda i: _dot_nt(jnp.concatenate([at[i], rt[i]], axis=0).astype(BF16), k4[i]))
    a_ak = each(lambda i: jnp.where(strict_ls, ar[i][0:c], 0.0).astype(BF16))
    a_rk = each(lambda i: jnp.where(incl_ls, ar[i][c:2 * c], 0.0).astype(BF16))
    a_rb = each(lambda i: jnp.where(incl_ls, _dot_nt(rt[i].astype(BF16), b4[i]), 0.0).astype(BF16))

    g1 = each(lambda i: _dot(a_ak[i], v4[i]))
    y0 = each(lambda i: _dot(a_rk[i], v4[i]))
    u0 = each(lambda i: _dot(t_ls[i], expand(g1[i])))
    ahat = each(lambda i: _dot(t_ls[i], x4[i]))
    rhat = each(lambda i: rt[i] + _dot(a_rb[i], expand(ahat[i])))
    y1 = each(lambda i: y0[i] + _dot(a_rb[i], expand(u0[i])))

    gm = each(lambda i: _dot_tn(bbar[i], ahat[i].astype(BF16)))
    nm = each(lambda i: _dot_tn(jnp.concatenate([bbar[i], kbar[i]], axis=0),
                                jnp.concatenate([u0[i].astype(BF16), v_ref[0, rows[i], :]], axis=0)))
    for i in range(nch):
        gmi = jnp.where(same_head, gm[i], 0.0) + jnp.where(eye, jnp.exp(cum_c[i]), 0.0)
        rhat_ref[0, rows[i], :] = rhat[i].astype(BF16)
        y1_ref[0, rows[i], :] = y1[i].astype(BF16)
        g_ref[0, rows[i], :] = collapse(gmi).astype(BF16)
        n_ref[0, rows[i], :] = collapse(jnp.where(same_head, nm[i], 0.0)).astype(BF16)


def _rw_prep(r, lw, k, v, a, b):
    bsz, s, d = r.shape
    rows = min(RW_C * RW_NCH, s)
    ltri = jnp.asarray(np.tril(np.ones((RW_C, RW_C), np.float32)), BF16)
    blk = pl.BlockSpec((1, rows, RW_Q), lambda bi, qi, j: (bi, j, qi))
    return pl.pallas_call(
        _rw_prep_body,
        out_shape=(jax.ShapeDtypeStruct((bsz, s, d), BF16),) * 4,
        grid=(bsz, d // RW_Q, s // rows),
        in_specs=[_const_spec((RW_C, RW_C))] + [blk] * 6,
        out_specs=(blk,) * 4,
        compiler_params=_cparams(("parallel", "parallel", "parallel")),
        name="rwkv_prep",
    )(ltri, r, lw, k, v, a, b)


def _rw_scan_body(rhat_ref, y1_ref, g_ref, n_ref, y_ref, h_sc):
    c = RW_C
    q = RW_Q
    nq = h_sc.shape[0]

    @pl.when(pl.program_id(1) == 0)
    def _():
        h_sc[...] = jnp.zeros(h_sc.shape, F32)

    ri = lax.broadcasted_iota(jnp.int32, (q, q), 0)
    ci = lax.broadcasted_iota(jnp.int32, (q, q), 1)
    same_head = lax.shift_right_logical(ri, 6) == lax.shift_right_logical(ci, 6)

    def step(ch, carry):
        rows = pl.ds(pl.multiple_of(ch * c, c), c)
        for qi in range(nq):
            lanes = slice(qi * q, (qi + 1) * q)
            hb = h_sc[qi].astype(BF16)
            y_ref[0, rows, lanes] = (_dot(rhat_ref[0, rows, lanes], hb) + y1_ref[0, rows, lanes]).astype(BF16)
            g_ls = g_ref[0, rows, lanes]
            n_ls = n_ref[0, rows, lanes]
            gbd = jnp.where(same_head, jnp.concatenate([g_ls] * (q // c), axis=0), 0.0)
            nbd = jnp.where(same_head, jnp.concatenate([n_ls] * (q // c), axis=0), 0.0)
            h_sc[qi] = _dot(gbd.astype(BF16), hb) + nbd
        return carry

    lax.fori_loop(0, rhat_ref.shape[1] // c, step, 0)


def _rw_scan(rhat, y1, g, n):
    bsz, s, d = rhat.shape
    rows = min(512, s)
    blk = pl.BlockSpec((1, rows, d), lambda bi, j: (bi, j, 0))
    return pl.pallas_call(
        _rw_scan_body,
        out_shape=jax.ShapeDtypeStruct((bsz, s, d), BF16),
        grid=(bsz, s // rows),
        in_specs=[blk] * 4,
        out_specs=blk,
        scratch_shapes=[pltpu.VMEM((d // RW_Q, RW_Q, RW_Q), F32)],
        compiler_params=_cparams(("parallel", "arbitrary")),
        name="rwkv_scan",
    )(rhat, y1, g, n)


def _rw_post_body(y_ref, r_ref, k_ref, v_ref, gg_ref, h_ref, vec_ref, w_ref, gpost_ref, sel_ref, selt_ref, o_ref):
    gn_w, gn_b, r_k = (vec_ref[i:i + 1, :] for i in range(3))
    f32 = lambda ref: ref[...].astype(F32)
    y = f32(y_ref)
    inv_n = 1.0 / HD
    mean = _head_expand(_head_sum(y, sel_ref) * inv_n, selt_ref)
    yc = y - mean
    var = _head_sum(yc * yc, sel_ref) * inv_n
    yn = yc * _head_expand(lax.rsqrt(var + GN_EPS), selt_ref) * gn_w + gn_b
    bonus = _head_expand(_head_sum(f32(r_ref) * f32(k_ref) * r_k, sel_ref), selt_ref) * f32(v_ref)
    z = ((yn + bonus) * f32(gg_ref)).astype(BF16)
    o_ref[...] = h_ref[...] + _rms(_dot(z, w_ref[...]), gpost_ref[...])


def _rw_post(y, r, k, v, gg, h, gn_w, gn_b, r_k, w_out, g_post):
    t, d = h.shape
    tm = min(PROJ_TM, t)
    vecs = _pad_rows(jnp.stack([gn_w, gn_b, r_k.reshape(d)]), 8)
    sel, selt = _head_selectors()
    tok = pl.BlockSpec((tm, d), lambda i: (i, 0))
    return pl.pallas_call(
        _rw_post_body,
        out_shape=jax.ShapeDtypeStruct((t, d), F32),
        grid=(t // tm,),
        in_specs=[tok] * 6 + [_const_spec((8, d)), _const_spec((d, d)), _const_spec((1, d)),
                              _const_spec((d, LANES)), _const_spec((LANES, d))],
        out_specs=tok,
        compiler_params=_cparams(("parallel",)),
        name="rwkv_post",
    )(y, r, k, v, gg, h, vecs, w_out.astype(BF16), g_post.reshape(1, d), sel, selt)


def _rwkv_mixer(h, g_pre, mu, w_in, w0, w_w2, a0, w_a2, w_g2, k_k, k_a, r_k, gn_w, gn_b, w_out, g_post):
    b, s, d = h.shape
    r, lw, k, v, a, bb, gg = _rw_proj(h, g_pre, mu, w_in, w0, w_w2, a0, w_a2, w_g2, k_k, k_a)
    rhat, y1, g, n = _rw_prep(r, lw, k, v, a, bb)
    y = _rw_scan(rhat, y1, g, n)
    f2 = lambda x: x.reshape(b * s, d)
    return _rw_post(f2(y), f2(r), f2(k), f2(v), f2(gg), f2(h), gn_w, gn_b, r_k, w_out, g_post).reshape(b, s, d)


def kernel(x, ffn1_norm_pre, ffn1_w_gu, ffn1_w_down, ffn1_norm_post, mix_norm_pre, nsa_w_in, nsa_pe_k,
           nsa_w_ck1, nsa_w_ck2, nsa_pe_v, nsa_w_cv1, nsa_w_cv2, nsa_w_out, rwkv_mu, rwkv_w_in, rwkv_w0,
           rwkv_w_w2, rwkv_a0, rwkv_w_a2, rwkv_w_g2, rwkv_k_k, rwkv_k_a, rwkv_r_k, rwkv_gn_w, rwkv_gn_b,
           rwkv_w_out, mix_norm_post, ffn2_norm_pre, ffn2_w_gu, ffn2_w_down, ffn2_norm_post):
    b, s, d = x.shape
    flat = lambda t: t.reshape(b * s, d)
    cube = lambda t: t.reshape(b, s, d)
    h = x
    depth = ffn1_norm_pre.shape[0]
    for i in range(depth):
        h = cube(_ffn(flat(h), ffn1_norm_pre[i], ffn1_w_gu[i], ffn1_w_down[i], ffn1_norm_post[i]))
        j = i // 2
        if i % 2 == 0:
            h = _nsa_mixer(h, mix_norm_pre[i], nsa_w_in[j], nsa_pe_k[j], nsa_w_ck1[j], nsa_w_ck2[j],
                           nsa_pe_v[j], nsa_w_cv1[j], nsa_w_cv2[j], nsa_w_out[j], mix_norm_post[i])
        else:
            h = _rwkv_mixer(h, mix_norm_pre[i], rwkv_mu[j], rwkv_w_in[j], rwkv_w0[j], rwkv_w_w2[j],
                            rwkv_a0[j], rwkv_w_a2[j], rwkv_w_g2[j], rwkv_k_k[j], rwkv_k_a[j], rwkv_r_k[j],
                            rwkv_gn_w[j], rwkv_gn_b[j], rwkv_w_out[j], mix_norm_post[i])
        h = cube(_ffn(flat(h), ffn2_norm_pre[i], ffn2_w_gu[i], ffn2_w_down[i], ffn2_norm_post[i]))
    return h
```

```python
import functools

import numpy as np
import jax
import jax.numpy as jnp
from jax import lax
from jax.experimental import pallas as pl
from jax.experimental.pallas import tpu as pltpu

F32 = jnp.float32
BF16 = jnp.bfloat16

D_MODEL = 1024
D_FF = 2816
HALF_STEP = 0.5
RMS_EPS = 1e-6
MASK_VALUE = -1e30

HD = 64
N_HEADS = 16
N_KV = 4
N_GRP = 4
KV_W = N_KV * HD
CMP_BLOCK = 32
CMP_STRIDE = 16
SEL_BLOCK = 64
N_SELECT = 16
WINDOW = 512
FORCED_SCORE = 1e4
N_GATE = 3 * N_HEADS

DECAY_LORA = 64
AAA_LORA = 64
GATE_LORA = 160
GN_EPS = 64e-5

LANES = 128
VMEM_LIMIT_BYTES = 56 * 1024 * 1024

FFN_TM = 512
FFN_CUTS = (0, 1536, 2816)
PROJ_TM = 512
ATT_T = 256
SEL_TQ = 256
GATE_ROWS = 16
SUM_ROWS = 16
LOG2E = 1.4426950408889634
RW_C = 64
RW_Q = 128
RW_NCH = 16


def _cparams(sem):
    return pltpu.CompilerParams(dimension_semantics=sem, vmem_limit_bytes=VMEM_LIMIT_BYTES)


def _rms(x, g):
    ms = jnp.mean(x * x, axis=-1, keepdims=True)
    return x * lax.rsqrt(ms + RMS_EPS) * g


def _const_spec(shape):
    nd = len(shape)
    return pl.BlockSpec(shape, lambda *_: (0,) * nd, pipeline_mode=pl.Buffered(1))


def _dot(a, b):
    return jnp.dot(a, b, preferred_element_type=F32)


def _dot_nt(a, b):
    return lax.dot_general(a, b, (((1,), (1,)), ((), ())), preferred_element_type=F32)


def _dot_tn(a, b):
    return lax.dot_general(a, b, (((0,), (0,)), ((), ())), preferred_element_type=F32)


def _split2(x):
    hi = x.astype(BF16)
    lo = (x - hi.astype(F32)).astype(BF16)
    return hi, lo


def _split3(x):
    hi = x.astype(BF16)
    r1 = x - hi.astype(F32)
    mid = r1.astype(BF16)
    lo = (r1 - mid.astype(F32)).astype(BF16)
    return hi, mid, lo


def _ffn_body(x_ref, gpre_ref, wgu_ref, wd_ref, gpost_ref, o_ref):
    x = x_ref[...]
    xn = _rms(x, gpre_ref[...]).astype(BF16)
    acc = None
    for lo, hi in zip(FFN_CUTS[:-1], FFN_CUTS[1:]):
        gate = _dot(xn, wgu_ref[:, lo:hi])
        up = _dot(xn, wgu_ref[:, D_FF + lo:D_FF + hi])
        act = (gate * jax.nn.sigmoid(gate) * up).astype(BF16)
        part = _dot(act, wd_ref[lo:hi, :])
        acc = part if acc is None else acc + part
    o_ref[...] = x + HALF_STEP * _rms(acc, gpost_ref[...])


def _ffn(h2, g_pre, w_gu, w_down, g_post):
    t = h2.shape[0]
    tm = min(FFN_TM, t)
    return pl.pallas_call(
        _ffn_body,
        out_shape=jax.ShapeDtypeStruct((t, D_MODEL), F32),
        grid=(t // tm,),
        in_specs=[
            pl.BlockSpec((tm, D_MODEL), lambda i: (i, 0)),
            _const_spec((1, D_MODEL)),
            _const_spec((D_MODEL, 2 * D_FF)),
            _const_spec((D_FF, D_MODEL)),
            _const_spec((1, D_MODEL)),
        ],
        out_specs=pl.BlockSpec((tm, D_MODEL), lambda i: (i, 0)),
        compiler_params=_cparams(("parallel",)),
        name="ffn",
    )(h2, g_pre.reshape(1, D_MODEL), w_gu.astype(BF16), w_down.astype(BF16), g_post.reshape(1, D_MODEL))


def _nsa_proj_body(h_ref, g_ref, wqt_ref, wc_ref, wk_ref, wvt_ref, wgt_ref,
                   qt_ref, kc_ref, vc_ref, ks_ref, kw_ref, vst_ref, vwt_ref, gt_ref):
    tm = h_ref.shape[1]
    u = _rms(h_ref[0], g_ref[...]).astype(BF16)
    qt_ref[0] = (_dot_nt(wqt_ref[...], u) * (HD ** -0.5 * LOG2E)).astype(BF16)
    c = _dot(u, wc_ref[...])
    for j in range(KV_W // LANES):
        kc_ref[0, j] = c[:, j * LANES:(j + 1) * LANES]
        vc_ref[0, j] = c[:, KV_W + j * LANES:KV_W + (j + 1) * LANES]
    kk = _dot(u, wk_ref[...])
    t0 = pl.program_id(1) * tm
    lane = lax.broadcasted_iota(jnp.int32, (tm, LANES), 1)
    blk = lax.shift_right_logical(t0 + lax.broadcasted_iota(jnp.int32, (tm, LANES), 0), 6)
    onehot = jnp.where(lane - HD == blk, 1.0, 0.0)
    for g in range(N_KV):
        ks_ref[0, g] = (kk[:, g * LANES:(g + 1) * LANES] + onehot).astype(BF16)
        kw_ref[0, g] = kk[:, (N_KV + g) * LANES:(N_KV + g + 1) * LANES].astype(BF16)
    vt = _dot_nt(wvt_ref[...], u)
    tk = vst_ref.shape[4]
    for j in range(tm // tk):
        cols = slice(j * tk, (j + 1) * tk)
        vst_ref[0, :, j] = vt[:KV_W, cols].reshape(N_KV, HD, tk).astype(BF16)
        vwt_ref[0, :, j] = vt[KV_W:, cols].reshape(N_KV, HD, tk).astype(BF16)
    gt_ref[0] = jax.nn.sigmoid(_dot_nt(wgt_ref[...], u)).reshape(N_KV, GATE_ROWS, tm)


def _nsa_proj(h, g_pre, w_in):
    b, s, _ = h.shape
    tm = min(PROJ_TM, s)
    tk = min(ATT_T, s)
    nq = N_HEADS * HD
    assert s // SEL_BLOCK <= HD
    col = lambda i: w_in[:, nq + i * KV_W:nq + (i + 1) * KV_W]
    w_qt = w_in[:, :nq].T.astype(BF16)
    w_c = jnp.concatenate([col(0), col(1)], axis=1).astype(BF16)
    pad_heads = lambda w: jnp.pad(w.reshape(D_MODEL, N_KV, HD), ((0, 0), (0, 0), (0, LANES - HD))).reshape(D_MODEL, N_KV * LANES)
    w_k = jnp.concatenate([pad_heads(col(2)), pad_heads(col(4))], axis=1).astype(BF16)
    w_vt = jnp.concatenate([col(3), col(5)], axis=1).T.astype(BF16)
    w_gl = w_in[:, nq + 6 * KV_W:].reshape(D_MODEL, N_KV, N_GRP * 3)
    w_gt = jnp.pad(w_gl, ((0, 0), (0, 0), (0, GATE_ROWS - N_GRP * 3))).reshape(D_MODEL, N_KV * GATE_ROWS).T.astype(BF16)
    tok = lambda w: pl.BlockSpec((1, tm, w), lambda bi, i: (bi, i, 0))
    cmp_spec = pl.BlockSpec((1, KV_W // LANES, tm, LANES), lambda bi, i: (bi, 0, i, 0))
    key = jax.ShapeDtypeStruct((b, N_KV, s, LANES), BF16)
    key_spec = pl.BlockSpec((1, N_KV, tm, LANES), lambda bi, i: (bi, 0, i, 0))
    valt = jax.ShapeDtypeStruct((b, N_KV, s // tk, HD, tk), BF16)
    valt_spec = pl.BlockSpec((1, N_KV, tm // tk, HD, tk), lambda bi, i: (bi, 0, i, 0, 0))
    return pl.pallas_call(
        _nsa_proj_body,
        out_shape=(
            jax.ShapeDtypeStruct((b, nq, s), BF16),
            jax.ShapeDtypeStruct((b, KV_W // LANES, s, LANES), F32),
            jax.ShapeDtypeStruct((b, KV_W // LANES, s, LANES), F32),
            key, key, valt, valt,
            jax.ShapeDtypeStruct((b, N_KV, GATE_ROWS, s), F32),
        ),
        grid=(b, s // tm),
        in_specs=[
            tok(D_MODEL),
            _const_spec((1, D_MODEL)),
            _const_spec((nq, D_MODEL)),
            _const_spec((D_MODEL, 2 * KV_W)),
            _const_spec((D_MODEL, 2 * N_KV * LANES)),
            _const_spec((2 * KV_W, D_MODEL)),
            _const_spec((N_KV * GATE_ROWS, D_MODEL)),
        ],
        out_specs=(
            pl.BlockSpec((1, nq, tm), lambda bi, i: (bi, 0, i)),
            cmp_spec, cmp_spec, key_spec, key_spec, valt_spec, valt_spec,
            pl.BlockSpec((1, N_KV, GATE_ROWS, tm), lambda bi, i: (bi, 0, 0, i)),
        ),
        compiler_params=_cparams(("parallel", "parallel")),
        name="nsa_proj",
    )(h, g_pre.reshape(1, D_MODEL), w_qt, w_c, w_k, w_vt, w_gt)


def _compress_body(x_ref, pe_ref, w1_ref, w2_ref, o_ref, *, transposed):
    half = CMP_BLOCK // 2
    n = x_ref.shape[2] // half
    ha = hb = None
    for l in range(half):
        xl = jnp.concatenate([x_ref[0, j, pl.ds(l, n, stride=half), :] for j in range(x_ref.shape[1])], axis=-1)
        a = _dot((xl + pe_ref[l:l + 1, :]).astype(BF16), w1_ref[l])
        b = _dot((xl + pe_ref[half + l:half + l + 1, :]).astype(BF16), w1_ref[half + l])
        ha = a if ha is None else ha + a
        hb = b if hb is None else hb + b
    hid = ha + pltpu.roll(hb, n - 1, 0)
    hid = (hid * jax.nn.sigmoid(hid)).astype(BF16)
    if transposed:
        o_ref[0] = _dot_nt(w2_ref[...], hid).reshape(N_KV, HD, n).astype(BF16)
    else:
        out = _dot(hid, w2_ref[...]).astype(BF16)
        for g in range(N_KV):
            o_ref[0, g] = out[:, g * HD:(g + 1) * HD]


def _compress(t, pe, w1, w2, transposed):
    b, _, s, _ = t.shape
    rows = s // (CMP_BLOCK // 2)
    eye = jnp.eye(N_KV, dtype=F32)
    wbd = jnp.einsum("ldc,gh->lgdhc", w1, eye).reshape(CMP_BLOCK, KV_W, KV_W).astype(BF16)
    pe_t = jnp.broadcast_to(pe[:, None, :], (CMP_BLOCK, N_KV, HD)).reshape(CMP_BLOCK, KV_W)
    w2bd = jnp.einsum("cd,gh->gchd", w2, eye).reshape(KV_W, KV_W)
    if transposed:
        w2bd = w2bd.T
        out_shape, out_block = (b, N_KV, HD, rows), (1, N_KV, HD, rows)
    else:
        out_shape, out_block = (b, N_KV, rows, HD), (1, N_KV, rows, HD)
    return pl.pallas_call(
        functools.partial(_compress_body, transposed=transposed),
        out_shape=jax.ShapeDtypeStruct(out_shape, BF16),
        grid=(b,),
        in_specs=[
            pl.BlockSpec((1, KV_W // LANES, s, LANES), lambda bi: (bi, 0, 0, 0)),
            _const_spec((CMP_BLOCK, KV_W)),
            _const_spec((CMP_BLOCK, KV_W, KV_W)),
            _const_spec((KV_W, KV_W)),
        ],
        out_specs=pl.BlockSpec(out_block, lambda bi: (bi, 0, 0, 0)),
        compiler_params=_cparams(("parallel",)),
        name="nsa_compress",
    )(t, pe_t, wbd, w2bd.astype(BF16))


def _alibi_slope(h):
    return float(2.0 ** (-8.0 * (h + 1) / N_HEADS))


def _cmp_probs(kc, qh, slope2, end_f, valid):
    s = _dot(kc, qh) + jnp.where(valid, slope2 * end_f, MASK_VALUE)
    return jnp.exp2(s - jnp.max(s, axis=0, keepdims=True)).astype(BF16)


def _cmp_geometry(n_cmp_rows, tq, t0):
    n_io = lax.broadcasted_iota(jnp.int32, (n_cmp_rows, tq), 0)
    t_io = t0 + lax.broadcasted_iota(jnp.int32, (n_cmp_rows, tq), 1)
    end = n_io * CMP_STRIDE + (CMP_BLOCK - 1)
    return end.astype(F32), (t_io >= end) & (n_io < n_cmp_rows - 1)


def _select_body(qt_ref, kc_ref, c2s_ref, mneg_ref, *, n_sel, n_cmp_rows):
    tq = qt_ref.shape[2]
    t0 = pl.program_id(1) * tq
    end_f, valid = _cmp_geometry(n_cmp_rows, tq, t0)
    s_io = lax.broadcasted_iota(jnp.int32, (n_sel, tq), 0)
    tt = t0 + lax.broadcasted_iota(jnp.int32, (n_sel, tq), 1)
    cur = lax.shift_right_logical(tt, 6)
    forced = (s_io == 0) | (s_io == cur) | (s_io == cur - 1)
    visible = s_io * SEL_BLOCK <= tt

    for g in range(N_KV):
        kc = kc_ref[0, g]
        heads = range(g * N_GRP, (g + 1) * N_GRP)
        probs = [_cmp_probs(kc, qt_ref[0, hh * HD:(hh + 1) * HD, :], _alibi_slope(hh) * LOG2E, end_f, valid)
                 for hh in heads]
        mass = [_dot(c2s_ref[...], p) for p in probs]
        imp = sum(m[0:n_sel] * (1.0 / m[n_sel:n_sel + 1]) for m in mass)
        x = jnp.where(forced, FORCED_SCORE, jnp.where(visible, imp, -1.0))
        cnt = jnp.zeros((n_sel, tq), F32)
        for j in range(n_sel):
            xj = x[j:j + 1, :]
            ge = jnp.where(xj >= x, 1.0, 0.0)
            gt = jnp.where(xj > x, 1.0, 0.0)
            cnt = cnt + jnp.where(s_io > j, ge, gt)
        mneg_ref[0, g, 0:n_sel, :] = jnp.where(cnt < float(N_SELECT), 0.0, MASK_VALUE).astype(BF16)
        if n_sel < HD:
            mneg_ref[0, g, n_sel:HD, :] = jnp.zeros((HD - n_sel, tq), BF16)


def _select(qt, k_cmp):
    b, nq, s = qt.shape
    tq = min(SEL_TQ, s)
    n_sel = s // SEL_BLOCK
    rows = k_cmp.shape[2]
    n_cmp = (s - CMP_BLOCK) // CMP_STRIDE + 1
    assert rows == n_cmp + 1
    cs = np.arange(rows) * CMP_STRIDE
    bs = np.arange(n_sel) * SEL_BLOCK
    overlap = np.clip(np.minimum(cs[:, None] + CMP_BLOCK, bs[None, :] + SEL_BLOCK)
                      - np.maximum(cs[:, None], bs[None, :]), 0, None)
    c2s_t = jnp.asarray(np.concatenate([(overlap / CMP_BLOCK).T, np.ones((SUM_ROWS, rows))], axis=0), BF16)
    return pl.pallas_call(
        functools.partial(_select_body, n_sel=n_sel, n_cmp_rows=rows),
        out_shape=jax.ShapeDtypeStruct((b, N_KV, HD, s), BF16),
        grid=(b, s // tq),
        in_specs=[
            pl.BlockSpec((1, nq, tq), lambda bi, i: (bi, 0, i)),
            pl.BlockSpec((1, N_KV, rows, HD), lambda bi, i: (bi, 0, 0, 0)),
            _const_spec((n_sel + SUM_ROWS, rows)),
        ],
        out_specs=pl.BlockSpec((1, N_KV, HD, tq), lambda bi, i: (bi, 0, 0, i)),
        compiler_params=_cparams(("parallel", "parallel")),
        name="nsa_select",
    )(qt, k_cmp, c2s_t)


def _attn_body(slope_ref, qt_ref, ks_ref, kw_ref, vst_ref, vwt_ref, kc_ref, vct_ref, mneg_ref, gt_ref,
               o_ref, s_sc, bias_sc, *, n_cmp_rows):
    tq = qt_ref.shape[2]
    tk = tq
    g = pl.program_id(1)
    qi = pl.program_id(2)
    t0 = qi * tq
    slopes = [slope_ref[g * N_GRP + r] for r in range(N_GRP)]
    q_rows = [qt_ref[0, r * HD:(r + 1) * HD, :] for r in range(N_GRP)]
    qk = [jnp.concatenate([q_rows[r], mneg_ref[0, 0]], axis=0) for r in range(N_GRP)]

    @pl.when(qi == 0)
    def _():
        key_io = lax.broadcasted_iota(jnp.int32, (tk, tq), 0)
        rel = key_io - lax.broadcasted_iota(jnp.int32, (tk, tq), 1)
        key_f = key_io.astype(F32)
        causal = jnp.where(rel <= 0, 0.0, MASK_VALUE)
        oldest = jnp.where(rel > 0, 0.0, MASK_VALUE)
        for r in range(N_GRP):
            base = slopes[r] * key_f
            bias_sc[0, r] = base
            bias_sc[1, r] = base + causal
            bias_sc[2, r] = base + oldest

    ones_rows = jnp.ones((SUM_ROWS, tk), BF16)

    def offsets(c):
        cf = ((c - qi) * tk).astype(F32)
        return [slopes[r] * cf for r in range(N_GRP)]

    def score_stage(c, k_ref, window, slot):
        kblk = k_ref[0, 0, pl.ds(pl.multiple_of(c * tk, tk), tk), :]
        variant = jnp.where(c == qi, 1, 0)
        if window:
            variant = jnp.where(c == qi - WINDOW // tk, 2, variant)
        offs = offsets(c)
        mcs = []
        for r in range(N_GRP):
            s = _dot(kblk, qk[r]) + bias_sc[variant, r]
            s_sc[int(window), slot, r] = s
            mcs.append(jnp.max(s, axis=0, keepdims=True) + offs[r])
        return tuple(mcs)

    def value_stage(c, carry, mcs, vt_ref, window, slot):
        vt = jnp.concatenate([vt_ref[0, 0, c], ones_rows], axis=0)
        offs = offsets(c)
        out = []
        for r in range(N_GRP):
            m, acc = carry[r]
            m_new = jnp.maximum(m, mcs[r])
            p = jnp.exp2(s_sc[int(window), slot, r] - (m_new - offs[r]))
            out.append((m_new, jnp.exp2(m - m_new) * acc + _dot(vt, p.astype(BF16))))
        return tuple(out)

    def first_scores(k_ref, c_lo, window):
        init = tuple((jnp.full((1, tq), MASK_VALUE, F32), jnp.zeros((HD + SUM_ROWS, tq), F32))
                     for _ in range(N_GRP))
        return init, score_stage(c_lo, k_ref, window, 0)

    def pipelined(k_ref, vt_ref, c_lo, window, state):
        steps = qi - c_lo

        def trips(n_chunks, first, count, state):
            def trip(k, state):
                carry, mcs = state
                c = first + n_chunks * k
                for i in range(n_chunks):
                    mcs_next = score_stage(c + i + 1, k_ref, window, (i + 1) % 2)
                    carry = value_stage(c + i, carry, mcs, vt_ref, window, i % 2)
                    mcs = mcs_next
                return carry, mcs
            return lax.fori_loop(0, count, trip, state)

        done = 0
        if not window:
            state = trips(4, c_lo, steps // 4, state)
            done = (steps // 4) * 4
        state = trips(2, c_lo + done, (steps - done) // 2, state)
        done = done + ((steps - done) // 2) * 2
        state = trips(1, c_lo + done, steps - done, state)
        return state, steps - done

    c_lo_w = jnp.maximum(qi - WINDOW // tk, 0)
    (carry_s, mcs_s), slot_s = pipelined(ks_ref, vst_ref, jnp.int32(0), False,
                                         first_scores(ks_ref, jnp.int32(0), False))
    state_w = first_scores(kw_ref, c_lo_w, True)
    res_s = value_stage(qi, carry_s, mcs_s, vst_ref, False, slot_s)
    (carry_w, mcs_w), slot_w = pipelined(kw_ref, vwt_ref, c_lo_w, True, state_w)

    end_f, valid_c = _cmp_geometry(n_cmp_rows, tq, t0)
    has_block = t0 + lax.broadcasted_iota(jnp.int32, (1, tq), 1) >= CMP_BLOCK - 1
    kc = kc_ref[0, 0]
    vct = jnp.concatenate([vct_ref[0, 0], jnp.ones((SUM_ROWS, n_cmp_rows), BF16)], axis=0)
    gate = gt_ref[0, 0]
    normed = lambda acc: acc[0:HD] * (1.0 / acc[HD:HD + 1])
    probs = [_cmp_probs(kc, q_rows[r], slopes[r], end_f, valid_c) for r in range(N_GRP)]
    res_w = value_stage(qi, carry_w, mcs_w, vwt_ref, True, slot_w)
    acc_c = [_dot(vct, p) for p in probs]
    for r in range(N_GRP):
        o_c = jnp.where(has_block, normed(acc_c[r]), 0.0)
        o_s = normed(res_s[r][1])
        o_w = normed(res_w[r][1])
        out = (gate[3 * r:3 * r + 1, :] * o_c + gate[3 * r + 1:3 * r + 2, :] * o_s
               + gate[3 * r + 2:3 * r + 3, :] * o_w)
        o_ref[0, r * HD:(r + 1) * HD, :] = out.astype(BF16)


def _attention(qt, ks, kw, vst, vwt, k_cmp, v_cmpt, mneg, gates_t):
    b, nq, s = qt.shape
    tq = vst.shape[4]
    assert WINDOW % tq == 0
    rows = k_cmp.shape[2]
    gw = N_GRP * HD
    slopes = jnp.asarray([_alibi_slope(h) * LOG2E for h in range(N_HEADS)], F32)
    keys = pl.BlockSpec((1, 1, s, LANES), lambda bi, g, i: (bi, g, 0, 0))
    vals = pl.BlockSpec((1, 1, s // tq, HD, tq), lambda bi, g, i: (bi, g, 0, 0, 0))
    return pl.pallas_call(
        functools.partial(_attn_body, n_cmp_rows=rows),
        out_shape=jax.ShapeDtypeStruct((b, nq, s), BF16),
        grid=(b, N_KV, s // tq),
        in_specs=[
            pl.BlockSpec(memory_space=pltpu.SMEM),
            pl.BlockSpec((1, gw, tq), lambda bi, g, i: (bi, g, i)),
            keys, keys, vals, vals,
            pl.BlockSpec((1, 1, rows, HD), lambda bi, g, i: (bi, g, 0, 0)),
            pl.BlockSpec((1, 1, HD, rows), lambda bi, g, i: (bi, g, 0, 0)),
            pl.BlockSpec((1, 1, HD, tq), lambda bi, g, i: (bi, g, 0, i)),
            pl.BlockSpec((1, 1, GATE_ROWS, tq), lambda bi, g, i: (bi, g, 0, i)),
        ],
        out_specs=pl.BlockSpec((1, gw, tq), lambda bi, g, i: (bi, g, i)),
        scratch_shapes=[pltpu.VMEM((2, 2, N_GRP, tq, tq), F32), pltpu.VMEM((3, N_GRP, tq, tq), F32)],
        compiler_params=_cparams(("parallel", "parallel", "arbitrary")),
        name="nsa_attention",
    )(slopes, qt, ks, kw, vst, vwt, k_cmp, v_cmpt, mneg, gates_t)


def _outproj_t_body(yt_ref, h_ref, w_ref, g_ref, o_ref):
    o_ref[0] = h_ref[0] + _rms(_dot_tn(yt_ref[0], w_ref[...]), g_ref[...])


def _outproj_t(yt, h, w_out, g_post):
    b, s, d = h.shape
    tm = min(PROJ_TM, s)
    return pl.pallas_call(
        _outproj_t_body,
        out_shape=jax.ShapeDtypeStruct((b, s, d), F32),
        grid=(b, s // tm),
        in_specs=[
            pl.BlockSpec((1, d, tm), lambda bi, i: (bi, 0, i)),
            pl.BlockSpec((1, tm, d), lambda bi, i: (bi, i, 0)),
            _const_spec((d, d)),
            _const_spec((1, d)),
        ],
        out_specs=pl.BlockSpec((1, tm, d), lambda bi, i: (bi, i, 0)),
        compiler_params=_cparams(("parallel", "parallel")),
        name="outproj_t",
    )(yt, h, w_out.astype(BF16), g_post.reshape(1, d))


def _nsa_mixer(h, g_pre, w_in, pe_k, w_ck1, w_ck2, pe_v, w_cv1, w_cv2, w_out, g_post):
    qt, kc, vc, ks, kw, vst, vwt, gates_t = _nsa_proj(h, g_pre, w_in)
    k_cmp = _compress(kc, pe_k, w_ck1, w_ck2, transposed=False)
    v_cmpt = _compress(vc, pe_v, w_cv1, w_cv2, transposed=True)
    mneg = _select(qt, k_cmp)
    out_t = _attention(qt, ks, kw, vst, vwt, k_cmp, v_cmpt, mneg, gates_t)
    return _outproj_t(out_t, h, w_out, g_post)


def _head_sum(x, sel_ref):
    return _dot(x.astype(BF16), sel_ref[...])


def _head_expand(x, selt_ref):
    hi, lo = _split2(x)
    return _dot(hi, selt_ref[...]) + _dot(lo, selt_ref[...])


def _softplus(x):
    return jnp.maximum(x, 0.0) + jnp.log(1.0 + jnp.exp(-jnp.abs(x)))


def _rw_proj_body(h_ref, hp_ref, g_ref, mu_ref, vec_ref, wr_ref, wk_ref, wv_ref, wd_ref, wa_ref, wg_ref,
                  w2d_ref, w2a_ref, w2g_ref, sel_ref, selt_ref,
                  r_ref, lw_ref, k_ref, v_ref, a_ref, b_ref, gg_ref):
    tm = h_ref.shape[1]
    u = _rms(h_ref[0], g_ref[...])
    prev = _rms(hp_ref[0], g_ref[...])[7:8, :]
    prev = jnp.where(pl.program_id(1) == 0, 0.0, prev)
    row = lax.broadcasted_iota(jnp.int32, (tm, D_MODEL), 0)
    u_prev = jnp.where(row == 0, prev, pltpu.roll(u, 1, 0))
    xx = u_prev - u

    def mix(i):
        return (u + xx * mu_ref[i:i + 1, :]).astype(BF16)

    w0, a0, k_k, k_a = (vec_ref[i:i + 1, :] for i in range(4))
    r = _dot(mix(0), wr_ref[...])
    k = _dot(mix(1), wk_ref[...])
    v = _dot(mix(2), wv_ref[...])
    d1 = jnp.tanh(_dot(mix(3), wd_ref[...])).astype(BF16)
    a1 = _dot(mix(4), wa_ref[...]).astype(BF16)
    g1 = jax.nn.sigmoid(_dot(mix(5), wg_ref[...])).astype(BF16)
    w = -_softplus(-(w0 + _dot(d1, w2d_ref[...]))) - 0.5
    alpha = jax.nn.sigmoid(a0 + _dot(a1, w2a_ref[...]))
    kk = k * k_k
    norm = jnp.sqrt(_head_sum(kk * kk, sel_ref))
    kk = kk * _head_expand(1.0 / jnp.maximum(norm, 1e-12), selt_ref)
    r_ref[0] = r.astype(BF16)
    lw_ref[0] = -jnp.exp(w)
    k_ref[0] = (k * (1.0 + (alpha - 1.0) * k_a)).astype(BF16)
    v_ref[0] = v.astype(BF16)
    a_ref[0] = (-kk).astype(BF16)
    b_ref[0] = (kk * alpha).astype(BF16)
    gg_ref[0] = _dot(g1, w2g_ref[...]).astype(BF16)


def _head_selectors():
    lane_head = np.arange(D_MODEL) // HD
    sel = (lane_head[:, None] == np.arange(LANES)[None, :]).astype(np.float32)
    return jnp.asarray(sel, BF16), jnp.asarray(sel.T, BF16)


def _pad_cols(w, n):
    return jnp.pad(w, ((0, 0), (0, n - w.shape[1])))


def _pad_rows(w, n):
    return jnp.pad(w, ((0, n - w.shape[0]), (0, 0)))


def _rw_proj(h, g_pre, mu, w_in, w0, w_w2, a0, w_a2, w_g2, k_k, k_a):
    b, s, d = h.shape
    tm = min(PROJ_TM, s)
    offs = np.cumsum((0, d, d, d, DECAY_LORA, AAA_LORA, GATE_LORA))
    cols = [w_in[:, int(offs[i]):int(offs[i + 1])] for i in range(6)]
    ld, lg = LANES, 2 * LANES
    wr, wk, wv = (c.astype(BF16) for c in cols[:3])
    wd = _pad_cols(cols[3], ld).astype(BF16)
    wa = _pad_cols(cols[4], ld).astype(BF16)
    wg = _pad_cols(cols[5], lg).astype(BF16)
    w2d = _pad_rows(w_w2, ld).astype(BF16)
    w2a = _pad_rows(w_a2, ld).astype(BF16)
    w2g = _pad_rows(w_g2, lg).astype(BF16)
    mu8 = _pad_rows(mu, 8)
    vecs = _pad_rows(jnp.stack([w0, a0, k_k, k_a]), 8)
    sel, selt = _head_selectors()
    tok = pl.BlockSpec((1, tm, d), lambda bi, i: (bi, i, 0))
    out = lambda dt: jax.ShapeDtypeStruct((b, s, d), dt)
    return pl.pallas_call(
        _rw_proj_body,
        out_shape=(out(BF16), out(F32), out(BF16), out(BF16), out(BF16), out(BF16), out(BF16)),
        grid=(b, s // tm),
        in_specs=[
            tok,
            pl.BlockSpec((1, 8, d), lambda bi, i: (bi, jnp.maximum(i * (tm // 8) - 1, 0), 0)),
            _const_spec((1, d)), _const_spec((8, d)), _const_spec((8, d)),
            _const_spec((d, d)), _const_spec((d, d)), _const_spec((d, d)),
            _const_spec((d, ld)), _const_spec((d, ld)), _const_spec((d, lg)),
            _const_spec((ld, d)), _const_spec((ld, d)), _const_spec((lg, d)),
            _const_spec((d, LANES)), _const_spec((LANES, d)),
        ],
        out_specs=(tok,) * 7,
        compiler_params=_cparams(("parallel", "parallel")),
        name="rwkv_proj",
    )(h, h, g_pre.reshape(1, d), mu8, vecs, wr, wk, wv, wd, wa, wg, w2d, w2a, w2g, sel, selt)


def _rw_prep_body(ltri_ref, r_ref, lw_ref, k_ref, v_ref, a_ref, b_ref,
                  rhat_ref, y1_ref, g_ref, n_ref):
    c = RW_C
    q = RW_Q
    nh = q // HD
    nch = r_ref.shape[1] // c
    lane_head = lax.shift_right_logical(lax.broadcasted_iota(jnp.int32, (c, q), 1), 6)
    ri = lax.broadcasted_iota(jnp.int32, (q, q), 0)
    ci = lax.broadcasted_iota(jnp.int32, (q, q), 1)
    same_head = lax.shift_right_logical(ri, 6) == lax.shift_right_logical(ci, 6)
    strict_bd = same_head & ((ci & (c - 1)) < (ri & (c - 1)))
    eye = ri == ci
    t_io = lax.broadcasted_iota(jnp.int32, (c, q), 0)
    j_io = lax.broadcasted_iota(jnp.int32, (c, q), 1) & (c - 1)
    strict_ls = j_io < t_io
    incl_ls = j_io <= t_io
    rows = [slice(i * c, (i + 1) * c) for i in range(nch)]

    def each(f):
        return [f(i) for i in range(nch)]

    def expand(x):
        return jnp.concatenate([jnp.where(lane_head == hh, x, 0.0) for hh in range(nh)], axis=0).astype(BF16)

    def collapse(x):
        out = x[0:c]
        for hh in range(1, nh):
            out = out + x[hh * c:(hh + 1) * c]
        return out

    lw = each(lambda i: lw_ref[0, rows[i], :])
    parts = each(lambda i: _dot(ltri_ref[...], jnp.concatenate(_split3(lw[i]), axis=1)))
    cum = each(lambda i: parts[i][:, 0:q] + parts[i][:, q:2 * q] + parts[i][:, 2 * q:3 * q])
    cum_c = each(lambda i: cum[i][c - 1:c, :])
    e_inv = each(lambda i: jnp.exp(-cum[i]))
    e_rem = each(lambda i: jnp.exp(cum_c[i] - cum[i]))
    at = each(lambda i: a_ref[0, rows[i], :] * jnp.exp(cum[i] - lw[i]))
    rt = each(lambda i: r_ref[0, rows[i], :] * jnp.exp(cum[i]))
    x4 = each(lambda i: expand(at[i]))
    b4 = each(lambda i: expand(b_ref[0, rows[i], :] * e_inv[i]))
    k4 = each(lambda i: expand(k_ref[0, rows[i], :] * e_inv[i]))
    v4 = each(lambda i: expand(v_ref[0, rows[i], :]))
    bbar = each(lambda i: (b_ref[0, rows[i], :] * e_rem[i]).astype(BF16))
    kbar = each(lambda i: (k_ref[0, rows[i], :] * e_rem[i]).astype(BF16))

    lbd = each(lambda i: jnp.where(strict_bd, _dot_nt(x4[i], b4[i]), 0.0))
    tbd = each(lambda i: jnp.where(eye, 1.0, lbd[i]))
    lb = each(lambda i: lbd[i].astype(BF16))
    p = each(lambda i: _dot(lb[i], lb[i]))
    n_lvl = int(np.log2(c)) - 1
    for lvl in range(n_lvl):
        pb = each(lambda i: p[i].astype(BF16))
        if lvl + 1 < n_lvl:
            both = each(lambda i: _dot(pb[i], jnp.concatenate([tbd[i].astype(BF16), pb[i]], axis=1)))
            tbd = each(lambda i: tbd[i] + both[i][:, 0:q])
            p = each(lambda i: both[i][:, q:2 * q])
        else:
            tbd = each(lambda i: tbd[i] + _dot(pb[i], tbd[i].astype(BF16)))
    t_ls = each(lambda i: collapse(tbd[i]).astype(BF16))

    w = nh * c
    ar = each(lambda i: _dot_nt(jnp.concatenate([at[i], rt[i]], axis=0).astype(BF16),
                                jnp.concatenate([k4[i], b4[i]], axis=0)))
    a_kk = each(lambda i: jnp.concatenate([jnp.where(strict_ls, ar[i][0:c, 0:w], 0.0),
                                           jnp.where(incl_ls, ar[i][c:2 * c, 0:w], 0.0)], axis=0).astype(BF16))
    a_rb = each(lambda i: jnp.where(incl_ls, ar[i][c:2 * c, w:2 * w], 0.0).astype(BF16))

    gy = each(lambda i: _dot(a_kk[i], v4[i]))
    g1 = each(lambda i: gy[i][0:c])
    ua = each(lambda i: _dot(t_ls[i], jnp.concatenate([expand(g1[i]), x4[i]], axis=1)))
    u0 = each(lambda i: ua[i][:, 0:q])
    ahat = each(lambda i: ua[i][:, q:2 * q])
    ry = each(lambda i: _dot(a_rb[i], jnp.concatenate([expand(ahat[i]), expand(u0[i])], axis=1)))
    rhat = each(lambda i: rt[i] + ry[i][:, 0:q])
    y1 = each(lambda i: gy[i][c:2 * c] + ry[i][:, q:2 * q])

    gm = each(lambda i: _dot_tn(bbar[i], ahat[i].astype(BF16)))
    nm = each(lambda i: _dot_tn(jnp.concatenate([bbar[i], kbar[i]], axis=0),
                                jnp.concatenate([u0[i].astype(BF16), v_ref[0, rows[i], :]], axis=0)))
    for i in range(nch):
        gmi = jnp.where(same_head, gm[i], 0.0) + jnp.where(eye, jnp.exp(cum_c[i]), 0.0)
        rhat_ref[0, rows[i], :] = rhat[i].astype(BF16)
        y1_ref[0, rows[i], :] = y1[i].astype(BF16)
        g_ref[0, rows[i], :] = collapse(gmi).astype(BF16)
        n_ref[0, rows[i], :] = collapse(jnp.where(same_head, nm[i], 0.0)).astype(BF16)


def _rw_prep(r, lw, k, v, a, b):
    bsz, s, d = r.shape
    rows = min(RW_C * RW_NCH, s)
    ltri = jnp.asarray(np.tril(np.ones((RW_C, RW_C), np.float32)), BF16)
    blk = pl.BlockSpec((1, rows, RW_Q), lambda bi, qi, j: (bi, j, qi))
    return pl.pallas_call(
        _rw_prep_body,
        out_shape=(jax.ShapeDtypeStruct((bsz, s, d), BF16),) * 4,
        grid=(bsz, d // RW_Q, s // rows),
        in_specs=[_const_spec((RW_C, RW_C))] + [blk] * 6,
        out_specs=(blk,) * 4,
        compiler_params=_cparams(("parallel", "parallel", "parallel")),
        name="rwkv_prep",
    )(ltri, r, lw, k, v, a, b)


def _rw_scan_body(rhat_ref, y1_ref, g_ref, n_ref, y_ref, h_sc):
    c = RW_C
    q = RW_Q
    nq = h_sc.shape[0]

    @pl.when(pl.program_id(1) == 0)
    def _():
        h_sc[...] = jnp.zeros(h_sc.shape, F32)

    ri = lax.broadcasted_iota(jnp.int32, (q, q), 0)
    ci = lax.broadcasted_iota(jnp.int32, (q, q), 1)
    same_head = lax.shift_right_logical(ri, 6) == lax.shift_right_logical(ci, 6)

    def step(ch, carry):
        rows = pl.ds(pl.multiple_of(ch * c, c), c)
        for qi in range(nq):
            lanes = slice(qi * q, (qi + 1) * q)
            hb = h_sc[qi].astype(BF16)
            y_ref[0, rows, lanes] = (_dot(rhat_ref[0, rows, lanes], hb) + y1_ref[0, rows, lanes]).astype(BF16)
            g_ls = g_ref[0, rows, lanes]
            n_ls = n_ref[0, rows, lanes]
            gbd = jnp.where(same_head, jnp.concatenate([g_ls] * (q // c), axis=0), 0.0)
            nbd = jnp.where(same_head, jnp.concatenate([n_ls] * (q // c), axis=0), 0.0)
            h_sc[qi] = _dot(gbd.astype(BF16), hb) + nbd
        return carry

    lax.fori_loop(0, rhat_ref.shape[1] // c, step, 0)


def _rw_scan(rhat, y1, g, n):
    bsz, s, d = rhat.shape
    rows = min(512, s)
    blk = pl.BlockSpec((1, rows, d), lambda bi, j: (bi, j, 0))
    return pl.pallas_call(
        _rw_scan_body,
        out_shape=jax.ShapeDtypeStruct((bsz, s, d), BF16),
        grid=(bsz, s // rows),
        in_specs=[blk] * 4,
        out_specs=blk,
        scratch_shapes=[pltpu.VMEM((d // RW_Q, RW_Q, RW_Q), F32)],
        compiler_params=_cparams(("parallel", "arbitrary")),
        name="rwkv_scan",
    )(rhat, y1, g, n)


def _rw_post_body(y_ref, r_ref, k_ref, v_ref, gg_ref, h_ref, vec_ref, w_ref, gpost_ref, sel_ref, selt_ref, o_ref):
    gn_w, gn_b, r_k = (vec_ref[i:i + 1, :] for i in range(3))
    f32 = lambda ref: ref[...].astype(F32)
    y = f32(y_ref)
    inv_n = 1.0 / HD
    mean = _head_expand(_head_sum(y, sel_ref) * inv_n, selt_ref)
    yc = y - mean
    var = _head_sum(yc * yc, sel_ref) * inv_n
    yn = yc * _head_expand(lax.rsqrt(var + GN_EPS), selt_ref) * gn_w + gn_b
    bonus = _head_expand(_head_sum(f32(r_ref) * f32(k_ref) * r_k, sel_ref), selt_ref) * f32(v_ref)
    z = ((yn + bonus) * f32(gg_ref)).astype(BF16)
    o_ref[...] = h_ref[...] + _rms(_dot(z, w_ref[...]), gpost_ref[...])


def _rw_post(y, r, k, v, gg, h, gn_w, gn_b, r_k, w_out, g_post):
    t, d = h.shape
    tm = min(PROJ_TM, t)
    vecs = _pad_rows(jnp.stack([gn_w, gn_b, r_k.reshape(d)]), 8)
    sel, selt = _head_selectors()
    tok = pl.BlockSpec((tm, d), lambda i: (i, 0))
    return pl.pallas_call(
        _rw_post_body,
        out_shape=jax.ShapeDtypeStruct((t, d), F32),
        grid=(t // tm,),
        in_specs=[tok] * 6 + [_const_spec((8, d)), _const_spec((d, d)), _const_spec((1, d)),
                              _const_spec((d, LANES)), _const_spec((LANES, d))],
        out_specs=tok,
        compiler_params=_cparams(("parallel",)),
        name="rwkv_post",
    )(y, r, k, v, gg, h, vecs, w_out.astype(BF16), g_post.reshape(1, d), sel, selt)


def _rwkv_mixer(h, g_pre, mu, w_in, w0, w_w2, a0, w_a2, w_g2, k_k, k_a, r_k, gn_w, gn_b, w_out, g_post):
    b, s, d = h.shape
    r, lw, k, v, a, bb, gg = _rw_proj(h, g_pre, mu, w_in, w0, w_w2, a0, w_a2, w_g2, k_k, k_a)
    rhat, y1, g, n = _rw_prep(r, lw, k, v, a, bb)
    y = _rw_scan(rhat, y1, g, n)
    f2 = lambda x: x.reshape(b * s, d)
    return _rw_post(f2(y), f2(r), f2(k), f2(v), f2(gg), f2(h), gn_w, gn_b, r_k, w_out, g_post).reshape(b, s, d)


def kernel(x, ffn1_norm_pre, ffn1_w_gu, ffn1_w_down, ffn1_norm_post, mix_norm_pre, nsa_w_in, nsa_pe_k,
           nsa_w_ck1, nsa_w_ck2, nsa_pe_v, nsa_w_cv1, nsa_w_cv2, nsa_w_out, rwkv_mu, rwkv_w_in, rwkv_w0,
           rwkv_w_w2, rwkv_a0, rwkv_w_a2, rwkv_w_g2, rwkv_k_k, rwkv_k_a, rwkv_r_k, rwkv_gn_w, rwkv_gn_b,
           rwkv_w_out, mix_norm_post, ffn2_norm_pre, ffn2_w_gu, ffn2_w_down, ffn2_norm_post):
    b, s, d = x.shape
    flat = lambda t: t.reshape(b * s, d)
    cube = lambda t: t.reshape(b, s, d)
    h = x
    depth = ffn1_norm_pre.shape[0]
    for i in range(depth):
        h = cube(_ffn(flat(h), ffn1_norm_pre[i], ffn1_w_gu[i], ffn1_w_down[i], ffn1_norm_post[i]))
        j = i // 2
        if i % 2 == 0:
            h = _nsa_mixer(h, mix_norm_pre[i], nsa_w_in[j], nsa_pe_k[j], nsa_w_ck1[j], nsa_w_ck2[j],
                           nsa_pe_v[j], nsa_w_cv1[j], nsa_w_cv2[j], nsa_w_out[j], mix_norm_post[i])
        else:
            h = _rwkv_mixer(h, mix_norm_pre[i], rwkv_mu[j], rwkv_w_in[j], rwkv_w0[j], rwkv_w_w2[j],
                            rwkv_a0[j], rwkv_w_a2[j], rwkv_w_g2[j], rwkv_k_k[j], rwkv_k_a[j], rwkv_r_k[j],
                            rwkv_gn_w[j], rwkv_gn_b[j], rwkv_w_out[j], mix_norm_post[i])
        h = cube(_ffn(flat(h), ffn2_norm_pre[i], ffn2_w_gu[i], ffn2_w_down[i], ffn2_norm_post[i]))
    return h
```

```python
import functools

import numpy as np
import jax
import jax.numpy as jnp
from jax import lax
from jax.experimental import pallas as pl
from jax.experimental.pallas import tpu as pltpu

F32 = jnp.float32
BF16 = jnp.bfloat16

D_MODEL = 1024
D_FF = 2816
HALF_STEP = 0.5
RMS_EPS = 1e-6
MASK_VALUE = -1e30

HD = 64
N_HEADS = 16
N_KV = 4
N_GRP = 4
KV_W = N_KV * HD
CMP_BLOCK = 32
CMP_STRIDE = 16
SEL_BLOCK = 64
N_SELECT = 16
WINDOW = 512
FORCED_SCORE = 1e4
N_GATE = 3 * N_HEADS

DECAY_LORA = 64
AAA_LORA = 64
GATE_LORA = 160
GN_EPS = 64e-5

LANES = 128
SUBLANES = 8
VMEM_LIMIT_BYTES = 56 * 1024 * 1024

FFN_TM = 512
FFN_CUTS = (0, 1536, 2816)
PROJ_TM = 512
ATT_T = 256
GATE_ROWS = 16
SUM_ROWS = 16
LOG2E = 1.4426950408889634
RW_C = 64
RW_Q = 128
RW_NCH = 16


def _cparams(sem):
    return pltpu.CompilerParams(dimension_semantics=sem, vmem_limit_bytes=VMEM_LIMIT_BYTES)


def _rms(x, g):
    ms = jnp.mean(x * x, axis=-1, keepdims=True)
    return x * lax.rsqrt(ms + RMS_EPS) * g


def _const_spec(shape):
    nd = len(shape)
    return pl.BlockSpec(shape, lambda *_: (0,) * nd, pipeline_mode=pl.Buffered(1))


def _dot(a, b):
    return jnp.dot(a, b, preferred_element_type=F32)


def _dot_nt(a, b):
    return lax.dot_general(a, b, (((1,), (1,)), ((), ())), preferred_element_type=F32)


def _dot_tn(a, b):
    return lax.dot_general(a, b, (((0,), (0,)), ((), ())), preferred_element_type=F32)


def _split2(x):
    hi = x.astype(BF16)
    lo = (x - hi.astype(F32)).astype(BF16)
    return hi, lo


def _split3(x):
    hi = x.astype(BF16)
    r1 = x - hi.astype(F32)
    mid = r1.astype(BF16)
    lo = (r1 - mid.astype(F32)).astype(BF16)
    return hi, mid, lo


def _ffn_body(x_ref, gpre_ref, wgu_ref, wd_ref, gpost_ref, o_ref):
    x = x_ref[...]
    xn = _rms(x, gpre_ref[...]).astype(BF16)
    acc = None
    for lo, hi in zip(FFN_CUTS[:-1], FFN_CUTS[1:]):
        gate = _dot(xn, wgu_ref[:, lo:hi])
        up = _dot(xn, wgu_ref[:, D_FF + lo:D_FF + hi])
        act = (gate * jax.nn.sigmoid(gate) * up).astype(BF16)
        part = _dot(act, wd_ref[lo:hi, :])
        acc = part if acc is None else acc + part
    o_ref[...] = x + HALF_STEP * _rms(acc, gpost_ref[...])


def _ffn(h2, g_pre, w_gu, w_down, g_post):
    t = h2.shape[0]
    tm = min(FFN_TM, t)
    return pl.pallas_call(
        _ffn_body,
        out_shape=jax.ShapeDtypeStruct((t, D_MODEL), F32),
        grid=(t // tm,),
        in_specs=[
            pl.BlockSpec((tm, D_MODEL), lambda i: (i, 0)),
            _const_spec((1, D_MODEL)),
            _const_spec((D_MODEL, 2 * D_FF)),
            _const_spec((D_FF, D_MODEL)),
            _const_spec((1, D_MODEL)),
        ],
        out_specs=pl.BlockSpec((tm, D_MODEL), lambda i: (i, 0)),
        compiler_params=_cparams(("parallel",)),
        name="ffn",
    )(h2, g_pre.reshape(1, D_MODEL), w_gu.astype(BF16), w_down.astype(BF16), g_post.reshape(1, D_MODEL))


def _nsa_proj_body(h_ref, g_ref, wqt_ref, wc_ref, wk_ref, wvt_ref, wgt_ref,
                   qt_ref, kc_ref, vc_ref, ks_ref, kw_ref, vst_ref, vwt_ref, gt_ref):
    tm = h_ref.shape[1]
    u = _rms(h_ref[0], g_ref[...]).astype(BF16)
    qt_ref[0] = (_dot_nt(wqt_ref[...], u) * (HD ** -0.5 * LOG2E)).astype(BF16)
    c = _dot(u, wc_ref[...])
    for j in range(KV_W // LANES):
        kc_ref[0, j] = c[:, j * LANES:(j + 1) * LANES]
        vc_ref[0, j] = c[:, KV_W + j * LANES:KV_W + (j + 1) * LANES]
    kk = _dot(u, wk_ref[...])
    t0 = pl.program_id(1) * tm
    lane = lax.broadcasted_iota(jnp.int32, (tm, LANES), 1)
    blk = lax.shift_right_logical(t0 + lax.broadcasted_iota(jnp.int32, (tm, LANES), 0), 6)
    onehot = jnp.where(lane - HD == blk, 1.0, 0.0)
    for g in range(N_KV):
        ks_ref[0, g] = (kk[:, g * LANES:(g + 1) * LANES] + onehot).astype(BF16)
        kw_ref[0, g] = kk[:, (N_KV + g) * LANES:(N_KV + g + 1) * LANES].astype(BF16)
    vt = _dot_nt(wvt_ref[...], u)
    tk = vst_ref.shape[4]
    for j in range(tm // tk):
        cols = slice(j * tk, (j + 1) * tk)
        vst_ref[0, :, j] = vt[:KV_W, cols].reshape(N_KV, HD, tk).astype(BF16)
        vwt_ref[0, :, j] = vt[KV_W:, cols].reshape(N_KV, HD, tk).astype(BF16)
    gt_ref[0] = jax.nn.sigmoid(_dot_nt(wgt_ref[...], u)).reshape(N_KV, GATE_ROWS, tm)


def _nsa_proj(h, g_pre, w_in):
    b, s, _ = h.shape
    tm = min(PROJ_TM, s)
    tk = min(ATT_T, s)
    nq = N_HEADS * HD
    assert s // SEL_BLOCK <= HD
    col = lambda i: w_in[:, nq + i * KV_W:nq + (i + 1) * KV_W]
    w_qt = w_in[:, :nq].T.astype(BF16)
    w_c = jnp.concatenate([col(0), col(1)], axis=1).astype(BF16)
    pad_heads = lambda w: jnp.pad(w.reshape(D_MODEL, N_KV, HD), ((0, 0), (0, 0), (0, LANES - HD))).reshape(D_MODEL, N_KV * LANES)
    w_k = jnp.concatenate([pad_heads(col(2)), pad_heads(col(4))], axis=1).astype(BF16)
    w_vt = jnp.concatenate([col(3), col(5)], axis=1).T.astype(BF16)
    w_gl = w_in[:, nq + 6 * KV_W:].reshape(D_MODEL, N_KV, N_GRP * 3)
    w_gt = jnp.pad(w_gl, ((0, 0), (0, 0), (0, GATE_ROWS - N_GRP * 3))).reshape(D_MODEL, N_KV * GATE_ROWS).T.astype(BF16)
    tok = lambda w: pl.BlockSpec((1, tm, w), lambda bi, i: (bi, i, 0))
    cmp_spec = pl.BlockSpec((1, KV_W // LANES, tm, LANES), lambda bi, i: (bi, 0, i, 0))
    key = jax.ShapeDtypeStruct((b, N_KV, s, LANES), BF16)
    key_spec = pl.BlockSpec((1, N_KV, tm, LANES), lambda bi, i: (bi, 0, i, 0))
    valt = jax.ShapeDtypeStruct((b, N_KV, s // tk, HD, tk), BF16)
    valt_spec = pl.BlockSpec((1, N_KV, tm // tk, HD, tk), lambda bi, i: (bi, 0, i, 0, 0))
    return pl.pallas_call(
        _nsa_proj_body,
        out_shape=(
            jax.ShapeDtypeStruct((b, nq, s), BF16),
            jax.ShapeDtypeStruct((b, KV_W // LANES, s, LANES), F32),
            jax.ShapeDtypeStruct((b, KV_W // LANES, s, LANES), F32),
            key, key, valt, valt,
            jax.ShapeDtypeStruct((b, N_KV, GATE_ROWS, s), F32),
        ),
        grid=(b, s // tm),
        in_specs=[
            tok(D_MODEL),
            _const_spec((1, D_MODEL)),
            _const_spec((nq, D_MODEL)),
            _const_spec((D_MODEL, 2 * KV_W)),
            _const_spec((D_MODEL, 2 * N_KV * LANES)),
            _const_spec((2 * KV_W, D_MODEL)),
            _const_spec((N_KV * GATE_ROWS, D_MODEL)),
        ],
        out_specs=(
            pl.BlockSpec((1, nq, tm), lambda bi, i: (bi, 0, i)),
            cmp_spec, cmp_spec, key_spec, key_spec, valt_spec, valt_spec,
            pl.BlockSpec((1, N_KV, GATE_ROWS, tm), lambda bi, i: (bi, 0, 0, i)),
        ),
        compiler_params=_cparams(("parallel", "parallel")),
        name="nsa_proj",
    )(h, g_pre.reshape(1, D_MODEL), w_qt, w_c, w_k, w_vt, w_gt)


def _compress_body(x_ref, pe_ref, w1_ref, w2_ref, o_ref, *, transposed):
    half = CMP_BLOCK // 2
    n = x_ref.shape[2] // half
    ha = hb = None
    for l in range(half):
        xl = jnp.concatenate([x_ref[0, j, pl.ds(l, n, stride=half), :] for j in range(x_ref.shape[1])], axis=-1)
        a = _dot((xl + pe_ref[l:l + 1, :]).astype(BF16), w1_ref[l])
        b = _dot((xl + pe_ref[half + l:half + l + 1, :]).astype(BF16), w1_ref[half + l])
        ha = a if ha is None else ha + a
        hb = b if hb is None else hb + b
    hid = ha + pltpu.roll(hb, n - 1, 0)
    hid = (hid * jax.nn.sigmoid(hid)).astype(BF16)
    if transposed:
        o_ref[0] = _dot_nt(w2_ref[...], hid).reshape(N_KV, HD, n).astype(BF16)
    else:
        out = _dot(hid, w2_ref[...]).astype(BF16)
        for g in range(N_KV):
            o_ref[0, g] = out[:, g * HD:(g + 1) * HD]


def _compress(t, pe, w1, w2, transposed):
    b, _, s, _ = t.shape
    rows = s // (CMP_BLOCK // 2)
    eye = jnp.eye(N_KV, dtype=F32)
    wbd = jnp.einsum("ldc,gh->lgdhc", w1, eye).reshape(CMP_BLOCK, KV_W, KV_W).astype(BF16)
    pe_t = jnp.broadcast_to(pe[:, None, :], (CMP_BLOCK, N_KV, HD)).reshape(CMP_BLOCK, KV_W)
    w2bd = jnp.einsum("cd,gh->gchd", w2, eye).reshape(KV_W, KV_W)
    if transposed:
        w2bd = w2bd.T
        out_shape, out_block = (b, N_KV, HD, rows), (1, N_KV, HD, rows)
    else:
        out_shape, out_block = (b, N_KV, rows, HD), (1, N_KV, rows, HD)
    return pl.pallas_call(
        functools.partial(_compress_body, transposed=transposed),
        out_shape=jax.ShapeDtypeStruct(out_shape, BF16),
        grid=(b,),
        in_specs=[
            pl.BlockSpec((1, KV_W // LANES, s, LANES), lambda bi: (bi, 0, 0, 0)),
            _const_spec((CMP_BLOCK, KV_W)),
            _const_spec((CMP_BLOCK, KV_W, KV_W)),
            _const_spec((KV_W, KV_W)),
        ],
        out_specs=pl.BlockSpec(out_block, lambda bi: (bi, 0, 0, 0)),
        compiler_params=_cparams(("parallel",)),
        name="nsa_compress",
    )(t, pe_t, wbd, w2bd.astype(BF16))


def _alibi_slope(h):
    return float(2.0 ** (-8.0 * (h + 1) / N_HEADS))


def _cmp_probs(kc, qh, slope2, end_f, valid):
    s = _dot(kc, qh) + jnp.where(valid, slope2 * end_f, MASK_VALUE)
    return jnp.exp2(s - jnp.max(s, axis=0, keepdims=True)).astype(BF16)


def _cmp_geometry(n_cmp_rows, tq, t0):
    n_io = lax.broadcasted_iota(jnp.int32, (n_cmp_rows, tq), 0)
    t_io = t0 + lax.broadcasted_iota(jnp.int32, (n_cmp_rows, tq), 1)
    end = n_io * CMP_STRIDE + (CMP_BLOCK - 1)
    return end.astype(F32), (t_io >= end) & (n_io < n_cmp_rows - 1)


def _select_body(qt_ref, kc_ref, c2s_ref, chunk_ref, mneg_ref, active_ref, *, n_sel, n_cmp_rows):
    tq = qt_ref.shape[2]
    t0 = pl.program_id(1) * tq
    end_f, valid = _cmp_geometry(n_cmp_rows, tq, t0)
    s_io = lax.broadcasted_iota(jnp.int32, (n_sel, tq), 0)
    tt = t0 + lax.broadcasted_iota(jnp.int32, (n_sel, tq), 1)
    cur = lax.shift_right_logical(tt, 6)
    forced = (s_io == 0) | (s_io == cur) | (s_io == cur - 1)
    visible = s_io * SEL_BLOCK <= tt

    for g in range(N_KV):
        kc = kc_ref[0, g]
        heads = range(g * N_GRP, (g + 1) * N_GRP)
        probs = [_cmp_probs(kc, qt_ref[0, hh * HD:(hh + 1) * HD, :], _alibi_slope(hh) * LOG2E, end_f, valid)
                 for hh in heads]
        mass = [_dot(c2s_ref[...], p) for p in probs]
        imp = sum(m[0:n_sel] * (1.0 / m[n_sel:n_sel + 1]) for m in mass)
        x = jnp.where(forced, FORCED_SCORE, jnp.where(visible, imp, -1.0))
        cnt = jnp.zeros((n_sel, tq), F32)
        for j in range(n_sel):
            xj = x[j:j + 1, :]
            ge = jnp.where(xj >= x, 1.0, 0.0)
            gt = jnp.where(xj > x, 1.0, 0.0)
            cnt = cnt + jnp.where(s_io > j, ge, gt)
        chosen = cnt < float(N_SELECT)
        mneg_ref[0, g, 0:n_sel, :] = jnp.where(chosen, 0.0, MASK_VALUE).astype(BF16)
        if n_sel < HD:
            mneg_ref[0, g, n_sel:HD, :] = jnp.zeros((HD - n_sel, tq), BF16)
        per_query = _dot(chunk_ref[...], jnp.where(chosen, 1.0, 0.0).astype(BF16))
        active_ref[0, g, 0] = _dot_nt(jnp.ones((SUBLANES, tq), BF16), per_query.astype(BF16))


def _select(qt, k_cmp):
    b, nq, s = qt.shape
    tq = min(ATT_T, s)
    n_sel = s // SEL_BLOCK
    rows = k_cmp.shape[2]
    n_cmp = (s - CMP_BLOCK) // CMP_STRIDE + 1
    assert rows == n_cmp + 1
    cs = np.arange(rows) * CMP_STRIDE
    bs = np.arange(n_sel) * SEL_BLOCK
    overlap = np.clip(np.minimum(cs[:, None] + CMP_BLOCK, bs[None, :] + SEL_BLOCK)
                      - np.maximum(cs[:, None], bs[None, :]), 0, None)
    c2s_t = jnp.asarray(np.concatenate([(overlap / CMP_BLOCK).T, np.ones((SUM_ROWS, rows))], axis=0), BF16)
    n_chunks = s // tq
    chunk_of_block = np.arange(n_sel) * SEL_BLOCK // tq
    chunk_mat = jnp.asarray(chunk_of_block[None, :] == np.arange(n_chunks)[:, None], BF16)
    return pl.pallas_call(
        functools.partial(_select_body, n_sel=n_sel, n_cmp_rows=rows),
        out_shape=(jax.ShapeDtypeStruct((b, N_KV, HD, s), BF16),
                   jax.ShapeDtypeStruct((b, N_KV, s // tq, SUBLANES, n_chunks), F32)),
        grid=(b, s // tq),
        in_specs=[
            pl.BlockSpec((1, nq, tq), lambda bi, i: (bi, 0, i)),
            pl.BlockSpec((1, N_KV, rows, HD), lambda bi, i: (bi, 0, 0, 0)),
            _const_spec((n_sel + SUM_ROWS, rows)),
            _const_spec((n_chunks, n_sel)),
        ],
        out_specs=(pl.BlockSpec((1, N_KV, HD, tq), lambda bi, i: (bi, 0, 0, i)),
                   pl.BlockSpec((1, N_KV, 1, SUBLANES, n_chunks), lambda bi, i: (bi, 0, i, 0, 0))),
        compiler_params=_cparams(("parallel", "parallel")),
        name="nsa_select",
    )(qt, k_cmp, c2s_t, chunk_mat)


def _attn_body(slope_ref, active_ref, qt_ref, ks_ref, kw_ref, vst_ref, vwt_ref, kc_ref, vct_ref, mneg_ref,
               gt_ref, o_ref, s_sc, bias_sc, todo_ref, *, n_cmp_rows):
    tq = qt_ref.shape[2]
    tk = tq
    g = pl.program_id(1)
    qi = pl.program_id(2)
    t0 = qi * tq
    slopes = [slope_ref[g * N_GRP + r] for r in range(N_GRP)]
    q_rows = [qt_ref[0, r * HD:(r + 1) * HD, :] for r in range(N_GRP)]
    qk = [jnp.concatenate([q_rows[r], mneg_ref[0, 0]], axis=0) for r in range(N_GRP)]

    @pl.when(qi == 0)
    def _():
        key_io = lax.broadcasted_iota(jnp.int32, (tk, tq), 0)
        rel = key_io - lax.broadcasted_iota(jnp.int32, (tk, tq), 1)
        key_f = key_io.astype(F32)
        causal = jnp.where(rel <= 0, 0.0, MASK_VALUE)
        oldest = jnp.where(rel > 0, 0.0, MASK_VALUE)
        for r in range(N_GRP):
            base = slopes[r] * key_f
            bias_sc[0, r] = base
            bias_sc[1, r] = base + causal
            bias_sc[2, r] = base + oldest

    ones_rows = jnp.ones((SUM_ROWS, tk), BF16)

    def offsets(c):
        cf = ((c - qi) * tk).astype(F32)
        return [slopes[r] * cf for r in range(N_GRP)]

    def score_stage(c, k_ref, window, slot):
        kblk = k_ref[0, 0, pl.ds(pl.multiple_of(c * tk, tk), tk), :]
        variant = jnp.where(c == qi, 1, 0)
        if window:
            variant = jnp.where(c == qi - WINDOW // tk, 2, variant)
        offs = offsets(c)
        mcs = []
        for r in range(N_GRP):
            s = _dot(kblk, qk[r]) + bias_sc[variant, r]
            s_sc[int(window), slot, r] = s
            mcs.append(jnp.max(s, axis=0, keepdims=True) + offs[r])
        return tuple(mcs)

    def value_stage(c, carry, mcs, vt_ref, window, slot):
        vt = jnp.concatenate([vt_ref[0, 0, c], ones_rows], axis=0)
        offs = offsets(c)
        out = []
        for r in range(N_GRP):
            m, acc = carry[r]
            m_new = jnp.maximum(m, mcs[r])
            p = jnp.exp2(s_sc[int(window), slot, r] - (m_new - offs[r]))
            out.append((m_new, jnp.exp2(m - m_new) * acc + _dot(vt, p.astype(BF16))))
        return tuple(out)

    def first_scores(k_ref, chunk_at, window):
        init = tuple((jnp.full((1, tq), MASK_VALUE, F32), jnp.zeros((HD + SUM_ROWS, tq), F32))
                     for _ in range(N_GRP))
        return init, score_stage(chunk_at(0), k_ref, window, 0)

    def pipelined(k_ref, vt_ref, chunk_at, steps, window, state):
        def trips(n_chunks, first, count, state):
            def trip(k, state):
                carry, mcs = state
                pos = first + n_chunks * k
                for i in range(n_chunks):
                    mcs_next = score_stage(chunk_at(pos + i + 1), k_ref, window, (i + 1) % 2)
                    carry = value_stage(chunk_at(pos + i), carry, mcs, vt_ref, window, i % 2)
                    mcs = mcs_next
                return carry, mcs
            return lax.fori_loop(0, count, trip, state)

        done = 0
        if not window:
            state = trips(4, 0, steps // 4, state)
            done = (steps // 4) * 4
        state = trips(2, done, (steps - done) // 2, state)
        done = done + ((steps - done) // 2) * 2
        state = trips(1, done, steps - done, state)
        return state, steps - done

    tile = (pl.program_id(0) * N_KV + g) * pl.num_programs(2) + qi

    def note_active(c, n):
        todo_ref[n] = c
        return n + active_ref[tile * pl.num_programs(2) + c]

    n_sel_chunks = lax.fori_loop(0, qi, note_active, jnp.int32(0))
    sel_chunk = lambda pos: jnp.where(pos < n_sel_chunks, todo_ref[jnp.minimum(pos, pl.num_programs(2) - 1)], qi)
    c_lo_w = jnp.maximum(qi - WINDOW // tk, 0)
    win_chunk = lambda pos: c_lo_w + pos

    (carry_s, mcs_s), slot_s = pipelined(ks_ref, vst_ref, sel_chunk, n_sel_chunks, False,
                                         first_scores(ks_ref, sel_chunk, False))
    state_w = first_scores(kw_ref, win_chunk, True)
    res_s = value_stage(qi, carry_s, mcs_s, vst_ref, False, slot_s)
    (carry_w, mcs_w), slot_w = pipelined(kw_ref, vwt_ref, win_chunk, qi - c_lo_w, True, state_w)

    end_f, valid_c = _cmp_geometry(n_cmp_rows, tq, t0)
    has_block = t0 + lax.broadcasted_iota(jnp.int32, (1, tq), 1) >= CMP_BLOCK - 1
    kc = kc_ref[0, 0]
    vct = jnp.concatenate([vct_ref[0, 0], jnp.ones((SUM_ROWS, n_cmp_rows), BF16)], axis=0)
    gate = gt_ref[0, 0]
    normed = lambda acc: acc[0:HD] * (1.0 / acc[HD:HD + 1])
    probs = [_cmp_probs(kc, q_rows[r], slopes[r], end_f, valid_c) for r in range(N_GRP)]
    res_w = value_stage(qi, carry_w, mcs_w, vwt_ref, True, slot_w)
    acc_c = [_dot(vct, p) for p in probs]
    for r in range(N_GRP):
        o_c = jnp.where(has_block, normed(acc_c[r]), 0.0)
        o_s = normed(res_s[r][1])
        o_w = normed(res_w[r][1])
        out = (gate[3 * r:3 * r + 1, :] * o_c + gate[3 * r + 1:3 * r + 2, :] * o_s
               + gate[3 * r + 2:3 * r + 3, :] * o_w)
        o_ref[0, r * HD:(r + 1) * HD, :] = out.astype(BF16)


def _attention(qt, ks, kw, vst, vwt, k_cmp, v_cmpt, mneg, active, gates_t):
    b, nq, s = qt.shape
    tq = vst.shape[4]
    assert WINDOW % tq == 0 and active.shape == (b, N_KV, s // tq, SUBLANES, s // tq)
    active = (active[:, :, :, 0, :] > 0.0).astype(jnp.int32).reshape(-1)
    rows = k_cmp.shape[2]
    gw = N_GRP * HD
    slopes = jnp.asarray([_alibi_slope(h) * LOG2E for h in range(N_HEADS)], F32)
    keys = pl.BlockSpec((1, 1, s, LANES), lambda bi, g, i: (bi, g, 0, 0))
    vals = pl.BlockSpec((1, 1, s // tq, HD, tq), lambda bi, g, i: (bi, g, 0, 0, 0))
    return pl.pallas_call(
        functools.partial(_attn_body, n_cmp_rows=rows),
        out_shape=jax.ShapeDtypeStruct((b, nq, s), BF16),
        grid=(b, N_KV, s // tq),
        in_specs=[
            pl.BlockSpec(memory_space=pltpu.SMEM),
            pl.BlockSpec(memory_space=pltpu.SMEM),
            pl.BlockSpec((1, gw, tq), lambda bi, g, i: (bi, g, i)),
            keys, keys, vals, vals,
            pl.BlockSpec((1, 1, rows, HD), lambda bi, g, i: (bi, g, 0, 0)),
            pl.BlockSpec((1, 1, HD, rows), lambda bi, g, i: (bi, g, 0, 0)),
            pl.BlockSpec((1, 1, HD, tq), lambda bi, g, i: (bi, g, 0, i)),
            pl.BlockSpec((1, 1, GATE_ROWS, tq), lambda bi, g, i: (bi, g, 0, i)),
        ],
        out_specs=pl.BlockSpec((1, gw, tq), lambda bi, g, i: (bi, g, i)),
        scratch_shapes=[pltpu.VMEM((2, 2, N_GRP, tq, tq), F32), pltpu.VMEM((3, N_GRP, tq, tq), F32),
                        pltpu.SMEM((s // tq,), jnp.int32)],
        compiler_params=_cparams(("parallel", "parallel", "arbitrary")),
        name="nsa_attention",
    )(slopes, active, qt, ks, kw, vst, vwt, k_cmp, v_cmpt, mneg, gates_t)


def _outproj_t_body(yt_ref, h_ref, w_ref, g_ref, o_ref):
    o_ref[0] = h_ref[0] + _rms(_dot_tn(yt_ref[0], w_ref[...]), g_ref[...])


def _outproj_t(yt, h, w_out, g_post):
    b, s, d = h.shape
    tm = min(PROJ_TM, s)
    return pl.pallas_call(
        _outproj_t_body,
        out_shape=jax.ShapeDtypeStruct((b, s, d), F32),
        grid=(b, s // tm),
        in_specs=[
            pl.BlockSpec((1, d, tm), lambda bi, i: (bi, 0, i)),
            pl.BlockSpec((1, tm, d), lambda bi, i: (bi, i, 0)),
            _const_spec((d, d)),
            _const_spec((1, d)),
        ],
        out_specs=pl.BlockSpec((1, tm, d), lambda bi, i: (bi, i, 0)),
        compiler_params=_cparams(("parallel", "parallel")),
        name="outproj_t",
    )(yt, h, w_out.astype(BF16), g_post.reshape(1, d))


def _nsa_mixer(h, g_pre, w_in, pe_k, w_ck1, w_ck2, pe_v, w_cv1, w_cv2, w_out, g_post):
    qt, kc, vc, ks, kw, vst, vwt, gates_t = _nsa_proj(h, g_pre, w_in)
    k_cmp = _compress(kc, pe_k, w_ck1, w_ck2, transposed=False)
    v_cmpt = _compress(vc, pe_v, w_cv1, w_cv2, transposed=True)
    mneg, active = _select(qt, k_cmp)
    out_t = _attention(qt, ks, kw, vst, vwt, k_cmp, v_cmpt, mneg, active, gates_t)
    return _outproj_t(out_t, h, w_out, g_post)


def _head_sum(x, sel_ref):
    return _dot(x.astype(BF16), sel_ref[...])


def _head_expand(x, selt_ref):
    hi, lo = _split2(x)
    return _dot(hi, selt_ref[...]) + _dot(lo, selt_ref[...])


def _softplus(x):
    return jnp.maximum(x, 0.0) + jnp.log(1.0 + jnp.exp(-jnp.abs(x)))


def _rw_proj_body(h_ref, hp_ref, g_ref, mu_ref, vec_ref, wr_ref, wk_ref, wv_ref, wd_ref, wa_ref, wg_ref,
                  w2d_ref, w2a_ref, w2g_ref, sel_ref, selt_ref,
                  r_ref, lw_ref, k_ref, v_ref, a_ref, b_ref, gg_ref):
    tm = h_ref.shape[1]
    u = _rms(h_ref[0], g_ref[...])
    prev = _rms(hp_ref[0], g_ref[...])[7:8, :]
    prev = jnp.where(pl.program_id(1) == 0, 0.0, prev)
    row = lax.broadcasted_iota(jnp.int32, (tm, D_MODEL), 0)
    u_prev = jnp.where(row == 0, prev, pltpu.roll(u, 1, 0))
    xx = u_prev - u

    def mix(i):
        return (u + xx * mu_ref[i:i + 1, :]).astype(BF16)

    w0, a0, k_k, k_a = (vec_ref[i:i + 1, :] for i in range(4))
    r = _dot(mix(0), wr_ref[...])
    k = _dot(mix(1), wk_ref[...])
    v = _dot(mix(2), wv_ref[...])
    d1 = jnp.tanh(_dot(mix(3), wd_ref[...])).astype(BF16)
    a1 = _dot(mix(4), wa_ref[...]).astype(BF16)
    g1 = jax.nn.sigmoid(_dot(mix(5), wg_ref[...])).astype(BF16)
    w = -_softplus(-(w0 + _dot(d1, w2d_ref[...]))) - 0.5
    alpha = jax.nn.sigmoid(a0 + _dot(a1, w2a_ref[...]))
    kk = k * k_k
    norm = jnp.sqrt(_head_sum(kk * kk, sel_ref))
    kk = kk * _head_expand(1.0 / jnp.maximum(norm, 1e-12), selt_ref)
    r_ref[0] = r.astype(BF16)
    lw_ref[0] = -jnp.exp(w)
    k_ref[0] = (k * (1.0 + (alpha - 1.0) * k_a)).astype(BF16)
    v_ref[0] = v.astype(BF16)
    a_ref[0] = (-kk).astype(BF16)
    b_ref[0] = (kk * alpha).astype(BF16)
    gg_ref[0] = _dot(g1, w2g_ref[...]).astype(BF16)


def _head_selectors():
    lane_head = np.arange(D_MODEL) // HD
    sel = (lane_head[:, None] == np.arange(LANES)[None, :]).astype(np.float32)
    return jnp.asarray(sel, BF16), jnp.asarray(sel.T, BF16)


def _pad_cols(w, n):
    return jnp.pad(w, ((0, 0), (0, n - w.shape[1])))


def _pad_rows(w, n):
    return jnp.pad(w, ((0, n - w.shape[0]), (0, 0)))


def _rw_proj(h, g_pre, mu, w_in, w0, w_w2, a0, w_a2, w_g2, k_k, k_a):
    b, s, d = h.shape
    tm = min(PROJ_TM, s)
    offs = np.cumsum((0, d, d, d, DECAY_LORA, AAA_LORA, GATE_LORA))
    cols = [w_in[:, int(offs[i]):int(offs[i + 1])] for i in range(6)]
    ld, lg = LANES, 2 * LANES
    wr, wk, wv = (c.astype(BF16) for c in cols[:3])
    wd = _pad_cols(cols[3], ld).astype(BF16)
    wa = _pad_cols(cols[4], ld).astype(BF16)
    wg = _pad_cols(cols[5], lg).astype(BF16)
    w2d = _pad_rows(w_w2, ld).astype(BF16)
    w2a = _pad_rows(w_a2, ld).astype(BF16)
    w2g = _pad_rows(w_g2, lg).astype(BF16)
    mu8 = _pad_rows(mu, 8)
    vecs = _pad_rows(jnp.stack([w0, a0, k_k, k_a]), 8)
    sel, selt = _head_selectors()
    tok = pl.BlockSpec((1, tm, d), lambda bi, i: (bi, i, 0))
    out = lambda dt: jax.ShapeDtypeStruct((b, s, d), dt)
    return pl.pallas_call(
        _rw_proj_body,
        out_shape=(out(BF16), out(F32), out(BF16), out(BF16), out(BF16), out(BF16), out(BF16)),
        grid=(b, s // tm),
        in_specs=[
            tok,
            pl.BlockSpec((1, 8, d), lambda bi, i: (bi, jnp.maximum(i * (tm // 8) - 1, 0), 0)),
            _const_spec((1, d)), _const_spec((8, d)), _const_spec((8, d)),
            _const_spec((d, d)), _const_spec((d, d)), _const_spec((d, d)),
            _const_spec((d, ld)), _const_spec((d, ld)), _const_spec((d, lg)),
            _const_spec((ld, d)), _const_spec((ld, d)), _const_spec((lg, d)),
            _const_spec((d, LANES)), _const_spec((LANES, d)),
        ],
        out_specs=(tok,) * 7,
        compiler_params=_cparams(("parallel", "parallel")),
        name="rwkv_proj",
    )(h, h, g_pre.reshape(1, d), mu8, vecs, wr, wk, wv, wd, wa, wg, w2d, w2a, w2g, sel, selt)


def _rw_prep_body(ltri_ref, r_ref, lw_ref, k_ref, v_ref, a_ref, b_ref,
                  rhat_ref, y1_ref, g_ref, n_ref):
    c = RW_C
    q = RW_Q
    nh = q // HD
    nch = r_ref.shape[1] // c
    lane_head = lax.shift_right_logical(lax.broadcasted_iota(jnp.int32, (c, q), 1), 6)
    ri = lax.broadcasted_iota(jnp.int32, (q, q), 0)
    ci = lax.broadcasted_iota(jnp.int32, (q, q), 1)
    same_head = lax.shift_right_logical(ri, 6) == lax.shift_right_logical(ci, 6)
    strict_bd = same_head & ((ci & (c - 1)) < (ri & (c - 1)))
    eye = ri == ci
    t_io = lax.broadcasted_iota(jnp.int32, (c, q), 0)
    j_io = lax.broadcasted_iota(jnp.int32, (c, q), 1) & (c - 1)
    strict_ls = j_io < t_io
    incl_ls = j_io <= t_io
    rows = [slice(i * c, (i + 1) * c) for i in range(nch)]

    def each(f):
        return [f(i) for i in range(nch)]

    def expand(x):
        return jnp.concatenate([jnp.where(lane_head == hh, x, 0.0) for hh in range(nh)], axis=0).astype(BF16)

    def collapse(x):
        out = x[0:c]
        for hh in range(1, nh):
            out = out + x[hh * c:(hh + 1) * c]
        return out

    lw = each(lambda i: lw_ref[0, rows[i], :])
    parts = each(lambda i: _dot(ltri_ref[...], jnp.concatenate(_split3(lw[i]), axis=1)))
    cum = each(lambda i: parts[i][:, 0:q] + parts[i][:, q:2 * q] + parts[i][:, 2 * q:3 * q])
    cum_c = each(lambda i: cum[i][c - 1:c, :])
    e_inv = each(lambda i: jnp.exp(-cum[i]))
    e_rem = each(lambda i: jnp.exp(cum_c[i] - cum[i]))
    at = each(lambda i: a_ref[0, rows[i], :] * jnp.exp(cum[i] - lw[i]))
    rt = each(lambda i: r_ref[0, rows[i], :] * jnp.exp(cum[i]))
    x4 = each(lambda i: expand(at[i]))
    b4 = each(lambda i: expand(b_ref[0, rows[i], :] * e_inv[i]))
    k4 = each(lambda i: expand(k_ref[0, rows[i], :] * e_inv[i]))
    v4 = each(lambda i: expand(v_ref[0, rows[i], :]))
    bbar = each(lambda i: (b_ref[0, rows[i], :] * e_rem[i]).astype(BF16))
    kbar = each(lambda i: (k_ref[0, rows[i], :] * e_rem[i]).astype(BF16))

    lbd = each(lambda i: jnp.where(strict_bd, _dot_nt(x4[i], b4[i]), 0.0))
    tbd = each(lambda i: jnp.where(eye, 1.0, lbd[i]))
    lb = each(lambda i: lbd[i].astype(BF16))
    p = each(lambda i: _dot(lb[i], lb[i]))
    n_lvl = int(np.log2(c)) - 1
    for lvl in range(n_lvl):
        pb = each(lambda i: p[i].astype(BF16))
        if lvl + 1 < n_lvl:
            both = each(lambda i: _dot(pb[i], jnp.concatenate([tbd[i].astype(BF16), pb[i]], axis=1)))
            tbd = each(lambda i: tbd[i] + both[i][:, 0:q])
            p = each(lambda i: both[i][:, q:2 * q])
        else:
            tbd = each(lambda i: tbd[i] + _dot(pb[i], tbd[i].astype(BF16)))
    t_ls = each(lambda i: collapse(tbd[i]).astype(BF16))

    w = nh * c
    ar = each(lambda i: _dot_nt(jnp.concatenate([at[i], rt[i]], axis=0).astype(BF16),
                                jnp.concatenate([k4[i], b4[i]], axis=0)))
    a_kk = each(lambda i: jnp.concatenate([jnp.where(strict_ls, ar[i][0:c, 0:w], 0.0),
                                           jnp.where(incl_ls, ar[i][c:2 * c, 0:w], 0.0)], axis=0).astype(BF16))
    a_rb = each(lambda i: jnp.where(incl_ls, ar[i][c:2 * c, w:2 * w], 0.0).astype(BF16))

    gy = each(lambda i: _dot(a_kk[i], v4[i]))
    g1 = each(lambda i: gy[i][0:c])
    ua = each(lambda i: _dot(t_ls[i], jnp.concatenate([expand(g1[i]), x4[i]], axis=1)))
    u0 = each(lambda i: ua[i][:, 0:q])
    ahat = each(lambda i: ua[i][:, q:2 * q])
    ry = each(lambda i: _dot(a_rb[i], jnp.concatenate([expand(ahat[i]), expand(u0[i])], axis=1)))
    rhat = each(lambda i: rt[i] + ry[i][:, 0:q])
    y1 = each(lambda i: gy[i][c:2 * c] + ry[i][:, q:2 * q])

    gm = each(lambda i: _dot_tn(bbar[i], ahat[i].astype(BF16)))
    nm = each(lambda i: _dot_tn(jnp.concatenate([bbar[i], kbar[i]], axis=0),
                                jnp.concatenate([u0[i].astype(BF16), v_ref[0, rows[i], :]], axis=0)))
    for i in range(nch):
        gmi = jnp.where(same_head, gm[i], 0.0) + jnp.where(eye, jnp.exp(cum_c[i]), 0.0)
        rhat_ref[0, rows[i], :] = rhat[i].astype(BF16)
        y1_ref[0, rows[i], :] = y1[i].astype(BF16)
        g_ref[0, rows[i], :] = collapse(gmi).astype(BF16)
        n_ref[0, rows[i], :] = collapse(jnp.where(same_head, nm[i], 0.0)).astype(BF16)


def _rw_prep(r, lw, k, v, a, b):
    bsz, s, d = r.shape
    rows = min(RW_C * RW_NCH, s)
    ltri = jnp.asarray(np.tril(np.ones((RW_C, RW_C), np.float32)), BF16)
    blk = pl.BlockSpec((1, rows, RW_Q), lambda bi, qi, j: (bi, j, qi))
    return pl.pallas_call(
        _rw_prep_body,
        out_shape=(jax.ShapeDtypeStruct((bsz, s, d), BF16),) * 4,
        grid=(bsz, d // RW_Q, s // rows),
        in_specs=[_const_spec((RW_C, RW_C))] + [blk] * 6,
        out_specs=(blk,) * 4,
        compiler_params=_cparams(("parallel", "parallel", "parallel")),
        name="rwkv_prep",
    )(ltri, r, lw, k, v, a, b)


def _rw_scan_body(rhat_ref, y1_ref, g_ref, n_ref, y_ref, h_sc):
    c = RW_C
    q = RW_Q
    nq = h_sc.shape[0]

    @pl.when(pl.program_id(1) == 0)
    def _():
        h_sc[...] = jnp.zeros(h_sc.shape, F32)

    ri = lax.broadcasted_iota(jnp.int32, (q, q), 0)
    ci = lax.broadcasted_iota(jnp.int32, (q, q), 1)
    same_head = lax.shift_right_logical(ri, 6) == lax.shift_right_logical(ci, 6)

    def step(ch, carry):
        rows = pl.ds(pl.multiple_of(ch * c, c), c)
        for qi in range(nq):
            lanes = slice(qi * q, (qi + 1) * q)
            hb = h_sc[qi].astype(BF16)
            y_ref[0, rows, lanes] = (_dot(rhat_ref[0, rows, lanes], hb) + y1_ref[0, rows, lanes]).astype(BF16)
            g_ls = g_ref[0, rows, lanes]
            n_ls = n_ref[0, rows, lanes]
            gbd = jnp.where(same_head, jnp.concatenate([g_ls] * (q // c), axis=0), 0.0)
            nbd = jnp.where(same_head, jnp.concatenate([n_ls] * (q // c), axis=0), 0.0)
            h_sc[qi] = _dot(gbd.astype(BF16), hb) + nbd
        return carry

    lax.fori_loop(0, rhat_ref.shape[1] // c, step, 0)


def _rw_scan(rhat, y1, g, n):
    bsz, s, d = rhat.shape
    rows = min(512, s)
    blk = pl.BlockSpec((1, rows, d), lambda bi, j: (bi, j, 0))
    return pl.pallas_call(
        _rw_scan_body,
        out_shape=jax.ShapeDtypeStruct((bsz, s, d), BF16),
        grid=(bsz, s // rows),
        in_specs=[blk] * 4,
        out_specs=blk,
        scratch_shapes=[pltpu.VMEM((d // RW_Q, RW_Q, RW_Q), F32)],
        compiler_params=_cparams(("parallel", "arbitrary")),
        name="rwkv_scan",
    )(rhat, y1, g, n)


def _rw_post_body(y_ref, r_ref, k_ref, v_ref, gg_ref, h_ref, vec_ref, w_ref, gpost_ref, sel_ref, selt_ref, o_ref):
    gn_w, gn_b, r_k = (vec_ref[i:i + 1, :] for i in range(3))
    f32 = lambda ref: ref[...].astype(F32)
    y = f32(y_ref)
    inv_n = 1.0 / HD
    mean = _head_expand(_head_sum(y, sel_ref) * inv_n, selt_ref)
    yc = y - mean
    var = _head_sum(yc * yc, sel_ref) * inv_n
    yn = yc * _head_expand(lax.rsqrt(var + GN_EPS), selt_ref) * gn_w + gn_b
    bonus = _head_expand(_head_sum(f32(r_ref) * f32(k_ref) * r_k, sel_ref), selt_ref) * f32(v_ref)
    z = ((yn + bonus) * f32(gg_ref)).astype(BF16)
    o_ref[...] = h_ref[...] + _rms(_dot(z, w_ref[...]), gpost_ref[...])


def _rw_post(y, r, k, v, gg, h, gn_w, gn_b, r_k, w_out, g_post):
    t, d = h.shape
    tm = min(PROJ_TM, t)
    vecs = _pad_rows(jnp.stack([gn_w, gn_b, r_k.reshape(d)]), 8)
    sel, selt = _head_selectors()
    tok = pl.BlockSpec((tm, d), lambda i: (i, 0))
    return pl.pallas_call(
        _rw_post_body,
        out_shape=jax.ShapeDtypeStruct((t, d), F32),
        grid=(t // tm,),
        in_specs=[tok] * 6 + [_const_spec((8, d)), _const_spec((d, d)), _const_spec((1, d)),
                              _const_spec((d, LANES)), _const_spec((LANES, d))],
        out_specs=tok,
        compiler_params=_cparams(("parallel",)),
        name="rwkv_post",
    )(y, r, k, v, gg, h, vecs, w_out.astype(BF16), g_post.reshape(1, d), sel, selt)


def _rwkv_mixer(h, g_pre, mu, w_in, w0, w_w2, a0, w_a2, w_g2, k_k, k_a, r_k, gn_w, gn_b, w_out, g_post):
    b, s, d = h.shape
    r, lw, k, v, a, bb, gg = _rw_proj(h, g_pre, mu, w_in, w0, w_w2, a0, w_a2, w_g2, k_k, k_a)
    rhat, y1, g, n = _rw_prep(r, lw, k, v, a, bb)
    y = _rw_scan(rhat, y1, g, n)
    f2 = lambda x: x.reshape(b * s, d)
    return _rw_post(f2(y), f2(r), f2(k), f2(v), f2(gg), f2(h), gn_w, gn_b, r_k, w_out, g_post).reshape(b, s, d)


def kernel(x, ffn1_norm_pre, ffn1_w_gu, ffn1_w_down, ffn1_norm_post, mix_norm_pre, nsa_w_in, nsa_pe_k,
           nsa_w_ck1, nsa_w_ck2, nsa_pe_v, nsa_w_cv1, nsa_w_cv2, nsa_w_out, rwkv_mu, rwkv_w_in, rwkv_w0,
           rwkv_w_w2, rwkv_a0, rwkv_w_a2, rwkv_w_g2, rwkv_k_k, rwkv_k_a, rwkv_r_k, rwkv_gn_w, rwkv_gn_b,
           rwkv_w_out, mix_norm_post, ffn2_norm_pre, ffn2_w_gu, ffn2_w_down, ffn2_norm_post):
    b, s, d = x.shape
    flat = lambda t: t.reshape(b * s, d)
    cube = lambda t: t.reshape(b, s, d)
    h = x
    depth = ffn1_norm_pre.shape[0]
    for i in range(depth):
        h = cube(_ffn(flat(h), ffn1_norm_pre[i], ffn1_w_gu[i], ffn1_w_down[i], ffn1_norm_post[i]))
        j = i // 2
        if i % 2 == 0:
            h = _nsa_mixer(h, mix_norm_pre[i], nsa_w_in[j], nsa_pe_k[j], nsa_w_ck1[j], nsa_w_ck2[j],
                           nsa_pe_v[j], nsa_w_cv1[j], nsa_w_cv2[j], nsa_w_out[j], mix_norm_post[i])
        else:
            h = _rwkv_mixer(h, mix_norm_pre[i], rwkv_mu[j], rwkv_w_in[j], rwkv_w0[j], rwkv_w_w2[j],
                            rwkv_a0[j], rwkv_w_a2[j], rwkv_w_g2[j], rwkv_k_k[j], rwkv_k_a[j], rwkv_r_k[j],
                            rwkv_gn_w[j], rwkv_gn_b[j], rwkv_w_out[j], mix_norm_post[i])
        h = cube(_ffn(flat(h), ffn2_norm_pre[i], ffn2_w_gu[i], ffn2_w_down[i], ffn2_norm_post[i]))
    return h
```

```python
import functools

import numpy as np
import jax
import jax.numpy as jnp
from jax import lax
from jax.experimental import pallas as pl
from jax.experimental.pallas import tpu as pltpu

F32 = jnp.float32
BF16 = jnp.bfloat16

D_MODEL = 1024
D_FF = 2816
HALF_STEP = 0.5
RMS_EPS = 1e-6
MASK_VALUE = -1e30

HD = 64
N_HEADS = 16
N_KV = 4
N_GRP = 4
KV_W = N_KV * HD
CMP_BLOCK = 32
CMP_STRIDE = 16
SEL_BLOCK = 64
N_SELECT = 16
WINDOW = 512
FORCED_SCORE = 1e4
N_GATE = 3 * N_HEADS

DECAY_LORA = 64
AAA_LORA = 64
GATE_LORA = 160
GN_EPS = 64e-5

LANES = 128
SUBLANES = 8
VMEM_LIMIT_BYTES = 56 * 1024 * 1024

FFN_TM = 512
FFN_CUTS = (0, 1536, 2816)
PROJ_TM = 512
ATT_T = 256
GATE_ROWS = 16
SUM_ROWS = 16
LOG2E = 1.4426950408889634
RW_C = 64
RW_Q = 128
RW_NCH = 16


def _cparams(sem):
    return pltpu.CompilerParams(dimension_semantics=sem, vmem_limit_bytes=VMEM_LIMIT_BYTES)


def _rms(x, g):
    ms = jnp.mean(x * x, axis=-1, keepdims=True)
    return x * lax.rsqrt(ms + RMS_EPS) * g


def _const_spec(shape):
    nd = len(shape)
    return pl.BlockSpec(shape, lambda *_: (0,) * nd, pipeline_mode=pl.Buffered(1))


def _dot(a, b):
    return jnp.dot(a, b, preferred_element_type=F32)


def _dot_nt(a, b):
    return lax.dot_general(a, b, (((1,), (1,)), ((), ())), preferred_element_type=F32)


def _dot_tn(a, b):
    return lax.dot_general(a, b, (((0,), (0,)), ((), ())), preferred_element_type=F32)


def _split2(x):
    hi = x.astype(BF16)
    lo = (x - hi.astype(F32)).astype(BF16)
    return hi, lo


def _split3(x):
    hi = x.astype(BF16)
    r1 = x - hi.astype(F32)
    mid = r1.astype(BF16)
    lo = (r1 - mid.astype(F32)).astype(BF16)
    return hi, mid, lo


def _ffn_body(x_ref, gpre_ref, wgu_ref, wd_ref, gpost_ref, o_ref):
    x = x_ref[...]
    xn = _rms(x, gpre_ref[...]).astype(BF16)
    acc = None
    for lo, hi in zip(FFN_CUTS[:-1], FFN_CUTS[1:]):
        gate = _dot(xn, wgu_ref[:, lo:hi])
        up = _dot(xn, wgu_ref[:, D_FF + lo:D_FF + hi])
        act = (gate * jax.nn.sigmoid(gate) * up).astype(BF16)
        part = _dot(act, wd_ref[lo:hi, :])
        acc = part if acc is None else acc + part
    o_ref[...] = x + HALF_STEP * _rms(acc, gpost_ref[...])


def _ffn(h2, g_pre, w_gu, w_down, g_post):
    t = h2.shape[0]
    tm = min(FFN_TM, t)
    return pl.pallas_call(
        _ffn_body,
        out_shape=jax.ShapeDtypeStruct((t, D_MODEL), F32),
        grid=(t // tm,),
        in_specs=[
            pl.BlockSpec((tm, D_MODEL), lambda i: (i, 0)),
            _const_spec((1, D_MODEL)),
            _const_spec((D_MODEL, 2 * D_FF)),
            _const_spec((D_FF, D_MODEL)),
            _const_spec((1, D_MODEL)),
        ],
        out_specs=pl.BlockSpec((tm, D_MODEL), lambda i: (i, 0)),
        compiler_params=_cparams(("parallel",)),
        name="ffn",
    )(h2, g_pre.reshape(1, D_MODEL), w_gu.astype(BF16), w_down.astype(BF16), g_post.reshape(1, D_MODEL))


def _nsa_proj_body(h_ref, g_ref, wqt_ref, wc_ref, wk_ref, wvt_ref, wgt_ref,
                   qt_ref, kc_ref, vc_ref, ks_ref, kw_ref, vst_ref, vwt_ref, gt_ref):
    tm = h_ref.shape[1]
    u = _rms(h_ref[0], g_ref[...]).astype(BF16)
    qt_ref[0] = (_dot_nt(wqt_ref[...], u) * (HD ** -0.5 * LOG2E)).astype(BF16)
    c = _dot(u, wc_ref[...])
    for j in range(KV_W // LANES):
        kc_ref[0, j] = c[:, j * LANES:(j + 1) * LANES]
        vc_ref[0, j] = c[:, KV_W + j * LANES:KV_W + (j + 1) * LANES]
    kk = _dot(u, wk_ref[...])
    t0 = pl.program_id(1) * tm
    lane = lax.broadcasted_iota(jnp.int32, (tm, LANES), 1)
    blk = lax.shift_right_logical(t0 + lax.broadcasted_iota(jnp.int32, (tm, LANES), 0), 6)
    onehot = jnp.where(lane - HD == blk, 1.0, 0.0)
    for g in range(N_KV):
        ks_ref[0, g] = (kk[:, g * LANES:(g + 1) * LANES] + onehot).astype(BF16)
        kw_ref[0, g] = kk[:, (N_KV + g) * LANES:(N_KV + g + 1) * LANES].astype(BF16)
    vt = _dot_nt(wvt_ref[...], u)
    tk = vst_ref.shape[4]
    for j in range(tm // tk):
        cols = slice(j * tk, (j + 1) * tk)
        vst_ref[0, :, j] = vt[:KV_W, cols].reshape(N_KV, HD, tk).astype(BF16)
        vwt_ref[0, :, j] = vt[KV_W:, cols].reshape(N_KV, HD, tk).astype(BF16)
    gt_ref[0] = jax.nn.sigmoid(_dot_nt(wgt_ref[...], u)).reshape(N_KV, GATE_ROWS, tm)


def _nsa_proj(h, g_pre, w_in):
    b, s, _ = h.shape
    tm = min(PROJ_TM, s)
    tk = min(ATT_T, s)
    nq = N_HEADS * HD
    assert s // SEL_BLOCK <= HD
    col = lambda i: w_in[:, nq + i * KV_W:nq + (i + 1) * KV_W]
    w_qt = w_in[:, :nq].T.astype(BF16)
    w_c = jnp.concatenate([col(0), col(1)], axis=1).astype(BF16)
    pad_heads = lambda w: jnp.pad(w.reshape(D_MODEL, N_KV, HD), ((0, 0), (0, 0), (0, LANES - HD))).reshape(D_MODEL, N_KV * LANES)
    w_k = jnp.concatenate([pad_heads(col(2)), pad_heads(col(4))], axis=1).astype(BF16)
    w_vt = jnp.concatenate([col(3), col(5)], axis=1).T.astype(BF16)
    w_gl = w_in[:, nq + 6 * KV_W:].reshape(D_MODEL, N_KV, N_GRP * 3)
    w_gt = jnp.pad(w_gl, ((0, 0), (0, 0), (0, GATE_ROWS - N_GRP * 3))).reshape(D_MODEL, N_KV * GATE_ROWS).T.astype(BF16)
    tok = lambda w: pl.BlockSpec((1, tm, w), lambda bi, i: (bi, i, 0))
    cmp_spec = pl.BlockSpec((1, KV_W // LANES, tm, LANES), lambda bi, i: (bi, 0, i, 0))
    key = jax.ShapeDtypeStruct((b, N_KV, s, LANES), BF16)
    key_spec = pl.BlockSpec((1, N_KV, tm, LANES), lambda bi, i: (bi, 0, i, 0))
    valt = jax.ShapeDtypeStruct((b, N_KV, s // tk, HD, tk), BF16)
    valt_spec = pl.BlockSpec((1, N_KV, tm // tk, HD, tk), lambda bi, i: (bi, 0, i, 0, 0))
    return pl.pallas_call(
        _nsa_proj_body,
        out_shape=(
            jax.ShapeDtypeStruct((b, nq, s), BF16),
            jax.ShapeDtypeStruct((b, KV_W // LANES, s, LANES), F32),
            jax.ShapeDtypeStruct((b, KV_W // LANES, s, LANES), F32),
            key, key, valt, valt,
            jax.ShapeDtypeStruct((b, N_KV, GATE_ROWS, s), F32),
        ),
        grid=(b, s // tm),
        in_specs=[
            tok(D_MODEL),
            _const_spec((1, D_MODEL)),
            _const_spec((nq, D_MODEL)),
            _const_spec((D_MODEL, 2 * KV_W)),
            _const_spec((D_MODEL, 2 * N_KV * LANES)),
            _const_spec((2 * KV_W, D_MODEL)),
            _const_spec((N_KV * GATE_ROWS, D_MODEL)),
        ],
        out_specs=(
            pl.BlockSpec((1, nq, tm), lambda bi, i: (bi, 0, i)),
            cmp_spec, cmp_spec, key_spec, key_spec, valt_spec, valt_spec,
            pl.BlockSpec((1, N_KV, GATE_ROWS, tm), lambda bi, i: (bi, 0, 0, i)),
        ),
        compiler_params=_cparams(("parallel", "parallel")),
        name="nsa_proj",
    )(h, g_pre.reshape(1, D_MODEL), w_qt, w_c, w_k, w_vt, w_gt)


def _compress_body(x_ref, pe_ref, w1_ref, w2_ref, o_ref, *, transposed):
    half = CMP_BLOCK // 2
    n = x_ref.shape[2] // half
    ha = hb = None
    for l in range(half):
        xl = jnp.concatenate([x_ref[0, j, pl.ds(l, n, stride=half), :] for j in range(x_ref.shape[1])], axis=-1)
        a = _dot((xl + pe_ref[l:l + 1, :]).astype(BF16), w1_ref[l])
        b = _dot((xl + pe_ref[half + l:half + l + 1, :]).astype(BF16), w1_ref[half + l])
        ha = a if ha is None else ha + a
        hb = b if hb is None else hb + b
    hid = ha + pltpu.roll(hb, n - 1, 0)
    hid = (hid * jax.nn.sigmoid(hid)).astype(BF16)
    if transposed:
        o_ref[0] = _dot_nt(w2_ref[...], hid).reshape(N_KV, HD, n).astype(BF16)
    else:
        out = _dot(hid, w2_ref[...]).astype(BF16)
        for g in range(N_KV):
            o_ref[0, g] = out[:, g * HD:(g + 1) * HD]


def _compress(t, pe, w1, w2, transposed):
    b, _, s, _ = t.shape
    rows = s // (CMP_BLOCK // 2)
    eye = jnp.eye(N_KV, dtype=F32)
    wbd = jnp.einsum("ldc,gh->lgdhc", w1, eye).reshape(CMP_BLOCK, KV_W, KV_W).astype(BF16)
    pe_t = jnp.broadcast_to(pe[:, None, :], (CMP_BLOCK, N_KV, HD)).reshape(CMP_BLOCK, KV_W)
    w2bd = jnp.einsum("cd,gh->gchd", w2, eye).reshape(KV_W, KV_W)
    if transposed:
        w2bd = w2bd.T
        out_shape, out_block = (b, N_KV, HD, rows), (1, N_KV, HD, rows)
    else:
        out_shape, out_block = (b, N_KV, rows, HD), (1, N_KV, rows, HD)
    return pl.pallas_call(
        functools.partial(_compress_body, transposed=transposed),
        out_shape=jax.ShapeDtypeStruct(out_shape, BF16),
        grid=(b,),
        in_specs=[
            pl.BlockSpec((1, KV_W // LANES, s, LANES), lambda bi: (bi, 0, 0, 0)),
            _const_spec((CMP_BLOCK, KV_W)),
            _const_spec((CMP_BLOCK, KV_W, KV_W)),
            _const_spec((KV_W, KV_W)),
        ],
        out_specs=pl.BlockSpec(out_block, lambda bi: (bi, 0, 0, 0)),
        compiler_params=_cparams(("parallel",)),
        name="nsa_compress",
    )(t, pe_t, wbd, w2bd.astype(BF16))


def _alibi_slope(h):
    return float(2.0 ** (-8.0 * (h + 1) / N_HEADS))


def _cmp_probs(kc, qh, slope2, end_f, valid):
    s = _dot(kc, qh) + jnp.where(valid, slope2 * end_f, MASK_VALUE)
    return jnp.exp2(s - jnp.max(s, axis=0, keepdims=True)).astype(BF16)


def _cmp_geometry(n_cmp_rows, tq, t0):
    n_io = lax.broadcasted_iota(jnp.int32, (n_cmp_rows, tq), 0)
    t_io = t0 + lax.broadcasted_iota(jnp.int32, (n_cmp_rows, tq), 1)
    end = n_io * CMP_STRIDE + (CMP_BLOCK - 1)
    return end.astype(F32), (t_io >= end) & (n_io < n_cmp_rows - 1)


def _select_body(qt_ref, kc_ref, vct_ref, c2s_ref, chunk_ref, mneg_ref, active_ref, oc_ref, *, n_sel, n_cmp_rows):
    tq = qt_ref.shape[2]
    t0 = pl.program_id(1) * tq
    end_f, valid = _cmp_geometry(n_cmp_rows, tq, t0)
    has_block = t0 + lax.broadcasted_iota(jnp.int32, (1, tq), 1) >= CMP_BLOCK - 1
    v_lo = n_sel + SUM_ROWS
    s_io = lax.broadcasted_iota(jnp.int32, (n_sel, tq), 0)
    tt = t0 + lax.broadcasted_iota(jnp.int32, (n_sel, tq), 1)
    cur = lax.shift_right_logical(tt, 6)
    forced = (s_io == 0) | (s_io == cur) | (s_io == cur - 1)
    visible = s_io * SEL_BLOCK <= tt

    for g in range(N_KV):
        kc = kc_ref[0, g]
        heads = range(g * N_GRP, (g + 1) * N_GRP)
        probs = [_cmp_probs(kc, qt_ref[0, hh * HD:(hh + 1) * HD, :], _alibi_slope(hh) * LOG2E, end_f, valid)
                 for hh in heads]
        lhs = jnp.concatenate([c2s_ref[...], vct_ref[0, g]], axis=0)
        mass = [_dot(lhs, p) for p in probs]
        inv_l = [1.0 / m[n_sel:n_sel + 1] for m in mass]
        imp = sum(m[0:n_sel] * il for m, il in zip(mass, inv_l))
        for hh, m, il in zip(heads, mass, inv_l):
            oc_ref[0, hh * HD:(hh + 1) * HD, :] = jnp.where(has_block, m[v_lo:v_lo + HD] * il, 0.0).astype(BF16)
        x = jnp.where(forced, FORCED_SCORE, jnp.where(visible, imp, -1.0))
        cnt = jnp.zeros((n_sel, tq), F32)
        for j in range(n_sel):
            xj = x[j:j + 1, :]
            ge = jnp.where(xj >= x, 1.0, 0.0)
            gt = jnp.where(xj > x, 1.0, 0.0)
            cnt = cnt + jnp.where(s_io > j, ge, gt)
        chosen = cnt < float(N_SELECT)
        mneg_ref[0, g, 0:n_sel, :] = jnp.where(chosen, 0.0, MASK_VALUE).astype(BF16)
        if n_sel < HD:
            mneg_ref[0, g, n_sel:HD, :] = jnp.zeros((HD - n_sel, tq), BF16)
        per_query = _dot(chunk_ref[...], jnp.where(chosen, 1.0, 0.0).astype(BF16))
        active_ref[0, g, 0] = _dot_nt(jnp.ones((SUBLANES, tq), BF16), per_query.astype(BF16))


def _select(qt, k_cmp, v_cmpt):
    b, nq, s = qt.shape
    tq = min(ATT_T, s)
    n_sel = s // SEL_BLOCK
    rows = k_cmp.shape[2]
    n_cmp = (s - CMP_BLOCK) // CMP_STRIDE + 1
    assert rows == n_cmp + 1
    cs = np.arange(rows) * CMP_STRIDE
    bs = np.arange(n_sel) * SEL_BLOCK
    overlap = np.clip(np.minimum(cs[:, None] + CMP_BLOCK, bs[None, :] + SEL_BLOCK)
                      - np.maximum(cs[:, None], bs[None, :]), 0, None)
    c2s_t = jnp.asarray(np.concatenate([(overlap / CMP_BLOCK).T, np.ones((SUM_ROWS, rows))], axis=0), BF16)
    n_chunks = s // tq
    chunk_of_block = np.arange(n_sel) * SEL_BLOCK // tq
    chunk_mat = jnp.asarray(chunk_of_block[None, :] == np.arange(n_chunks)[:, None], BF16)
    return pl.pallas_call(
        functools.partial(_select_body, n_sel=n_sel, n_cmp_rows=rows),
        out_shape=(jax.ShapeDtypeStruct((b, N_KV, HD, s), BF16),
                   jax.ShapeDtypeStruct((b, N_KV, s // tq, SUBLANES, n_chunks), F32),
                   jax.ShapeDtypeStruct((b, nq, s), BF16)),
        grid=(b, s // tq),
        in_specs=[
            pl.BlockSpec((1, nq, tq), lambda bi, i: (bi, 0, i)),
            pl.BlockSpec((1, N_KV, rows, HD), lambda bi, i: (bi, 0, 0, 0)),
            pl.BlockSpec((1, N_KV, HD, rows), lambda bi, i: (bi, 0, 0, 0)),
            _const_spec((n_sel + SUM_ROWS, rows)),
            _const_spec((n_chunks, n_sel)),
        ],
        out_specs=(pl.BlockSpec((1, N_KV, HD, tq), lambda bi, i: (bi, 0, 0, i)),
                   pl.BlockSpec((1, N_KV, 1, SUBLANES, n_chunks), lambda bi, i: (bi, 0, i, 0, 0)),
                   pl.BlockSpec((1, nq, tq), lambda bi, i: (bi, 0, i))),
        compiler_params=_cparams(("parallel", "parallel")),
        name="nsa_select",
    )(qt, k_cmp, v_cmpt, c2s_t, chunk_mat)


def _attn_body(slope_ref, active_ref, qt_ref, ks_ref, kw_ref, vst_ref, vwt_ref, oc_ref, mneg_ref,
               gt_ref, o_ref, s_sc, bias_sc, todo_ref):
    tq = qt_ref.shape[2]
    tk = tq
    g = pl.program_id(1)
    qi = pl.program_id(2)
    slopes = [slope_ref[g * N_GRP + r] for r in range(N_GRP)]
    q_rows = [qt_ref[0, r * HD:(r + 1) * HD, :] for r in range(N_GRP)]
    qk = [jnp.concatenate([q_rows[r], mneg_ref[0, 0]], axis=0) for r in range(N_GRP)]

    @pl.when(qi == 0)
    def _():
        key_io = lax.broadcasted_iota(jnp.int32, (tk, tq), 0)
        rel = key_io - lax.broadcasted_iota(jnp.int32, (tk, tq), 1)
        key_f = key_io.astype(F32)
        causal = jnp.where(rel <= 0, 0.0, MASK_VALUE)
        oldest = jnp.where(rel > 0, 0.0, MASK_VALUE)
        for r in range(N_GRP):
            base = slopes[r] * key_f
            bias_sc[0, r] = base
            bias_sc[1, r] = base + causal
            bias_sc[2, r] = base + oldest

    ones_rows = jnp.ones((SUM_ROWS, tk), BF16)

    def offsets(c):
        cf = ((c - qi) * tk).astype(F32)
        return [slopes[r] * cf for r in range(N_GRP)]

    def score_stage(c, k_ref, window, slot):
        kblk = k_ref[0, 0, pl.ds(pl.multiple_of(c * tk, tk), tk), :]
        variant = jnp.where(c == qi, 1, 0)
        if window:
            variant = jnp.where(c == qi - WINDOW // tk, 2, variant)
        offs = offsets(c)
        mcs = []
        for r in range(N_GRP):
            s = _dot(kblk, qk[r]) + bias_sc[variant, r]
            s_sc[int(window), slot, r] = s
            mcs.append(jnp.max(s, axis=0, keepdims=True) + offs[r])
        return tuple(mcs)

    def value_stage(c, carry, mcs, vt_ref, window, slot):
        vt = jnp.concatenate([vt_ref[0, 0, c], ones_rows], axis=0)
        offs = offsets(c)
        out = []
        for r in range(N_GRP):
            m, acc = carry[r]
            m_new = jnp.maximum(m, mcs[r])
            p = jnp.exp2(s_sc[int(window), slot, r] - (m_new - offs[r]))
            out.append((m_new, jnp.exp2(m - m_new) * acc + _dot(vt, p.astype(BF16))))
        return tuple(out)

    def first_scores(k_ref, chunk_at, window):
        init = tuple((jnp.full((1, tq), MASK_VALUE, F32), jnp.zeros((HD + SUM_ROWS, tq), F32))
                     for _ in range(N_GRP))
        return init, score_stage(chunk_at(0), k_ref, window, 0)

    def pipelined(k_ref, vt_ref, chunk_at, steps, window, state):
        def trips(n_chunks, first, count, state):
            def trip(k, state):
                carry, mcs = state
                pos = first + n_chunks * k
                for i in range(n_chunks):
                    mcs_next = score_stage(chunk_at(pos + i + 1), k_ref, window, (i + 1) % 2)
                    carry = value_stage(chunk_at(pos + i), carry, mcs, vt_ref, window, i % 2)
                    mcs = mcs_next
                return carry, mcs
            return lax.fori_loop(0, count, trip, state)

        done = 0
        if not window:
            state = trips(4, 0, steps // 4, state)
            done = (steps // 4) * 4
        state = trips(2, done, (steps - done) // 2, state)
        done = done + ((steps - done) // 2) * 2
        state = trips(1, done, steps - done, state)
        return state, steps - done

    tile = (pl.program_id(0) * N_KV + g) * pl.num_programs(2) + qi

    def note_active(c, n):
        todo_ref[n] = c
        return n + active_ref[tile * pl.num_programs(2) + c]

    n_sel_chunks = lax.fori_loop(0, qi, note_active, jnp.int32(0))
    sel_chunk = lambda pos: jnp.where(pos < n_sel_chunks, todo_ref[jnp.minimum(pos, pl.num_programs(2) - 1)], qi)
    c_lo_w = jnp.maximum(qi - WINDOW // tk, 0)
    win_chunk = lambda pos: c_lo_w + pos

    (carry_s, mcs_s), slot_s = pipelined(ks_ref, vst_ref, sel_chunk, n_sel_chunks, False,
                                         first_scores(ks_ref, sel_chunk, False))
    state_w = first_scores(kw_ref, win_chunk, True)
    res_s = value_stage(qi, carry_s, mcs_s, vst_ref, False, slot_s)
    (carry_w, mcs_w), slot_w = pipelined(kw_ref, vwt_ref, win_chunk, qi - c_lo_w, True, state_w)

    res_w = value_stage(qi, carry_w, mcs_w, vwt_ref, True, slot_w)

    gate = gt_ref[0, 0]
    normed = lambda acc: acc[0:HD] * (1.0 / acc[HD:HD + 1])
    for r in range(N_GRP):
        o_c = oc_ref[0, r * HD:(r + 1) * HD, :].astype(F32)
        o_s = normed(res_s[r][1])
        o_w = normed(res_w[r][1])
        out = (gate[3 * r:3 * r + 1, :] * o_c + gate[3 * r + 1:3 * r + 2, :] * o_s
               + gate[3 * r + 2:3 * r + 3, :] * o_w)
        o_ref[0, r * HD:(r + 1) * HD, :] = out.astype(BF16)


def _attention(qt, ks, kw, vst, vwt, oc_t, mneg, active, gates_t):
    b, nq, s = qt.shape
    tq = vst.shape[4]
    assert WINDOW % tq == 0 and active.shape == (b, N_KV, s // tq, SUBLANES, s // tq)
    active = (active[:, :, :, 0, :] > 0.0).astype(jnp.int32).reshape(-1)
    gw = N_GRP * HD
    slopes = jnp.asarray([_alibi_slope(h) * LOG2E for h in range(N_HEADS)], F32)
    keys = pl.BlockSpec((1, 1, s, LANES), lambda bi, g, i: (bi, g, 0, 0))
    vals = pl.BlockSpec((1, 1, s // tq, HD, tq), lambda bi, g, i: (bi, g, 0, 0, 0))
    return pl.pallas_call(
        _attn_body,
        out_shape=jax.ShapeDtypeStruct((b, nq, s), BF16),
        grid=(b, N_KV, s // tq),
        in_specs=[
            pl.BlockSpec(memory_space=pltpu.SMEM),
            pl.BlockSpec(memory_space=pltpu.SMEM),
            pl.BlockSpec((1, gw, tq), lambda bi, g, i: (bi, g, i)),
            keys, keys, vals, vals,
            pl.BlockSpec((1, gw, tq), lambda bi, g, i: (bi, g, i)),
            pl.BlockSpec((1, 1, HD, tq), lambda bi, g, i: (bi, g, 0, i)),
            pl.BlockSpec((1, 1, GATE_ROWS, tq), lambda bi, g, i: (bi, g, 0, i)),
        ],
        out_specs=pl.BlockSpec((1, gw, tq), lambda bi, g, i: (bi, g, i)),
        scratch_shapes=[pltpu.VMEM((2, 2, N_GRP, tq, tq), F32), pltpu.VMEM((3, N_GRP, tq, tq), F32),
                        pltpu.SMEM((s // tq,), jnp.int32)],
        compiler_params=_cparams(("parallel", "parallel", "arbitrary")),
        name="nsa_attention",
    )(slopes, active, qt, ks, kw, vst, vwt, oc_t, mneg, gates_t)


def _outproj_t_body(yt_ref, h_ref, w_ref, g_ref, o_ref):
    o_ref[0] = h_ref[0] + _rms(_dot_tn(yt_ref[0], w_ref[...]), g_ref[...])


def _outproj_t(yt, h, w_out, g_post):
    b, s, d = h.shape
    tm = min(PROJ_TM, s)
    return pl.pallas_call(
        _outproj_t_body,
        out_shape=jax.ShapeDtypeStruct((b, s, d), F32),
        grid=(b, s // tm),
        in_specs=[
            pl.BlockSpec((1, d, tm), lambda bi, i: (bi, 0, i)),
            pl.BlockSpec((1, tm, d), lambda bi, i: (bi, i, 0)),
            _const_spec((d, d)),
            _const_spec((1, d)),
        ],
        out_specs=pl.BlockSpec((1, tm, d), lambda bi, i: (bi, i, 0)),
        compiler_params=_cparams(("parallel", "parallel")),
        name="outproj_t",
    )(yt, h, w_out.astype(BF16), g_post.reshape(1, d))


def _nsa_mixer(h, g_pre, w_in, pe_k, w_ck1, w_ck2, pe_v, w_cv1, w_cv2, w_out, g_post):
    qt, kc, vc, ks, kw, vst, vwt, gates_t = _nsa_proj(h, g_pre, w_in)
    k_cmp = _compress(kc, pe_k, w_ck1, w_ck2, transposed=False)
    v_cmpt = _compress(vc, pe_v, w_cv1, w_cv2, transposed=True)
    mneg, active, oc_t = _select(qt, k_cmp, v_cmpt)
    out_t = _attention(qt, ks, kw, vst, vwt, oc_t, mneg, active, gates_t)
    return _outproj_t(out_t, h, w_out, g_post)


def _head_sum(x, sel_ref):
    return _dot(x.astype(BF16), sel_ref[...])


def _head_expand(x, selt_ref):
    hi, lo = _split2(x)
    return _dot(hi, selt_ref[...]) + _dot(lo, selt_ref[...])


def _softplus(x):
    return jnp.maximum(x, 0.0) + jnp.log(1.0 + jnp.exp(-jnp.abs(x)))


def _rw_proj_body(h_ref, hp_ref, g_ref, mu_ref, vec_ref, wr_ref, wk_ref, wv_ref, wd_ref, wa_ref, wg_ref,
                  w2d_ref, w2a_ref, w2g_ref, sel_ref, selt_ref,
                  r_ref, lw_ref, k_ref, v_ref, a_ref, b_ref, gg_ref):
    tm = h_ref.shape[1]
    u = _rms(h_ref[0], g_ref[...])
    prev = _rms(hp_ref[0], g_ref[...])[7:8, :]
    prev = jnp.where(pl.program_id(1) == 0, 0.0, prev)
    row = lax.broadcasted_iota(jnp.int32, (tm, D_MODEL), 0)
    u_prev = jnp.where(row == 0, prev, pltpu.roll(u, 1, 0))
    xx = u_prev - u

    def mix(i):
        return (u + xx * mu_ref[i:i + 1, :]).astype(BF16)

    w0, a0, k_k, k_a = (vec_ref[i:i + 1, :] for i in range(4))
    r = _dot(mix(0), wr_ref[...])
    k = _dot(mix(1), wk_ref[...])
    v = _dot(mix(2), wv_ref[...])
    d1 = jnp.tanh(_dot(mix(3), wd_ref[...])).astype(BF16)
    a1 = _dot(mix(4), wa_ref[...]).astype(BF16)
    g1 = jax.nn.sigmoid(_dot(mix(5), wg_ref[...])).astype(BF16)
    w = -_softplus(-(w0 + _dot(d1, w2d_ref[...]))) - 0.5
    alpha = jax.nn.sigmoid(a0 + _dot(a1, w2a_ref[...]))
    kk = k * k_k
    norm = jnp.sqrt(_head_sum(kk * kk, sel_ref))
    kk = kk * _head_expand(1.0 / jnp.maximum(norm, 1e-12), selt_ref)
    r_ref[0] = r.astype(BF16)
    lw_ref[0] = -jnp.exp(w)
    k_ref[0] = (k * (1.0 + (alpha - 1.0) * k_a)).astype(BF16)
    v_ref[0] = v.astype(BF16)
    a_ref[0] = (-kk).astype(BF16)
    b_ref[0] = (kk * alpha).astype(BF16)
    gg_ref[0] = _dot(g1, w2g_ref[...]).astype(BF16)


def _head_selectors():
    lane_head = np.arange(D_MODEL) // HD
    sel = (lane_head[:, None] == np.arange(LANES)[None, :]).astype(np.float32)
    return jnp.asarray(sel, BF16), jnp.asarray(sel.T, BF16)


def _pad_cols(w, n):
    return jnp.pad(w, ((0, 0), (0, n - w.shape[1])))


def _pad_rows(w, n):
    return jnp.pad(w, ((0, n - w.shape[0]), (0, 0)))


def _rw_proj(h, g_pre, mu, w_in, w0, w_w2, a0, w_a2, w_g2, k_k, k_a):
    b, s, d = h.shape
    tm = min(PROJ_TM, s)
    offs = np.cumsum((0, d, d, d, DECAY_LORA, AAA_LORA, GATE_LORA))
    cols = [w_in[:, int(offs[i]):int(offs[i + 1])] for i in range(6)]
    ld, lg = LANES, 2 * LANES
    wr, wk, wv = (c.astype(BF16) for c in cols[:3])
    wd = _pad_cols(cols[3], ld).astype(BF16)
    wa = _pad_cols(cols[4], ld).astype(BF16)
    wg = _pad_cols(cols[5], lg).astype(BF16)
    w2d = _pad_rows(w_w2, ld).astype(BF16)
    w2a = _pad_rows(w_a2, ld).astype(BF16)
    w2g = _pad_rows(w_g2, lg).astype(BF16)
    mu8 = _pad_rows(mu, 8)
    vecs = _pad_rows(jnp.stack([w0, a0, k_k, k_a]), 8)
    sel, selt = _head_selectors()
    tok = pl.BlockSpec((1, tm, d), lambda bi, i: (bi, i, 0))
    out = lambda dt: jax.ShapeDtypeStruct((b, s, d), dt)
    return pl.pallas_call(
        _rw_proj_body,
        out_shape=(out(BF16), out(F32), out(BF16), out(BF16), out(BF16), out(BF16), out(BF16)),
        grid=(b, s // tm),
        in_specs=[
            tok,
            pl.BlockSpec((1, 8, d), lambda bi, i: (bi, jnp.maximum(i * (tm // 8) - 1, 0), 0)),
            _const_spec((1, d)), _const_spec((8, d)), _const_spec((8, d)),
            _const_spec((d, d)), _const_spec((d, d)), _const_spec((d, d)),
            _const_spec((d, ld)), _const_spec((d, ld)), _const_spec((d, lg)),
            _const_spec((ld, d)), _const_spec((ld, d)), _const_spec((lg, d)),
            _const_spec((d, LANES)), _const_spec((LANES, d)),
        ],
        out_specs=(tok,) * 7,
        compiler_params=_cparams(("parallel", "parallel")),
        name="rwkv_proj",
    )(h, h, g_pre.reshape(1, d), mu8, vecs, wr, wk, wv, wd, wa, wg, w2d, w2a, w2g, sel, selt)


def _rw_prep_body(ltri_ref, r_ref, lw_ref, k_ref, v_ref, a_ref, b_ref,
                  rhat_ref, y1_ref, g_ref, n_ref):
    c = RW_C
    q = RW_Q
    nh = q // HD
    nch = r_ref.shape[1] // c
    lane_head = lax.shift_right_logical(lax.broadcasted_iota(jnp.int32, (c, q), 1), 6)
    ri = lax.broadcasted_iota(jnp.int32, (q, q), 0)
    ci = lax.broadcasted_iota(jnp.int32, (q, q), 1)
    same_head = lax.shift_right_logical(ri, 6) == lax.shift_right_logical(ci, 6)
    strict_bd = same_head & ((ci & (c - 1)) < (ri & (c - 1)))
    eye = ri == ci
    t_io = lax.broadcasted_iota(jnp.int32, (c, q), 0)
    j_io = lax.broadcasted_iota(jnp.int32, (c, q), 1) & (c - 1)
    strict_ls = j_io < t_io
    incl_ls = j_io <= t_io
    rows = [slice(i * c, (i + 1) * c) for i in range(nch)]

    def each(f):
        return [f(i) for i in range(nch)]

    def expand(x):
        return jnp.concatenate([jnp.where(lane_head == hh, x, 0.0) for hh in range(nh)], axis=0).astype(BF16)

    def collapse(x):
        out = x[0:c]
        for hh in range(1, nh):
            out = out + x[hh * c:(hh + 1) * c]
        return out

    lw = each(lambda i: lw_ref[0, rows[i], :])
    parts = each(lambda i: _dot(ltri_ref[...], jnp.concatenate(_split3(lw[i]), axis=1)))
    cum = each(lambda i: parts[i][:, 0:q] + parts[i][:, q:2 * q] + parts[i][:, 2 * q:3 * q])
    cum_c = each(lambda i: cum[i][c - 1:c, :])
    e_inv = each(lambda i: jnp.exp(-cum[i]))
    e_rem = each(lambda i: jnp.exp(cum_c[i] - cum[i]))
    at = each(lambda i: a_ref[0, rows[i], :] * jnp.exp(cum[i] - lw[i]))
    rt = each(lambda i: r_ref[0, rows[i], :] * jnp.exp(cum[i]))
    x4 = each(lambda i: expand(at[i]))
    b4 = each(lambda i: expand(b_ref[0, rows[i], :] * e_inv[i]))
    k4 = each(lambda i: expand(k_ref[0, rows[i], :] * e_inv[i]))
    v4 = each(lambda i: expand(v_ref[0, rows[i], :]))
    bbar = each(lambda i: (b_ref[0, rows[i], :] * e_rem[i]).astype(BF16))
    kbar = each(lambda i: (k_ref[0, rows[i], :] * e_rem[i]).astype(BF16))

    lbd = each(lambda i: jnp.where(strict_bd, _dot_nt(x4[i], b4[i]), 0.0))
    tbd = each(lambda i: jnp.where(eye, 1.0, lbd[i]))
    lb = each(lambda i: lbd[i].astype(BF16))
    p = each(lambda i: _dot(lb[i], lb[i]))
    n_lvl = int(np.log2(c)) - 1
    for lvl in range(n_lvl):
        pb = each(lambda i: p[i].astype(BF16))
        if lvl + 1 < n_lvl:
            both = each(lambda i: _dot(pb[i], jnp.concatenate([tbd[i].astype(BF16), pb[i]], axis=1)))
            tbd = each(lambda i: tbd[i] + both[i][:, 0:q])
            p = each(lambda i: both[i][:, q:2 * q])
        else:
            tbd = each(lambda i: tbd[i] + _dot(pb[i], tbd[i].astype(BF16)))
    t_ls = each(lambda i: collapse(tbd[i]).astype(BF16))

    w = nh * c
    ar = each(lambda i: _dot_nt(jnp.concatenate([at[i], rt[i]], axis=0).astype(BF16),
                                jnp.concatenate([k4[i], b4[i]], axis=0)))
    a_kk = each(lambda i: jnp.concatenate([jnp.where(strict_ls, ar[i][0:c, 0:w], 0.0),
                                           jnp.where(incl_ls, ar[i][c:2 * c, 0:w], 0.0)], axis=0).astype(BF16))
    a_rb = each(lambda i: jnp.where(incl_ls, ar[i][c:2 * c, w:2 * w], 0.0).astype(BF16))

    gy = each(lambda i: _dot(a_kk[i], v4[i]))
    g1 = each(lambda i: gy[i][0:c])
    ua = each(lambda i: _dot(t_ls[i], jnp.concatenate([expand(g1[i]), x4[i]], axis=1)))
    u0 = each(lambda i: ua[i][:, 0:q])
    ahat = each(lambda i: ua[i][:, q:2 * q])
    ry = each(lambda i: _dot(a_rb[i], jnp.concatenate([expand(ahat[i]), expand(u0[i])], axis=1)))
    rhat = each(lambda i: rt[i] + ry[i][:, 0:q])
    y1 = each(lambda i: gy[i][c:2 * c] + ry[i][:, q:2 * q])

    gm = each(lambda i: _dot_tn(bbar[i], ahat[i].astype(BF16)))
    nm = each(lambda i: _dot_tn(jnp.concatenate([bbar[i], kbar[i]], axis=0),
                                jnp.concatenate([u0[i].astype(BF16), v_ref[0, rows[i], :]], axis=0)))
    for i in range(nch):
        gmi = jnp.where(same_head, gm[i], 0.0) + jnp.where(eye, jnp.exp(cum_c[i]), 0.0)
        rhat_ref[0, rows[i], :] = rhat[i].astype(BF16)
        y1_ref[0, rows[i], :] = y1[i].astype(BF16)
        g_ref[0, rows[i], :] = collapse(gmi).astype(BF16)
        n_ref[0, rows[i], :] = collapse(jnp.where(same_head, nm[i], 0.0)).astype(BF16)


def _rw_prep(r, lw, k, v, a, b):
    bsz, s, d = r.shape
    rows = min(RW_C * RW_NCH, s)
    ltri = jnp.asarray(np.tril(np.ones((RW_C, RW_C), np.float32)), BF16)
    blk = pl.BlockSpec((1, rows, RW_Q), lambda bi, qi, j: (bi, j, qi))
    return pl.pallas_call(
        _rw_prep_body,
        out_shape=(jax.ShapeDtypeStruct((bsz, s, d), BF16),) * 4,
        grid=(bsz, d // RW_Q, s // rows),
        in_specs=[_const_spec((RW_C, RW_C))] + [blk] * 6,
        out_specs=(blk,) * 4,
        compiler_params=_cparams(("parallel", "parallel", "parallel")),
        name="rwkv_prep",
    )(ltri, r, lw, k, v, a, b)


def _rw_scan_body(rhat_ref, y1_ref, g_ref, n_ref, y_ref, h_sc):
    c = RW_C
    q = RW_Q
    nq = h_sc.shape[0]

    @pl.when(pl.program_id(1) == 0)
    def _():
        h_sc[...] = jnp.zeros(h_sc.shape, F32)

    ri = lax.broadcasted_iota(jnp.int32, (q, q), 0)
    ci = lax.broadcasted_iota(jnp.int32, (q, q), 1)
    same_head = lax.shift_right_logical(ri, 6) == lax.shift_right_logical(ci, 6)

    def step(ch, carry):
        rows = pl.ds(pl.multiple_of(ch * c, c), c)
        for qi in range(nq):
            lanes = slice(qi * q, (qi + 1) * q)
            hb = h_sc[qi].astype(BF16)
            y_ref[0, rows, lanes] = (_dot(rhat_ref[0, rows, lanes], hb) + y1_ref[0, rows, lanes]).astype(BF16)
            g_ls = g_ref[0, rows, lanes]
            n_ls = n_ref[0, rows, lanes]
            gbd = jnp.where(same_head, jnp.concatenate([g_ls] * (q // c), axis=0), 0.0)
            nbd = jnp.where(same_head, jnp.concatenate([n_ls] * (q // c), axis=0), 0.0)
            h_sc[qi] = _dot(gbd.astype(BF16), hb) + nbd
        return carry

    lax.fori_loop(0, rhat_ref.shape[1] // c, step, 0)


def _rw_scan(rhat, y1, g, n):
    bsz, s, d = rhat.shape
    rows = min(512, s)
    blk = pl.BlockSpec((1, rows, d), lambda bi, j: (bi, j, 0))
    return pl.pallas_call(
        _rw_scan_body,
        out_shape=jax.ShapeDtypeStruct((bsz, s, d), BF16),
        grid=(bsz, s // rows),
        in_specs=[blk] * 4,
        out_specs=blk,
        scratch_shapes=[pltpu.VMEM((d // RW_Q, RW_Q, RW_Q), F32)],
        compiler_params=_cparams(("parallel", "arbitrary")),
        name="rwkv_scan",
    )(rhat, y1, g, n)


def _rw_post_body(y_ref, r_ref, k_ref, v_ref, gg_ref, h_ref, vec_ref, w_ref, gpost_ref, sel_ref, selt_ref, o_ref):
    gn_w, gn_b, r_k = (vec_ref[i:i + 1, :] for i in range(3))
    f32 = lambda ref: ref[...].astype(F32)
    y = f32(y_ref)
    inv_n = 1.0 / HD
    mean = _head_expand(_head_sum(y, sel_ref) * inv_n, selt_ref)
    yc = y - mean
    var = _head_sum(yc * yc, sel_ref) * inv_n
    yn = yc * _head_expand(lax.rsqrt(var + GN_EPS), selt_ref) * gn_w + gn_b
    bonus = _head_expand(_head_sum(f32(r_ref) * f32(k_ref) * r_k, sel_ref), selt_ref) * f32(v_ref)
    z = ((yn + bonus) * f32(gg_ref)).astype(BF16)
    o_ref[...] = h_ref[...] + _rms(_dot(z, w_ref[...]), gpost_ref[...])


def _rw_post(y, r, k, v, gg, h, gn_w, gn_b, r_k, w_out, g_post):
    t, d = h.shape
    tm = min(PROJ_TM, t)
    vecs = _pad_rows(jnp.stack([gn_w, gn_b, r_k.reshape(d)]), 8)
    sel, selt = _head_selectors()
    tok = pl.BlockSpec((tm, d), lambda i: (i, 0))
    return pl.pallas_call(
        _rw_post_body,
        out_shape=jax.ShapeDtypeStruct((t, d), F32),
        grid=(t // tm,),
        in_specs=[tok] * 6 + [_const_spec((8, d)), _const_spec((d, d)), _const_spec((1, d)),
                              _const_spec((d, LANES)), _const_spec((LANES, d))],
        out_specs=tok,
        compiler_params=_cparams(("parallel",)),
        name="rwkv_post",
    )(y, r, k, v, gg, h, vecs, w_out.astype(BF16), g_post.reshape(1, d), sel, selt)


def _rwkv_mixer(h, g_pre, mu, w_in, w0, w_w2, a0, w_a2, w_g2, k_k, k_a, r_k, gn_w, gn_b, w_out, g_post):
    b, s, d = h.shape
    r, lw, k, v, a, bb, gg = _rw_proj(h, g_pre, mu, w_in, w0, w_w2, a0, w_a2, w_g2, k_k, k_a)
    rhat, y1, g, n = _rw_prep(r, lw, k, v, a, bb)
    y = _rw_scan(rhat, y1, g, n)
    f2 = lambda x: x.reshape(b * s, d)
    return _rw_post(f2(y), f2(r), f2(k), f2(v), f2(gg), f2(h), gn_w, gn_b, r_k, w_out, g_post).reshape(b, s, d)


def kernel(x, ffn1_norm_pre, ffn1_w_gu, ffn1_w_down, ffn1_norm_post, mix_norm_pre, nsa_w_in, nsa_pe_k,
           nsa_w_ck1, nsa_w_ck2, nsa_pe_v, nsa_w_cv1, nsa_w_cv2, nsa_w_out, rwkv_mu, rwkv_w_in, rwkv_w0,
           rwkv_w_w2, rwkv_a0, rwkv_w_a2, rwkv_w_g2, rwkv_k_k, rwkv_k_a, rwkv_r_k, rwkv_gn_w, rwkv_gn_b,
           rwkv_w_out, mix_norm_post, ffn2_norm_pre, ffn2_w_gu, ffn2_w_down, ffn2_norm_post):
    b, s, d = x.shape
    flat = lambda t: t.reshape(b * s, d)
    cube = lambda t: t.reshape(b, s, d)
    h = x
    depth = ffn1_norm_pre.shape[0]
    for i in range(depth):
        h = cube(_ffn(flat(h), ffn1_norm_pre[i], ffn1_w_gu[i], ffn1_w_down[i], ffn1_norm_post[i]))
        j = i // 2
        if i % 2 == 0:
            h = _nsa_mixer(h, mix_norm_pre[i], nsa_w_in[j], nsa_pe_k[j], nsa_w_ck1[j], nsa_w_ck2[j],
                           nsa_pe_v[j], nsa_w_cv1[j], nsa_w_cv2[j], nsa_w_out[j], mix_norm_post[i])
        else:
            h = _rwkv_mixer(h, mix_norm_pre[i], rwkv_mu[j], rwkv_w_in[j], rwkv_w0[j], rwkv_w_w2[j],
                            rwkv_a0[j], rwkv_w_a2[j], rwkv_w_g2[j], rwkv_k_k[j], rwkv_k_a[j], rwkv_r_k[j],
                            rwkv_gn_w[j], rwkv_gn_b[j], rwkv_w_out[j], mix_norm_post[i])
        h = cube(_ffn(flat(h), ffn2_norm_pre[i], ffn2_w_gu[i], ffn2_w_down[i], ffn2_norm_post[i]))
    return h
```

```python
import functools

import numpy as np
import jax
import jax.numpy as jnp
from jax import lax
from jax.experimental import pallas as pl
from jax.experimental.pallas import tpu as pltpu

F32 = jnp.float32
BF16 = jnp.bfloat16

D_MODEL = 1024
D_FF = 2816
HALF_STEP = 0.5
RMS_EPS = 1e-6
MASK_VALUE = -1e30

HD = 64
N_HEADS = 16
N_KV = 4
N_GRP = 4
KV_W = N_KV * HD
CMP_BLOCK = 32
CMP_STRIDE = 16
SEL_BLOCK = 64
N_SELECT = 16
WINDOW = 512
FORCED_SCORE = 1e4
N_GATE = 3 * N_HEADS

DECAY_LORA = 64
AAA_LORA = 64
GATE_LORA = 160
GN_EPS = 64e-5

LANES = 128
SUBLANES = 8
VMEM_LIMIT_BYTES = 56 * 1024 * 1024

FFN_TM = 512
FFN_CUTS = (0, 1536, 2816)
PROJ_TM = 512
ATT_T = 256
GATE_ROWS = 16
SUM_ROWS = 16
LOG2E = 1.4426950408889634
RW_C = 64
RW_Q = 128
RW_NCH = 16


def _cparams(sem):
    return pltpu.CompilerParams(dimension_semantics=sem, vmem_limit_bytes=VMEM_LIMIT_BYTES)


def _rms(x, g):
    ms = jnp.mean(x * x, axis=-1, keepdims=True)
    return x * lax.rsqrt(ms + RMS_EPS) * g


def _const_spec(shape):
    nd = len(shape)
    return pl.BlockSpec(shape, lambda *_: (0,) * nd, pipeline_mode=pl.Buffered(1))


def _dot(a, b):
    return jnp.dot(a, b, preferred_element_type=F32)


def _dot_nt(a, b):
    return lax.dot_general(a, b, (((1,), (1,)), ((), ())), preferred_element_type=F32)


def _dot_tn(a, b):
    return lax.dot_general(a, b, (((0,), (0,)), ((), ())), preferred_element_type=F32)


def _split2(x):
    hi = x.astype(BF16)
    lo = (x - hi.astype(F32)).astype(BF16)
    return hi, lo


def _split3(x):
    hi = x.astype(BF16)
    r1 = x - hi.astype(F32)
    mid = r1.astype(BF16)
    lo = (r1 - mid.astype(F32)).astype(BF16)
    return hi, mid, lo


def _ffn_body(x_ref, gpre_ref, wgu_ref, wd_ref, gpost_ref, o_ref):
    x = x_ref[...]
    xn = _rms(x, gpre_ref[...]).astype(BF16)
    acc = None
    for lo, hi in zip(FFN_CUTS[:-1], FFN_CUTS[1:]):
        gate = _dot(xn, wgu_ref[:, lo:hi])
        up = _dot(xn, wgu_ref[:, D_FF + lo:D_FF + hi])
        act = (gate * jax.nn.sigmoid(gate) * up).astype(BF16)
        part = _dot(act, wd_ref[lo:hi, :])
        acc = part if acc is None else acc + part
    o_ref[...] = x + HALF_STEP * _rms(acc, gpost_ref[...])


def _ffn(h2, g_pre, w_gu, w_down, g_post):
    t = h2.shape[0]
    tm = min(FFN_TM, t)
    return pl.pallas_call(
        _ffn_body,
        out_shape=jax.ShapeDtypeStruct((t, D_MODEL), F32),
        grid=(t // tm,),
        in_specs=[
            pl.BlockSpec((tm, D_MODEL), lambda i: (i, 0)),
            _const_spec((1, D_MODEL)),
            _const_spec((D_MODEL, 2 * D_FF)),
            _const_spec((D_FF, D_MODEL)),
            _const_spec((1, D_MODEL)),
        ],
        out_specs=pl.BlockSpec((tm, D_MODEL), lambda i: (i, 0)),
        compiler_params=_cparams(("parallel",)),
        name="ffn",
    )(h2, g_pre.reshape(1, D_MODEL), w_gu.astype(BF16), w_down.astype(BF16), g_post.reshape(1, D_MODEL))


def _nsa_proj_body(h_ref, g_ref, wqt_ref, wc_ref, wk_ref, wvt_ref, wgt_ref,
                   qt_ref, kc_ref, vc_ref, k_ref, vt_ref, gt_ref):
    tm = h_ref.shape[1]
    u = _rms(h_ref[0], g_ref[...]).astype(BF16)
    qt_ref[0] = (_dot_nt(wqt_ref[...], u) * (HD ** -0.5 * LOG2E)).astype(BF16)
    c = _dot(u, wc_ref[...])
    for j in range(KV_W // LANES):
        kc_ref[0, j] = c[:, j * LANES:(j + 1) * LANES]
        vc_ref[0, j] = c[:, KV_W + j * LANES:KV_W + (j + 1) * LANES]
    kk = _dot(u, wk_ref[...])
    t0 = pl.program_id(1) * tm
    lane = lax.broadcasted_iota(jnp.int32, (tm, LANES), 1)
    blk = lax.shift_right_logical(t0 + lax.broadcasted_iota(jnp.int32, (tm, LANES), 0), 6)
    onehot = jnp.where(lane - HD == blk, 1.0, 0.0)
    for g in range(N_KV):
        k_ref[0, g, 0] = (kk[:, g * LANES:(g + 1) * LANES] + onehot).astype(BF16)
        k_ref[0, g, 1] = kk[:, (N_KV + g) * LANES:(N_KV + g + 1) * LANES].astype(BF16)
    vt = _dot_nt(wvt_ref[...], u)
    tk = vt_ref.shape[5]
    for j in range(tm // tk):
        cols = slice(j * tk, (j + 1) * tk)
        vt_ref[0, :, 0, j] = vt[:KV_W, cols].reshape(N_KV, HD, tk).astype(BF16)
        vt_ref[0, :, 1, j] = vt[KV_W:, cols].reshape(N_KV, HD, tk).astype(BF16)
    gt_ref[0] = jax.nn.sigmoid(_dot_nt(wgt_ref[...], u)).reshape(N_KV, GATE_ROWS, tm)


def _nsa_proj(h, g_pre, w_in):
    b, s, _ = h.shape
    tm = min(PROJ_TM, s)
    tk = min(ATT_T, s)
    nq = N_HEADS * HD
    assert s // SEL_BLOCK <= HD
    col = lambda i: w_in[:, nq + i * KV_W:nq + (i + 1) * KV_W]
    w_qt = w_in[:, :nq].T.astype(BF16)
    w_c = jnp.concatenate([col(0), col(1)], axis=1).astype(BF16)
    pad_heads = lambda w: jnp.pad(w.reshape(D_MODEL, N_KV, HD), ((0, 0), (0, 0), (0, LANES - HD))).reshape(D_MODEL, N_KV * LANES)
    w_k = jnp.concatenate([pad_heads(col(2)), pad_heads(col(4))], axis=1).astype(BF16)
    w_vt = jnp.concatenate([col(3), col(5)], axis=1).T.astype(BF16)
    w_gl = w_in[:, nq + 6 * KV_W:].reshape(D_MODEL, N_KV, N_GRP * 3)
    w_gt = jnp.pad(w_gl, ((0, 0), (0, 0), (0, GATE_ROWS - N_GRP * 3))).reshape(D_MODEL, N_KV * GATE_ROWS).T.astype(BF16)
    tok = lambda w: pl.BlockSpec((1, tm, w), lambda bi, i: (bi, i, 0))
    cmp_spec = pl.BlockSpec((1, KV_W // LANES, tm, LANES), lambda bi, i: (bi, 0, i, 0))
    key = jax.ShapeDtypeStruct((b, N_KV, 2, s, LANES), BF16)
    key_spec = pl.BlockSpec((1, N_KV, 2, tm, LANES), lambda bi, i: (bi, 0, 0, i, 0))
    valt = jax.ShapeDtypeStruct((b, N_KV, 2, s // tk, HD, tk), BF16)
    valt_spec = pl.BlockSpec((1, N_KV, 2, tm // tk, HD, tk), lambda bi, i: (bi, 0, 0, i, 0, 0))
    return pl.pallas_call(
        _nsa_proj_body,
        out_shape=(
            jax.ShapeDtypeStruct((b, nq, s), BF16),
            jax.ShapeDtypeStruct((b, KV_W // LANES, s, LANES), F32),
            jax.ShapeDtypeStruct((b, KV_W // LANES, s, LANES), F32),
            key, valt,
            jax.ShapeDtypeStruct((b, N_KV, GATE_ROWS, s), F32),
        ),
        grid=(b, s // tm),
        in_specs=[
            tok(D_MODEL),
            _const_spec((1, D_MODEL)),
            _const_spec((nq, D_MODEL)),
            _const_spec((D_MODEL, 2 * KV_W)),
            _const_spec((D_MODEL, 2 * N_KV * LANES)),
            _const_spec((2 * KV_W, D_MODEL)),
            _const_spec((N_KV * GATE_ROWS, D_MODEL)),
        ],
        out_specs=(
            pl.BlockSpec((1, nq, tm), lambda bi, i: (bi, 0, i)),
            cmp_spec, cmp_spec, key_spec, valt_spec,
            pl.BlockSpec((1, N_KV, GATE_ROWS, tm), lambda bi, i: (bi, 0, 0, i)),
        ),
        compiler_params=_cparams(("parallel", "parallel")),
        name="nsa_proj",
    )(h, g_pre.reshape(1, D_MODEL), w_qt, w_c, w_k, w_vt, w_gt)


def _compress_body(x_ref, pe_ref, w1_ref, w2_ref, o_ref, *, transposed):
    half = CMP_BLOCK // 2
    n = x_ref.shape[2] // half
    ha = hb = None
    for l in range(half):
        xl = jnp.concatenate([x_ref[0, j, pl.ds(l, n, stride=half), :] for j in range(x_ref.shape[1])], axis=-1)
        a = _dot((xl + pe_ref[l:l + 1, :]).astype(BF16), w1_ref[l])
        b = _dot((xl + pe_ref[half + l:half + l + 1, :]).astype(BF16), w1_ref[half + l])
        ha = a if ha is None else ha + a
        hb = b if hb is None else hb + b
    hid = ha + pltpu.roll(hb, n - 1, 0)
    hid = (hid * jax.nn.sigmoid(hid)).astype(BF16)
    if transposed:
        o_ref[0] = _dot_nt(w2_ref[...], hid).reshape(N_KV, HD, n).astype(BF16)
    else:
        out = _dot(hid, w2_ref[...]).astype(BF16)
        for g in range(N_KV):
            o_ref[0, g] = out[:, g * HD:(g + 1) * HD]


def _compress(t, pe, w1, w2, transposed):
    b, _, s, _ = t.shape
    rows = s // (CMP_BLOCK // 2)
    eye = jnp.eye(N_KV, dtype=F32)
    wbd = jnp.einsum("ldc,gh->lgdhc", w1, eye).reshape(CMP_BLOCK, KV_W, KV_W).astype(BF16)
    pe_t = jnp.broadcast_to(pe[:, None, :], (CMP_BLOCK, N_KV, HD)).reshape(CMP_BLOCK, KV_W)
    w2bd = jnp.einsum("cd,gh->gchd", w2, eye).reshape(KV_W, KV_W)
    if transposed:
        w2bd = w2bd.T
        out_shape, out_block = (b, N_KV, HD, rows), (1, N_KV, HD, rows)
    else:
        out_shape, out_block = (b, N_KV, rows, HD), (1, N_KV, rows, HD)
    return pl.pallas_call(
        functools.partial(_compress_body, transposed=transposed),
        out_shape=jax.ShapeDtypeStruct(out_shape, BF16),
        grid=(b,),
        in_specs=[
            pl.BlockSpec((1, KV_W // LANES, s, LANES), lambda bi: (bi, 0, 0, 0)),
            _const_spec((CMP_BLOCK, KV_W)),
            _const_spec((CMP_BLOCK, KV_W, KV_W)),
            _const_spec((KV_W, KV_W)),
        ],
        out_specs=pl.BlockSpec(out_block, lambda bi: (bi, 0, 0, 0)),
        compiler_params=_cparams(("parallel",)),
        name="nsa_compress",
    )(t, pe_t, wbd, w2bd.astype(BF16))


def _alibi_slope(h):
    return float(2.0 ** (-8.0 * (h + 1) / N_HEADS))


def _cmp_probs(kc, qh, slope2, end_f, valid):
    s = _dot(kc, qh) + jnp.where(valid, slope2 * end_f, MASK_VALUE)
    return jnp.exp2(s - jnp.max(s, axis=0, keepdims=True)).astype(BF16)


def _cmp_geometry(n_cmp_rows, tq, t0):
    n_io = lax.broadcasted_iota(jnp.int32, (n_cmp_rows, tq), 0)
    t_io = t0 + lax.broadcasted_iota(jnp.int32, (n_cmp_rows, tq), 1)
    end = n_io * CMP_STRIDE + (CMP_BLOCK - 1)
    return end.astype(F32), (t_io >= end) & (n_io < n_cmp_rows - 1)


def _select_body(qt_ref, kc_ref, vct_ref, c2s_ref, chunk_ref, mneg_ref, active_ref, oc_ref, *, n_sel, n_cmp_rows):
    tq = qt_ref.shape[2]
    t0 = pl.program_id(1) * tq
    end_f, valid = _cmp_geometry(n_cmp_rows, tq, t0)
    has_block = t0 + lax.broadcasted_iota(jnp.int32, (1, tq), 1) >= CMP_BLOCK - 1
    v_lo = n_sel + SUM_ROWS
    s_io = lax.broadcasted_iota(jnp.int32, (n_sel, tq), 0)
    tt = t0 + lax.broadcasted_iota(jnp.int32, (n_sel, tq), 1)
    cur = lax.shift_right_logical(tt, 6)
    forced = (s_io == 0) | (s_io == cur) | (s_io == cur - 1)
    visible = s_io * SEL_BLOCK <= tt

    for g in range(N_KV):
        kc = kc_ref[0, g]
        heads = range(g * N_GRP, (g + 1) * N_GRP)
        probs = [_cmp_probs(kc, qt_ref[0, hh * HD:(hh + 1) * HD, :], _alibi_slope(hh) * LOG2E, end_f, valid)
                 for hh in heads]
        lhs = jnp.concatenate([c2s_ref[...], vct_ref[0, g]], axis=0)
        mass = [_dot(lhs, p) for p in probs]
        inv_l = [1.0 / m[n_sel:n_sel + 1] for m in mass]
        imp = sum(m[0:n_sel] * il for m, il in zip(mass, inv_l))
        for hh, m, il in zip(heads, mass, inv_l):
            oc_ref[0, hh * HD:(hh + 1) * HD, :] = jnp.where(has_block, m[v_lo:v_lo + HD] * il, 0.0).astype(BF16)
        x = jnp.where(forced, FORCED_SCORE, jnp.where(visible, imp, -1.0))
        cnt = jnp.zeros((n_sel, tq), F32)
        for j in range(n_sel):
            xj = x[j:j + 1, :]
            ge = jnp.where(xj >= x, 1.0, 0.0)
            gt = jnp.where(xj > x, 1.0, 0.0)
            cnt = cnt + jnp.where(s_io > j, ge, gt)
        chosen = cnt < float(N_SELECT)
        mneg_ref[0, g, 0:n_sel, :] = jnp.where(chosen, 0.0, MASK_VALUE).astype(BF16)
        if n_sel < HD:
            mneg_ref[0, g, n_sel:HD, :] = jnp.zeros((HD - n_sel, tq), BF16)
        per_query = _dot(chunk_ref[...], jnp.where(chosen, 1.0, 0.0).astype(BF16))
        active_ref[0, g, 0] = _dot_nt(jnp.ones((SUBLANES, tq), BF16), per_query.astype(BF16))


def _select(qt, k_cmp, v_cmpt):
    b, nq, s = qt.shape
    tq = min(ATT_T, s)
    n_sel = s // SEL_BLOCK
    rows = k_cmp.shape[2]
    n_cmp = (s - CMP_BLOCK) // CMP_STRIDE + 1
    assert rows == n_cmp + 1
    cs = np.arange(rows) * CMP_STRIDE
    bs = np.arange(n_sel) * SEL_BLOCK
    overlap = np.clip(np.minimum(cs[:, None] + CMP_BLOCK, bs[None, :] + SEL_BLOCK)
                      - np.maximum(cs[:, None], bs[None, :]), 0, None)
    c2s_t = jnp.asarray(np.concatenate([(overlap / CMP_BLOCK).T, np.ones((SUM_ROWS, rows))], axis=0), BF16)
    n_chunks = s // tq
    chunk_of_block = np.arange(n_sel) * SEL_BLOCK // tq
    chunk_mat = jnp.asarray(chunk_of_block[None, :] == np.arange(n_chunks)[:, None], BF16)
    return pl.pallas_call(
        functools.partial(_select_body, n_sel=n_sel, n_cmp_rows=rows),
        out_shape=(jax.ShapeDtypeStruct((b, N_KV, HD, s), BF16),
                   jax.ShapeDtypeStruct((b, N_KV, s // tq, SUBLANES, n_chunks), F32),
                   jax.ShapeDtypeStruct((b, nq, s), BF16)),
        grid=(b, s // tq),
        in_specs=[
            pl.BlockSpec((1, nq, tq), lambda bi, i: (bi, 0, i)),
            pl.BlockSpec((1, N_KV, rows, HD), lambda bi, i: (bi, 0, 0, 0)),
            pl.BlockSpec((1, N_KV, HD, rows), lambda bi, i: (bi, 0, 0, 0)),
            _const_spec((n_sel + SUM_ROWS, rows)),
            _const_spec((n_chunks, n_sel)),
        ],
        out_specs=(pl.BlockSpec((1, N_KV, HD, tq), lambda bi, i: (bi, 0, 0, i)),
                   pl.BlockSpec((1, N_KV, 1, SUBLANES, n_chunks), lambda bi, i: (bi, 0, i, 0, 0)),
                   pl.BlockSpec((1, nq, tq), lambda bi, i: (bi, 0, i))),
        compiler_params=_cparams(("parallel", "parallel")),
        name="nsa_select",
    )(qt, k_cmp, v_cmpt, c2s_t, chunk_mat)


def _attn_body(slope_ref, active_ref, qt_ref, k_ref, vt_ref, oc_ref, mneg_ref,
               gt_ref, o_ref, s_sc, bias_sc, m_sc, acc_sc, todo_ref):
    tq = qt_ref.shape[2]
    tk = tq
    g = pl.program_id(1)
    qi = pl.program_id(2)
    slopes = [slope_ref[g * N_GRP + r] for r in range(N_GRP)]
    q_rows = [qt_ref[0, r * HD:(r + 1) * HD, :] for r in range(N_GRP)]
    qk = [jnp.concatenate([q_rows[r], mneg_ref[0, 0]], axis=0) for r in range(N_GRP)]

    @pl.when(qi == 0)
    def _():
        key_io = lax.broadcasted_iota(jnp.int32, (tk, tq), 0)
        rel = key_io - lax.broadcasted_iota(jnp.int32, (tk, tq), 1)
        key_f = key_io.astype(F32)
        causal = jnp.where(rel <= 0, 0.0, MASK_VALUE)
        oldest = jnp.where(rel > 0, 0.0, MASK_VALUE)
        for r in range(N_GRP):
            base = slopes[r] * key_f
            bias_sc[0, r] = base
            bias_sc[1, r] = base + causal
            bias_sc[2, r] = base + oldest

    ones_rows = jnp.ones((SUM_ROWS, tk), BF16)

    def offsets(c):
        cf = ((c - qi) * tk).astype(F32)
        return [slopes[r] * cf for r in range(N_GRP)]

    tile = (pl.program_id(0) * N_KV + g) * pl.num_programs(2) + qi

    def note_active(c, n):
        todo_ref[n] = c
        return n + active_ref[tile * pl.num_programs(2) + c]

    n_sel = lax.fori_loop(0, qi, note_active, jnp.int32(0)) + 1
    c_lo_w = jnp.maximum(qi - WINDOW // tk, 0)
    n_items = n_sel + (qi - c_lo_w) + 1

    def item(pos):
        window = (pos >= n_sel).astype(jnp.int32)
        listed = todo_ref[jnp.minimum(pos, pl.num_programs(2) - 1)]
        c = jnp.where(pos < n_sel - 1, listed, jnp.where(pos == n_sel - 1, qi, c_lo_w + pos - n_sel))
        return window, c

    def score_stage(pos, slot):
        window, c = item(pos)
        kblk = k_ref[0, 0, window, pl.ds(pl.multiple_of(c * tk, tk), tk), :]
        variant = jnp.where(c == qi, 1, jnp.where((window == 1) & (c == qi - WINDOW // tk), 2, 0))
        offs = offsets(c)
        mcs = []
        for r in range(N_GRP):
            s = _dot(kblk, qk[r]) + bias_sc[variant, r]
            s_sc[slot, r] = s
            mcs.append(jnp.max(s, axis=0, keepdims=True) + offs[r])
        return tuple(mcs)

    def value_stage(pos, mcs, slot):
        window, c = item(pos)
        vt = jnp.concatenate([vt_ref[0, 0, window, c], ones_rows], axis=0)
        offs = offsets(c)
        for r in range(N_GRP):
            m = m_sc[window, r, 0:1, :]
            m_new = jnp.maximum(m, mcs[r])
            p = jnp.exp2(s_sc[slot, r] - (m_new - offs[r]))
            acc_sc[window, r] = jnp.exp2(m - m_new) * acc_sc[window, r] + _dot(vt, p.astype(BF16))
            m_sc[window, r, 0:1, :] = m_new

    m_sc[...] = jnp.full(m_sc.shape, MASK_VALUE, F32)
    acc_sc[...] = jnp.zeros(acc_sc.shape, F32)

    def trips(n, first, count, mcs):
        def trip(k, mcs):
            pos = first + n * k
            for i in range(n):
                mcs_next = score_stage(pos + i + 1, (i + 1) % 2)
                value_stage(pos + i, mcs, i % 2)
                mcs = mcs_next
            return mcs
        return lax.fori_loop(0, count, trip, mcs)

    steps = n_items - 1
    mcs = trips(4, 0, steps // 4, score_stage(jnp.int32(0), 0))
    done = (steps // 4) * 4
    mcs = trips(2, done, (steps - done) // 2, mcs)
    done = done + ((steps - done) // 2) * 2
    mcs = trips(1, done, steps - done, mcs)
    value_stage(steps, mcs, steps - done)

    gate = gt_ref[0, 0]
    normed = lambda acc: acc[0:HD] * (1.0 / acc[HD:HD + 1])
    for r in range(N_GRP):
        o_c = oc_ref[0, r * HD:(r + 1) * HD, :].astype(F32)
        o_s = normed(acc_sc[0, r])
        o_w = normed(acc_sc[1, r])
        out = (gate[3 * r:3 * r + 1, :] * o_c + gate[3 * r + 1:3 * r + 2, :] * o_s
               + gate[3 * r + 2:3 * r + 3, :] * o_w)
        o_ref[0, r * HD:(r + 1) * HD, :] = out.astype(BF16)


def _attention(qt, keys, vals_t, oc_t, mneg, active, gates_t):
    b, nq, s = qt.shape
    tq = vals_t.shape[5]
    assert WINDOW % tq == 0 and active.shape == (b, N_KV, s // tq, SUBLANES, s // tq)
    active = (active[:, :, :, 0, :] > 0.0).astype(jnp.int32).reshape(-1)
    gw = N_GRP * HD
    slopes = jnp.asarray([_alibi_slope(h) * LOG2E for h in range(N_HEADS)], F32)
    key_spec = pl.BlockSpec((1, 1, 2, s, LANES), lambda bi, g, i: (bi, g, 0, 0, 0))
    val_spec = pl.BlockSpec((1, 1, 2, s // tq, HD, tq), lambda bi, g, i: (bi, g, 0, 0, 0, 0))
    return pl.pallas_call(
        _attn_body,
        out_shape=jax.ShapeDtypeStruct((b, nq, s), BF16),
        grid=(b, N_KV, s // tq),
        in_specs=[
            pl.BlockSpec(memory_space=pltpu.SMEM),
            pl.BlockSpec(memory_space=pltpu.SMEM),
            pl.BlockSpec((1, gw, tq), lambda bi, g, i: (bi, g, i)),
            key_spec, val_spec,
            pl.BlockSpec((1, gw, tq), lambda bi, g, i: (bi, g, i)),
            pl.BlockSpec((1, 1, HD, tq), lambda bi, g, i: (bi, g, 0, i)),
            pl.BlockSpec((1, 1, GATE_ROWS, tq), lambda bi, g, i: (bi, g, 0, i)),
        ],
        out_specs=pl.BlockSpec((1, gw, tq), lambda bi, g, i: (bi, g, i)),
        scratch_shapes=[
            pltpu.VMEM((2, N_GRP, tq, tq), F32),
            pltpu.VMEM((3, N_GRP, tq, tq), F32),
            pltpu.VMEM((2, N_GRP, SUBLANES, tq), F32),
            pltpu.VMEM((2, N_GRP, HD + SUM_ROWS, tq), F32),
            pltpu.SMEM((s // tq,), jnp.int32),
        ],
        compiler_params=_cparams(("parallel", "parallel", "arbitrary")),
        name="nsa_attention",
    )(slopes, active, qt, keys, vals_t, oc_t, mneg, gates_t)


def _outproj_t_body(yt_ref, h_ref, w_ref, g_ref, o_ref):
    o_ref[0] = h_ref[0] + _rms(_dot_tn(yt_ref[0], w_ref[...]), g_ref[...])


def _outproj_t(yt, h, w_out, g_post):
    b, s, d = h.shape
    tm = min(PROJ_TM, s)
    return pl.pallas_call(
        _outproj_t_body,
        out_shape=jax.ShapeDtypeStruct((b, s, d), F32),
        grid=(b, s // tm),
        in_specs=[
            pl.BlockSpec((1, d, tm), lambda bi, i: (bi, 0, i)),
            pl.BlockSpec((1, tm, d), lambda bi, i: (bi, i, 0)),
            _const_spec((d, d)),
            _const_spec((1, d)),
        ],
        out_specs=pl.BlockSpec((1, tm, d), lambda bi, i: (bi, i, 0)),
        compiler_params=_cparams(("parallel", "parallel")),
        name="outproj_t",
    )(yt, h, w_out.astype(BF16), g_post.reshape(1, d))


def _nsa_mixer(h, g_pre, w_in, pe_k, w_ck1, w_ck2, pe_v, w_cv1, w_cv2, w_out, g_post):
    qt, kc, vc, keys, vals_t, gates_t = _nsa_proj(h, g_pre, w_in)
    k_cmp = _compress(kc, pe_k, w_ck1, w_ck2, transposed=False)
    v_cmpt = _compress(vc, pe_v, w_cv1, w_cv2, transposed=True)
    mneg, active, oc_t = _select(qt, k_cmp, v_cmpt)
    out_t = _attention(qt, keys, vals_t, oc_t, mneg, active, gates_t)
    return _outproj_t(out_t, h, w_out, g_post)


def _head_sum(x, sel_ref):
    return _dot(x.astype(BF16), sel_ref[...])


def _head_expand(x, selt_ref):
    hi, lo = _split2(x)
    return _dot(hi, selt_ref[...]) + _dot(lo, selt_ref[...])


def _softplus(x):
    return jnp.maximum(x, 0.0) + jnp.log(1.0 + jnp.exp(-jnp.abs(x)))


def _rw_proj_body(h_ref, hp_ref, g_ref, mu_ref, vec_ref, wr_ref, wk_ref, wv_ref, wd_ref, wa_ref, wg_ref,
                  w2d_ref, w2a_ref, w2g_ref, sel_ref, selt_ref,
                  r_ref, lw_ref, k_ref, v_ref, a_ref, b_ref, gg_ref):
    tm = h_ref.shape[1]
    u = _rms(h_ref[0], g_ref[...])
    prev = _rms(hp_ref[0], g_ref[...])[7:8, :]
    prev = jnp.where(pl.program_id(1) == 0, 0.0, prev)
    row = lax.broadcasted_iota(jnp.int32, (tm, D_MODEL), 0)
    u_prev = jnp.where(row == 0, prev, pltpu.roll(u, 1, 0))
    xx = u_prev - u

    def mix(i):
        return (u + xx * mu_ref[i:i + 1, :]).astype(BF16)

    w0, a0, k_k, k_a = (vec_ref[i:i + 1, :] for i in range(4))
    r = _dot(mix(0), wr_ref[...])
    k = _dot(mix(1), wk_ref[...])
    v = _dot(mix(2), wv_ref[...])
    d1 = jnp.tanh(_dot(mix(3), wd_ref[...])).astype(BF16)
    a1 = _dot(mix(4), wa_ref[...]).astype(BF16)
    g1 = jax.nn.sigmoid(_dot(mix(5), wg_ref[...])).astype(BF16)
    w = -_softplus(-(w0 + _dot(d1, w2d_ref[...]))) - 0.5
    alpha = jax.nn.sigmoid(a0 + _dot(a1, w2a_ref[...]))
    kk = k * k_k
    norm = jnp.sqrt(_head_sum(kk * kk, sel_ref))
    kk = kk * _head_expand(1.0 / jnp.maximum(norm, 1e-12), selt_ref)
    r_ref[0] = r.astype(BF16)
    lw_ref[0] = -jnp.exp(w)
    k_ref[0] = (k * (1.0 + (alpha - 1.0) * k_a)).astype(BF16)
    v_ref[0] = v.astype(BF16)
    a_ref[0] = (-kk).astype(BF16)
    b_ref[0] = (kk * alpha).astype(BF16)
    gg_ref[0] = _dot(g1, w2g_ref[...]).astype(BF16)


def _head_selectors():
    lane_head = np.arange(D_MODEL) // HD
    sel = (lane_head[:, None] == np.arange(LANES)[None, :]).astype(np.float32)
    return jnp.asarray(sel, BF16), jnp.asarray(sel.T, BF16)


def _pad_cols(w, n):
    return jnp.pad(w, ((0, 0), (0, n - w.shape[1])))


def _pad_rows(w, n):
    return jnp.pad(w, ((0, n - w.shape[0]), (0, 0)))


def _rw_proj(h, g_pre, mu, w_in, w0, w_w2, a0, w_a2, w_g2, k_k, k_a):
    b, s, d = h.shape
    tm = min(PROJ_TM, s)
    offs = np.cumsum((0, d, d, d, DECAY_LORA, AAA_LORA, GATE_LORA))
    cols = [w_in[:, int(offs[i]):int(offs[i + 1])] for i in range(6)]
    ld, lg = LANES, 2 * LANES
    wr, wk, wv = (c.astype(BF16) for c in cols[:3])
    wd = _pad_cols(cols[3], ld).astype(BF16)
    wa = _pad_cols(cols[4], ld).astype(BF16)
    wg = _pad_cols(cols[5], lg).astype(BF16)
    w2d = _pad_rows(w_w2, ld).astype(BF16)
    w2a = _pad_rows(w_a2, ld).astype(BF16)
    w2g = _pad_rows(w_g2, lg).astype(BF16)
    mu8 = _pad_rows(mu, 8)
    vecs = _pad_rows(jnp.stack([w0, a0, k_k, k_a]), 8)
    sel, selt = _head_selectors()
    tok = pl.BlockSpec((1, tm, d), lambda bi, i: (bi, i, 0))
    out = lambda dt: jax.ShapeDtypeStruct((b, s, d), dt)
    return pl.pallas_call(
        _rw_proj_body,
        out_shape=(out(BF16), out(F32), out(BF16), out(BF16), out(BF16), out(BF16), out(BF16)),
        grid=(b, s // tm),
        in_specs=[
            tok,
            pl.BlockSpec((1, 8, d), lambda bi, i: (bi, jnp.maximum(i * (tm // 8) - 1, 0), 0)),
            _const_spec((1, d)), _const_spec((8, d)), _const_spec((8, d)),
            _const_spec((d, d)), _const_spec((d, d)), _const_spec((d, d)),
            _const_spec((d, ld)), _const_spec((d, ld)), _const_spec((d, lg)),
            _const_spec((ld, d)), _const_spec((ld, d)), _const_spec((lg, d)),
            _const_spec((d, LANES)), _const_spec((LANES, d)),
        ],
        out_specs=(tok,) * 7,
        compiler_params=_cparams(("parallel", "parallel")),
        name="rwkv_proj",
    )(h, h, g_pre.reshape(1, d), mu8, vecs, wr, wk, wv, wd, wa, wg, w2d, w2a, w2g, sel, selt)


def _rw_prep_body(ltri_ref, r_ref, lw_ref, k_ref, v_ref, a_ref, b_ref,
                  rhat_ref, y1_ref, g_ref, n_ref):
    c = RW_C
    q = RW_Q
    nh = q // HD
    nch = r_ref.shape[1] // c
    lane_head = lax.shift_right_logical(lax.broadcasted_iota(jnp.int32, (c, q), 1), 6)
    ri = lax.broadcasted_iota(jnp.int32, (q, q), 0)
    ci = lax.broadcasted_iota(jnp.int32, (q, q), 1)
    same_head = lax.shift_right_logical(ri, 6) == lax.shift_right_logical(ci, 6)
    strict_bd = same_head & ((ci & (c - 1)) < (ri & (c - 1)))
    eye = ri == ci
    t_io = lax.broadcasted_iota(jnp.int32, (c, q), 0)
    j_io = lax.broadcasted_iota(jnp.int32, (c, q), 1) & (c - 1)
    strict_ls = j_io < t_io
    incl_ls = j_io <= t_io
    rows = [slice(i * c, (i + 1) * c) for i in range(nch)]

    def each(f):
        return [f(i) for i in range(nch)]

    def expand(x):
        return jnp.concatenate([jnp.where(lane_head == hh, x, 0.0) for hh in range(nh)], axis=0).astype(BF16)

    def collapse(x):
        out = x[0:c]
        for hh in range(1, nh):
            out = out + x[hh * c:(hh + 1) * c]
        return out

    lw = each(lambda i: lw_ref[0, rows[i], :])
    parts = each(lambda i: _dot(ltri_ref[...], jnp.concatenate(_split3(lw[i]), axis=1)))
    cum = each(lambda i: parts[i][:, 0:q] + parts[i][:, q:2 * q] + parts[i][:, 2 * q:3 * q])
    cum_c = each(lambda i: cum[i][c - 1:c, :])
    e_inv = each(lambda i: jnp.exp(-cum[i]))
    e_rem = each(lambda i: jnp.exp(cum_c[i] - cum[i]))
    at = each(lambda i: a_ref[0, rows[i], :] * jnp.exp(cum[i] - lw[i]))
    rt = each(lambda i: r_ref[0, rows[i], :] * jnp.exp(cum[i]))
    x4 = each(lambda i: expand(at[i]))
    b4 = each(lambda i: expand(b_ref[0, rows[i], :] * e_inv[i]))
    k4 = each(lambda i: expand(k_ref[0, rows[i], :] * e_inv[i]))
    v4 = each(lambda i: expand(v_ref[0, rows[i], :]))
    bbar = each(lambda i: (b_ref[0, rows[i], :] * e_rem[i]).astype(BF16))
    kbar = each(lambda i: (k_ref[0, rows[i], :] * e_rem[i]).astype(BF16))

    lbd = each(lambda i: jnp.where(strict_bd, _dot_nt(x4[i], b4[i]), 0.0))
    tbd = each(lambda i: jnp.where(eye, 1.0, lbd[i]))
    lb = each(lambda i: lbd[i].astype(BF16))
    p = each(lambda i: _dot(lb[i], lb[i]))
    n_lvl = int(np.log2(c)) - 1
    for lvl in range(n_lvl):
        pb = each(lambda i: p[i].astype(BF16))
        if lvl + 1 < n_lvl:
            both = each(lambda i: _dot(pb[i], jnp.concatenate([tbd[i].astype(BF16), pb[i]], axis=1)))
            tbd = each(lambda i: tbd[i] + both[i][:, 0:q])
            p = each(lambda i: both[i][:, q:2 * q])
        else:
            tbd = each(lambda i: tbd[i] + _dot(pb[i], tbd[i].astype(BF16)))
    t_ls = each(lambda i: collapse(tbd[i]).astype(BF16))

    w = nh * c
    ar = each(lambda i: _dot_nt(jnp.concatenate([at[i], rt[i]], axis=0).astype(BF16),
                                jnp.concatenate([k4[i], b4[i]], axis=0)))
    a_kk = each(lambda i: jnp.concatenate([jnp.where(strict_ls, ar[i][0:c, 0:w], 0.0),
                                           jnp.where(incl_ls, ar[i][c:2 * c, 0:w], 0.0)], axis=0).astype(BF16))
    a_rb = each(lambda i: jnp.where(incl_ls, ar[i][c:2 * c, w:2 * w], 0.0).astype(BF16))

    gy = each(lambda i: _dot(a_kk[i], v4[i]))
    g1 = each(lambda i: gy[i][0:c])
    ua = each(lambda i: _dot(t_ls[i], jnp.concatenate([expand(g1[i]), x4[i]], axis=1)))
    u0 = each(lambda i: ua[i][:, 0:q])
    ahat = each(lambda i: ua[i][:, q:2 * q])
    ry = each(lambda i: _dot(a_rb[i], jnp.concatenate([expand(ahat[i]), expand(u0[i])], axis=1)))
    rhat = each(lambda i: rt[i] + ry[i][:, 0:q])
    y1 = each(lambda i: gy[i][c:2 * c] + ry[i][:, q:2 * q])

    gm = each(lambda i: _dot_tn(bbar[i], ahat[i].astype(BF16)))
    nm = each(lambda i: _dot_tn(jnp.concatenate([bbar[i], kbar[i]], axis=0),
                                jnp.concatenate([u0[i].astype(BF16), v_ref[0, rows[i], :]], axis=0)))
    for i in range(nch):
        gmi = jnp.where(same_head, gm[i], 0.0) + jnp.where(eye, jnp.exp(cum_c[i]), 0.0)
        rhat_ref[0, rows[i], :] = rhat[i].astype(BF16)
        y1_ref[0, rows[i], :] = y1[i].astype(BF16)
        g_ref[0, rows[i], :] = collapse(gmi).astype(BF16)
        n_ref[0, rows[i], :] = collapse(jnp.where(same_head, nm[i], 0.0)).astype(BF16)


def _rw_prep(r, lw, k, v, a, b):
    bsz, s, d = r.shape
    rows = min(RW_C * RW_NCH, s)
    ltri = jnp.asarray(np.tril(np.ones((RW_C, RW_C), np.float32)), BF16)
    blk = pl.BlockSpec((1, rows, RW_Q), lambda bi, qi, j: (bi, j, qi))
    return pl.pallas_call(
        _rw_prep_body,
        out_shape=(jax.ShapeDtypeStruct((bsz, s, d), BF16),) * 4,
        grid=(bsz, d // RW_Q, s // rows),
        in_specs=[_const_spec((RW_C, RW_C))] + [blk] * 6,
        out_specs=(blk,) * 4,
        compiler_params=_cparams(("parallel", "parallel", "parallel")),
        name="rwkv_prep",
    )(ltri, r, lw, k, v, a, b)


def _rw_scan_body(rhat_ref, y1_ref, g_ref, n_ref, y_ref, h_sc):
    c = RW_C
    q = RW_Q
    nq = h_sc.shape[0]

    @pl.when(pl.program_id(1) == 0)
    def _():
        h_sc[...] = jnp.zeros(h_sc.shape, F32)

    ri = lax.broadcasted_iota(jnp.int32, (q, q), 0)
    ci = lax.broadcasted_iota(jnp.int32, (q, q), 1)
    same_head = lax.shift_right_logical(ri, 6) == lax.shift_right_logical(ci, 6)

    def step(ch, carry):
        rows = pl.ds(pl.multiple_of(ch * c, c), c)
        for qi in range(nq):
            lanes = slice(qi * q, (qi + 1) * q)
            hb = h_sc[qi].astype(BF16)
            y_ref[0, rows, lanes] = (_dot(rhat_ref[0, rows, lanes], hb) + y1_ref[0, rows, lanes]).astype(BF16)
            g_ls = g_ref[0, rows, lanes]
            n_ls = n_ref[0, rows, lanes]
            gbd = jnp.where(same_head, jnp.concatenate([g_ls] * (q // c), axis=0), 0.0)
            nbd = jnp.where(same_head, jnp.concatenate([n_ls] * (q // c), axis=0), 0.0)
            h_sc[qi] = _dot(gbd.astype(BF16), hb) + nbd
        return carry

    lax.fori_loop(0, rhat_ref.shape[1] // c, step, 0)


def _rw_scan(rhat, y1, g, n):
    bsz, s, d = rhat.shape
    rows = min(512, s)
    blk = pl.BlockSpec((1, rows, d), lambda bi, j: (bi, j, 0))
    return pl.pallas_call(
        _rw_scan_body,
        out_shape=jax.ShapeDtypeStruct((bsz, s, d), BF16),
        grid=(bsz, s // rows),
        in_specs=[blk] * 4,
        out_specs=blk,
        scratch_shapes=[pltpu.VMEM((d // RW_Q, RW_Q, RW_Q), F32)],
        compiler_params=_cparams(("parallel", "arbitrary")),
        name="rwkv_scan",
    )(rhat, y1, g, n)


def _rw_post_body(y_ref, r_ref, k_ref, v_ref, gg_ref, h_ref, vec_ref, w_ref, gpost_ref, sel_ref, selt_ref, o_ref):
    gn_w, gn_b, r_k = (vec_ref[i:i + 1, :] for i in range(3))
    f32 = lambda ref: ref[...].astype(F32)
    y = f32(y_ref)
    inv_n = 1.0 / HD
    mean = _head_expand(_head_sum(y, sel_ref) * inv_n, selt_ref)
    yc = y - mean
    var = _head_sum(yc * yc, sel_ref) * inv_n
    yn = yc * _head_expand(lax.rsqrt(var + GN_EPS), selt_ref) * gn_w + gn_b
    bonus = _head_expand(_head_sum(f32(r_ref) * f32(k_ref) * r_k, sel_ref), selt_ref) * f32(v_ref)
    z = ((yn + bonus) * f32(gg_ref)).astype(BF16)
    o_ref[...] = h_ref[...] + _rms(_dot(z, w_ref[...]), gpost_ref[...])


def _rw_post(y, r, k, v, gg, h, gn_w, gn_b, r_k, w_out, g_post):
    t, d = h.shape
    tm = min(PROJ_TM, t)
    vecs = _pad_rows(jnp.stack([gn_w, gn_b, r_k.reshape(d)]), 8)
    sel, selt = _head_selectors()
    tok = pl.BlockSpec((tm, d), lambda i: (i, 0))
    return pl.pallas_call(
        _rw_post_body,
        out_shape=jax.ShapeDtypeStruct((t, d), F32),
        grid=(t // tm,),
        in_specs=[tok] * 6 + [_const_spec((8, d)), _const_spec((d, d)), _const_spec((1, d)),
                              _const_spec((d, LANES)), _const_spec((LANES, d))],
        out_specs=tok,
        compiler_params=_cparams(("parallel",)),
        name="rwkv_post",
    )(y, r, k, v, gg, h, vecs, w_out.astype(BF16), g_post.reshape(1, d), sel, selt)


def _rwkv_mixer(h, g_pre, mu, w_in, w0, w_w2, a0, w_a2, w_g2, k_k, k_a, r_k, gn_w, gn_b, w_out, g_post):
    b, s, d = h.shape
    r, lw, k, v, a, bb, gg = _rw_proj(h, g_pre, mu, w_in, w0, w_w2, a0, w_a2, w_g2, k_k, k_a)
    rhat, y1, g, n = _rw_prep(r, lw, k, v, a, bb)
    y = _rw_scan(rhat, y1, g, n)
    f2 = lambda x: x.reshape(b * s, d)
    return _rw_post(f2(y), f2(r), f2(k), f2(v), f2(gg), f2(h), gn_w, gn_b, r_k, w_out, g_post).reshape(b, s, d)


def kernel(x, ffn1_norm_pre, ffn1_w_gu, ffn1_w_down, ffn1_norm_post, mix_norm_pre, nsa_w_in, nsa_pe_k,
           nsa_w_ck1, nsa_w_ck2, nsa_pe_v, nsa_w_cv1, nsa_w_cv2, nsa_w_out, rwkv_mu, rwkv_w_in, rwkv_w0,
           rwkv_w_w2, rwkv_a0, rwkv_w_a2, rwkv_w_g2, rwkv_k_k, rwkv_k_a, rwkv_r_k, rwkv_gn_w, rwkv_gn_b,
           rwkv_w_out, mix_norm_post, ffn2_norm_pre, ffn2_w_gu, ffn2_w_down, ffn2_norm_post):
    b, s, d = x.shape
    flat = lambda t: t.reshape(b * s, d)
    cube = lambda t: t.reshape(b, s, d)
    h = x
    depth = ffn1_norm_pre.shape[0]
    for i in range(depth):
        h = cube(_ffn(flat(h), ffn1_norm_pre[i], ffn1_w_gu[i], ffn1_w_down[i], ffn1_norm_post[i]))
        j = i // 2
        if i % 2 == 0:
            h = _nsa_mixer(h, mix_norm_pre[i], nsa_w_in[j], nsa_pe_k[j], nsa_w_ck1[j], nsa_w_ck2[j],
                           nsa_pe_v[j], nsa_w_cv1[j], nsa_w_cv2[j], nsa_w_out[j], mix_norm_post[i])
        else:
            h = _rwkv_mixer(h, mix_norm_pre[i], rwkv_mu[j], rwkv_w_in[j], rwkv_w0[j], rwkv_w_w2[j],
                            rwkv_a0[j], rwkv_w_a2[j], rwkv_w_g2[j], rwkv_k_k[j], rwkv_k_a[j], rwkv_r_k[j],
                            rwkv_gn_w[j], rwkv_gn_b[j], rwkv_w_out[j], mix_norm_post[i])
        h = cube(_ffn(flat(h), ffn2_norm_pre[i], ffn2_w_gu[i], ffn2_w_down[i], ffn2_norm_post[i]))
    return h
```

```python
import functools

import numpy as np
import jax
import jax.numpy as jnp
from jax import lax
from jax.experimental import pallas as pl
from jax.experimental.pallas import tpu as pltpu

F32 = jnp.float32
BF16 = jnp.bfloat16

D_MODEL = 1024
D_FF = 2816
HALF_STEP = 0.5
RMS_EPS = 1e-6
MASK_VALUE = -1e30

HD = 64
N_HEADS = 16
N_KV = 4
N_GRP = 4
KV_W = N_KV * HD
CMP_BLOCK = 32
CMP_STRIDE = 16
SEL_BLOCK = 64
N_SELECT = 16
WINDOW = 512
FORCED_SCORE = 1e4
N_GATE = 3 * N_HEADS

DECAY_LORA = 64
AAA_LORA = 64
GATE_LORA = 160
GN_EPS = 64e-5

LANES = 128
SUBLANES = 8
VMEM_LIMIT_BYTES = 56 * 1024 * 1024

FFN_TM = 512
FFN_CUTS = (0, 1536, 2816)
PROJ_TM = 512
ATT_T = 256
GATE_ROWS = 16
SUM_ROWS = 16
LOG2E = 1.4426950408889634
RW_C = 64
RW_Q = 128
RW_NCH = 16


def _cparams(sem):
    return pltpu.CompilerParams(dimension_semantics=sem, vmem_limit_bytes=VMEM_LIMIT_BYTES)


def _rms(x, g):
    ms = jnp.mean(x * x, axis=-1, keepdims=True)
    return x * lax.rsqrt(ms + RMS_EPS) * g


def _const_spec(shape):
    nd = len(shape)
    return pl.BlockSpec(shape, lambda *_: (0,) * nd, pipeline_mode=pl.Buffered(1))


def _dot(a, b):
    return jnp.dot(a, b, preferred_element_type=F32)


def _dot_nt(a, b):
    return lax.dot_general(a, b, (((1,), (1,)), ((), ())), preferred_element_type=F32)


def _dot_tn(a, b):
    return lax.dot_general(a, b, (((0,), (0,)), ((), ())), preferred_element_type=F32)


def _split2(x):
    hi = x.astype(BF16)
    lo = (x - hi.astype(F32)).astype(BF16)
    return hi, lo


def _split3(x):
    hi = x.astype(BF16)
    r1 = x - hi.astype(F32)
    mid = r1.astype(BF16)
    lo = (r1 - mid.astype(F32)).astype(BF16)
    return hi, mid, lo


def _ffn_body(x_ref, gpre_ref, wgu_ref, wd_ref, gpost_ref, o_ref):
    x = x_ref[...]
    xn = _rms(x, gpre_ref[...]).astype(BF16)
    acc = None
    for lo, hi in zip(FFN_CUTS[:-1], FFN_CUTS[1:]):
        gate = _dot(xn, wgu_ref[:, lo:hi])
        up = _dot(xn, wgu_ref[:, D_FF + lo:D_FF + hi])
        act = (gate * jax.nn.sigmoid(gate) * up).astype(BF16)
        part = _dot(act, wd_ref[lo:hi, :])
        acc = part if acc is None else acc + part
    o_ref[...] = x + HALF_STEP * _rms(acc, gpost_ref[...])


def _ffn(h2, g_pre, w_gu, w_down, g_post):
    t = h2.shape[0]
    tm = min(FFN_TM, t)
    return pl.pallas_call(
        _ffn_body,
        out_shape=jax.ShapeDtypeStruct((t, D_MODEL), F32),
        grid=(t // tm,),
        in_specs=[
            pl.BlockSpec((tm, D_MODEL), lambda i: (i, 0)),
            _const_spec((1, D_MODEL)),
            _const_spec((D_MODEL, 2 * D_FF)),
            _const_spec((D_FF, D_MODEL)),
            _const_spec((1, D_MODEL)),
        ],
        out_specs=pl.BlockSpec((tm, D_MODEL), lambda i: (i, 0)),
        compiler_params=_cparams(("parallel",)),
        name="ffn",
    )(h2, g_pre.reshape(1, D_MODEL), w_gu.astype(BF16), w_down.astype(BF16), g_post.reshape(1, D_MODEL))


def _nsa_proj_body(h_ref, g_ref, wqt_ref, wc_ref, wk_ref, wvt_ref, wgt_ref,
                   qt_ref, kc_ref, vc_ref, k_ref, vt_ref, gt_ref):
    tm = h_ref.shape[1]
    u = _rms(h_ref[0], g_ref[...]).astype(BF16)
    qt_ref[0] = (_dot_nt(wqt_ref[...], u) * (HD ** -0.5 * LOG2E)).astype(BF16)
    c = _dot(u, wc_ref[...])
    for j in range(KV_W // LANES):
        kc_ref[0, j] = c[:, j * LANES:(j + 1) * LANES]
        vc_ref[0, j] = c[:, KV_W + j * LANES:KV_W + (j + 1) * LANES]
    kk = _dot(u, wk_ref[...])
    t0 = pl.program_id(1) * tm
    lane = lax.broadcasted_iota(jnp.int32, (tm, LANES), 1)
    blk = lax.shift_right_logical(t0 + lax.broadcasted_iota(jnp.int32, (tm, LANES), 0), 6)
    onehot = jnp.where(lane - HD == blk, 1.0, 0.0)
    for g in range(N_KV):
        k_ref[0, g, 0] = (kk[:, g * LANES:(g + 1) * LANES] + onehot).astype(BF16)
        k_ref[0, g, 1] = kk[:, (N_KV + g) * LANES:(N_KV + g + 1) * LANES].astype(BF16)
    vt = _dot_nt(wvt_ref[...], u)
    tk = vt_ref.shape[5]
    for j in range(tm // tk):
        cols = slice(j * tk, (j + 1) * tk)
        vt_ref[0, :, 0, j] = vt[:KV_W, cols].reshape(N_KV, HD, tk).astype(BF16)
        vt_ref[0, :, 1, j] = vt[KV_W:, cols].reshape(N_KV, HD, tk).astype(BF16)
    gt_ref[0] = jax.nn.sigmoid(_dot_nt(wgt_ref[...], u)).reshape(N_KV, GATE_ROWS, tm)


def _nsa_proj(h, g_pre, w_in):
    b, s, _ = h.shape
    tm = min(PROJ_TM, s)
    tk = min(ATT_T, s)
    nq = N_HEADS * HD
    assert s // SEL_BLOCK <= HD
    col = lambda i: w_in[:, nq + i * KV_W:nq + (i + 1) * KV_W]
    w_qt = w_in[:, :nq].T.astype(BF16)
    w_c = jnp.concatenate([col(0), col(1)], axis=1).astype(BF16)
    pad_heads = lambda w: jnp.pad(w.reshape(D_MODEL, N_KV, HD), ((0, 0), (0, 0), (0, LANES - HD))).reshape(D_MODEL, N_KV * LANES)
    w_k = jnp.concatenate([pad_heads(col(2)), pad_heads(col(4))], axis=1).astype(BF16)
    w_vt = jnp.concatenate([col(3), col(5)], axis=1).T.astype(BF16)
    w_gl = w_in[:, nq + 6 * KV_W:].reshape(D_MODEL, N_KV, N_GRP * 3)
    w_gt = jnp.pad(w_gl, ((0, 0), (0, 0), (0, GATE_ROWS - N_GRP * 3))).reshape(D_MODEL, N_KV * GATE_ROWS).T.astype(BF16)
    tok = lambda w: pl.BlockSpec((1, tm, w), lambda bi, i: (bi, i, 0))
    cmp_spec = pl.BlockSpec((1, KV_W // LANES, tm, LANES), lambda bi, i: (bi, 0, i, 0))
    key = jax.ShapeDtypeStruct((b, N_KV, 2, s, LANES), BF16)
    key_spec = pl.BlockSpec((1, N_KV, 2, tm, LANES), lambda bi, i: (bi, 0, 0, i, 0))
    valt = jax.ShapeDtypeStruct((b, N_KV, 2, s // tk, HD, tk), BF16)
    valt_spec = pl.BlockSpec((1, N_KV, 2, tm // tk, HD, tk), lambda bi, i: (bi, 0, 0, i, 0, 0))
    return pl.pallas_call(
        _nsa_proj_body,
        out_shape=(
            jax.ShapeDtypeStruct((b, nq, s), BF16),
            jax.ShapeDtypeStruct((b, KV_W // LANES, s, LANES), F32),
            jax.ShapeDtypeStruct((b, KV_W // LANES, s, LANES), F32),
            key, valt,
            jax.ShapeDtypeStruct((b, N_KV, GATE_ROWS, s), F32),
        ),
        grid=(b, s // tm),
        in_specs=[
            tok(D_MODEL),
            _const_spec((1, D_MODEL)),
            _const_spec((nq, D_MODEL)),
            _const_spec((D_MODEL, 2 * KV_W)),
            _const_spec((D_MODEL, 2 * N_KV * LANES)),
            _const_spec((2 * KV_W, D_MODEL)),
            _const_spec((N_KV * GATE_ROWS, D_MODEL)),
        ],
        out_specs=(
            pl.BlockSpec((1, nq, tm), lambda bi, i: (bi, 0, i)),
            cmp_spec, cmp_spec, key_spec, valt_spec,
            pl.BlockSpec((1, N_KV, GATE_ROWS, tm), lambda bi, i: (bi, 0, 0, i)),
        ),
        compiler_params=_cparams(("parallel", "parallel")),
        name="nsa_proj",
    )(h, g_pre.reshape(1, D_MODEL), w_qt, w_c, w_k, w_vt, w_gt)


def _compress_body(x_ref, pe_ref, w1_ref, w2_ref, o_ref, *, transposed):
    half = CMP_BLOCK // 2
    n = x_ref.shape[2] // half
    ha = hb = None
    for l in range(half):
        xl = jnp.concatenate([x_ref[0, j, pl.ds(l, n, stride=half), :] for j in range(x_ref.shape[1])], axis=-1)
        a = _dot((xl + pe_ref[l:l + 1, :]).astype(BF16), w1_ref[l])
        b = _dot((xl + pe_ref[half + l:half + l + 1, :]).astype(BF16), w1_ref[half + l])
        ha = a if ha is None else ha + a
        hb = b if hb is None else hb + b
    hid = ha + pltpu.roll(hb, n - 1, 0)
    hid = (hid * jax.nn.sigmoid(hid)).astype(BF16)
    if transposed:
        o_ref[0] = _dot_nt(w2_ref[...], hid).reshape(N_KV, HD, n).astype(BF16)
    else:
        out = _dot(hid, w2_ref[...]).astype(BF16)
        for g in range(N_KV):
            o_ref[0, g] = out[:, g * HD:(g + 1) * HD]


def _compress(t, pe, w1, w2, transposed):
    b, _, s, _ = t.shape
    rows = s // (CMP_BLOCK // 2)
    eye = jnp.eye(N_KV, dtype=F32)
    wbd = jnp.einsum("ldc,gh->lgdhc", w1, eye).reshape(CMP_BLOCK, KV_W, KV_W).astype(BF16)
    pe_t = jnp.broadcast_to(pe[:, None, :], (CMP_BLOCK, N_KV, HD)).reshape(CMP_BLOCK, KV_W)
    w2bd = jnp.einsum("cd,gh->gchd", w2, eye).reshape(KV_W, KV_W)
    if transposed:
        w2bd = w2bd.T
        out_shape, out_block = (b, N_KV, HD, rows), (1, N_KV, HD, rows)
    else:
        out_shape, out_block = (b, N_KV, rows, HD), (1, N_KV, rows, HD)
    return pl.pallas_call(
        functools.partial(_compress_body, transposed=transposed),
        out_shape=jax.ShapeDtypeStruct(out_shape, BF16),
        grid=(b,),
        in_specs=[
            pl.BlockSpec((1, KV_W // LANES, s, LANES), lambda bi: (bi, 0, 0, 0)),
            _const_spec((CMP_BLOCK, KV_W)),
            _const_spec((CMP_BLOCK, KV_W, KV_W)),
            _const_spec((KV_W, KV_W)),
        ],
        out_specs=pl.BlockSpec(out_block, lambda bi: (bi, 0, 0, 0)),
        compiler_params=_cparams(("parallel",)),
        name="nsa_compress",
    )(t, pe_t, wbd, w2bd.astype(BF16))


def _alibi_slope(h):
    return float(2.0 ** (-8.0 * (h + 1) / N_HEADS))


def _cmp_probs(kc, qh, slope2, end_f, valid):
    s = _dot(kc, qh) + jnp.where(valid, slope2 * end_f, MASK_VALUE)
    return jnp.exp2(s - jnp.max(s, axis=0, keepdims=True)).astype(BF16)


def _cmp_geometry(n_cmp_rows, tq, t0):
    n_io = lax.broadcasted_iota(jnp.int32, (n_cmp_rows, tq), 0)
    t_io = t0 + lax.broadcasted_iota(jnp.int32, (n_cmp_rows, tq), 1)
    end = n_io * CMP_STRIDE + (CMP_BLOCK - 1)
    return end.astype(F32), (t_io >= end) & (n_io < n_cmp_rows - 1)


def _select_body(qt_ref, kc_ref, vct_ref, c2s_ref, chunk_ref, mneg_ref, active_ref, oc_ref, *, n_sel, n_cmp_rows):
    tq = qt_ref.shape[2]
    t0 = pl.program_id(1) * tq
    has_block = t0 + lax.broadcasted_iota(jnp.int32, (1, tq), 1) >= CMP_BLOCK - 1
    v_lo = n_sel + SUM_ROWS
    s_io = lax.broadcasted_iota(jnp.int32, (n_sel, tq), 0)
    tt = t0 + lax.broadcasted_iota(jnp.int32, (n_sel, tq), 1)
    cur = lax.shift_right_logical(tt, 6)
    forced = (s_io == 0) | (s_io == cur) | (s_io == cur - 1)
    visible = s_io * SEL_BLOCK <= tt

    end_f, valid = _cmp_geometry(n_cmp_rows, tq, t0)

    def compressed(g):
        kc = kc_ref[0, g]
        heads = range(g * N_GRP, (g + 1) * N_GRP)
        probs = [_cmp_probs(kc, qt_ref[0, hh * HD:(hh + 1) * HD, :], _alibi_slope(hh) * LOG2E, end_f, valid)
                 for hh in heads]
        lhs = jnp.concatenate([c2s_ref[...], vct_ref[0, g]], axis=0)
        mass = [_dot(lhs, p) for p in probs]
        inv_l = [1.0 / m[n_sel:n_sel + 1] for m in mass]
        for hh, m, il in zip(heads, mass, inv_l):
            oc_ref[0, hh * HD:(hh + 1) * HD, :] = jnp.where(has_block, m[v_lo:v_lo + HD] * il, 0.0).astype(BF16)
        return sum(m[0:n_sel] * il for m, il in zip(mass, inv_l))

    for g in range(N_KV):
        imp = compressed(g)
        x = jnp.where(forced, FORCED_SCORE, jnp.where(visible, imp, -1.0))
        cnt = jnp.zeros((n_sel, tq), F32)
        for j in range(n_sel):
            xj = x[j:j + 1, :]
            ge = jnp.where(xj >= x, 1.0, 0.0)
            gt = jnp.where(xj > x, 1.0, 0.0)
            cnt = cnt + jnp.where(s_io > j, ge, gt)
        chosen = cnt < float(N_SELECT)
        mneg_ref[0, g, 0:n_sel, :] = jnp.where(chosen, 0.0, MASK_VALUE).astype(BF16)
        if n_sel < HD:
            mneg_ref[0, g, n_sel:HD, :] = jnp.zeros((HD - n_sel, tq), BF16)
        per_query = _dot(chunk_ref[...], jnp.where(chosen, 1.0, 0.0).astype(BF16))
        active_ref[0, g, 0] = _dot_nt(jnp.ones((SUBLANES, tq), BF16), per_query.astype(BF16))


def _select(qt, k_cmp, v_cmpt):
    b, nq, s = qt.shape
    tq = min(ATT_T, s)
    n_sel = s // SEL_BLOCK
    rows = k_cmp.shape[2]
    n_cmp = (s - CMP_BLOCK) // CMP_STRIDE + 1
    assert rows == n_cmp + 1
    cs = np.arange(rows) * CMP_STRIDE
    bs = np.arange(n_sel) * SEL_BLOCK
    overlap = np.clip(np.minimum(cs[:, None] + CMP_BLOCK, bs[None, :] + SEL_BLOCK)
                      - np.maximum(cs[:, None], bs[None, :]), 0, None)
    c2s_t = jnp.asarray(np.concatenate([(overlap / CMP_BLOCK).T, np.ones((SUM_ROWS, rows))], axis=0), BF16)
    n_chunks = s // tq
    chunk_of_block = np.arange(n_sel) * SEL_BLOCK // tq
    chunk_mat = jnp.asarray(chunk_of_block[None, :] == np.arange(n_chunks)[:, None], BF16)
    return pl.pallas_call(
        functools.partial(_select_body, n_sel=n_sel, n_cmp_rows=rows),
        out_shape=(jax.ShapeDtypeStruct((b, N_KV, HD, s), BF16),
                   jax.ShapeDtypeStruct((b, N_KV, s // tq, SUBLANES, n_chunks), F32),
                   jax.ShapeDtypeStruct((b, nq, s), BF16)),
        grid=(b, s // tq),
        in_specs=[
            pl.BlockSpec((1, nq, tq), lambda bi, i: (bi, 0, i)),
            pl.BlockSpec((1, N_KV, rows, HD), lambda bi, i: (bi, 0, 0, 0)),
            pl.BlockSpec((1, N_KV, HD, rows), lambda bi, i: (bi, 0, 0, 0)),
            _const_spec((n_sel + SUM_ROWS, rows)),
            _const_spec((n_chunks, n_sel)),
        ],
        out_specs=(pl.BlockSpec((1, N_KV, HD, tq), lambda bi, i: (bi, 0, 0, i)),
                   pl.BlockSpec((1, N_KV, 1, SUBLANES, n_chunks), lambda bi, i: (bi, 0, i, 0, 0)),
                   pl.BlockSpec((1, nq, tq), lambda bi, i: (bi, 0, i))),
        compiler_params=_cparams(("parallel", "parallel")),
        name="nsa_select",
    )(qt, k_cmp, v_cmpt, c2s_t, chunk_mat)


def _attn_body(slope_ref, active_ref, qt_ref, k_ref, vt_ref, oc_ref, mneg_ref,
               gt_ref, o_ref, s_sc, bias_sc, m_sc, acc_sc, todo_ref):
    tq = qt_ref.shape[2]
    tk = tq
    g = pl.program_id(1)
    qi = pl.program_id(2)
    slopes = [slope_ref[g * N_GRP + r] for r in range(N_GRP)]
    q_rows = [qt_ref[0, r * HD:(r + 1) * HD, :] for r in range(N_GRP)]
    qk = [jnp.concatenate([q_rows[r], mneg_ref[0, 0]], axis=0) for r in range(N_GRP)]

    @pl.when(qi == 0)
    def _():
        key_io = lax.broadcasted_iota(jnp.int32, (tk, tq), 0)
        rel = key_io - lax.broadcasted_iota(jnp.int32, (tk, tq), 1)
        key_f = key_io.astype(F32)
        causal = jnp.where(rel <= 0, 0.0, MASK_VALUE)
        oldest = jnp.where(rel > 0, 0.0, MASK_VALUE)
        for r in range(N_GRP):
            base = slopes[r] * key_f
            bias_sc[0, r] = base
            bias_sc[1, r] = base + causal
            bias_sc[2, r] = base + oldest

    ones_rows = jnp.ones((SUM_ROWS, tk), BF16)

    def offsets(c):
        cf = ((c - qi) * tk).astype(F32)
        return [slopes[r] * cf for r in range(N_GRP)]

    tile = (pl.program_id(0) * N_KV + g) * pl.num_programs(2) + qi

    def note_active(c, n):
        todo_ref[n] = c
        return n + active_ref[tile * pl.num_programs(2) + c]

    n_sel = lax.fori_loop(0, qi, note_active, jnp.int32(0)) + 1
    c_lo_w = jnp.maximum(qi - WINDOW // tk, 0)
    n_items = n_sel + (qi - c_lo_w) + 1

    def item(pos):
        window = (pos >= n_sel).astype(jnp.int32)
        listed = todo_ref[jnp.minimum(pos, pl.num_programs(2) - 1)]
        c = jnp.where(pos < n_sel - 1, listed, jnp.where(pos == n_sel - 1, qi, c_lo_w + pos - n_sel))
        return window, c

    def score_stage(pos, slot):
        window, c = item(pos)
        kblk = k_ref[0, 0, window, pl.ds(pl.multiple_of(c * tk, tk), tk), :]
        variant = jnp.where(c == qi, 1, jnp.where((window == 1) & (c == qi - WINDOW // tk), 2, 0))
        offs = offsets(c)
        mcs = []
        for r in range(N_GRP):
            s = _dot(kblk, qk[r]) + bias_sc[variant, r]
            s_sc[slot, r] = s
            mcs.append(jnp.max(s, axis=0, keepdims=True) + offs[r])
        return tuple(mcs)

    def value_stage(pos, mcs, slot):
        window, c = item(pos)
        vt = jnp.concatenate([vt_ref[0, 0, window, c], ones_rows], axis=0)
        offs = offsets(c)
        for r in range(N_GRP):
            m = m_sc[window, r, 0:1, :]
            m_new = jnp.maximum(m, mcs[r])
            p = jnp.exp2(s_sc[slot, r] - (m_new - offs[r]))
            acc_sc[window, r] = jnp.exp2(m - m_new) * acc_sc[window, r] + _dot(vt, p.astype(BF16))
            m_sc[window, r, 0:1, :] = m_new

    m_sc[...] = jnp.full(m_sc.shape, MASK_VALUE, F32)
    acc_sc[...] = jnp.zeros(acc_sc.shape, F32)

    def trips(n, first, count, mcs):
        def trip(k, mcs):
            pos = first + n * k
            for i in range(n):
                mcs_next = score_stage(pos + i + 1, (i + 1) % 2)
                value_stage(pos + i, mcs, i % 2)
                mcs = mcs_next
            return mcs
        return lax.fori_loop(0, count, trip, mcs)

    steps = n_items - 1
    mcs = trips(4, 0, steps // 4, score_stage(jnp.int32(0), 0))
    done = (steps // 4) * 4
    mcs = trips(2, done, (steps - done) // 2, mcs)
    done = done + ((steps - done) // 2) * 2
    mcs = trips(1, done, steps - done, mcs)
    value_stage(steps, mcs, steps - done)

    gate = gt_ref[0, 0]
    normed = lambda acc: acc[0:HD] * (1.0 / acc[HD:HD + 1])
    for r in range(N_GRP):
        o_c = oc_ref[0, r * HD:(r + 1) * HD, :].astype(F32)
        o_s = normed(acc_sc[0, r])
        o_w = normed(acc_sc[1, r])
        out = (gate[3 * r:3 * r + 1, :] * o_c + gate[3 * r + 1:3 * r + 2, :] * o_s
               + gate[3 * r + 2:3 * r + 3, :] * o_w)
        o_ref[0, r * HD:(r + 1) * HD, :] = out.astype(BF16)


def _attention(qt, keys, vals_t, oc_t, mneg, active, gates_t):
    b, nq, s = qt.shape
    tq = vals_t.shape[5]
    assert WINDOW % tq == 0 and active.shape == (b, N_KV, s // tq, SUBLANES, s // tq)
    active = (active[:, :, :, 0, :] > 0.0).astype(jnp.int32).reshape(-1)
    gw = N_GRP * HD
    slopes = jnp.asarray([_alibi_slope(h) * LOG2E for h in range(N_HEADS)], F32)
    key_spec = pl.BlockSpec((1, 1, 2, s, LANES), lambda bi, g, i: (bi, g, 0, 0, 0))
    val_spec = pl.BlockSpec((1, 1, 2, s // tq, HD, tq), lambda bi, g, i: (bi, g, 0, 0, 0, 0))
    return pl.pallas_call(
        _attn_body,
        out_shape=jax.ShapeDtypeStruct((b, nq, s), BF16),
        grid=(b, N_KV, s // tq),
        in_specs=[
            pl.BlockSpec(memory_space=pltpu.SMEM),
            pl.BlockSpec(memory_space=pltpu.SMEM),
            pl.BlockSpec((1, gw, tq), lambda bi, g, i: (bi, g, i)),
            key_spec, val_spec,
            pl.BlockSpec((1, gw, tq), lambda bi, g, i: (bi, g, i)),
            pl.BlockSpec((1, 1, HD, tq), lambda bi, g, i: (bi, g, 0, i)),
            pl.BlockSpec((1, 1, GATE_ROWS, tq), lambda bi, g, i: (bi, g, 0, i)),
        ],
        out_specs=pl.BlockSpec((1, gw, tq), lambda bi, g, i: (bi, g, i)),
        scratch_shapes=[
            pltpu.VMEM((2, N_GRP, tq, tq), F32),
            pltpu.VMEM((3, N_GRP, tq, tq), F32),
            pltpu.VMEM((2, N_GRP, SUBLANES, tq), F32),
            pltpu.VMEM((2, N_GRP, HD + SUM_ROWS, tq), F32),
            pltpu.SMEM((s // tq,), jnp.int32),
        ],
        compiler_params=_cparams(("parallel", "parallel", "arbitrary")),
        name="nsa_attention",
    )(slopes, active, qt, keys, vals_t, oc_t, mneg, gates_t)


def _outproj_t_body(yt_ref, h_ref, w_ref, g_ref, o_ref):
    o_ref[0] = h_ref[0] + _rms(_dot_tn(yt_ref[0], w_ref[...]), g_ref[...])


def _outproj_t(yt, h, w_out, g_post):
    b, s, d = h.shape
    tm = min(PROJ_TM, s)
    return pl.pallas_call(
        _outproj_t_body,
        out_shape=jax.ShapeDtypeStruct((b, s, d), F32),
        grid=(b, s // tm),
        in_specs=[
            pl.BlockSpec((1, d, tm), lambda bi, i: (bi, 0, i)),
            pl.BlockSpec((1, tm, d), lambda bi, i: (bi, i, 0)),
            _const_spec((d, d)),
            _const_spec((1, d)),
        ],
        out_specs=pl.BlockSpec((1, tm, d), lambda bi, i: (bi, i, 0)),
        compiler_params=_cparams(("parallel", "parallel")),
        name="outproj_t",
    )(yt, h, w_out.astype(BF16), g_post.reshape(1, d))


def _nsa_mixer(h, g_pre, w_in, pe_k, w_ck1, w_ck2, pe_v, w_cv1, w_cv2, w_out, g_post):
    qt, kc, vc, keys, vals_t, gates_t = _nsa_proj(h, g_pre, w_in)
    k_cmp = _compress(kc, pe_k, w_ck1, w_ck2, transposed=False)
    v_cmpt = _compress(vc, pe_v, w_cv1, w_cv2, transposed=True)
    mneg, active, oc_t = _select(qt, k_cmp, v_cmpt)
    out_t = _attention(qt, keys, vals_t, oc_t, mneg, active, gates_t)
    return _outproj_t(out_t, h, w_out, g_post)


def _head_sum(x, sel_ref):
    return _dot(x.astype(BF16), sel_ref[...])


def _head_expand(x, selt_ref):
    return _dot(x.astype(BF16), selt_ref[...])


def _softplus(x):
    return jnp.maximum(x, 0.0) + jnp.log(1.0 + jnp.exp(-jnp.abs(x)))


def _rw_proj_body(h_ref, hp_ref, g_ref, mu_ref, vec_ref, wr_ref, wk_ref, wv_ref, wd_ref, wa_ref, wg_ref,
                  w2d_ref, w2a_ref, w2g_ref, sel_ref, selt_ref,
                  r_ref, lw_ref, k_ref, v_ref, a_ref, b_ref, gg_ref):
    tm = h_ref.shape[1]
    u = _rms(h_ref[0], g_ref[...])
    prev = _rms(hp_ref[0], g_ref[...])[7:8, :]
    prev = jnp.where(pl.program_id(1) == 0, 0.0, prev)
    row = lax.broadcasted_iota(jnp.int32, (tm, D_MODEL), 0)
    u_prev = jnp.where(row == 0, prev, pltpu.roll(u, 1, 0))
    xx = u_prev - u

    def mix(i):
        return (u + xx * mu_ref[i:i + 1, :]).astype(BF16)

    w0, a0, k_k, k_a = (vec_ref[i:i + 1, :] for i in range(4))
    r = _dot(mix(0), wr_ref[...])
    k = _dot(mix(1), wk_ref[...])
    v = _dot(mix(2), wv_ref[...])
    d1 = jnp.tanh(_dot(mix(3), wd_ref[...])).astype(BF16)
    a1 = _dot(mix(4), wa_ref[...]).astype(BF16)
    g1 = jax.nn.sigmoid(_dot(mix(5), wg_ref[...])).astype(BF16)
    w = -_softplus(-(w0 + _dot(d1, w2d_ref[...]))) - 0.5
    alpha = jax.nn.sigmoid(a0 + _dot(a1, w2a_ref[...]))
    kk = k * k_k
    norm = jnp.sqrt(_head_sum(kk * kk, sel_ref))
    kk = kk * _head_expand(1.0 / jnp.maximum(norm, 1e-12), selt_ref)
    r_ref[0] = r.astype(BF16)
    lw_ref[0] = -jnp.exp(w)
    k_ref[0] = (k * (1.0 + (alpha - 1.0) * k_a)).astype(BF16)
    v_ref[0] = v.astype(BF16)
    a_ref[0] = (-kk).astype(BF16)
    b_ref[0] = (kk * alpha).astype(BF16)
    gg_ref[0] = _dot(g1, w2g_ref[...]).astype(BF16)


def _head_selectors():
    lane_head = np.arange(D_MODEL) // HD
    sel = (lane_head[:, None] == np.arange(LANES)[None, :]).astype(np.float32)
    return jnp.asarray(sel, BF16), jnp.asarray(sel.T, BF16)


def _pad_cols(w, n):
    return jnp.pad(w, ((0, 0), (0, n - w.shape[1])))


def _pad_rows(w, n):
    return jnp.pad(w, ((0, n - w.shape[0]), (0, 0)))


def _rw_proj(h, g_pre, mu, w_in, w0, w_w2, a0, w_a2, w_g2, k_k, k_a):
    b, s, d = h.shape
    tm = min(PROJ_TM, s)
    offs = np.cumsum((0, d, d, d, DECAY_LORA, AAA_LORA, GATE_LORA))
    cols = [w_in[:, int(offs[i]):int(offs[i + 1])] for i in range(6)]
    ld, lg = LANES, 2 * LANES
    wr, wk, wv = (c.astype(BF16) for c in cols[:3])
    wd = _pad_cols(cols[3], ld).astype(BF16)
    wa = _pad_cols(cols[4], ld).astype(BF16)
    wg = _pad_cols(cols[5], lg).astype(BF16)
    w2d = _pad_rows(w_w2, ld).astype(BF16)
    w2a = _pad_rows(w_a2, ld).astype(BF16)
    w2g = _pad_rows(w_g2, lg).astype(BF16)
    mu8 = _pad_rows(mu, 8)
    vecs = _pad_rows(jnp.stack([w0, a0, k_k, k_a]), 8)
    sel, selt = _head_selectors()
    tok = pl.BlockSpec((1, tm, d), lambda bi, i: (bi, i, 0))
    out = lambda dt: jax.ShapeDtypeStruct((b, s, d), dt)
    return pl.pallas_call(
        _rw_proj_body,
        out_shape=(out(BF16), out(F32), out(BF16), out(BF16), out(BF16), out(BF16), out(BF16)),
        grid=(b, s // tm),
        in_specs=[
            tok,
            pl.BlockSpec((1, 8, d), lambda bi, i: (bi, jnp.maximum(i * (tm // 8) - 1, 0), 0)),
            _const_spec((1, d)), _const_spec((8, d)), _const_spec((8, d)),
            _const_spec((d, d)), _const_spec((d, d)), _const_spec((d, d)),
            _const_spec((d, ld)), _const_spec((d, ld)), _const_spec((d, lg)),
            _const_spec((ld, d)), _const_spec((ld, d)), _const_spec((lg, d)),
            _const_spec((d, LANES)), _const_spec((LANES, d)),
        ],
        out_specs=(tok,) * 7,
        compiler_params=_cparams(("parallel", "parallel")),
        name="rwkv_proj",
    )(h, h, g_pre.reshape(1, d), mu8, vecs, wr, wk, wv, wd, wa, wg, w2d, w2a, w2g, sel, selt)


def _rw_prep_body(ltri_ref, r_ref, lw_ref, k_ref, v_ref, a_ref, b_ref,
                  rhat_ref, y1_ref, g_ref, n_ref):
    c = RW_C
    q = RW_Q
    nh = q // HD
    nch = r_ref.shape[1] // c
    lane_head = lax.shift_right_logical(lax.broadcasted_iota(jnp.int32, (c, q), 1), 6)
    ri = lax.broadcasted_iota(jnp.int32, (q, q), 0)
    ci = lax.broadcasted_iota(jnp.int32, (q, q), 1)
    same_head = lax.shift_right_logical(ri, 6) == lax.shift_right_logical(ci, 6)
    strict_bd = same_head & ((ci & (c - 1)) < (ri & (c - 1)))
    eye = ri == ci
    t_io = lax.broadcasted_iota(jnp.int32, (c, q), 0)
    j_io = lax.broadcasted_iota(jnp.int32, (c, q), 1) & (c - 1)
    strict_ls = j_io < t_io
    incl_ls = j_io <= t_io
    rows = [slice(i * c, (i + 1) * c) for i in range(nch)]

    def each(f):
        return [f(i) for i in range(nch)]

    def expand(x):
        return jnp.concatenate([jnp.where(lane_head == hh, x, 0.0) for hh in range(nh)], axis=0).astype(BF16)

    def collapse(x):
        out = x[0:c]
        for hh in range(1, nh):
            out = out + x[hh * c:(hh + 1) * c]
        return out

    lw = each(lambda i: lw_ref[0, rows[i], :])
    parts = each(lambda i: _dot(ltri_ref[...], jnp.concatenate(_split3(lw[i]), axis=1)))
    cum = each(lambda i: parts[i][:, 0:q] + parts[i][:, q:2 * q] + parts[i][:, 2 * q:3 * q])
    cum_c = each(lambda i: cum[i][c - 1:c, :])
    e_inv = each(lambda i: jnp.exp(-cum[i]))
    e_rem = each(lambda i: jnp.exp(cum_c[i] - cum[i]))
    at = each(lambda i: a_ref[0, rows[i], :] * jnp.exp(cum[i] - lw[i]))
    rt = each(lambda i: r_ref[0, rows[i], :] * jnp.exp(cum[i]))
    x4 = each(lambda i: expand(at[i]))
    b4 = each(lambda i: expand(b_ref[0, rows[i], :] * e_inv[i]))
    k4 = each(lambda i: expand(k_ref[0, rows[i], :] * e_inv[i]))
    v4 = each(lambda i: expand(v_ref[0, rows[i], :]))
    bbar = each(lambda i: (b_ref[0, rows[i], :] * e_rem[i]).astype(BF16))
    kbar = each(lambda i: (k_ref[0, rows[i], :] * e_rem[i]).astype(BF16))

    lbd = each(lambda i: jnp.where(strict_bd, _dot_nt(x4[i], b4[i]), 0.0))
    tbd = each(lambda i: jnp.where(eye, 1.0, lbd[i]))
    lb = each(lambda i: lbd[i].astype(BF16))
    p = each(lambda i: _dot(lb[i], lb[i]))
    n_lvl = int(np.log2(c)) - 1
    for lvl in range(n_lvl):
        pb = each(lambda i: p[i].astype(BF16))
        if lvl + 1 < n_lvl:
            both = each(lambda i: _dot(pb[i], jnp.concatenate([tbd[i].astype(BF16), pb[i]], axis=1)))
            tbd = each(lambda i: tbd[i] + both[i][:, 0:q])
            p = each(lambda i: both[i][:, q:2 * q])
        else:
            tbd = each(lambda i: tbd[i] + _dot(pb[i], tbd[i].astype(BF16)))
    t_ls = each(lambda i: collapse(tbd[i]).astype(BF16))

    w = nh * c
    ar = each(lambda i: _dot_nt(jnp.concatenate([at[i], rt[i]], axis=0).astype(BF16),
                                jnp.concatenate([k4[i], b4[i]], axis=0)))
    a_kk = each(lambda i: jnp.concatenate([jnp.where(strict_ls, ar[i][0:c, 0:w], 0.0),
                                           jnp.where(incl_ls, ar[i][c:2 * c, 0:w], 0.0)], axis=0).astype(BF16))
    a_rb = each(lambda i: jnp.where(incl_ls, ar[i][c:2 * c, w:2 * w], 0.0).astype(BF16))

    gy = each(lambda i: _dot(a_kk[i], v4[i]))
    g1 = each(lambda i: gy[i][0:c])
    ua = each(lambda i: _dot(t_ls[i], jnp.concatenate([expand(g1[i]), x4[i]], axis=1)))
    u0 = each(lambda i: ua[i][:, 0:q])
    ahat = each(lambda i: ua[i][:, q:2 * q])
    ry = each(lambda i: _dot(a_rb[i], jnp.concatenate([expand(ahat[i]), expand(u0[i])], axis=1)))
    rhat = each(lambda i: rt[i] + ry[i][:, 0:q])
    y1 = each(lambda i: gy[i][c:2 * c] + ry[i][:, q:2 * q])

    gm = each(lambda i: _dot_tn(bbar[i], ahat[i].astype(BF16)))
    nm = each(lambda i: _dot_tn(jnp.concatenate([bbar[i], kbar[i]], axis=0),
                                jnp.concatenate([u0[i].astype(BF16), v_ref[0, rows[i], :]], axis=0)))
    for i in range(nch):
        gmi = jnp.where(same_head, gm[i], 0.0) + jnp.where(eye, jnp.exp(cum_c[i]), 0.0)
        rhat_ref[0, rows[i], :] = rhat[i].astype(BF16)
        y1_ref[0, rows[i], :] = y1[i].astype(BF16)
        g_ref[0, rows[i], :] = collapse(gmi).astype(BF16)
        n_ref[0, rows[i], :] = collapse(jnp.where(same_head, nm[i], 0.0)).astype(BF16)


def _rw_prep(r, lw, k, v, a, b):
    bsz, s, d = r.shape
    rows = min(RW_C * RW_NCH, s)
    ltri = jnp.asarray(np.tril(np.ones((RW_C, RW_C), np.float32)), BF16)
    blk = pl.BlockSpec((1, rows, RW_Q), lambda bi, qi, j: (bi, j, qi))
    return pl.pallas_call(
        _rw_prep_body,
        out_shape=(jax.ShapeDtypeStruct((bsz, s, d), BF16),) * 4,
        grid=(bsz, d // RW_Q, s // rows),
        in_specs=[_const_spec((RW_C, RW_C))] + [blk] * 6,
        out_specs=(blk,) * 4,
        compiler_params=_cparams(("parallel", "parallel", "parallel")),
        name="rwkv_prep",
    )(ltri, r, lw, k, v, a, b)


def _rw_scan_body(rhat_ref, y1_ref, g_ref, n_ref, y_ref, h_sc):
    c = RW_C
    q = RW_Q
    nb, nq = h_sc.shape[0], h_sc.shape[1]

    @pl.when(pl.program_id(1) == 0)
    def _():
        h_sc[...] = jnp.zeros(h_sc.shape, F32)

    ri = lax.broadcasted_iota(jnp.int32, (q, q), 0)
    ci = lax.broadcasted_iota(jnp.int32, (q, q), 1)
    same_head = lax.shift_right_logical(ri, 6) == lax.shift_right_logical(ci, 6)

    def step(ch, carry):
        rows = pl.ds(pl.multiple_of(ch * c, c), c)
        for bi in range(nb):
            for qi in range(nq):
                lanes = slice(qi * q, (qi + 1) * q)
                hb = h_sc[bi, qi].astype(BF16)
                y = _dot(rhat_ref[bi, rows, lanes], hb) + y1_ref[bi, rows, lanes]
                y_ref[bi, rows, lanes] = y.astype(BF16)
                g_ls = g_ref[bi, rows, lanes]
                n_ls = n_ref[bi, rows, lanes]
                gbd = jnp.where(same_head, jnp.concatenate([g_ls] * (q // c), axis=0), 0.0)
                nbd = jnp.where(same_head, jnp.concatenate([n_ls] * (q // c), axis=0), 0.0)
                h_sc[bi, qi] = _dot(gbd.astype(BF16), hb) + nbd
        return carry

    lax.fori_loop(0, rhat_ref.shape[1] // c, step, 0)


def _rw_scan(rhat, y1, g, n):
    bsz, s, d = rhat.shape
    rows = min(512, s)
    nb = 2 if bsz % 2 == 0 else 1
    blk = pl.BlockSpec((nb, rows, d), lambda bi, j: (bi, j, 0))
    return pl.pallas_call(
        _rw_scan_body,
        out_shape=jax.ShapeDtypeStruct((bsz, s, d), BF16),
        grid=(bsz // nb, s // rows),
        in_specs=[blk] * 4,
        out_specs=blk,
        scratch_shapes=[pltpu.VMEM((nb, d // RW_Q, RW_Q, RW_Q), F32)],
        compiler_params=_cparams(("parallel", "arbitrary")),
        name="rwkv_scan",
    )(rhat, y1, g, n)


def _rw_post_body(y_ref, r_ref, k_ref, v_ref, gg_ref, h_ref, vec_ref, w_ref, gpost_ref, sel_ref, selt_ref, o_ref):
    gn_w, gn_b, r_k = (vec_ref[i:i + 1, :] for i in range(3))
    f32 = lambda ref: ref[...].astype(F32)
    y = f32(y_ref)
    inv_n = 1.0 / HD
    mean = _head_expand(_head_sum(y, sel_ref) * inv_n, selt_ref)
    yc = y - mean
    var = _head_sum(yc * yc, sel_ref) * inv_n
    yn = yc * _head_expand(lax.rsqrt(var + GN_EPS), selt_ref) * gn_w + gn_b
    bonus = _head_expand(_head_sum(f32(r_ref) * f32(k_ref) * r_k, sel_ref), selt_ref) * f32(v_ref)
    z = ((yn + bonus) * f32(gg_ref)).astype(BF16)
    o_ref[...] = h_ref[...] + _rms(_dot(z, w_ref[...]), gpost_ref[...])


def _rw_post(y, r, k, v, gg, h, gn_w, gn_b, r_k, w_out, g_post):
    t, d = h.shape
    tm = min(PROJ_TM, t)
    vecs = _pad_rows(jnp.stack([gn_w, gn_b, r_k.reshape(d)]), 8)
    sel, selt = _head_selectors()
    tok = pl.BlockSpec((tm, d), lambda i: (i, 0))
    return pl.pallas_call(
        _rw_post_body,
        out_shape=jax.ShapeDtypeStruct((t, d), F32),
        grid=(t // tm,),
        in_specs=[tok] * 6 + [_const_spec((8, d)), _const_spec((d, d)), _const_spec((1, d)),
                              _const_spec((d, LANES)), _const_spec((LANES, d))],
        out_specs=tok,
        compiler_params=_cparams(("parallel",)),
        name="rwkv_post",
    )(y, r, k, v, gg, h, vecs, w_out.astype(BF16), g_post.reshape(1, d), sel, selt)


def _rwkv_mixer(h, g_pre, mu, w_in, w0, w_w2, a0, w_a2, w_g2, k_k, k_a, r_k, gn_w, gn_b, w_out, g_post):
    b, s, d = h.shape
    r, lw, k, v, a, bb, gg = _rw_proj(h, g_pre, mu, w_in, w0, w_w2, a0, w_a2, w_g2, k_k, k_a)
    rhat, y1, g, n = _rw_prep(r, lw, k, v, a, bb)
    y = _rw_scan(rhat, y1, g, n)
    f2 = lambda x: x.reshape(b * s, d)
    return _rw_post(f2(y), f2(r), f2(k), f2(v), f2(gg), f2(h), gn_w, gn_b, r_k, w_out, g_post).reshape(b, s, d)


def kernel(x, ffn1_norm_pre, ffn1_w_gu, ffn1_w_down, ffn1_norm_post, mix_norm_pre, nsa_w_in, nsa_pe_k,
           nsa_w_ck1, nsa_w_ck2, nsa_pe_v, nsa_w_cv1, nsa_w_cv2, nsa_w_out, rwkv_mu, rwkv_w_in, rwkv_w0,
           rwkv_w_w2, rwkv_a0, rwkv_w_a2, rwkv_w_g2, rwkv_k_k, rwkv_k_a, rwkv_r_k, rwkv_gn_w, rwkv_gn_b,
           rwkv_w_out, mix_norm_post, ffn2_norm_pre, ffn2_w_gu, ffn2_w_down, ffn2_norm_post):
    b, s, d = x.shape
    flat = lambda t: t.reshape(b * s, d)
    cube = lambda t: t.reshape(b, s, d)
    h = x
    depth = ffn1_norm_pre.shape[0]
    for i in range(depth):
        h = cube(_ffn(flat(h), ffn1_norm_pre[i], ffn1_w_gu[i], ffn1_w_down[i], ffn1_norm_post[i]))
        j = i // 2
        if i % 2 == 0:
            h = _nsa_mixer(h, mix_norm_pre[i], nsa_w_in[j], nsa_pe_k[j], nsa_w_ck1[j], nsa_w_ck2[j],
                           nsa_pe_v[j], nsa_w_cv1[j], nsa_w_cv2[j], nsa_w_out[j], mix_norm_post[i])
        else:
            h = _rwkv_mixer(h, mix_norm_pre[i], rwkv_mu[j], rwkv_w_in[j], rwkv_w0[j], rwkv_w_w2[j],
                            rwkv_a0[j], rwkv_w_a2[j], rwkv_w_g2[j], rwkv_k_k[j], rwkv_k_a[j], rwkv_r_k[j],
                            rwkv_gn_w[j], rwkv_gn_b[j], rwkv_w_out[j], mix_norm_post[i])
        h = cube(_ffn(flat(h), ffn2_norm_pre[i], ffn2_w_gu[i], ffn2_w_down[i], ffn2_norm_post[i]))
    return h
```

```python
import functools

import numpy as np
import jax
import jax.numpy as jnp
from jax import lax
from jax.experimental import pallas as pl
from jax.experimental.pallas import tpu as pltpu

F32 = jnp.float32
BF16 = jnp.bfloat16

D_MODEL = 1024
D_FF = 2816
HALF_STEP = 0.5
RMS_EPS = 1e-6
MASK_VALUE = -1e30

HD = 64
N_HEADS = 16
N_KV = 4
N_GRP = 4
KV_W = N_KV * HD
CMP_BLOCK = 32
CMP_STRIDE = 16
SEL_BLOCK = 64
N_SELECT = 16
WINDOW = 512
FORCED_SCORE = 1e4

DECAY_LORA = 64
AAA_LORA = 64
GATE_LORA = 160
GN_EPS = 64e-5

LANES = 128
SUBLANES = 8
VMEM_LIMIT_BYTES = 56 * 1024 * 1024

FFN_TM = 512
FFN_CUTS = (0, 1536, 2816)
PROJ_TM = 512
ATT_T = 256
GATE_ROWS = 16
SUM_ROWS = 16
LOG2E = 1.4426950408889634
RW_C = 64
RW_Q = 128
RW_NCH = 16


def _cparams(sem):
    return pltpu.CompilerParams(dimension_semantics=sem, vmem_limit_bytes=VMEM_LIMIT_BYTES)


def _rms(x, g):
    ms = jnp.mean(x * x, axis=-1, keepdims=True)
    return x * lax.rsqrt(ms + RMS_EPS) * g


def _const_spec(shape):
    nd = len(shape)
    return pl.BlockSpec(shape, lambda *_: (0,) * nd, pipeline_mode=pl.Buffered(1))


def _dot(a, b):
    return jnp.dot(a, b, preferred_element_type=F32)


def _dot_nt(a, b):
    return lax.dot_general(a, b, (((1,), (1,)), ((), ())), preferred_element_type=F32)


def _dot_tn(a, b):
    return lax.dot_general(a, b, (((0,), (0,)), ((), ())), preferred_element_type=F32)


def _split3(x):
    hi = x.astype(BF16)
    r1 = x - hi.astype(F32)
    mid = r1.astype(BF16)
    lo = (r1 - mid.astype(F32)).astype(BF16)
    return hi, mid, lo


def _ffn_body(x_ref, gpre_ref, wgu_ref, wd_ref, gpost_ref, o_ref):
    x = x_ref[...]
    xn = _rms(x, gpre_ref[...]).astype(BF16)
    acc = None
    for lo, hi in zip(FFN_CUTS[:-1], FFN_CUTS[1:]):
        gate = _dot(xn, wgu_ref[:, lo:hi])
        up = _dot(xn, wgu_ref[:, D_FF + lo:D_FF + hi])
        act = (gate * jax.nn.sigmoid(gate) * up).astype(BF16)
        part = _dot(act, wd_ref[lo:hi, :])
        acc = part if acc is None else acc + part
    o_ref[...] = x + HALF_STEP * _rms(acc, gpost_ref[...])


def _ffn(h2, g_pre, w_gu, w_down, g_post):
    t = h2.shape[0]
    tm = min(FFN_TM, t)
    return pl.pallas_call(
        _ffn_body,
        out_shape=jax.ShapeDtypeStruct((t, D_MODEL), F32),
        grid=(t // tm,),
        in_specs=[
            pl.BlockSpec((tm, D_MODEL), lambda i: (i, 0)),
            _const_spec((1, D_MODEL)),
            _const_spec((D_MODEL, 2 * D_FF)),
            _const_spec((D_FF, D_MODEL)),
            _const_spec((1, D_MODEL)),
        ],
        out_specs=pl.BlockSpec((tm, D_MODEL), lambda i: (i, 0)),
        compiler_params=_cparams(("parallel",)),
        name="ffn",
    )(h2, g_pre.reshape(1, D_MODEL), w_gu.astype(BF16), w_down.astype(BF16), g_post.reshape(1, D_MODEL))


def _nsa_proj_body(h_ref, g_ref, wqt_ref, wc_ref, wk_ref, wvt_ref, wgt_ref,
                   qt_ref, kc_ref, vc_ref, k_ref, vt_ref, gt_ref):
    tm = h_ref.shape[1]
    u = _rms(h_ref[0], g_ref[...]).astype(BF16)
    qt_ref[0] = (_dot_nt(wqt_ref[...], u) * (HD ** -0.5 * LOG2E)).astype(BF16)
    c = _dot(u, wc_ref[...])
    for j in range(KV_W // LANES):
        kc_ref[0, j] = c[:, j * LANES:(j + 1) * LANES]
        vc_ref[0, j] = c[:, KV_W + j * LANES:KV_W + (j + 1) * LANES]
    kk = _dot(u, wk_ref[...])
    t0 = pl.program_id(1) * tm
    lane = lax.broadcasted_iota(jnp.int32, (tm, LANES), 1)
    blk = lax.shift_right_logical(t0 + lax.broadcasted_iota(jnp.int32, (tm, LANES), 0), 6)
    onehot = jnp.where(lane - HD == blk, 1.0, 0.0)
    for g in range(N_KV):
        k_ref[0, g, 0] = (kk[:, g * LANES:(g + 1) * LANES] + onehot).astype(BF16)
        k_ref[0, g, 1] = kk[:, (N_KV + g) * LANES:(N_KV + g + 1) * LANES].astype(BF16)
    vt = _dot_nt(wvt_ref[...], u)
    tk = vt_ref.shape[5]
    for j in range(tm // tk):
        cols = slice(j * tk, (j + 1) * tk)
        vt_ref[0, :, 0, j] = vt[:KV_W, cols].reshape(N_KV, HD, tk).astype(BF16)
        vt_ref[0, :, 1, j] = vt[KV_W:, cols].reshape(N_KV, HD, tk).astype(BF16)
    gt_ref[0] = jax.nn.sigmoid(_dot_nt(wgt_ref[...], u)).reshape(N_KV, GATE_ROWS, tm)


def _nsa_proj(h, g_pre, w_in):
    b, s, _ = h.shape
    tm = min(PROJ_TM, s)
    tk = min(ATT_T, s)
    nq = N_HEADS * HD
    assert s // SEL_BLOCK <= HD
    col = lambda i: w_in[:, nq + i * KV_W:nq + (i + 1) * KV_W]
    w_qt = w_in[:, :nq].T.astype(BF16)
    w_c = jnp.concatenate([col(0), col(1)], axis=1).astype(BF16)
    pad_heads = lambda w: jnp.pad(w.reshape(D_MODEL, N_KV, HD), ((0, 0), (0, 0), (0, LANES - HD))).reshape(D_MODEL, N_KV * LANES)
    w_k = jnp.concatenate([pad_heads(col(2)), pad_heads(col(4))], axis=1).astype(BF16)
    w_vt = jnp.concatenate([col(3), col(5)], axis=1).T.astype(BF16)
    w_gl = w_in[:, nq + 6 * KV_W:].reshape(D_MODEL, N_KV, N_GRP * 3)
    w_gt = jnp.pad(w_gl, ((0, 0), (0, 0), (0, GATE_ROWS - N_GRP * 3))).reshape(D_MODEL, N_KV * GATE_ROWS).T.astype(BF16)
    tok = lambda w: pl.BlockSpec((1, tm, w), lambda bi, i: (bi, i, 0))
    cmp_spec = pl.BlockSpec((1, KV_W // LANES, tm, LANES), lambda bi, i: (bi, 0, i, 0))
    key = jax.ShapeDtypeStruct((b, N_KV, 2, s, LANES), BF16)
    key_spec = pl.BlockSpec((1, N_KV, 2, tm, LANES), lambda bi, i: (bi, 0, 0, i, 0))
    valt = jax.ShapeDtypeStruct((b, N_KV, 2, s // tk, HD, tk), BF16)
    valt_spec = pl.BlockSpec((1, N_KV, 2, tm // tk, HD, tk), lambda bi, i: (bi, 0, 0, i, 0, 0))
    return pl.pallas_call(
        _nsa_proj_body,
        out_shape=(
            jax.ShapeDtypeStruct((b, nq, s), BF16),
            jax.ShapeDtypeStruct((b, KV_W // LANES, s, LANES), F32),
            jax.ShapeDtypeStruct((b, KV_W // LANES, s, LANES), F32),
            key, valt,
            jax.ShapeDtypeStruct((b, N_KV, GATE_ROWS, s), F32),
        ),
        grid=(b, s // tm),
        in_specs=[
            tok(D_MODEL),
            _const_spec((1, D_MODEL)),
            _const_spec((nq, D_MODEL)),
            _const_spec((D_MODEL, 2 * KV_W)),
            _const_spec((D_MODEL, 2 * N_KV * LANES)),
            _const_spec((2 * KV_W, D_MODEL)),
            _const_spec((N_KV * GATE_ROWS, D_MODEL)),
        ],
        out_specs=(
            pl.BlockSpec((1, nq, tm), lambda bi, i: (bi, 0, i)),
            cmp_spec, cmp_spec, key_spec, valt_spec,
            pl.BlockSpec((1, N_KV, GATE_ROWS, tm), lambda bi, i: (bi, 0, 0, i)),
        ),
        compiler_params=_cparams(("parallel", "parallel")),
        name="nsa_proj",
    )(h, g_pre.reshape(1, D_MODEL), w_qt, w_c, w_k, w_vt, w_gt)


def _compress_body(x_ref, pe_ref, w1_ref, w2_ref, o_ref, *, transposed):
    half = CMP_BLOCK // 2
    n = x_ref.shape[2] // half
    ha = hb = None
    for l in range(half):
        xl = jnp.concatenate([x_ref[0, j, pl.ds(l, n, stride=half), :] for j in range(x_ref.shape[1])], axis=-1)
        a = _dot((xl + pe_ref[l:l + 1, :]).astype(BF16), w1_ref[l])
        b = _dot((xl + pe_ref[half + l:half + l + 1, :]).astype(BF16), w1_ref[half + l])
        ha = a if ha is None else ha + a
        hb = b if hb is None else hb + b
    hid = ha + pltpu.roll(hb, n - 1, 0)
    hid = (hid * jax.nn.sigmoid(hid)).astype(BF16)
    if transposed:
        o_ref[0] = _dot_nt(w2_ref[...], hid).reshape(N_KV, HD, n).astype(BF16)
    else:
        out = _dot(hid, w2_ref[...]).astype(BF16)
        for g in range(N_KV):
            o_ref[0, g] = out[:, g * HD:(g + 1) * HD]


def _compress(t, pe, w1, w2, transposed):
    b, _, s, _ = t.shape
    rows = s // (CMP_BLOCK // 2)
    eye = jnp.eye(N_KV, dtype=F32)
    wbd = jnp.einsum("ldc,gh->lgdhc", w1, eye).reshape(CMP_BLOCK, KV_W, KV_W).astype(BF16)
    pe_t = jnp.broadcast_to(pe[:, None, :], (CMP_BLOCK, N_KV, HD)).reshape(CMP_BLOCK, KV_W)
    w2bd = jnp.einsum("cd,gh->gchd", w2, eye).reshape(KV_W, KV_W)
    if transposed:
        w2bd = w2bd.T
        out_shape, out_block = (b, N_KV, HD, rows), (1, N_KV, HD, rows)
    else:
        out_shape, out_block = (b, N_KV, rows, HD), (1, N_KV, rows, HD)
    return pl.pallas_call(
        functools.partial(_compress_body, transposed=transposed),
        out_shape=jax.ShapeDtypeStruct(out_shape, BF16),
        grid=(b,),
        in_specs=[
            pl.BlockSpec((1, KV_W // LANES, s, LANES), lambda bi: (bi, 0, 0, 0)),
            _const_spec((CMP_BLOCK, KV_W)),
            _const_spec((CMP_BLOCK, KV_W, KV_W)),
            _const_spec((KV_W, KV_W)),
        ],
        out_specs=pl.BlockSpec(out_block, lambda bi: (bi, 0, 0, 0)),
        compiler_params=_cparams(("parallel",)),
        name="nsa_compress",
    )(t, pe_t, wbd, w2bd.astype(BF16))


def _alibi_slope(h):
    return float(2.0 ** (-8.0 * (h + 1) / N_HEADS))


def _cmp_probs(kc, qh, slope2, end_f, valid):
    s = _dot(kc, qh) + jnp.where(valid, slope2 * end_f, MASK_VALUE)
    return jnp.exp2(s - jnp.max(s, axis=0, keepdims=True)).astype(BF16)


def _cmp_geometry(n_cmp_rows, tq, t0):
    n_io = lax.broadcasted_iota(jnp.int32, (n_cmp_rows, tq), 0)
    t_io = t0 + lax.broadcasted_iota(jnp.int32, (n_cmp_rows, tq), 1)
    end = n_io * CMP_STRIDE + (CMP_BLOCK - 1)
    return end.astype(F32), (t_io >= end) & (n_io < n_cmp_rows - 1)


def _select_body(qt_ref, kc_ref, vct_ref, c2s_ref, chunk_ref, mneg_ref, active_ref, oc_ref, *, n_sel, n_cmp_rows):
    tq = qt_ref.shape[2]
    t0 = pl.program_id(1) * tq
    has_block = t0 + lax.broadcasted_iota(jnp.int32, (1, tq), 1) >= CMP_BLOCK - 1
    v_lo = n_sel + SUM_ROWS
    s_io = lax.broadcasted_iota(jnp.int32, (n_sel, tq), 0)
    tt = t0 + lax.broadcasted_iota(jnp.int32, (n_sel, tq), 1)
    cur = lax.shift_right_logical(tt, 6)
    forced = (s_io == 0) | (s_io == cur) | (s_io == cur - 1)
    visible = s_io * SEL_BLOCK <= tt

    end_f, valid = _cmp_geometry(n_cmp_rows, tq, t0)

    def compressed(g):
        kc = kc_ref[0, g]
        heads = range(g * N_GRP, (g + 1) * N_GRP)
        probs = [_cmp_probs(kc, qt_ref[0, hh * HD:(hh + 1) * HD, :], _alibi_slope(hh) * LOG2E, end_f, valid)
                 for hh in heads]
        lhs = jnp.concatenate([c2s_ref[...], vct_ref[0, g]], axis=0)
        mass = [_dot(lhs, p) for p in probs]
        inv_l = [1.0 / m[n_sel:n_sel + 1] for m in mass]
        for hh, m, il in zip(heads, mass, inv_l):
            oc_ref[0, hh * HD:(hh + 1) * HD, :] = jnp.where(has_block, m[v_lo:v_lo + HD] * il, 0.0).astype(BF16)
        return sum(m[0:n_sel] * il for m, il in zip(mass, inv_l))

    def select(g, imp):
        x = jnp.where(forced, FORCED_SCORE, jnp.where(visible, imp, -1.0))
        cnt = jnp.zeros((n_sel, tq), F32)
        for j in range(n_sel):
            xj = x[j:j + 1, :]
            ge = jnp.where(xj >= x, 1.0, 0.0)
            gt = jnp.where(xj > x, 1.0, 0.0)
            cnt = cnt + jnp.where(s_io > j, ge, gt)
        chosen = cnt < float(N_SELECT)
        mneg_ref[0, g, 0:n_sel, :] = jnp.where(chosen, 0.0, MASK_VALUE).astype(BF16)
        if n_sel < HD:
            mneg_ref[0, g, n_sel:HD, :] = jnp.zeros((HD - n_sel, tq), BF16)
        per_query = _dot(chunk_ref[...], jnp.where(chosen, 1.0, 0.0).astype(BF16))
        active_ref[0, g, 0] = _dot_nt(jnp.ones((SUBLANES, tq), BF16), per_query.astype(BF16))

    imps = [compressed(0)]
    for g in range(1, N_KV):
        imps.append(compressed(g))
        select(g - 1, imps[g - 1])
    select(N_KV - 1, imps[N_KV - 1])


def _select(qt, k_cmp, v_cmpt):
    b, nq, s = qt.shape
    tq = min(ATT_T, s)
    n_sel = s // SEL_BLOCK
    rows = k_cmp.shape[2]
    n_cmp = (s - CMP_BLOCK) // CMP_STRIDE + 1
    assert rows == n_cmp + 1
    cs = np.arange(rows) * CMP_STRIDE
    bs = np.arange(n_sel) * SEL_BLOCK
    overlap = np.clip(np.minimum(cs[:, None] + CMP_BLOCK, bs[None, :] + SEL_BLOCK)
                      - np.maximum(cs[:, None], bs[None, :]), 0, None)
    c2s_t = jnp.asarray(np.concatenate([(overlap / CMP_BLOCK).T, np.ones((SUM_ROWS, rows))], axis=0), BF16)
    n_chunks = s // tq
    chunk_of_block = np.arange(n_sel) * SEL_BLOCK // tq
    chunk_mat = jnp.asarray(chunk_of_block[None, :] == np.arange(n_chunks)[:, None], BF16)
    return pl.pallas_call(
        functools.partial(_select_body, n_sel=n_sel, n_cmp_rows=rows),
        out_shape=(jax.ShapeDtypeStruct((b, N_KV, HD, s), BF16),
                   jax.ShapeDtypeStruct((b, N_KV, s // tq, SUBLANES, n_chunks), F32),
                   jax.ShapeDtypeStruct((b, nq, s), BF16)),
        grid=(b, s // tq),
        in_specs=[
            pl.BlockSpec((1, nq, tq), lambda bi, i: (bi, 0, i)),
            pl.BlockSpec((1, N_KV, rows, HD), lambda bi, i: (bi, 0, 0, 0)),
            pl.BlockSpec((1, N_KV, HD, rows), lambda bi, i: (bi, 0, 0, 0)),
            _const_spec((n_sel + SUM_ROWS, rows)),
            _const_spec((n_chunks, n_sel)),
        ],
        out_specs=(pl.BlockSpec((1, N_KV, HD, tq), lambda bi, i: (bi, 0, 0, i)),
                   pl.BlockSpec((1, N_KV, 1, SUBLANES, n_chunks), lambda bi, i: (bi, 0, i, 0, 0)),
                   pl.BlockSpec((1, nq, tq), lambda bi, i: (bi, 0, i))),
        compiler_params=_cparams(("parallel", "parallel")),
        name="nsa_select",
    )(qt, k_cmp, v_cmpt, c2s_t, chunk_mat)


def _attn_body(slope_ref, active_ref, qt_ref, k_ref, vt_ref, oc_ref, mneg_ref,
               gt_ref, o_ref, s_sc, bias_sc, m_sc, acc_sc, todo_ref):
    tq = qt_ref.shape[2]
    tk = tq
    g = pl.program_id(1)
    qi = pl.program_id(2)
    slopes = [slope_ref[g * N_GRP + r] for r in range(N_GRP)]
    q_rows = [qt_ref[0, r * HD:(r + 1) * HD, :] for r in range(N_GRP)]
    qk = [jnp.concatenate([q_rows[r], mneg_ref[0, 0]], axis=0) for r in range(N_GRP)]

    @pl.when(qi == 0)
    def _():
        key_io = lax.broadcasted_iota(jnp.int32, (tk, tq), 0)
        rel = key_io - lax.broadcasted_iota(jnp.int32, (tk, tq), 1)
        key_f = key_io.astype(F32)
        causal = jnp.where(rel <= 0, 0.0, MASK_VALUE)
        oldest = jnp.where(rel > 0, 0.0, MASK_VALUE)
        for r in range(N_GRP):
            base = slopes[r] * key_f
            bias_sc[0, r] = base
            bias_sc[1, r] = base + causal
            bias_sc[2, r] = base + oldest

    ones_rows = jnp.ones((SUM_ROWS, tk), BF16)

    def offsets(c):
        cf = ((c - qi) * tk).astype(F32)
        return [slopes[r] * cf for r in range(N_GRP)]

    tile = (pl.program_id(0) * N_KV + g) * pl.num_programs(2) + qi

    def note_active(c, n):
        todo_ref[n] = c
        return n + active_ref[tile * pl.num_programs(2) + c]

    todo_ref[0] = 0
    n_sel = lax.fori_loop(0, qi, note_active, jnp.int32(0)) + 1
    c_lo_w = jnp.maximum(qi - WINDOW // tk, 0)
    n_items = n_sel + (qi - c_lo_w) + 1

    def item(pos):
        window = (pos >= n_sel).astype(jnp.int32)
        listed = todo_ref[jnp.clip(pos, 0, jnp.maximum(n_sel - 2, 0))]
        c = jnp.where(pos < n_sel - 1, listed, jnp.where(pos == n_sel - 1, qi, c_lo_w + pos - n_sel))
        return window, c

    def score_stage(pos, slot):
        window, c = item(pos)
        kblk = k_ref[0, 0, window, pl.ds(pl.multiple_of(c * tk, tk), tk), :]
        variant = jnp.where(c == qi, 1, jnp.where((window == 1) & (c == qi - WINDOW // tk), 2, 0))
        offs = offsets(c)
        mcs = []
        for r in range(N_GRP):
            s = _dot(kblk, qk[r]) + bias_sc[variant, r]
            s_sc[slot, r] = s
            mcs.append(jnp.max(s, axis=0, keepdims=True) + offs[r])
        return tuple(mcs)

    def value_stage(pos, mcs, slot):
        window, c = item(pos)
        vt = jnp.concatenate([vt_ref[0, 0, window, c], ones_rows], axis=0)
        offs = offsets(c)
        for r in range(N_GRP):
            m = m_sc[window, r, 0:1, :]
            m_new = jnp.maximum(m, mcs[r])
            p = jnp.exp2(s_sc[slot, r] - (m_new - offs[r]))
            acc_sc[window, r] = jnp.exp2(m - m_new) * acc_sc[window, r] + _dot(vt, p.astype(BF16))
            m_sc[window, r, 0:1, :] = m_new

    m_sc[...] = jnp.full(m_sc.shape, MASK_VALUE, F32)
    acc_sc[...] = jnp.zeros(acc_sc.shape, F32)

    def trips(n, first, count, mcs):
        def trip(k, mcs):
            pos = first + n * k
            for i in range(n):
                mcs_next = score_stage(pos + i + 1, (i + 1) % 2)
                value_stage(pos + i, mcs, i % 2)
                mcs = mcs_next
            return mcs
        return lax.fori_loop(0, count, trip, mcs)

    steps = n_items - 1
    mcs = trips(4, 0, steps // 4, score_stage(jnp.int32(0), 0))
    done = (steps // 4) * 4
    mcs = trips(2, done, (steps - done) // 2, mcs)
    done = done + ((steps - done) // 2) * 2
    mcs = trips(1, done, steps - done, mcs)
    value_stage(steps, mcs, steps - done)

    gate = gt_ref[0, 0]
    normed = lambda acc: acc[0:HD] * (1.0 / acc[HD:HD + 1])
    for r in range(N_GRP):
        o_c = oc_ref[0, r * HD:(r + 1) * HD, :].astype(F32)
        o_s = normed(acc_sc[0, r])
        o_w = normed(acc_sc[1, r])
        out = (gate[3 * r:3 * r + 1, :] * o_c + gate[3 * r + 1:3 * r + 2, :] * o_s
               + gate[3 * r + 2:3 * r + 3, :] * o_w)
        o_ref[0, r * HD:(r + 1) * HD, :] = out.astype(BF16)


def _attention(qt, keys, vals_t, oc_t, mneg, active, gates_t):
    b, nq, s = qt.shape
    tq = vals_t.shape[5]
    assert WINDOW % tq == 0 and active.shape == (b, N_KV, s // tq, SUBLANES, s // tq)
    active = (active[:, :, :, 0, :] > 0.0).astype(jnp.int32).reshape(-1)
    gw = N_GRP * HD
    slopes = jnp.asarray([_alibi_slope(h) * LOG2E for h in range(N_HEADS)], F32)
    key_spec = pl.BlockSpec((1, 1, 2, s, LANES), lambda bi, g, i: (bi, g, 0, 0, 0))
    val_spec = pl.BlockSpec((1, 1, 2, s // tq, HD, tq), lambda bi, g, i: (bi, g, 0, 0, 0, 0))
    return pl.pallas_call(
        _attn_body,
        out_shape=jax.ShapeDtypeStruct((b, nq, s), BF16),
        grid=(b, N_KV, s // tq),
        in_specs=[
            pl.BlockSpec(memory_space=pltpu.SMEM),
            pl.BlockSpec(memory_space=pltpu.SMEM),
            pl.BlockSpec((1, gw, tq), lambda bi, g, i: (bi, g, i)),
            key_spec, val_spec,
            pl.BlockSpec((1, gw, tq), lambda bi, g, i: (bi, g, i)),
            pl.BlockSpec((1, 1, HD, tq), lambda bi, g, i: (bi, g, 0, i)),
            pl.BlockSpec((1, 1, GATE_ROWS, tq), lambda bi, g, i: (bi, g, 0, i)),
        ],
        out_specs=pl.BlockSpec((1, gw, tq), lambda bi, g, i: (bi, g, i)),
        scratch_shapes=[
            pltpu.VMEM((2, N_GRP, tq, tq), F32),
            pltpu.VMEM((3, N_GRP, tq, tq), F32),
            pltpu.VMEM((2, N_GRP, SUBLANES, tq), F32),
            pltpu.VMEM((2, N_GRP, HD + SUM_ROWS, tq), F32),
            pltpu.SMEM((s // tq,), jnp.int32),
        ],
        compiler_params=_cparams(("parallel", "parallel", "arbitrary")),
        name="nsa_attention",
    )(slopes, active, qt, keys, vals_t, oc_t, mneg, gates_t)


def _outproj_t_body(yt_ref, h_ref, w_ref, g_ref, o_ref):
    o_ref[0] = h_ref[0] + _rms(_dot_tn(yt_ref[0], w_ref[...]), g_ref[...])


def _outproj_t(yt, h, w_out, g_post):
    b, s, d = h.shape
    tm = min(PROJ_TM, s)
    return pl.pallas_call(
        _outproj_t_body,
        out_shape=jax.ShapeDtypeStruct((b, s, d), F32),
        grid=(b, s // tm),
        in_specs=[
            pl.BlockSpec((1, d, tm), lambda bi, i: (bi, 0, i)),
            pl.BlockSpec((1, tm, d), lambda bi, i: (bi, i, 0)),
            _const_spec((d, d)),
            _const_spec((1, d)),
        ],
        out_specs=pl.BlockSpec((1, tm, d), lambda bi, i: (bi, i, 0)),
        compiler_params=_cparams(("parallel", "parallel")),
        name="outproj_t",
    )(yt, h, w_out.astype(BF16), g_post.reshape(1, d))


def _nsa_mixer(h, g_pre, w_in, pe_k, w_ck1, w_ck2, pe_v, w_cv1, w_cv2, w_out, g_post):
    qt, kc, vc, keys, vals_t, gates_t = _nsa_proj(h, g_pre, w_in)
    k_cmp = _compress(kc, pe_k, w_ck1, w_ck2, transposed=False)
    v_cmpt = _compress(vc, pe_v, w_cv1, w_cv2, transposed=True)
    mneg, active, oc_t = _select(qt, k_cmp, v_cmpt)
    out_t = _attention(qt, keys, vals_t, oc_t, mneg, active, gates_t)
    return _outproj_t(out_t, h, w_out, g_post)


def _head_sum(x, sel_ref):
    return _dot(x.astype(BF16), sel_ref[...])


def _head_expand(x, selt_ref):
    return _dot(x.astype(BF16), selt_ref[...])


def _softplus(x):
    return jnp.maximum(x, 0.0) + jnp.log(1.0 + jnp.exp(-jnp.abs(x)))


def _rw_proj_body(h_ref, hp_ref, g_ref, mu_ref, vec_ref, wr_ref, wk_ref, wv_ref, wd_ref, wa_ref, wg_ref,
                  w2d_ref, w2a_ref, w2g_ref, sel_ref, selt_ref,
                  r_ref, lw_ref, k_ref, v_ref, a_ref, b_ref, gg_ref):
    tm = h_ref.shape[1]
    u = _rms(h_ref[0], g_ref[...])
    prev = _rms(hp_ref[0], g_ref[...])[7:8, :]
    prev = jnp.where(pl.program_id(1) == 0, 0.0, prev)
    row = lax.broadcasted_iota(jnp.int32, (tm, D_MODEL), 0)
    u_prev = jnp.where(row == 0, prev, pltpu.roll(u, 1, 0))
    xx = u_prev - u

    def mix(i):
        return (u + xx * mu_ref[i:i + 1, :]).astype(BF16)

    w0, a0, k_k, k_a = (vec_ref[i:i + 1, :] for i in range(4))
    r = _dot(mix(0), wr_ref[...])
    k = _dot(mix(1), wk_ref[...])
    v = _dot(mix(2), wv_ref[...])
    d1 = jnp.tanh(_dot(mix(3), wd_ref[...])).astype(BF16)
    a1 = _dot(mix(4), wa_ref[...]).astype(BF16)
    g1 = jax.nn.sigmoid(_dot(mix(5), wg_ref[...])).astype(BF16)
    w = -_softplus(-(w0 + _dot(d1, w2d_ref[...]))) - 0.5
    alpha = jax.nn.sigmoid(a0 + _dot(a1, w2a_ref[...]))
    kk = k * k_k
    norm = jnp.sqrt(_head_sum(kk * kk, sel_ref))
    kk = kk * _head_expand(1.0 / jnp.maximum(norm, 1e-12), selt_ref)
    r_ref[0] = r.astype(BF16)
    lw_ref[0] = -jnp.exp(w)
    k_ref[0] = (k * (1.0 + (alpha - 1.0) * k_a)).astype(BF16)
    v_ref[0] = v.astype(BF16)
    a_ref[0] = (-kk).astype(BF16)
    b_ref[0] = (kk * alpha).astype(BF16)
    gg_ref[0] = _dot(g1, w2g_ref[...]).astype(BF16)


def _head_selectors():
    lane_head = np.arange(D_MODEL) // HD
    sel = (lane_head[:, None] == np.arange(LANES)[None, :]).astype(np.float32)
    return jnp.asarray(sel, BF16), jnp.asarray(sel.T, BF16)


def _pad_cols(w, n):
    return jnp.pad(w, ((0, 0), (0, n - w.shape[1])))


def _pad_rows(w, n):
    return jnp.pad(w, ((0, n - w.shape[0]), (0, 0)))


def _rw_proj(h, g_pre, mu, w_in, w0, w_w2, a0, w_a2, w_g2, k_k, k_a):
    b, s, d = h.shape
    tm = min(PROJ_TM, s)
    offs = np.cumsum((0, d, d, d, DECAY_LORA, AAA_LORA, GATE_LORA))
    cols = [w_in[:, int(offs[i]):int(offs[i + 1])] for i in range(6)]
    ld, lg = LANES, 2 * LANES
    wr, wk, wv = (c.astype(BF16) for c in cols[:3])
    wd = _pad_cols(cols[3], ld).astype(BF16)
    wa = _pad_cols(cols[4], ld).astype(BF16)
    wg = _pad_cols(cols[5], lg).astype(BF16)
    w2d = _pad_rows(w_w2, ld).astype(BF16)
    w2a = _pad_rows(w_a2, ld).astype(BF16)
    w2g = _pad_rows(w_g2, lg).astype(BF16)
    mu8 = _pad_rows(mu, 8)
    vecs = _pad_rows(jnp.stack([w0, a0, k_k, k_a]), 8)
    sel, selt = _head_selectors()
    tok = pl.BlockSpec((1, tm, d), lambda bi, i: (bi, i, 0))
    out = lambda dt: jax.ShapeDtypeStruct((b, s, d), dt)
    return pl.pallas_call(
        _rw_proj_body,
        out_shape=(out(BF16), out(F32), out(BF16), out(BF16), out(BF16), out(BF16), out(BF16)),
        grid=(b, s // tm),
        in_specs=[
            tok,
            pl.BlockSpec((1, 8, d), lambda bi, i: (bi, jnp.maximum(i * (tm // 8) - 1, 0), 0)),
            _const_spec((1, d)), _const_spec((8, d)), _const_spec((8, d)),
            _const_spec((d, d)), _const_spec((d, d)), _const_spec((d, d)),
            _const_spec((d, ld)), _const_spec((d, ld)), _const_spec((d, lg)),
            _const_spec((ld, d)), _const_spec((ld, d)), _const_spec((lg, d)),
            _const_spec((d, LANES)), _const_spec((LANES, d)),
        ],
        out_specs=(tok,) * 7,
        compiler_params=_cparams(("parallel", "parallel")),
        name="rwkv_proj",
    )(h, h, g_pre.reshape(1, d), mu8, vecs, wr, wk, wv, wd, wa, wg, w2d, w2a, w2g, sel, selt)


def _rw_prep_body(ltri_ref, r_ref, lw_ref, k_ref, v_ref, a_ref, b_ref,
                  rhat_ref, y1_ref, g_ref, n_ref):
    c = RW_C
    q = RW_Q
    nh = q // HD
    nch = r_ref.shape[1] // c
    lane_head = lax.shift_right_logical(lax.broadcasted_iota(jnp.int32, (c, q), 1), 6)
    ri = lax.broadcasted_iota(jnp.int32, (q, q), 0)
    ci = lax.broadcasted_iota(jnp.int32, (q, q), 1)
    same_head = lax.shift_right_logical(ri, 6) == lax.shift_right_logical(ci, 6)
    strict_bd = same_head & ((ci & (c - 1)) < (ri & (c - 1)))
    eye = ri == ci
    t_io = lax.broadcasted_iota(jnp.int32, (c, q), 0)
    j_io = lax.broadcasted_iota(jnp.int32, (c, q), 1) & (c - 1)
    strict_ls = j_io < t_io
    incl_ls = j_io <= t_io
    rows = [slice(i * c, (i + 1) * c) for i in range(nch)]

    def each(f):
        return [f(i) for i in range(nch)]

    def expand(x):
        return jnp.concatenate([jnp.where(lane_head == hh, x, 0.0) for hh in range(nh)], axis=0).astype(BF16)

    def collapse(x):
        out = x[0:c]
        for hh in range(1, nh):
            out = out + x[hh * c:(hh + 1) * c]
        return out

    lw = each(lambda i: lw_ref[0, rows[i], :])
    parts = each(lambda i: _dot(ltri_ref[...], jnp.concatenate(_split3(lw[i]), axis=1)))
    cum = each(lambda i: parts[i][:, 0:q] + parts[i][:, q:2 * q] + parts[i][:, 2 * q:3 * q])
    cum_c = each(lambda i: cum[i][c - 1:c, :])
    e_inv = each(lambda i: jnp.exp(-cum[i]))
    e_rem = each(lambda i: jnp.exp(cum_c[i] - cum[i]))
    at = each(lambda i: a_ref[0, rows[i], :] * jnp.exp(cum[i] - lw[i]))
    rt = each(lambda i: r_ref[0, rows[i], :] * jnp.exp(cum[i]))
    x4 = each(lambda i: expand(at[i]))
    b4 = each(lambda i: expand(b_ref[0, rows[i], :] * e_inv[i]))
    k4 = each(lambda i: expand(k_ref[0, rows[i], :] * e_inv[i]))
    v4 = each(lambda i: expand(v_ref[0, rows[i], :]))
    bbar = each(lambda i: (b_ref[0, rows[i], :] * e_rem[i]).astype(BF16))
    kbar = each(lambda i: (k_ref[0, rows[i], :] * e_rem[i]).astype(BF16))

    lbd = each(lambda i: jnp.where(strict_bd, _dot_nt(x4[i], b4[i]), 0.0))
    tbd = each(lambda i: jnp.where(eye, 1.0, lbd[i]))
    lb = each(lambda i: lbd[i].astype(BF16))
    p = each(lambda i: _dot(lb[i], lb[i]))
    n_lvl = int(np.log2(c)) - 1
    for lvl in range(n_lvl):
        pb = each(lambda i: p[i].astype(BF16))
        if lvl + 1 < n_lvl:
            both = each(lambda i: _dot(pb[i], jnp.concatenate([tbd[i].astype(BF16), pb[i]], axis=1)))
            tbd = each(lambda i: tbd[i] + both[i][:, 0:q])
            p = each(lambda i: both[i][:, q:2 * q])
        else:
            tbd = each(lambda i: tbd[i] + _dot(pb[i], tbd[i].astype(BF16)))
    t_ls = each(lambda i: collapse(tbd[i]).astype(BF16))

    w = nh * c
    ar = each(lambda i: _dot_nt(jnp.concatenate([at[i], rt[i]], axis=0).astype(BF16),
                                jnp.concatenate([k4[i], b4[i]], axis=0)))
    a_kk = each(lambda i: jnp.concatenate([jnp.where(strict_ls, ar[i][0:c, 0:w], 0.0),
                                           jnp.where(incl_ls, ar[i][c:2 * c, 0:w], 0.0)], axis=0).astype(BF16))
    a_rb = each(lambda i: jnp.where(incl_ls, ar[i][c:2 * c, w:2 * w], 0.0).astype(BF16))

    gy = each(lambda i: _dot(a_kk[i], v4[i]))
    g1 = each(lambda i: gy[i][0:c])
    ua = each(lambda i: _dot(t_ls[i], jnp.concatenate([expand(g1[i]), x4[i]], axis=1)))
    u0 = each(lambda i: ua[i][:, 0:q])
    ahat = each(lambda i: ua[i][:, q:2 * q])
    ry = each(lambda i: _dot(a_rb[i], jnp.concatenate([expand(ahat[i]), expand(u0[i])], axis=1)))
    rhat = each(lambda i: rt[i] + ry[i][:, 0:q])
    y1 = each(lambda i: gy[i][c:2 * c] + ry[i][:, q:2 * q])

    gm = each(lambda i: _dot_tn(bbar[i], ahat[i].astype(BF16)))
    nm = each(lambda i: _dot_tn(jnp.concatenate([bbar[i], kbar[i]], axis=0),
                                jnp.concatenate([u0[i].astype(BF16), v_ref[0, rows[i], :]], axis=0)))
    for i in range(nch):
        gmi = jnp.where(same_head, gm[i], 0.0) + jnp.where(eye, jnp.exp(cum_c[i]), 0.0)
        rhat_ref[0, rows[i], :] = rhat[i].astype(BF16)
        y1_ref[0, rows[i], :] = y1[i].astype(BF16)
        g_ref[0, rows[i], :] = collapse(gmi).astype(BF16)
        n_ref[0, rows[i], :] = collapse(jnp.where(same_head, nm[i], 0.0)).astype(BF16)


def _rw_prep(r, lw, k, v, a, b):
    bsz, s, d = r.shape
    rows = min(RW_C * RW_NCH, s)
    ltri = jnp.asarray(np.tril(np.ones((RW_C, RW_C), np.float32)), BF16)
    blk = pl.BlockSpec((1, rows, RW_Q), lambda bi, qi, j: (bi, j, qi))
    return pl.pallas_call(
        _rw_prep_body,
        out_shape=(jax.ShapeDtypeStruct((bsz, s, d), BF16),) * 4,
        grid=(bsz, d // RW_Q, s // rows),
        in_specs=[_const_spec((RW_C, RW_C))] + [blk] * 6,
        out_specs=(blk,) * 4,
        compiler_params=_cparams(("parallel", "parallel", "parallel")),
        name="rwkv_prep",
    )(ltri, r, lw, k, v, a, b)


def _rw_scan_body(rhat_ref, y1_ref, g_ref, n_ref, y_ref, h_sc):
    c = RW_C
    q = RW_Q
    nb, nq = h_sc.shape[0], h_sc.shape[1]

    @pl.when(pl.program_id(1) == 0)
    def _():
        h_sc[...] = jnp.zeros(h_sc.shape, F32)

    ri = lax.broadcasted_iota(jnp.int32, (q, q), 0)
    ci = lax.broadcasted_iota(jnp.int32, (q, q), 1)
    same_head = lax.shift_right_logical(ri, 6) == lax.shift_right_logical(ci, 6)

    def step(ch, carry):
        rows = pl.ds(pl.multiple_of(ch * c, c), c)
        for bi in range(nb):
            for qi in range(nq):
                lanes = slice(qi * q, (qi + 1) * q)
                hb = h_sc[bi, qi].astype(BF16)
                y = _dot(rhat_ref[bi, rows, lanes], hb) + y1_ref[bi, rows, lanes]
                y_ref[bi, rows, lanes] = y.astype(BF16)
                g_ls = g_ref[bi, rows, lanes]
                n_ls = n_ref[bi, rows, lanes]
                gbd = jnp.where(same_head, jnp.concatenate([g_ls] * (q // c), axis=0), 0.0)
                nbd = jnp.where(same_head, jnp.concatenate([n_ls] * (q // c), axis=0), 0.0)
                h_sc[bi, qi] = _dot(gbd.astype(BF16), hb) + nbd
        return carry

    lax.fori_loop(0, rhat_ref.shape[1] // c, step, 0)


def _rw_scan(rhat, y1, g, n):
    bsz, s, d = rhat.shape
    rows = min(512, s)
    nb = 2 if bsz % 2 == 0 else 1
    blk = pl.BlockSpec((nb, rows, d), lambda bi, j: (bi, j, 0))
    return pl.pallas_call(
        _rw_scan_body,
        out_shape=jax.ShapeDtypeStruct((bsz, s, d), BF16),
        grid=(bsz // nb, s // rows),
        in_specs=[blk] * 4,
        out_specs=blk,
        scratch_shapes=[pltpu.VMEM((nb, d // RW_Q, RW_Q, RW_Q), F32)],
        compiler_params=_cparams(("parallel", "arbitrary")),
        name="rwkv_scan",
    )(rhat, y1, g, n)


def _rw_post_body(y_ref, r_ref, k_ref, v_ref, gg_ref, h_ref, vec_ref, w_ref, gpost_ref, sel_ref, selt_ref, o_ref):
    gn_w, gn_b, r_k = (vec_ref[i:i + 1, :] for i in range(3))
    f32 = lambda ref: ref[...].astype(F32)
    y = f32(y_ref)
    inv_n = 1.0 / HD
    mean = _head_expand(_head_sum(y, sel_ref) * inv_n, selt_ref)
    yc = y - mean
    var = _head_sum(yc * yc, sel_ref) * inv_n
    yn = yc * _head_expand(lax.rsqrt(var + GN_EPS), selt_ref) * gn_w + gn_b
    bonus = _head_expand(_head_sum(f32(r_ref) * f32(k_ref) * r_k, sel_ref), selt_ref) * f32(v_ref)
    z = ((yn + bonus) * f32(gg_ref)).astype(BF16)
    o_ref[...] = h_ref[...] + _rms(_dot(z, w_ref[...]), gpost_ref[...])


def _rw_post(y, r, k, v, gg, h, gn_w, gn_b, r_k, w_out, g_post):
    t, d = h.shape
    tm = min(PROJ_TM, t)
    vecs = _pad_rows(jnp.stack([gn_w, gn_b, r_k.reshape(d)]), 8)
    sel, selt = _head_selectors()
    tok = pl.BlockSpec((tm, d), lambda i: (i, 0))
    return pl.pallas_call(
        _rw_post_body,
        out_shape=jax.ShapeDtypeStruct((t, d), F32),
        grid=(t // tm,),
        in_specs=[tok] * 6 + [_const_spec((8, d)), _const_spec((d, d)), _const_spec((1, d)),
                              _const_spec((d, LANES)), _const_spec((LANES, d))],
        out_specs=tok,
        compiler_params=_cparams(("parallel",)),
        name="rwkv_post",
    )(y, r, k, v, gg, h, vecs, w_out.astype(BF16), g_post.reshape(1, d), sel, selt)


def _rwkv_mixer(h, g_pre, mu, w_in, w0, w_w2, a0, w_a2, w_g2, k_k, k_a, r_k, gn_w, gn_b, w_out, g_post):
    b, s, d = h.shape
    r, lw, k, v, a, bb, gg = _rw_proj(h, g_pre, mu, w_in, w0, w_w2, a0, w_a2, w_g2, k_k, k_a)
    rhat, y1, g, n = _rw_prep(r, lw, k, v, a, bb)
    y = _rw_scan(rhat, y1, g, n)
    f2 = lambda x: x.reshape(b * s, d)
    return _rw_post(f2(y), f2(r), f2(k), f2(v), f2(gg), f2(h), gn_w, gn_b, r_k, w_out, g_post).reshape(b, s, d)


def kernel(x, ffn1_norm_pre, ffn1_w_gu, ffn1_w_down, ffn1_norm_post, mix_norm_pre, nsa_w_in, nsa_pe_k,
           nsa_w_ck1, nsa_w_ck2, nsa_pe_v, nsa_w_cv1, nsa_w_cv2, nsa_w_out, rwkv_mu, rwkv_w_in, rwkv_w0,
           rwkv_w_w2, rwkv_a0, rwkv_w_a2, rwkv_w_g2, rwkv_k_k, rwkv_k_a, rwkv_r_k, rwkv_gn_w, rwkv_gn_b,
           rwkv_w_out, mix_norm_post, ffn2_norm_pre, ffn2_w_gu, ffn2_w_down, ffn2_norm_post):
    b, s, d = x.shape
    flat = lambda t: t.reshape(b * s, d)
    cube = lambda t: t.reshape(b, s, d)
    h = x
    depth = ffn1_norm_pre.shape[0]
    for i in range(depth):
        h = cube(_ffn(flat(h), ffn1_norm_pre[i], ffn1_w_gu[i], ffn1_w_down[i], ffn1_norm_post[i]))
        j = i // 2
        if i % 2 == 0:
            h = _nsa_mixer(h, mix_norm_pre[i], nsa_w_in[j], nsa_pe_k[j], nsa_w_ck1[j], nsa_w_ck2[j],
                           nsa_pe_v[j], nsa_w_cv1[j], nsa_w_cv2[j], nsa_w_out[j], mix_norm_post[i])
        else:
            h = _rwkv_mixer(h, mix_norm_pre[i], rwkv_mu[j], rwkv_w_in[j], rwkv_w0[j], rwkv_w_w2[j],
                            rwkv_a0[j], rwkv_w_a2[j], rwkv_w_g2[j], rwkv_k_k[j], rwkv_k_a[j], rwkv_r_k[j],
                            rwkv_gn_w[j], rwkv_gn_b[j], rwkv_w_out[j], mix_norm_post[i])
        h = cube(_ffn(flat(h), ffn2_norm_pre[i], ffn2_w_gu[i], ffn2_w_down[i], ffn2_norm_post[i]))
    return h
```

```python
import functools

import numpy as np
import jax
import jax.numpy as jnp
from jax import lax
from jax.experimental import pallas as pl
from jax.experimental.pallas import tpu as pltpu

F32 = jnp.float32
BF16 = jnp.bfloat16

D_MODEL = 1024
D_FF = 2816
HALF_STEP = 0.5
RMS_EPS = 1e-6
MASK_VALUE = -1e30

HD = 64
N_HEADS = 16
N_KV = 4
N_GRP = 4
KV_W = N_KV * HD
CMP_BLOCK = 32
CMP_STRIDE = 16
SEL_BLOCK = 64
N_SELECT = 16
WINDOW = 512
FORCED_SCORE = 1e4

DECAY_LORA = 64
AAA_LORA = 64
GATE_LORA = 160
GN_EPS = 64e-5

LANES = 128
SUBLANES = 8
VMEM_LIMIT_BYTES = 56 * 1024 * 1024

FFN_TM = 512
FFN_CUTS = (0, 1536, 2816)
PROJ_TM = 512
ATT_T = 256
GATE_ROWS = 16
SUM_ROWS = 16
LOG2E = 1.4426950408889634
RW_C = 64
RW_Q = 128
RW_NCH = 16


def _cparams(sem):
    return pltpu.CompilerParams(dimension_semantics=sem, vmem_limit_bytes=VMEM_LIMIT_BYTES)


def _rms(x, g):
    ms = jnp.mean(x * x, axis=-1, keepdims=True)
    return x * lax.rsqrt(ms + RMS_EPS) * g


def _const_spec(shape):
    nd = len(shape)
    return pl.BlockSpec(shape, lambda *_: (0,) * nd, pipeline_mode=pl.Buffered(1))


def _dot(a, b):
    return jnp.dot(a, b, preferred_element_type=F32)


def _dot_nt(a, b):
    return lax.dot_general(a, b, (((1,), (1,)), ((), ())), preferred_element_type=F32)


def _dot_tn(a, b):
    return lax.dot_general(a, b, (((0,), (0,)), ((), ())), preferred_element_type=F32)


def _split3(x):
    hi = x.astype(BF16)
    r1 = x - hi.astype(F32)
    mid = r1.astype(BF16)
    lo = (r1 - mid.astype(F32)).astype(BF16)
    return hi, mid, lo


def _ffn_body(x_ref, gpre_ref, wgu_ref, wd_ref, gpost_ref, o_ref):
    x = x_ref[...]
    xn = _rms(x, gpre_ref[...]).astype(BF16)
    acc = None
    for lo, hi in zip(FFN_CUTS[:-1], FFN_CUTS[1:]):
        gate = _dot(xn, wgu_ref[:, lo:hi])
        up = _dot(xn, wgu_ref[:, D_FF + lo:D_FF + hi])
        act = (gate * jax.nn.sigmoid(gate) * up).astype(BF16)
        part = _dot(act, wd_ref[lo:hi, :])
        acc = part if acc is None else acc + part
    o_ref[...] = x + HALF_STEP * _rms(acc, gpost_ref[...])


def _ffn(h2, g_pre, w_gu, w_down, g_post):
    t = h2.shape[0]
    tm = min(FFN_TM, t)
    return pl.pallas_call(
        _ffn_body,
        out_shape=jax.ShapeDtypeStruct((t, D_MODEL), F32),
        grid=(t // tm,),
        in_specs=[
            pl.BlockSpec((tm, D_MODEL), lambda i: (i, 0)),
            _const_spec((1, D_MODEL)),
            _const_spec((D_MODEL, 2 * D_FF)),
            _const_spec((D_FF, D_MODEL)),
            _const_spec((1, D_MODEL)),
        ],
        out_specs=pl.BlockSpec((tm, D_MODEL), lambda i: (i, 0)),
        compiler_params=_cparams(("parallel",)),
        name="ffn",
    )(h2, g_pre.reshape(1, D_MODEL), w_gu.astype(BF16), w_down.astype(BF16), g_post.reshape(1, D_MODEL))


def _nsa_proj_body(h_ref, g_ref, wqt_ref, wc_ref, wk_ref, wvt_ref, wgt_ref,
                   qt_ref, kc_ref, vc_ref, k_ref, vt_ref, gt_ref):
    tm = h_ref.shape[1]
    u = _rms(h_ref[0], g_ref[...]).astype(BF16)
    qt_ref[0] = (_dot_nt(wqt_ref[...], u) * (HD ** -0.5 * LOG2E)).astype(BF16)
    c = _dot(u, wc_ref[...])
    for j in range(KV_W // LANES):
        kc_ref[0, j] = c[:, j * LANES:(j + 1) * LANES]
        vc_ref[0, j] = c[:, KV_W + j * LANES:KV_W + (j + 1) * LANES]
    kk = _dot(u, wk_ref[...])
    t0 = pl.program_id(1) * tm
    lane = lax.broadcasted_iota(jnp.int32, (tm, LANES), 1)
    blk = lax.shift_right_logical(t0 + lax.broadcasted_iota(jnp.int32, (tm, LANES), 0), 6)
    onehot = jnp.where(lane - HD == blk, 1.0, 0.0)
    for g in range(N_KV):
        k_ref[0, g, 0] = (kk[:, g * LANES:(g + 1) * LANES] + onehot).astype(BF16)
        k_ref[0, g, 1] = kk[:, (N_KV + g) * LANES:(N_KV + g + 1) * LANES].astype(BF16)
    vt = _dot_nt(wvt_ref[...], u)
    tk = vt_ref.shape[5]
    for j in range(tm // tk):
        cols = slice(j * tk, (j + 1) * tk)
        vt_ref[0, :, 0, j] = vt[:KV_W, cols].reshape(N_KV, HD, tk).astype(BF16)
        vt_ref[0, :, 1, j] = vt[KV_W:, cols].reshape(N_KV, HD, tk).astype(BF16)
    gt_ref[0] = jax.nn.sigmoid(_dot_nt(wgt_ref[...], u)).reshape(N_KV, GATE_ROWS, tm)


def _nsa_proj(h, g_pre, w_in):
    b, s, _ = h.shape
    tm = min(PROJ_TM, s)
    tk = min(ATT_T, s)
    nq = N_HEADS * HD
    assert s // SEL_BLOCK <= HD
    col = lambda i: w_in[:, nq + i * KV_W:nq + (i + 1) * KV_W]
    w_qt = w_in[:, :nq].T.astype(BF16)
    w_c = jnp.concatenate([col(0), col(1)], axis=1).astype(BF16)
    pad_heads = lambda w: jnp.pad(w.reshape(D_MODEL, N_KV, HD), ((0, 0), (0, 0), (0, LANES - HD))).reshape(D_MODEL, N_KV * LANES)
    w_k = jnp.concatenate([pad_heads(col(2)), pad_heads(col(4))], axis=1).astype(BF16)
    w_vt = jnp.concatenate([col(3), col(5)], axis=1).T.astype(BF16)
    w_gl = w_in[:, nq + 6 * KV_W:].reshape(D_MODEL, N_KV, N_GRP * 3)
    w_gt = jnp.pad(w_gl, ((0, 0), (0, 0), (0, GATE_ROWS - N_GRP * 3))).reshape(D_MODEL, N_KV * GATE_ROWS).T.astype(BF16)
    tok = lambda w: pl.BlockSpec((1, tm, w), lambda bi, i: (bi, i, 0))
    cmp_spec = pl.BlockSpec((1, KV_W // LANES, tm, LANES), lambda bi, i: (bi, 0, i, 0))
    key = jax.ShapeDtypeStruct((b, N_KV, 2, s, LANES), BF16)
    key_spec = pl.BlockSpec((1, N_KV, 2, tm, LANES), lambda bi, i: (bi, 0, 0, i, 0))
    valt = jax.ShapeDtypeStruct((b, N_KV, 2, s // tk, HD, tk), BF16)
    valt_spec = pl.BlockSpec((1, N_KV, 2, tm // tk, HD, tk), lambda bi, i: (bi, 0, 0, i, 0, 0))
    return pl.pallas_call(
        _nsa_proj_body,
        out_shape=(
            jax.ShapeDtypeStruct((b, nq, s), BF16),
            jax.ShapeDtypeStruct((b, KV_W // LANES, s, LANES), F32),
            jax.ShapeDtypeStruct((b, KV_W // LANES, s, LANES), F32),
            key, valt,
            jax.ShapeDtypeStruct((b, N_KV, GATE_ROWS, s), F32),
        ),
        grid=(b, s // tm),
        in_specs=[
            tok(D_MODEL),
            _const_spec((1, D_MODEL)),
            _const_spec((nq, D_MODEL)),
            _const_spec((D_MODEL, 2 * KV_W)),
            _const_spec((D_MODEL, 2 * N_KV * LANES)),
            _const_spec((2 * KV_W, D_MODEL)),
            _const_spec((N_KV * GATE_ROWS, D_MODEL)),
        ],
        out_specs=(
            pl.BlockSpec((1, nq, tm), lambda bi, i: (bi, 0, i)),
            cmp_spec, cmp_spec, key_spec, valt_spec,
            pl.BlockSpec((1, N_KV, GATE_ROWS, tm), lambda bi, i: (bi, 0, 0, i)),
        ),
        compiler_params=_cparams(("parallel", "parallel")),
        name="nsa_proj",
    )(h, g_pre.reshape(1, D_MODEL), w_qt, w_c, w_k, w_vt, w_gt)


def _compress_body(x_ref, pe_ref, w1_ref, w2_ref, o_ref, *, transposed):
    half = CMP_BLOCK // 2
    n = x_ref.shape[2] // half
    ha = hb = None
    for l in range(half):
        xl = jnp.concatenate([x_ref[0, j, pl.ds(l, n, stride=half), :] for j in range(x_ref.shape[1])], axis=-1)
        a = _dot((xl + pe_ref[l:l + 1, :]).astype(BF16), w1_ref[l])
        b = _dot((xl + pe_ref[half + l:half + l + 1, :]).astype(BF16), w1_ref[half + l])
        ha = a if ha is None else ha + a
        hb = b if hb is None else hb + b
    hid = ha + pltpu.roll(hb, n - 1, 0)
    hid = (hid * jax.nn.sigmoid(hid)).astype(BF16)
    if transposed:
        o_ref[0] = _dot_nt(w2_ref[...], hid).reshape(N_KV, HD, n).astype(BF16)
    else:
        out = _dot(hid, w2_ref[...]).astype(BF16)
        for g in range(N_KV):
            o_ref[0, g] = out[:, g * HD:(g + 1) * HD]


def _compress(t, pe, w1, w2, transposed):
    b, _, s, _ = t.shape
    rows = s // (CMP_BLOCK // 2)
    eye = jnp.eye(N_KV, dtype=F32)
    wbd = jnp.einsum("ldc,gh->lgdhc", w1, eye).reshape(CMP_BLOCK, KV_W, KV_W).astype(BF16)
    pe_t = jnp.broadcast_to(pe[:, None, :], (CMP_BLOCK, N_KV, HD)).reshape(CMP_BLOCK, KV_W)
    w2bd = jnp.einsum("cd,gh->gchd", w2, eye).reshape(KV_W, KV_W)
    if transposed:
        w2bd = w2bd.T
        out_shape, out_block = (b, N_KV, HD, rows), (1, N_KV, HD, rows)
    else:
        out_shape, out_block = (b, N_KV, rows, HD), (1, N_KV, rows, HD)
    return pl.pallas_call(
        functools.partial(_compress_body, transposed=transposed),
        out_shape=jax.ShapeDtypeStruct(out_shape, BF16),
        grid=(b,),
        in_specs=[
            pl.BlockSpec((1, KV_W // LANES, s, LANES), lambda bi: (bi, 0, 0, 0)),
            _const_spec((CMP_BLOCK, KV_W)),
            _const_spec((CMP_BLOCK, KV_W, KV_W)),
            _const_spec((KV_W, KV_W)),
        ],
        out_specs=pl.BlockSpec(out_block, lambda bi: (bi, 0, 0, 0)),
        compiler_params=_cparams(("parallel",)),
        name="nsa_compress",
    )(t, pe_t, wbd, w2bd.astype(BF16))


def _alibi_slope(h):
    return float(2.0 ** (-8.0 * (h + 1) / N_HEADS))


def _cmp_probs(kc, qh, slope2, end_f, valid):
    s = _dot(kc, qh) + jnp.where(valid, slope2 * end_f, MASK_VALUE)
    return jnp.exp2(s - jnp.max(s, axis=0, keepdims=True)).astype(BF16)


def _cmp_geometry(n_cmp_rows, tq, t0):
    n_io = lax.broadcasted_iota(jnp.int32, (n_cmp_rows, tq), 0)
    t_io = t0 + lax.broadcasted_iota(jnp.int32, (n_cmp_rows, tq), 1)
    end = n_io * CMP_STRIDE + (CMP_BLOCK - 1)
    return end.astype(F32), (t_io >= end) & (n_io < n_cmp_rows - 1)


def _select_body(qt_ref, kc_ref, vct_ref, c2s_ref, chunk_ref, mneg_ref, active_ref, oc_ref, *, n_sel, n_cmp_rows):
    tq = qt_ref.shape[2]
    t0 = pl.program_id(1) * tq
    has_block = t0 + lax.broadcasted_iota(jnp.int32, (1, tq), 1) >= CMP_BLOCK - 1
    v_lo = n_sel + SUM_ROWS
    s_io = lax.broadcasted_iota(jnp.int32, (n_sel, tq), 0)
    tt = t0 + lax.broadcasted_iota(jnp.int32, (n_sel, tq), 1)
    cur = lax.shift_right_logical(tt, 6)
    forced = (s_io == 0) | (s_io == cur) | (s_io == cur - 1)
    visible = s_io * SEL_BLOCK <= tt

    end_f, valid = _cmp_geometry(n_cmp_rows, tq, t0)

    def compressed(g):
        kc = kc_ref[0, g]
        heads = range(g * N_GRP, (g + 1) * N_GRP)
        probs = [_cmp_probs(kc, qt_ref[0, hh * HD:(hh + 1) * HD, :], _alibi_slope(hh) * LOG2E, end_f, valid)
                 for hh in heads]
        lhs = jnp.concatenate([c2s_ref[...], vct_ref[0, g]], axis=0)
        mass = [_dot(lhs, p) for p in probs]
        inv_l = [1.0 / m[n_sel:n_sel + 1] for m in mass]
        for hh, m, il in zip(heads, mass, inv_l):
            oc_ref[0, hh * HD:(hh + 1) * HD, :] = jnp.where(has_block, m[v_lo:v_lo + HD] * il, 0.0).astype(BF16)
        return sum(m[0:n_sel] * il for m, il in zip(mass, inv_l))

    def select(g, imp):
        x = jnp.where(forced, FORCED_SCORE, jnp.where(visible, imp, -1.0))
        cnt = jnp.zeros((n_sel, tq), F32)
        for j in range(n_sel):
            xj = x[j:j + 1, :]
            ge = jnp.where(xj >= x, 1.0, 0.0)
            gt = jnp.where(xj > x, 1.0, 0.0)
            cnt = cnt + jnp.where(s_io > j, ge, gt)
        chosen = cnt < float(N_SELECT)
        mneg_ref[0, g, 0:n_sel, :] = jnp.where(chosen, 0.0, MASK_VALUE).astype(BF16)
        if n_sel < HD:
            mneg_ref[0, g, n_sel:HD, :] = jnp.zeros((HD - n_sel, tq), BF16)
        per_query = _dot(chunk_ref[...], jnp.where(chosen, 1.0, 0.0).astype(BF16))
        active_ref[0, g, 0] = _dot_nt(jnp.ones((SUBLANES, tq), BF16), per_query.astype(BF16))

    imps = [compressed(0)]
    for g in range(1, N_KV):
        imps.append(compressed(g))
        select(g - 1, imps[g - 1])
    select(N_KV - 1, imps[N_KV - 1])


def _select(qt, k_cmp, v_cmpt):
    b, nq, s = qt.shape
    tq = min(ATT_T, s)
    n_sel = s // SEL_BLOCK
    rows = k_cmp.shape[2]
    n_cmp = (s - CMP_BLOCK) // CMP_STRIDE + 1
    assert rows == n_cmp + 1
    cs = np.arange(rows) * CMP_STRIDE
    bs = np.arange(n_sel) * SEL_BLOCK
    overlap = np.clip(np.minimum(cs[:, None] + CMP_BLOCK, bs[None, :] + SEL_BLOCK)
                      - np.maximum(cs[:, None], bs[None, :]), 0, None)
    c2s_t = jnp.asarray(np.concatenate([(overlap / CMP_BLOCK).T, np.ones((SUM_ROWS, rows))], axis=0), BF16)
    n_chunks = s // tq
    chunk_of_block = np.arange(n_sel) * SEL_BLOCK // tq
    chunk_mat = jnp.asarray(chunk_of_block[None, :] == np.arange(n_chunks)[:, None], BF16)
    return pl.pallas_call(
        functools.partial(_select_body, n_sel=n_sel, n_cmp_rows=rows),
        out_shape=(jax.ShapeDtypeStruct((b, N_KV, HD, s), BF16),
                   jax.ShapeDtypeStruct((b, N_KV, s // tq, SUBLANES, n_chunks), F32),
                   jax.ShapeDtypeStruct((b, nq, s), BF16)),
        grid=(b, s // tq),
        in_specs=[
            pl.BlockSpec((1, nq, tq), lambda bi, i: (bi, 0, i)),
            pl.BlockSpec((1, N_KV, rows, HD), lambda bi, i: (bi, 0, 0, 0)),
            pl.BlockSpec((1, N_KV, HD, rows), lambda bi, i: (bi, 0, 0, 0)),
            _const_spec((n_sel + SUM_ROWS, rows)),
            _const_spec((n_chunks, n_sel)),
        ],
        out_specs=(pl.BlockSpec((1, N_KV, HD, tq), lambda bi, i: (bi, 0, 0, i)),
                   pl.BlockSpec((1, N_KV, 1, SUBLANES, n_chunks), lambda bi, i: (bi, 0, i, 0, 0)),
                   pl.BlockSpec((1, nq, tq), lambda bi, i: (bi, 0, i))),
        compiler_params=_cparams(("parallel", "parallel")),
        name="nsa_select",
    )(qt, k_cmp, v_cmpt, c2s_t, chunk_mat)


def _attn_body(slope_ref, active_ref, qt_ref, k_ref, vt_ref, oc_ref, mneg_ref,
               gt_ref, o_ref, s_sc, bias_sc, m_sc, acc_sc, todo_ref):
    tq = qt_ref.shape[2]
    tk = tq
    g = pl.program_id(1)
    qi = pl.program_id(2)
    slopes = [slope_ref[g * N_GRP + r] for r in range(N_GRP)]
    q_rows = [qt_ref[0, r * HD:(r + 1) * HD, :] for r in range(N_GRP)]
    qk = [jnp.concatenate([q_rows[r], mneg_ref[0, 0]], axis=0) for r in range(N_GRP)]

    @pl.when(qi == 0)
    def _():
        key_io = lax.broadcasted_iota(jnp.int32, (tk, tq), 0)
        rel = key_io - lax.broadcasted_iota(jnp.int32, (tk, tq), 1)
        key_f = key_io.astype(F32)
        causal = jnp.where(rel <= 0, 0.0, MASK_VALUE)
        oldest = jnp.where(rel > 0, 0.0, MASK_VALUE)
        for r in range(N_GRP):
            base = slopes[r] * key_f
            bias_sc[0, r] = base
            bias_sc[1, r] = base + causal
            bias_sc[2, r] = base + oldest

    ones_rows = jnp.ones((SUM_ROWS, tk), BF16)

    def offsets(c):
        cf = ((c - qi) * tk).astype(F32)
        return [slopes[r] * cf for r in range(N_GRP)]

    tile = (pl.program_id(0) * N_KV + g) * pl.num_programs(2) + qi

    def note_active(c, n):
        todo_ref[n] = c
        return n + active_ref[tile * pl.num_programs(2) + c]

    todo_ref[0] = 0
    n_sel = lax.fori_loop(0, qi, note_active, jnp.int32(0)) + 1
    c_lo_w = jnp.maximum(qi - WINDOW // tk, 0)
    n_items = n_sel + (qi - c_lo_w) + 1

    def item(pos):
        window = (pos >= n_sel).astype(jnp.int32)
        listed = todo_ref[jnp.clip(pos, 0, jnp.maximum(n_sel - 2, 0))]
        c = jnp.where(pos < n_sel - 1, listed, jnp.where(pos == n_sel - 1, qi, c_lo_w + pos - n_sel))
        return window, c

    def score_stage(pos, slot):
        window, c = item(pos)
        kblk = k_ref[0, 0, window, pl.ds(pl.multiple_of(c * tk, tk), tk), :]
        variant = jnp.where(c == qi, 1, jnp.where((window == 1) & (c == qi - WINDOW // tk), 2, 0))
        offs = offsets(c)
        mcs = []
        for r in range(N_GRP):
            s = _dot(kblk, qk[r]) + bias_sc[variant, r]
            s_sc[slot, r] = s
            mcs.append(jnp.max(s, axis=0, keepdims=True) + offs[r])
        return tuple(mcs)

    def value_stage(pos, mcs, slot):
        window, c = item(pos)
        vt = jnp.concatenate([vt_ref[0, 0, window, c], ones_rows], axis=0)
        offs = offsets(c)
        for r in range(N_GRP):
            m = m_sc[window, r, 0:1, :]
            m_new = jnp.maximum(m, mcs[r])
            p = jnp.exp2(s_sc[slot, r] - (m_new - offs[r]))
            acc_sc[window, r] = jnp.exp2(m - m_new) * acc_sc[window, r] + _dot(vt, p.astype(BF16))
            m_sc[window, r, 0:1, :] = m_new

    m_sc[...] = jnp.full(m_sc.shape, MASK_VALUE, F32)
    acc_sc[...] = jnp.zeros(acc_sc.shape, F32)

    def trips(n, first, count, mcs):
        def trip(k, mcs):
            pos = first + n * k
            for i in range(n):
                mcs_next = score_stage(pos + i + 1, (i + 1) % 2)
                value_stage(pos + i, mcs, i % 2)
                mcs = mcs_next
            return mcs
        return lax.fori_loop(0, count, trip, mcs)

    steps = n_items - 1
    mcs = trips(4, 0, steps // 4, score_stage(jnp.int32(0), 0))
    done = (steps // 4) * 4
    mcs = trips(2, done, (steps - done) // 2, mcs)
    done = done + ((steps - done) // 2) * 2
    mcs = trips(1, done, steps - done, mcs)
    value_stage(steps, mcs, steps - done)

    gate = gt_ref[0, 0]
    normed = lambda acc: acc[0:HD] * (1.0 / acc[HD:HD + 1])
    for r in range(N_GRP):
        o_c = oc_ref[0, r * HD:(r + 1) * HD, :].astype(F32)
        o_s = normed(acc_sc[0, r])
        o_w = normed(acc_sc[1, r])
        out = (gate[3 * r:3 * r + 1, :] * o_c + gate[3 * r + 1:3 * r + 2, :] * o_s
               + gate[3 * r + 2:3 * r + 3, :] * o_w)
        o_ref[0, r * HD:(r + 1) * HD, :] = out.astype(BF16)


def _attention(qt, keys, vals_t, oc_t, mneg, active, gates_t):
    b, nq, s = qt.shape
    tq = vals_t.shape[5]
    assert WINDOW % tq == 0 and active.shape == (b, N_KV, s // tq, SUBLANES, s // tq)
    active = (active[:, :, :, 0, :] > 0.0).astype(jnp.int32).reshape(-1)
    gw = N_GRP * HD
    slopes = jnp.asarray([_alibi_slope(h) * LOG2E for h in range(N_HEADS)], F32)
    key_spec = pl.BlockSpec((1, 1, 2, s, LANES), lambda bi, g, i: (bi, g, 0, 0, 0))
    val_spec = pl.BlockSpec((1, 1, 2, s // tq, HD, tq), lambda bi, g, i: (bi, g, 0, 0, 0, 0))
    return pl.pallas_call(
        _attn_body,
        out_shape=jax.ShapeDtypeStruct((b, nq, s), BF16),
        grid=(b, N_KV, s // tq),
        in_specs=[
            pl.BlockSpec(memory_space=pltpu.SMEM),
            pl.BlockSpec(memory_space=pltpu.SMEM),
            pl.BlockSpec((1, gw, tq), lambda bi, g, i: (bi, g, i)),
            key_spec, val_spec,
            pl.BlockSpec((1, gw, tq), lambda bi, g, i: (bi, g, i)),
            pl.BlockSpec((1, 1, HD, tq), lambda bi, g, i: (bi, g, 0, i)),
            pl.BlockSpec((1, 1, GATE_ROWS, tq), lambda bi, g, i: (bi, g, 0, i)),
        ],
        out_specs=pl.BlockSpec((1, gw, tq), lambda bi, g, i: (bi, g, i)),
        scratch_shapes=[
            pltpu.VMEM((2, N_GRP, tq, tq), F32),
            pltpu.VMEM((3, N_GRP, tq, tq), F32),
            pltpu.VMEM((2, N_GRP, SUBLANES, tq), F32),
            pltpu.VMEM((2, N_GRP, HD + SUM_ROWS, tq), F32),
            pltpu.SMEM((s // tq,), jnp.int32),
        ],
        compiler_params=_cparams(("parallel", "parallel", "arbitrary")),
        name="nsa_attention",
    )(slopes, active, qt, keys, vals_t, oc_t, mneg, gates_t)


def _outproj_t_body(yt_ref, h_ref, w_ref, g_ref, o_ref):
    o_ref[0] = h_ref[0] + _rms(_dot_tn(yt_ref[0], w_ref[...]), g_ref[...])


def _outproj_t(yt, h, w_out, g_post):
    b, s, d = h.shape
    tm = min(PROJ_TM, s)
    return pl.pallas_call(
        _outproj_t_body,
        out_shape=jax.ShapeDtypeStruct((b, s, d), F32),
        grid=(b, s // tm),
        in_specs=[
            pl.BlockSpec((1, d, tm), lambda bi, i: (bi, 0, i)),
            pl.BlockSpec((1, tm, d), lambda bi, i: (bi, i, 0)),
            _const_spec((d, d)),
            _const_spec((1, d)),
        ],
        out_specs=pl.BlockSpec((1, tm, d), lambda bi, i: (bi, i, 0)),
        compiler_params=_cparams(("parallel", "parallel")),
        name="outproj_t",
    )(yt, h, w_out.astype(BF16), g_post.reshape(1, d))


def _nsa_mixer(h, g_pre, w_in, pe_k, w_ck1, w_ck2, pe_v, w_cv1, w_cv2, w_out, g_post):
    qt, kc, vc, keys, vals_t, gates_t = _nsa_proj(h, g_pre, w_in)
    k_cmp = _compress(kc, pe_k, w_ck1, w_ck2, transposed=False)
    v_cmpt = _compress(vc, pe_v, w_cv1, w_cv2, transposed=True)
    mneg, active, oc_t = _select(qt, k_cmp, v_cmpt)
    out_t = _attention(qt, keys, vals_t, oc_t, mneg, active, gates_t)
    return _outproj_t(out_t, h, w_out, g_post)


def _head_sum(x, sel_ref):
    return _dot(x.astype(BF16), sel_ref[...])


def _head_expand(x, selt_ref):
    return _dot(x.astype(BF16), selt_ref[...])


def _softplus(x):
    return jnp.maximum(x, 0.0) + jnp.log(1.0 + jnp.exp(-jnp.abs(x)))


def _rw_proj_body(h_ref, hp_ref, g_ref, mu_ref, vec_ref, wr_ref, wk_ref, wv_ref, wd_ref, wa_ref, wg_ref,
                  w2d_ref, w2a_ref, w2g_ref, sel_ref, selt_ref,
                  r_ref, lw_ref, k_ref, v_ref, a_ref, b_ref, gg_ref):
    tm = h_ref.shape[1]
    u = _rms(h_ref[0], g_ref[...])
    prev = _rms(hp_ref[0], g_ref[...])[7:8, :]
    prev = jnp.where(pl.program_id(1) == 0, 0.0, prev)
    row = lax.broadcasted_iota(jnp.int32, (tm, D_MODEL), 0)
    u_prev = jnp.where(row == 0, prev, pltpu.roll(u, 1, 0))
    xx = u_prev - u

    def mix(i):
        return (u + xx * mu_ref[i:i + 1, :]).astype(BF16)

    w0, a0, k_k, k_a = (vec_ref[i:i + 1, :] for i in range(4))
    k = _dot(mix(1), wk_ref[...])
    d1 = jnp.tanh(_dot(mix(3), wd_ref[...])).astype(BF16)
    kk = k * k_k
    norm = jnp.sqrt(_head_sum(kk * kk, sel_ref))
    a1 = _dot(mix(4), wa_ref[...]).astype(BF16)
    r = _dot(mix(0), wr_ref[...])
    w = -_softplus(-(w0 + _dot(d1, w2d_ref[...]))) - 0.5
    kk = kk * _head_expand(1.0 / jnp.maximum(norm, 1e-12), selt_ref)
    g1 = jax.nn.sigmoid(_dot(mix(5), wg_ref[...])).astype(BF16)
    alpha = jax.nn.sigmoid(a0 + _dot(a1, w2a_ref[...]))
    v = _dot(mix(2), wv_ref[...])
    r_ref[0] = r.astype(BF16)
    lw_ref[0] = -jnp.exp(w)
    k_ref[0] = (k * (1.0 + (alpha - 1.0) * k_a)).astype(BF16)
    v_ref[0] = v.astype(BF16)
    a_ref[0] = (-kk).astype(BF16)
    b_ref[0] = (kk * alpha).astype(BF16)
    gg_ref[0] = _dot(g1, w2g_ref[...]).astype(BF16)


def _head_selectors():
    lane_head = np.arange(D_MODEL) // HD
    sel = (lane_head[:, None] == np.arange(LANES)[None, :]).astype(np.float32)
    return jnp.asarray(sel, BF16), jnp.asarray(sel.T, BF16)


def _pad_cols(w, n):
    return jnp.pad(w, ((0, 0), (0, n - w.shape[1])))


def _pad_rows(w, n):
    return jnp.pad(w, ((0, n - w.shape[0]), (0, 0)))


def _rw_proj(h, g_pre, mu, w_in, w0, w_w2, a0, w_a2, w_g2, k_k, k_a):
    b, s, d = h.shape
    tm = min(PROJ_TM, s)
    offs = np.cumsum((0, d, d, d, DECAY_LORA, AAA_LORA, GATE_LORA))
    cols = [w_in[:, int(offs[i]):int(offs[i + 1])] for i in range(6)]
    ld, lg = LANES, 2 * LANES
    wr, wk, wv = (c.astype(BF16) for c in cols[:3])
    wd = _pad_cols(cols[3], ld).astype(BF16)
    wa = _pad_cols(cols[4], ld).astype(BF16)
    wg = _pad_cols(cols[5], lg).astype(BF16)
    w2d = _pad_rows(w_w2, ld).astype(BF16)
    w2a = _pad_rows(w_a2, ld).astype(BF16)
    w2g = _pad_rows(w_g2, lg).astype(BF16)
    mu8 = _pad_rows(mu, 8)
    vecs = _pad_rows(jnp.stack([w0, a0, k_k, k_a]), 8)
    sel, selt = _head_selectors()
    tok = pl.BlockSpec((1, tm, d), lambda bi, i: (bi, i, 0))
    out = lambda dt: jax.ShapeDtypeStruct((b, s, d), dt)
    return pl.pallas_call(
        _rw_proj_body,
        out_shape=(out(BF16), out(F32), out(BF16), out(BF16), out(BF16), out(BF16), out(BF16)),
        grid=(b, s // tm),
        in_specs=[
            tok,
            pl.BlockSpec((1, 8, d), lambda bi, i: (bi, jnp.maximum(i * (tm // 8) - 1, 0), 0)),
            _const_spec((1, d)), _const_spec((8, d)), _const_spec((8, d)),
            _const_spec((d, d)), _const_spec((d, d)), _const_spec((d, d)),
            _const_spec((d, ld)), _const_spec((d, ld)), _const_spec((d, lg)),
            _const_spec((ld, d)), _const_spec((ld, d)), _const_spec((lg, d)),
            _const_spec((d, LANES)), _const_spec((LANES, d)),
        ],
        out_specs=(tok,) * 7,
        compiler_params=_cparams(("parallel", "parallel")),
        name="rwkv_proj",
    )(h, h, g_pre.reshape(1, d), mu8, vecs, wr, wk, wv, wd, wa, wg, w2d, w2a, w2g, sel, selt)


def _rw_prep_body(ltri_ref, r_ref, lw_ref, k_ref, v_ref, a_ref, b_ref,
                  rhat_ref, y1_ref, g_ref, n_ref):
    c = RW_C
    q = RW_Q
    nh = q // HD
    nch = r_ref.shape[1] // c
    lane_head = lax.shift_right_logical(lax.broadcasted_iota(jnp.int32, (c, q), 1), 6)
    ri = lax.broadcasted_iota(jnp.int32, (q, q), 0)
    ci = lax.broadcasted_iota(jnp.int32, (q, q), 1)
    same_head = lax.shift_right_logical(ri, 6) == lax.shift_right_logical(ci, 6)
    strict_bd = same_head & ((ci & (c - 1)) < (ri & (c - 1)))
    eye = ri == ci
    t_io = lax.broadcasted_iota(jnp.int32, (c, q), 0)
    j_io = lax.broadcasted_iota(jnp.int32, (c, q), 1) & (c - 1)
    strict_ls = j_io < t_io
    incl_ls = j_io <= t_io
    rows = [slice(i * c, (i + 1) * c) for i in range(nch)]

    def each(f):
        return [f(i) for i in range(nch)]

    def expand(x):
        return jnp.concatenate([jnp.where(lane_head == hh, x, 0.0) for hh in range(nh)], axis=0).astype(BF16)

    def collapse(x):
        out = x[0:c]
        for hh in range(1, nh):
            out = out + x[hh * c:(hh + 1) * c]
        return out

    lw = each(lambda i: lw_ref[0, rows[i], :])
    parts = each(lambda i: _dot(ltri_ref[...], jnp.concatenate(_split3(lw[i]), axis=1)))
    cum = each(lambda i: parts[i][:, 0:q] + parts[i][:, q:2 * q] + parts[i][:, 2 * q:3 * q])
    cum_c = each(lambda i: cum[i][c - 1:c, :])
    e_inv = each(lambda i: jnp.exp(-cum[i]))
    e_rem = each(lambda i: jnp.exp(cum_c[i] - cum[i]))
    at = each(lambda i: a_ref[0, rows[i], :] * jnp.exp(cum[i] - lw[i]))
    rt = each(lambda i: r_ref[0, rows[i], :] * jnp.exp(cum[i]))
    x4 = each(lambda i: expand(at[i]))
    b4 = each(lambda i: expand(b_ref[0, rows[i], :] * e_inv[i]))
    k4 = each(lambda i: expand(k_ref[0, rows[i], :] * e_inv[i]))
    v4 = each(lambda i: expand(v_ref[0, rows[i], :]))
    bbar = each(lambda i: (b_ref[0, rows[i], :] * e_rem[i]).astype(BF16))
    kbar = each(lambda i: (k_ref[0, rows[i], :] * e_rem[i]).astype(BF16))

    lbd = each(lambda i: jnp.where(strict_bd, _dot_nt(x4[i], b4[i]), 0.0))
    tbd = each(lambda i: jnp.where(eye, 1.0, lbd[i]))
    lb = each(lambda i: lbd[i].astype(BF16))
    p = each(lambda i: _dot(lb[i], lb[i]))
    n_lvl = int(np.log2(c)) - 1
    for lvl in range(n_lvl):
        pb = each(lambda i: p[i].astype(BF16))
        if lvl + 1 < n_lvl:
            both = each(lambda i: _dot(pb[i], jnp.concatenate([tbd[i].astype(BF16), pb[i]], axis=1)))
            tbd = each(lambda i: tbd[i] + both[i][:, 0:q])
            p = each(lambda i: both[i][:, q:2 * q])
        else:
            tbd = each(lambda i: tbd[i] + _dot(pb[i], tbd[i].astype(BF16)))
    t_ls = each(lambda i: collapse(tbd[i]).astype(BF16))

    w = nh * c
    ar = each(lambda i: _dot_nt(jnp.concatenate([at[i], rt[i]], axis=0).astype(BF16),
                                jnp.concatenate([k4[i], b4[i]], axis=0)))
    a_kk = each(lambda i: jnp.concatenate([jnp.where(strict_ls, ar[i][0:c, 0:w], 0.0),
                                           jnp.where(incl_ls, ar[i][c:2 * c, 0:w], 0.0)], axis=0).astype(BF16))
    a_rb = each(lambda i: jnp.where(incl_ls, ar[i][c:2 * c, w:2 * w], 0.0).astype(BF16))

    gy = each(lambda i: _dot(a_kk[i], v4[i]))
    g1 = each(lambda i: gy[i][0:c])
    ua = each(lambda i: _dot(t_ls[i], jnp.concatenate([expand(g1[i]), x4[i]], axis=1)))
    u0 = each(lambda i: ua[i][:, 0:q])
    ahat = each(lambda i: ua[i][:, q:2 * q])
    ry = each(lambda i: _dot(a_rb[i], jnp.concatenate([expand(ahat[i]), expand(u0[i])], axis=1)))
    rhat = each(lambda i: rt[i] + ry[i][:, 0:q])
    y1 = each(lambda i: gy[i][c:2 * c] + ry[i][:, q:2 * q])

    gm = each(lambda i: _dot_tn(bbar[i], ahat[i].astype(BF16)))
    nm = each(lambda i: _dot_tn(jnp.concatenate([bbar[i], kbar[i]], axis=0),
                                jnp.concatenate([u0[i].astype(BF16), v_ref[0, rows[i], :]], axis=0)))
    for i in range(nch):
        gmi = jnp.where(same_head, gm[i], 0.0) + jnp.where(eye, jnp.exp(cum_c[i]), 0.0)
        rhat_ref[0, rows[i], :] = rhat[i].astype(BF16)
        y1_ref[0, rows[i], :] = y1[i].astype(BF16)
        g_ref[0, rows[i], :] = collapse(gmi).astype(BF16)
        n_ref[0, rows[i], :] = collapse(jnp.where(same_head, nm[i], 0.0)).astype(BF16)


def _rw_prep(r, lw, k, v, a, b):
    bsz, s, d = r.shape
    rows = min(RW_C * RW_NCH, s)
    ltri = jnp.asarray(np.tril(np.ones((RW_C, RW_C), np.float32)), BF16)
    blk = pl.BlockSpec((1, rows, RW_Q), lambda bi, qi, j: (bi, j, qi))
    return pl.pallas_call(
        _rw_prep_body,
        out_shape=(jax.ShapeDtypeStruct((bsz, s, d), BF16),) * 4,
        grid=(bsz, d // RW_Q, s // rows),
        in_specs=[_const_spec((RW_C, RW_C))] + [blk] * 6,
        out_specs=(blk,) * 4,
        compiler_params=_cparams(("parallel", "parallel", "parallel")),
        name="rwkv_prep",
    )(ltri, r, lw, k, v, a, b)


def _rw_scan_body(rhat_ref, y1_ref, g_ref, n_ref, y_ref, h_sc):
    c = RW_C
    q = RW_Q
    nb, nq = h_sc.shape[0], h_sc.shape[1]

    @pl.when(pl.program_id(1) == 0)
    def _():
        h_sc[...] = jnp.zeros(h_sc.shape, F32)

    ri = lax.broadcasted_iota(jnp.int32, (q, q), 0)
    ci = lax.broadcasted_iota(jnp.int32, (q, q), 1)
    same_head = lax.shift_right_logical(ri, 6) == lax.shift_right_logical(ci, 6)

    def step(ch, carry):
        rows = pl.ds(pl.multiple_of(ch * c, c), c)
        for bi in range(nb):
            for qi in range(nq):
                lanes = slice(qi * q, (qi + 1) * q)
                hb = h_sc[bi, qi].astype(BF16)
                y = _dot(rhat_ref[bi, rows, lanes], hb) + y1_ref[bi, rows, lanes]
                y_ref[bi, rows, lanes] = y.astype(BF16)
                g_ls = g_ref[bi, rows, lanes]
                n_ls = n_ref[bi, rows, lanes]
                gbd = jnp.where(same_head, jnp.concatenate([g_ls] * (q // c), axis=0), 0.0)
                nbd = jnp.where(same_head, jnp.concatenate([n_ls] * (q // c), axis=0), 0.0)
                h_sc[bi, qi] = _dot(gbd.astype(BF16), hb) + nbd
        return carry

    lax.fori_loop(0, rhat_ref.shape[1] // c, step, 0)


def _rw_scan(rhat, y1, g, n):
    bsz, s, d = rhat.shape
    rows = min(512, s)
    nb = 2 if bsz % 2 == 0 else 1
    blk = pl.BlockSpec((nb, rows, d), lambda bi, j: (bi, j, 0))
    return pl.pallas_call(
        _rw_scan_body,
        out_shape=jax.ShapeDtypeStruct((bsz, s, d), BF16),
        grid=(bsz // nb, s // rows),
        in_specs=[blk] * 4,
        out_specs=blk,
        scratch_shapes=[pltpu.VMEM((nb, d // RW_Q, RW_Q, RW_Q), F32)],
        compiler_params=_cparams(("parallel", "arbitrary")),
        name="rwkv_scan",
    )(rhat, y1, g, n)


def _rw_post_body(y_ref, r_ref, k_ref, v_ref, gg_ref, h_ref, vec_ref, w_ref, gpost_ref, sel_ref, selt_ref, o_ref):
    gn_w, gn_b, r_k = (vec_ref[i:i + 1, :] for i in range(3))
    f32 = lambda ref: ref[...].astype(F32)
    y = f32(y_ref)
    inv_n = 1.0 / HD
    y_sum = _head_sum(y, sel_ref)
    rk_sum = _head_sum(f32(r_ref) * f32(k_ref) * r_k, sel_ref)
    yc = y - _head_expand(y_sum * inv_n, selt_ref)
    bonus = _head_expand(rk_sum, selt_ref) * f32(v_ref)
    var = _head_sum(yc * yc, sel_ref) * inv_n
    yn = yc * _head_expand(lax.rsqrt(var + GN_EPS), selt_ref) * gn_w + gn_b
    z = ((yn + bonus) * f32(gg_ref)).astype(BF16)
    o_ref[...] = h_ref[...] + _rms(_dot(z, w_ref[...]), gpost_ref[...])


def _rw_post(y, r, k, v, gg, h, gn_w, gn_b, r_k, w_out, g_post):
    t, d = h.shape
    tm = min(PROJ_TM, t)
    vecs = _pad_rows(jnp.stack([gn_w, gn_b, r_k.reshape(d)]), 8)
    sel, selt = _head_selectors()
    tok = pl.BlockSpec((tm, d), lambda i: (i, 0))
    return pl.pallas_call(
        _rw_post_body,
        out_shape=jax.ShapeDtypeStruct((t, d), F32),
        grid=(t // tm,),
        in_specs=[tok] * 6 + [_const_spec((8, d)), _const_spec((d, d)), _const_spec((1, d)),
                              _const_spec((d, LANES)), _const_spec((LANES, d))],
        out_specs=tok,
        compiler_params=_cparams(("parallel",)),
        name="rwkv_post",
    )(y, r, k, v, gg, h, vecs, w_out.astype(BF16), g_post.reshape(1, d), sel, selt)


def _rwkv_mixer(h, g_pre, mu, w_in, w0, w_w2, a0, w_a2, w_g2, k_k, k_a, r_k, gn_w, gn_b, w_out, g_post):
    b, s, d = h.shape
    r, lw, k, v, a, bb, gg = _rw_proj(h, g_pre, mu, w_in, w0, w_w2, a0, w_a2, w_g2, k_k, k_a)
    rhat, y1, g, n = _rw_prep(r, lw, k, v, a, bb)
    y = _rw_scan(rhat, y1, g, n)
    f2 = lambda x: x.reshape(b * s, d)
    return _rw_post(f2(y), f2(r), f2(k), f2(v), f2(gg), f2(h), gn_w, gn_b, r_k, w_out, g_post).reshape(b, s, d)


def kernel(x, ffn1_norm_pre, ffn1_w_gu, ffn1_w_down, ffn1_norm_post, mix_norm_pre, nsa_w_in, nsa_pe_k,
           nsa_w_ck1, nsa_w_ck2, nsa_pe_v, nsa_w_cv1, nsa_w_cv2, nsa_w_out, rwkv_mu, rwkv_w_in, rwkv_w0,
           rwkv_w_w2, rwkv_a0, rwkv_w_a2, rwkv_w_g2, rwkv_k_k, rwkv_k_a, rwkv_r_k, rwkv_gn_w, rwkv_gn_b,
           rwkv_w_out, mix_norm_post, ffn2_norm_pre, ffn2_w_gu, ffn2_w_down, ffn2_norm_post):
    b, s, d = x.shape
    flat = lambda t: t.reshape(b * s, d)
    cube = lambda t: t.reshape(b, s, d)
    h = x
    depth = ffn1_norm_pre.shape[0]
    for i in range(depth):
        h = cube(_ffn(flat(h), ffn1_norm_pre[i], ffn1_w_gu[i], ffn1_w_down[i], ffn1_norm_post[i]))
        j = i // 2
        if i % 2 == 0:
            h = _nsa_mixer(h, mix_norm_pre[i], nsa_w_in[j], nsa_pe_k[j], nsa_w_ck1[j], nsa_w_ck2[j],
                           nsa_pe_v[j], nsa_w_cv1[j], nsa_w_cv2[j], nsa_w_out[j], mix_norm_post[i])
        else:
            h = _rwkv_mixer(h, mix_norm_pre[i], rwkv_mu[j], rwkv_w_in[j], rwkv_w0[j], rwkv_w_w2[j],
                            rwkv_a0[j], rwkv_w_a2[j], rwkv_w_g2[j], rwkv_k_k[j], rwkv_k_a[j], rwkv_r_k[j],
                            rwkv_gn_w[j], rwkv_gn_b[j], rwkv_w_out[j], mix_norm_post[i])
        h = cube(_ffn(flat(h), ffn2_norm_pre[i], ffn2_w_gu[i], ffn2_w_down[i], ffn2_norm_post[i]))
    return h
```

```python
import functools

import numpy as np
import jax
import jax.numpy as jnp
from jax import lax
from jax.experimental import pallas as pl
from jax.experimental.pallas import tpu as pltpu

F32 = jnp.float32
BF16 = jnp.bfloat16

D_MODEL = 1024
D_FF = 2816
HALF_STEP = 0.5
RMS_EPS = 1e-6
MASK_VALUE = -1e30

HD = 64
N_HEADS = 16
N_KV = 4
N_GRP = 4
KV_W = N_KV * HD
CMP_BLOCK = 32
CMP_STRIDE = 16
SEL_BLOCK = 64
N_SELECT = 16
WINDOW = 512
FORCED_SCORE = 1e4

DECAY_LORA = 64
AAA_LORA = 64
GATE_LORA = 160
GN_EPS = 64e-5

LANES = 128
SUBLANES = 8
VMEM_LIMIT_BYTES = 56 * 1024 * 1024

FFN_TM = 512
FFN_CUTS = (0, 1536, 2816)
PROJ_TM = 512
ATT_T = 256
GATE_ROWS = 16
SUM_ROWS = 16
LOG2E = 1.4426950408889634
RW_C = 64
RW_Q = 128
RW_NCH = 16


def _cparams(sem):
    return pltpu.CompilerParams(dimension_semantics=sem, vmem_limit_bytes=VMEM_LIMIT_BYTES)


def _rms(x, g):
    ms = jnp.mean(x * x, axis=-1, keepdims=True)
    return x * lax.rsqrt(ms + RMS_EPS) * g


def _const_spec(shape):
    nd = len(shape)
    return pl.BlockSpec(shape, lambda *_: (0,) * nd, pipeline_mode=pl.Buffered(1))


def _dot(a, b):
    return jnp.dot(a, b, preferred_element_type=F32)


def _dot_nt(a, b):
    return lax.dot_general(a, b, (((1,), (1,)), ((), ())), preferred_element_type=F32)


def _dot_tn(a, b):
    return lax.dot_general(a, b, (((0,), (0,)), ((), ())), preferred_element_type=F32)


def _split3(x):
    hi = x.astype(BF16)
    r1 = x - hi.astype(F32)
    mid = r1.astype(BF16)
    lo = (r1 - mid.astype(F32)).astype(BF16)
    return hi, mid, lo


def _ffn_body(x_ref, gpre_ref, wgu_ref, wd_ref, gpost_ref, o_ref):
    x = x_ref[...]
    xn = _rms(x, gpre_ref[...]).astype(BF16)
    acc = None
    for lo, hi in zip(FFN_CUTS[:-1], FFN_CUTS[1:]):
        gate = _dot(xn, wgu_ref[:, lo:hi])
        up = _dot(xn, wgu_ref[:, D_FF + lo:D_FF + hi])
        act = (gate * jax.nn.sigmoid(gate) * up).astype(BF16)
        part = _dot(act, wd_ref[lo:hi, :])
        acc = part if acc is None else acc + part
    o_ref[...] = x + HALF_STEP * _rms(acc, gpost_ref[...])


def _ffn(h2, g_pre, w_gu, w_down, g_post):
    t = h2.shape[0]
    tm = min(FFN_TM, t)
    return pl.pallas_call(
        _ffn_body,
        out_shape=jax.ShapeDtypeStruct((t, D_MODEL), F32),
        grid=(t // tm,),
        in_specs=[
            pl.BlockSpec((tm, D_MODEL), lambda i: (i, 0)),
            _const_spec((1, D_MODEL)),
            _const_spec((D_MODEL, 2 * D_FF)),
            _const_spec((D_FF, D_MODEL)),
            _const_spec((1, D_MODEL)),
        ],
        out_specs=pl.BlockSpec((tm, D_MODEL), lambda i: (i, 0)),
        compiler_params=_cparams(("parallel",)),
        name="ffn",
    )(h2, g_pre.reshape(1, D_MODEL), w_gu.astype(BF16), w_down.astype(BF16), g_post.reshape(1, D_MODEL))


def _nsa_proj_body(h_ref, g_ref, wqt_ref, wc_ref, wk_ref, wvt_ref, wgt_ref,
                   qt_ref, kc_ref, vc_ref, k_ref, vt_ref, gt_ref):
    tm = h_ref.shape[1]
    u = _rms(h_ref[0], g_ref[...]).astype(BF16)
    qt_ref[0] = (_dot_nt(wqt_ref[...], u) * (HD ** -0.5 * LOG2E)).astype(BF16)
    c = _dot(u, wc_ref[...])
    for j in range(KV_W // LANES):
        kc_ref[0, j] = c[:, j * LANES:(j + 1) * LANES]
        vc_ref[0, j] = c[:, KV_W + j * LANES:KV_W + (j + 1) * LANES]
    kk = _dot(u, wk_ref[...])
    t0 = pl.program_id(1) * tm
    lane = lax.broadcasted_iota(jnp.int32, (tm, LANES), 1)
    blk = lax.shift_right_logical(t0 + lax.broadcasted_iota(jnp.int32, (tm, LANES), 0), 6)
    onehot = jnp.where(lane - HD == blk, 1.0, 0.0)
    for g in range(N_KV):
        k_ref[0, g, 0] = (kk[:, g * LANES:(g + 1) * LANES] + onehot).astype(BF16)
        k_ref[0, g, 1] = kk[:, (N_KV + g) * LANES:(N_KV + g + 1) * LANES].astype(BF16)
    vt = _dot_nt(wvt_ref[...], u)
    tk = vt_ref.shape[5]
    for j in range(tm // tk):
        cols = slice(j * tk, (j + 1) * tk)
        vt_ref[0, :, 0, j] = vt[:KV_W, cols].reshape(N_KV, HD, tk).astype(BF16)
        vt_ref[0, :, 1, j] = vt[KV_W:, cols].reshape(N_KV, HD, tk).astype(BF16)
    gt_ref[0] = jax.nn.sigmoid(_dot_nt(wgt_ref[...], u)).reshape(N_KV, GATE_ROWS, tm)


def _nsa_proj(h, g_pre, w_in):
    b, s, _ = h.shape
    tm = min(PROJ_TM, s)
    tk = min(ATT_T, s)
    nq = N_HEADS * HD
    assert s // SEL_BLOCK <= HD
    col = lambda i: w_in[:, nq + i * KV_W:nq + (i + 1) * KV_W]
    w_qt = w_in[:, :nq].T.astype(BF16)
    w_c = jnp.concatenate([col(0), col(1)], axis=1).astype(BF16)
    pad_heads = lambda w: jnp.pad(w.reshape(D_MODEL, N_KV, HD), ((0, 0), (0, 0), (0, LANES - HD))).reshape(D_MODEL, N_KV * LANES)
    w_k = jnp.concatenate([pad_heads(col(2)), pad_heads(col(4))], axis=1).astype(BF16)
    w_vt = jnp.concatenate([col(3), col(5)], axis=1).T.astype(BF16)
    w_gl = w_in[:, nq + 6 * KV_W:].reshape(D_MODEL, N_KV, N_GRP * 3)
    w_gt = jnp.pad(w_gl, ((0, 0), (0, 0), (0, GATE_ROWS - N_GRP * 3))).reshape(D_MODEL, N_KV * GATE_ROWS).T.astype(BF16)
    tok = lambda w: pl.BlockSpec((1, tm, w), lambda bi, i: (bi, i, 0))
    cmp_spec = pl.BlockSpec((1, KV_W // LANES, tm, LANES), lambda bi, i: (bi, 0, i, 0))
    key = jax.ShapeDtypeStruct((b, N_KV, 2, s, LANES), BF16)
    key_spec = pl.BlockSpec((1, N_KV, 2, tm, LANES), lambda bi, i: (bi, 0, 0, i, 0))
    valt = jax.ShapeDtypeStruct((b, N_KV, 2, s // tk, HD, tk), BF16)
    valt_spec = pl.BlockSpec((1, N_KV, 2, tm // tk, HD, tk), lambda bi, i: (bi, 0, 0, i, 0, 0))
    return pl.pallas_call(
        _nsa_proj_body,
        out_shape=(
            jax.ShapeDtypeStruct((b, nq, s), BF16),
            jax.ShapeDtypeStruct((b, KV_W // LANES, s, LANES), F32),
            jax.ShapeDtypeStruct((b, KV_W // LANES, s, LANES), F32),
            key, valt,
            jax.ShapeDtypeStruct((b, N_KV, GATE_ROWS, s), F32),
        ),
        grid=(b, s // tm),
        in_specs=[
            tok(D_MODEL),
            _const_spec((1, D_MODEL)),
            _const_spec((nq, D_MODEL)),
            _const_spec((D_MODEL, 2 * KV_W)),
            _const_spec((D_MODEL, 2 * N_KV * LANES)),
            _const_spec((2 * KV_W, D_MODEL)),
            _const_spec((N_KV * GATE_ROWS, D_MODEL)),
        ],
        out_specs=(
            pl.BlockSpec((1, nq, tm), lambda bi, i: (bi, 0, i)),
            cmp_spec, cmp_spec, key_spec, valt_spec,
            pl.BlockSpec((1, N_KV, GATE_ROWS, tm), lambda bi, i: (bi, 0, 0, i)),
        ),
        compiler_params=_cparams(("parallel", "parallel")),
        name="nsa_proj",
    )(h, g_pre.reshape(1, D_MODEL), w_qt, w_c, w_k, w_vt, w_gt)


def _compress_body(x_ref, pe_ref, w1_ref, w2_ref, o_ref, *, transposed):
    half = CMP_BLOCK // 2
    n = x_ref.shape[2] // half
    ha = hb = None
    for l in range(half):
        xl = jnp.concatenate([x_ref[0, j, pl.ds(l, n, stride=half), :] for j in range(x_ref.shape[1])], axis=-1)
        a = _dot((xl + pe_ref[l:l + 1, :]).astype(BF16), w1_ref[l])
        b = _dot((xl + pe_ref[half + l:half + l + 1, :]).astype(BF16), w1_ref[half + l])
        ha = a if ha is None else ha + a
        hb = b if hb is None else hb + b
    hid = ha + pltpu.roll(hb, n - 1, 0)
    hid = (hid * jax.nn.sigmoid(hid)).astype(BF16)
    if transposed:
        o_ref[0] = _dot_nt(w2_ref[...], hid).reshape(N_KV, HD, n).astype(BF16)
    else:
        out = _dot(hid, w2_ref[...]).astype(BF16)
        for g in range(N_KV):
            o_ref[0, g] = out[:, g * HD:(g + 1) * HD]


def _compress(t, pe, w1, w2, transposed):
    b, _, s, _ = t.shape
    rows = s // (CMP_BLOCK // 2)
    eye = jnp.eye(N_KV, dtype=F32)
    wbd = jnp.einsum("ldc,gh->lgdhc", w1, eye).reshape(CMP_BLOCK, KV_W, KV_W).astype(BF16)
    pe_t = jnp.broadcast_to(pe[:, None, :], (CMP_BLOCK, N_KV, HD)).reshape(CMP_BLOCK, KV_W)
    w2bd = jnp.einsum("cd,gh->gchd", w2, eye).reshape(KV_W, KV_W)
    if transposed:
        w2bd = w2bd.T
        out_shape, out_block = (b, N_KV, HD, rows), (1, N_KV, HD, rows)
    else:
        out_shape, out_block = (b, N_KV, rows, HD), (1, N_KV, rows, HD)
    return pl.pallas_call(
        functools.partial(_compress_body, transposed=transposed),
        out_shape=jax.ShapeDtypeStruct(out_shape, BF16),
        grid=(b,),
        in_specs=[
            pl.BlockSpec((1, KV_W // LANES, s, LANES), lambda bi: (bi, 0, 0, 0)),
            _const_spec((CMP_BLOCK, KV_W)),
            _const_spec((CMP_BLOCK, KV_W, KV_W)),
            _const_spec((KV_W, KV_W)),
        ],
        out_specs=pl.BlockSpec(out_block, lambda bi: (bi, 0, 0, 0)),
        compiler_params=_cparams(("parallel",)),
        name="nsa_compress",
    )(t, pe_t, wbd, w2bd.astype(BF16))


def _alibi_slope(h):
    return float(2.0 ** (-8.0 * (h + 1) / N_HEADS))


def _cmp_probs(kc, qh, slope2, end_f, valid):
    s = _dot(kc, qh) + jnp.where(valid, slope2 * end_f, MASK_VALUE)
    return jnp.exp2(s - jnp.max(s, axis=0, keepdims=True)).astype(BF16)


def _cmp_geometry(n_cmp_rows, tq, t0):
    n_io = lax.broadcasted_iota(jnp.int32, (n_cmp_rows, tq), 0)
    t_io = t0 + lax.broadcasted_iota(jnp.int32, (n_cmp_rows, tq), 1)
    end = n_io * CMP_STRIDE + (CMP_BLOCK - 1)
    return end.astype(F32), (t_io >= end) & (n_io < n_cmp_rows - 1)


def _select_body(qt_ref, kc_ref, vct_ref, c2s_ref, chunk_ref, mneg_ref, active_ref, oc_ref, *, n_sel, n_cmp_rows):
    tq = qt_ref.shape[2]
    t0 = pl.program_id(1) * tq
    has_block = t0 + lax.broadcasted_iota(jnp.int32, (1, tq), 1) >= CMP_BLOCK - 1
    v_lo = n_sel + SUM_ROWS
    s_io = lax.broadcasted_iota(jnp.int32, (n_sel, tq), 0)
    tt = t0 + lax.broadcasted_iota(jnp.int32, (n_sel, tq), 1)
    cur = lax.shift_right_logical(tt, 6)
    forced = (s_io == 0) | (s_io == cur) | (s_io == cur - 1)
    visible = s_io * SEL_BLOCK <= tt

    end_f, valid = _cmp_geometry(n_cmp_rows, tq, t0)

    def compressed(g):
        kc = kc_ref[0, g]
        heads = range(g * N_GRP, (g + 1) * N_GRP)
        probs = [_cmp_probs(kc, qt_ref[0, hh * HD:(hh + 1) * HD, :], _alibi_slope(hh) * LOG2E, end_f, valid)
                 for hh in heads]
        lhs = jnp.concatenate([c2s_ref[...], vct_ref[0, g]], axis=0)
        mass = [_dot(lhs, p) for p in probs]
        inv_l = [1.0 / m[n_sel:n_sel + 1] for m in mass]
        for hh, m, il in zip(heads, mass, inv_l):
            oc_ref[0, hh * HD:(hh + 1) * HD, :] = jnp.where(has_block, m[v_lo:v_lo + HD] * il, 0.0).astype(BF16)
        return sum(m[0:n_sel] * il for m, il in zip(mass, inv_l))

    def select(g, imp):
        x = jnp.where(forced, FORCED_SCORE, jnp.where(visible, imp, -1.0))
        cnt = jnp.zeros((n_sel, tq), F32)
        for j in range(n_sel):
            xj = x[j:j + 1, :]
            ge = jnp.where(xj >= x, 1.0, 0.0)
            gt = jnp.where(xj > x, 1.0, 0.0)
            cnt = cnt + jnp.where(s_io > j, ge, gt)
        chosen = cnt < float(N_SELECT)
        mneg_ref[0, g, 0:n_sel, :] = jnp.where(chosen, 0.0, MASK_VALUE).astype(BF16)
        if n_sel < HD:
            mneg_ref[0, g, n_sel:HD, :] = jnp.zeros((HD - n_sel, tq), BF16)
        per_query = _dot(chunk_ref[...], jnp.where(chosen, 1.0, 0.0).astype(BF16))
        active_ref[0, g, 0] = _dot_nt(jnp.ones((SUBLANES, tq), BF16), per_query.astype(BF16))

    imps = [compressed(g) for g in range(N_KV)]
    for g in range(N_KV):
        select(g, imps[g])


def _select(qt, k_cmp, v_cmpt):
    b, nq, s = qt.shape
    tq = min(ATT_T, s)
    n_sel = s // SEL_BLOCK
    rows = k_cmp.shape[2]
    n_cmp = (s - CMP_BLOCK) // CMP_STRIDE + 1
    assert rows == n_cmp + 1
    cs = np.arange(rows) * CMP_STRIDE
    bs = np.arange(n_sel) * SEL_BLOCK
    overlap = np.clip(np.minimum(cs[:, None] + CMP_BLOCK, bs[None, :] + SEL_BLOCK)
                      - np.maximum(cs[:, None], bs[None, :]), 0, None)
    c2s_t = jnp.asarray(np.concatenate([(overlap / CMP_BLOCK).T, np.ones((SUM_ROWS, rows))], axis=0), BF16)
    n_chunks = s // tq
    chunk_of_block = np.arange(n_sel) * SEL_BLOCK // tq
    chunk_mat = jnp.asarray(chunk_of_block[None, :] == np.arange(n_chunks)[:, None], BF16)
    return pl.pallas_call(
        functools.partial(_select_body, n_sel=n_sel, n_cmp_rows=rows),
        out_shape=(jax.ShapeDtypeStruct((b, N_KV, HD, s), BF16),
                   jax.ShapeDtypeStruct((b, N_KV, s // tq, SUBLANES, n_chunks), F32),
                   jax.ShapeDtypeStruct((b, nq, s), BF16)),
        grid=(b, s // tq),
        in_specs=[
            pl.BlockSpec((1, nq, tq), lambda bi, i: (bi, 0, i)),
            pl.BlockSpec((1, N_KV, rows, HD), lambda bi, i: (bi, 0, 0, 0)),
            pl.BlockSpec((1, N_KV, HD, rows), lambda bi, i: (bi, 0, 0, 0)),
            _const_spec((n_sel + SUM_ROWS, rows)),
            _const_spec((n_chunks, n_sel)),
        ],
        out_specs=(pl.BlockSpec((1, N_KV, HD, tq), lambda bi, i: (bi, 0, 0, i)),
                   pl.BlockSpec((1, N_KV, 1, SUBLANES, n_chunks), lambda bi, i: (bi, 0, i, 0, 0)),
                   pl.BlockSpec((1, nq, tq), lambda bi, i: (bi, 0, i))),
        compiler_params=_cparams(("parallel", "parallel")),
        name="nsa_select",
    )(qt, k_cmp, v_cmpt, c2s_t, chunk_mat)


def _attn_body(slope_ref, active_ref, qt_ref, k_ref, vt_ref, oc_ref, mneg_ref,
               gt_ref, o_ref, s_sc, bias_sc, m_sc, acc_sc, todo_ref):
    tq = qt_ref.shape[2]
    tk = tq
    g = pl.program_id(1)
    qi = pl.program_id(2)
    slopes = [slope_ref[g * N_GRP + r] for r in range(N_GRP)]
    q_rows = [qt_ref[0, r * HD:(r + 1) * HD, :] for r in range(N_GRP)]
    qk = [jnp.concatenate([q_rows[r], mneg_ref[0, 0]], axis=0) for r in range(N_GRP)]

    @pl.when(qi == 0)
    def _():
        key_io = lax.broadcasted_iota(jnp.int32, (tk, tq), 0)
        rel = key_io - lax.broadcasted_iota(jnp.int32, (tk, tq), 1)
        key_f = key_io.astype(F32)
        causal = jnp.where(rel <= 0, 0.0, MASK_VALUE)
        oldest = jnp.where(rel > 0, 0.0, MASK_VALUE)
        for r in range(N_GRP):
            base = slopes[r] * key_f
            bias_sc[0, r] = base
            bias_sc[1, r] = base + causal
            bias_sc[2, r] = base + oldest

    ones_rows = jnp.ones((SUM_ROWS, tk), BF16)

    def offsets(c):
        cf = ((c - qi) * tk).astype(F32)
        return [slopes[r] * cf for r in range(N_GRP)]

    tile = (pl.program_id(0) * N_KV + g) * pl.num_programs(2) + qi

    def note_active(c, n):
        todo_ref[n] = c
        return n + active_ref[tile * pl.num_programs(2) + c]

    todo_ref[0] = 0
    n_sel = lax.fori_loop(0, qi, note_active, jnp.int32(0)) + 1
    c_lo_w = jnp.maximum(qi - WINDOW // tk, 0)
    n_items = n_sel + (qi - c_lo_w) + 1

    def item(pos):
        window = (pos >= n_sel).astype(jnp.int32)
        listed = todo_ref[jnp.clip(pos, 0, jnp.maximum(n_sel - 2, 0))]
        c = jnp.where(pos < n_sel - 1, listed, jnp.where(pos == n_sel - 1, qi, c_lo_w + pos - n_sel))
        return window, c

    def score_stage(pos, slot):
        window, c = item(pos)
        kblk = k_ref[0, 0, window, pl.ds(pl.multiple_of(c * tk, tk), tk), :]
        variant = jnp.where(c == qi, 1, jnp.where((window == 1) & (c == qi - WINDOW // tk), 2, 0))
        offs = offsets(c)
        mcs = []
        for r in range(N_GRP):
            s = _dot(kblk, qk[r]) + bias_sc[variant, r]
            s_sc[slot, r] = s
            mcs.append(jnp.max(s, axis=0, keepdims=True) + offs[r])
        return tuple(mcs)

    def value_stage(pos, mcs, slot):
        window, c = item(pos)
        vt = jnp.concatenate([vt_ref[0, 0, window, c], ones_rows], axis=0)
        offs = offsets(c)
        for r in range(N_GRP):
            m = m_sc[window, r, 0:1, :]
            m_new = jnp.maximum(m, mcs[r])
            p = jnp.exp2(s_sc[slot, r] - (m_new - offs[r]))
            acc_sc[window, r] = jnp.exp2(m - m_new) * acc_sc[window, r] + _dot(vt, p.astype(BF16))
            m_sc[window, r, 0:1, :] = m_new

    m_sc[...] = jnp.full(m_sc.shape, MASK_VALUE, F32)
    acc_sc[...] = jnp.zeros(acc_sc.shape, F32)

    def trips(n, first, count, mcs):
        def trip(k, mcs):
            pos = first + n * k
            for i in range(n):
                mcs_next = score_stage(pos + i + 1, (i + 1) % 2)
                value_stage(pos + i, mcs, i % 2)
                mcs = mcs_next
            return mcs
        return lax.fori_loop(0, count, trip, mcs)

    steps = n_items - 1
    mcs = trips(4, 0, steps // 4, score_stage(jnp.int32(0), 0))
    done = (steps // 4) * 4
    mcs = trips(2, done, (steps - done) // 2, mcs)
    done = done + ((steps - done) // 2) * 2
    mcs = trips(1, done, steps - done, mcs)
    value_stage(steps, mcs, steps - done)

    gate = gt_ref[0, 0]
    normed = lambda acc: acc[0:HD] * (1.0 / acc[HD:HD + 1])
    for r in range(N_GRP):
        o_c = oc_ref[0, r * HD:(r + 1) * HD, :].astype(F32)
        o_s = normed(acc_sc[0, r])
        o_w = normed(acc_sc[1, r])
        out = (gate[3 * r:3 * r + 1, :] * o_c + gate[3 * r + 1:3 * r + 2, :] * o_s
               + gate[3 * r + 2:3 * r + 3, :] * o_w)
        o_ref[0, r * HD:(r + 1) * HD, :] = out.astype(BF16)


def _attention(qt, keys, vals_t, oc_t, mneg, active, gates_t):
    b, nq, s = qt.shape
    tq = vals_t.shape[5]
    assert WINDOW % tq == 0 and active.shape == (b, N_KV, s // tq, SUBLANES, s // tq)
    active = (active[:, :, :, 0, :] > 0.0).astype(jnp.int32).reshape(-1)
    gw = N_GRP * HD
    slopes = jnp.asarray([_alibi_slope(h) * LOG2E for h in range(N_HEADS)], F32)
    key_spec = pl.BlockSpec((1, 1, 2, s, LANES), lambda bi, g, i: (bi, g, 0, 0, 0))
    val_spec = pl.BlockSpec((1, 1, 2, s // tq, HD, tq), lambda bi, g, i: (bi, g, 0, 0, 0, 0))
    return pl.pallas_call(
        _attn_body,
        out_shape=jax.ShapeDtypeStruct((b, nq, s), BF16),
        grid=(b, N_KV, s // tq),
        in_specs=[
            pl.BlockSpec(memory_space=pltpu.SMEM),
            pl.BlockSpec(memory_space=pltpu.SMEM),
            pl.BlockSpec((1, gw, tq), lambda bi, g, i: (bi, g, i)),
            key_spec, val_spec,
            pl.BlockSpec((1, gw, tq), lambda bi, g, i: (bi, g, i)),
            pl.BlockSpec((1, 1, HD, tq), lambda bi, g, i: (bi, g, 0, i)),
            pl.BlockSpec((1, 1, GATE_ROWS, tq), lambda bi, g, i: (bi, g, 0, i)),
        ],
        out_specs=pl.BlockSpec((1, gw, tq), lambda bi, g, i: (bi, g, i)),
        scratch_shapes=[
            pltpu.VMEM((2, N_GRP, tq, tq), F32),
            pltpu.VMEM((3, N_GRP, tq, tq), F32),
            pltpu.VMEM((2, N_GRP, SUBLANES, tq), F32),
            pltpu.VMEM((2, N_GRP, HD + SUM_ROWS, tq), F32),
            pltpu.SMEM((s // tq,), jnp.int32),
        ],
        compiler_params=_cparams(("parallel", "parallel", "arbitrary")),
        name="nsa_attention",
    )(slopes, active, qt, keys, vals_t, oc_t, mneg, gates_t)


def _outproj_t_body(yt_ref, h_ref, w_ref, g_ref, o_ref):
    o_ref[0] = h_ref[0] + _rms(_dot_tn(yt_ref[0], w_ref[...]), g_ref[...])


def _outproj_t(yt, h, w_out, g_post):
    b, s, d = h.shape
    tm = min(PROJ_TM, s)
    return pl.pallas_call(
        _outproj_t_body,
        out_shape=jax.ShapeDtypeStruct((b, s, d), F32),
        grid=(b, s // tm),
        in_specs=[
            pl.BlockSpec((1, d, tm), lambda bi, i: (bi, 0, i)),
            pl.BlockSpec((1, tm, d), lambda bi, i: (bi, i, 0)),
            _const_spec((d, d)),
            _const_spec((1, d)),
        ],
        out_specs=pl.BlockSpec((1, tm, d), lambda bi, i: (bi, i, 0)),
        compiler_params=_cparams(("parallel", "parallel")),
        name="outproj_t",
    )(yt, h, w_out.astype(BF16), g_post.reshape(1, d))


def _nsa_mixer(h, g_pre, w_in, pe_k, w_ck1, w_ck2, pe_v, w_cv1, w_cv2, w_out, g_post):
    qt, kc, vc, keys, vals_t, gates_t = _nsa_proj(h, g_pre, w_in)
    k_cmp = _compress(kc, pe_k, w_ck1, w_ck2, transposed=False)
    v_cmpt = _compress(vc, pe_v, w_cv1, w_cv2, transposed=True)
    mneg, active, oc_t = _select(qt, k_cmp, v_cmpt)
    out_t = _attention(qt, keys, vals_t, oc_t, mneg, active, gates_t)
    return _outproj_t(out_t, h, w_out, g_post)


def _head_sum(x, sel_ref):
    return _dot(x.astype(BF16), sel_ref[...])


def _head_expand(x, selt_ref):
    return _dot(x.astype(BF16), selt_ref[...])


def _softplus(x):
    return jnp.maximum(x, 0.0) + jnp.log(1.0 + jnp.exp(-jnp.abs(x)))


def _rw_proj_body(h_ref, hp_ref, g_ref, mu_ref, vec_ref, wr_ref, wk_ref, wv_ref, wd_ref, wa_ref, wg_ref,
                  w2d_ref, w2a_ref, w2g_ref, sel_ref, selt_ref,
                  r_ref, lw_ref, k_ref, v_ref, a_ref, b_ref, gg_ref):
    tm = h_ref.shape[1]
    u = _rms(h_ref[0], g_ref[...])
    prev = _rms(hp_ref[0], g_ref[...])[7:8, :]
    prev = jnp.where(pl.program_id(1) == 0, 0.0, prev)
    row = lax.broadcasted_iota(jnp.int32, (tm, D_MODEL), 0)
    u_prev = jnp.where(row == 0, prev, pltpu.roll(u, 1, 0))
    xx = u_prev - u

    def mix(i):
        return (u + xx * mu_ref[i:i + 1, :]).astype(BF16)

    w0, a0, k_k, k_a = (vec_ref[i:i + 1, :] for i in range(4))
    k = _dot(mix(1), wk_ref[...])
    d1 = jnp.tanh(_dot(mix(3), wd_ref[...])).astype(BF16)
    kk = k * k_k
    norm = jnp.sqrt(_head_sum(kk * kk, sel_ref))
    a1 = _dot(mix(4), wa_ref[...]).astype(BF16)
    r = _dot(mix(0), wr_ref[...])
    w = -_softplus(-(w0 + _dot(d1, w2d_ref[...]))) - 0.5
    kk = kk * _head_expand(1.0 / jnp.maximum(norm, 1e-12), selt_ref)
    g1 = jax.nn.sigmoid(_dot(mix(5), wg_ref[...])).astype(BF16)
    alpha = jax.nn.sigmoid(a0 + _dot(a1, w2a_ref[...]))
    v = _dot(mix(2), wv_ref[...])
    r_ref[0] = r.astype(BF16)
    lw_ref[0] = -jnp.exp(w)
    k_ref[0] = (k * (1.0 + (alpha - 1.0) * k_a)).astype(BF16)
    v_ref[0] = v.astype(BF16)
    a_ref[0] = (-kk).astype(BF16)
    b_ref[0] = (kk * alpha).astype(BF16)
    gg_ref[0] = _dot(g1, w2g_ref[...]).astype(BF16)


def _head_selectors():
    lane_head = np.arange(D_MODEL) // HD
    sel = (lane_head[:, None] == np.arange(LANES)[None, :]).astype(np.float32)
    return jnp.asarray(sel, BF16), jnp.asarray(sel.T, BF16)


def _pad_cols(w, n):
    return jnp.pad(w, ((0, 0), (0, n - w.shape[1])))


def _pad_rows(w, n):
    return jnp.pad(w, ((0, n - w.shape[0]), (0, 0)))


def _rw_proj(h, g_pre, mu, w_in, w0, w_w2, a0, w_a2, w_g2, k_k, k_a):
    b, s, d = h.shape
    tm = min(PROJ_TM, s)
    offs = np.cumsum((0, d, d, d, DECAY_LORA, AAA_LORA, GATE_LORA))
    cols = [w_in[:, int(offs[i]):int(offs[i + 1])] for i in range(6)]
    ld, lg = LANES, 2 * LANES
    wr, wk, wv = (c.astype(BF16) for c in cols[:3])
    wd = _pad_cols(cols[3], ld).astype(BF16)
    wa = _pad_cols(cols[4], ld).astype(BF16)
    wg = _pad_cols(cols[5], lg).astype(BF16)
    w2d = _pad_rows(w_w2, ld).astype(BF16)
    w2a = _pad_rows(w_a2, ld).astype(BF16)
    w2g = _pad_rows(w_g2, lg).astype(BF16)
    mu8 = _pad_rows(mu, 8)
    vecs = _pad_rows(jnp.stack([w0, a0, k_k, k_a]), 8)
    sel, selt = _head_selectors()
    tok = pl.BlockSpec((1, tm, d), lambda bi, i: (bi, i, 0))
    out = lambda dt: jax.ShapeDtypeStruct((b, s, d), dt)
    return pl.pallas_call(
        _rw_proj_body,
        out_shape=(out(BF16), out(F32), out(BF16), out(BF16), out(BF16), out(BF16), out(BF16)),
        grid=(b, s // tm),
        in_specs=[
            tok,
            pl.BlockSpec((1, 8, d), lambda bi, i: (bi, jnp.maximum(i * (tm // 8) - 1, 0), 0)),
            _const_spec((1, d)), _const_spec((8, d)), _const_spec((8, d)),
            _const_spec((d, d)), _const_spec((d, d)), _const_spec((d, d)),
            _const_spec((d, ld)), _const_spec((d, ld)), _const_spec((d, lg)),
            _const_spec((ld, d)), _const_spec((ld, d)), _const_spec((lg, d)),
            _const_spec((d, LANES)), _const_spec((LANES, d)),
        ],
        out_specs=(tok,) * 7,
        compiler_params=_cparams(("parallel", "parallel")),
        name="rwkv_proj",
    )(h, h, g_pre.reshape(1, d), mu8, vecs, wr, wk, wv, wd, wa, wg, w2d, w2a, w2g, sel, selt)


def _rw_prep_body(ltri_ref, r_ref, lw_ref, k_ref, v_ref, a_ref, b_ref,
                  rhat_ref, y1_ref, g_ref, n_ref):
    c = RW_C
    q = RW_Q
    nh = q // HD
    nch = r_ref.shape[1] // c
    lane_head = lax.shift_right_logical(lax.broadcasted_iota(jnp.int32, (c, q), 1), 6)
    ri = lax.broadcasted_iota(jnp.int32, (q, q), 0)
    ci = lax.broadcasted_iota(jnp.int32, (q, q), 1)
    same_head = lax.shift_right_logical(ri, 6) == lax.shift_right_logical(ci, 6)
    strict_bd = same_head & ((ci & (c - 1)) < (ri & (c - 1)))
    eye = ri == ci
    t_io = lax.broadcasted_iota(jnp.int32, (c, q), 0)
    j_io = lax.broadcasted_iota(jnp.int32, (c, q), 1) & (c - 1)
    strict_ls = j_io < t_io
    incl_ls = j_io <= t_io
    rows = [slice(i * c, (i + 1) * c) for i in range(nch)]

    def each(f):
        return [f(i) for i in range(nch)]

    def expand(x):
        return jnp.concatenate([jnp.where(lane_head == hh, x, 0.0) for hh in range(nh)], axis=0).astype(BF16)

    def collapse(x):
        out = x[0:c]
        for hh in range(1, nh):
            out = out + x[hh * c:(hh + 1) * c]
        return out

    lw = each(lambda i: lw_ref[0, rows[i], :])
    parts = each(lambda i: _dot(ltri_ref[...], jnp.concatenate(_split3(lw[i]), axis=1)))
    cum = each(lambda i: parts[i][:, 0:q] + parts[i][:, q:2 * q] + parts[i][:, 2 * q:3 * q])
    cum_c = each(lambda i: cum[i][c - 1:c, :])
    e_inv = each(lambda i: jnp.exp(-cum[i]))
    e_rem = each(lambda i: jnp.exp(cum_c[i] - cum[i]))
    at = each(lambda i: a_ref[0, rows[i], :] * jnp.exp(cum[i] - lw[i]))
    rt = each(lambda i: r_ref[0, rows[i], :] * jnp.exp(cum[i]))
    x4 = each(lambda i: expand(at[i]))
    b4 = each(lambda i: expand(b_ref[0, rows[i], :] * e_inv[i]))
    k4 = each(lambda i: expand(k_ref[0, rows[i], :] * e_inv[i]))
    v4 = each(lambda i: expand(v_ref[0, rows[i], :]))
    bbar = each(lambda i: (b_ref[0, rows[i], :] * e_rem[i]).astype(BF16))
    kbar = each(lambda i: (k_ref[0, rows[i], :] * e_rem[i]).astype(BF16))

    lbd = each(lambda i: jnp.where(strict_bd, _dot_nt(x4[i], b4[i]), 0.0))
    tbd = each(lambda i: jnp.where(eye, 1.0, lbd[i]))
    lb = each(lambda i: lbd[i].astype(BF16))
    p = each(lambda i: _dot(lb[i], lb[i]))
    n_lvl = int(np.log2(c)) - 1
    for lvl in range(n_lvl):
        pb = each(lambda i: p[i].astype(BF16))
        if lvl + 1 < n_lvl:
            both = each(lambda i: _dot(pb[i], jnp.concatenate([tbd[i].astype(BF16), pb[i]], axis=1)))
            tbd = each(lambda i: tbd[i] + both[i][:, 0:q])
            p = each(lambda i: both[i][:, q:2 * q])
        else:
            tbd = each(lambda i: tbd[i] + _dot(pb[i], tbd[i].astype(BF16)))
    t_ls = each(lambda i: collapse(tbd[i]).astype(BF16))

    w = nh * c
    ar = each(lambda i: _dot_nt(jnp.concatenate([at[i], rt[i]], axis=0).astype(BF16),
                                jnp.concatenate([k4[i], b4[i]], axis=0)))
    a_kk = each(lambda i: jnp.concatenate([jnp.where(strict_ls, ar[i][0:c, 0:w], 0.0),
                                           jnp.where(incl_ls, ar[i][c:2 * c, 0:w], 0.0)], axis=0).astype(BF16))
    a_rb = each(lambda i: jnp.where(incl_ls, ar[i][c:2 * c, w:2 * w], 0.0).astype(BF16))

    gy = each(lambda i: _dot(a_kk[i], v4[i]))
    g1 = each(lambda i: gy[i][0:c])
    ua = each(lambda i: _dot(t_ls[i], jnp.concatenate([expand(g1[i]), x4[i]], axis=1)))
    u0 = each(lambda i: ua[i][:, 0:q])
    ahat = each(lambda i: ua[i][:, q:2 * q])
    ry = each(lambda i: _dot(a_rb[i], jnp.concatenate([expand(ahat[i]), expand(u0[i])], axis=1)))
    rhat = each(lambda i: rt[i] + ry[i][:, 0:q])
    y1 = each(lambda i: gy[i][c:2 * c] + ry[i][:, q:2 * q])

    gm = each(lambda i: _dot_tn(bbar[i], ahat[i].astype(BF16)))
    nm = each(lambda i: _dot_tn(jnp.concatenate([bbar[i], kbar[i]], axis=0),
                                jnp.concatenate([u0[i].astype(BF16), v_ref[0, rows[i], :]], axis=0)))
    for i in range(nch):
        gmi = jnp.where(same_head, gm[i], 0.0) + jnp.where(eye, jnp.exp(cum_c[i]), 0.0)
        rhat_ref[0, rows[i], :] = rhat[i].astype(BF16)
        y1_ref[0, rows[i], :] = y1[i].astype(BF16)
        g_ref[0, rows[i], :] = collapse(gmi).astype(BF16)
        n_ref[0, rows[i], :] = collapse(jnp.where(same_head, nm[i], 0.0)).astype(BF16)


def _rw_prep(r, lw, k, v, a, b):
    bsz, s, d = r.shape
    rows = min(RW_C * RW_NCH, s)
    ltri = jnp.asarray(np.tril(np.ones((RW_C, RW_C), np.float32)), BF16)
    blk = pl.BlockSpec((1, rows, RW_Q), lambda bi, qi, j: (bi, j, qi))
    return pl.pallas_call(
        _rw_prep_body,
        out_shape=(jax.ShapeDtypeStruct((bsz, s, d), BF16),) * 4,
        grid=(bsz, d // RW_Q, s // rows),
        in_specs=[_const_spec((RW_C, RW_C))] + [blk] * 6,
        out_specs=(blk,) * 4,
        compiler_params=_cparams(("parallel", "parallel", "parallel")),
        name="rwkv_prep",
    )(ltri, r, lw, k, v, a, b)


def _rw_scan_body(rhat_ref, y1_ref, g_ref, n_ref, y_ref, h_sc):
    c = RW_C
    q = RW_Q
    nb, nq = h_sc.shape[0], h_sc.shape[1]

    @pl.when(pl.program_id(1) == 0)
    def _():
        h_sc[...] = jnp.zeros(h_sc.shape, F32)

    ri = lax.broadcasted_iota(jnp.int32, (q, q), 0)
    ci = lax.broadcasted_iota(jnp.int32, (q, q), 1)
    same_head = lax.shift_right_logical(ri, 6) == lax.shift_right_logical(ci, 6)

    def step(ch, carry):
        rows = pl.ds(pl.multiple_of(ch * c, c), c)
        for bi in range(nb):
            for qi in range(nq):
                lanes = slice(qi * q, (qi + 1) * q)
                hb = h_sc[bi, qi].astype(BF16)
                y = _dot(rhat_ref[bi, rows, lanes], hb) + y1_ref[bi, rows, lanes]
                y_ref[bi, rows, lanes] = y.astype(BF16)
                g_ls = g_ref[bi, rows, lanes]
                n_ls = n_ref[bi, rows, lanes]
                gbd = jnp.where(same_head, jnp.concatenate([g_ls] * (q // c), axis=0), 0.0)
                nbd = jnp.where(same_head, jnp.concatenate([n_ls] * (q // c), axis=0), 0.0)
                h_sc[bi, qi] = _dot(gbd.astype(BF16), hb) + nbd
        return carry

    lax.fori_loop(0, rhat_ref.shape[1] // c, step, 0)


def _rw_scan(rhat, y1, g, n):
    bsz, s, d = rhat.shape
    rows = min(512, s)
    nb = 2 if bsz % 2 == 0 else 1
    blk = pl.BlockSpec((nb, rows, d), lambda bi, j: (bi, j, 0))
    return pl.pallas_call(
        _rw_scan_body,
        out_shape=jax.ShapeDtypeStruct((bsz, s, d), BF16),
        grid=(bsz // nb, s // rows),
        in_specs=[blk] * 4,
        out_specs=blk,
        scratch_shapes=[pltpu.VMEM((nb, d // RW_Q, RW_Q, RW_Q), F32)],
        compiler_params=_cparams(("parallel", "arbitrary")),
        name="rwkv_scan",
    )(rhat, y1, g, n)


def _rw_post_body(y_ref, r_ref, k_ref, v_ref, gg_ref, h_ref, vec_ref, w_ref, gpost_ref, sel_ref, selt_ref, o_ref):
    gn_w, gn_b, r_k = (vec_ref[i:i + 1, :] for i in range(3))
    f32 = lambda ref: ref[...].astype(F32)
    y = f32(y_ref)
    inv_n = 1.0 / HD
    y_sum = _head_sum(y, sel_ref)
    rk_sum = _head_sum(f32(r_ref) * f32(k_ref) * r_k, sel_ref)
    yc = y - _head_expand(y_sum * inv_n, selt_ref)
    bonus = _head_expand(rk_sum, selt_ref) * f32(v_ref)
    var = _head_sum(yc * yc, sel_ref) * inv_n
    yn = yc * _head_expand(lax.rsqrt(var + GN_EPS), selt_ref) * gn_w + gn_b
    z = ((yn + bonus) * f32(gg_ref)).astype(BF16)
    o_ref[...] = h_ref[...] + _rms(_dot(z, w_ref[...]), gpost_ref[...])


def _rw_post(y, r, k, v, gg, h, gn_w, gn_b, r_k, w_out, g_post):
    t, d = h.shape
    tm = min(PROJ_TM, t)
    vecs = _pad_rows(jnp.stack([gn_w, gn_b, r_k.reshape(d)]), 8)
    sel, selt = _head_selectors()
    tok = pl.BlockSpec((tm, d), lambda i: (i, 0))
    return pl.pallas_call(
        _rw_post_body,
        out_shape=jax.ShapeDtypeStruct((t, d), F32),
        grid=(t // tm,),
        in_specs=[tok] * 6 + [_const_spec((8, d)), _const_spec((d, d)), _const_spec((1, d)),
                              _const_spec((d, LANES)), _const_spec((LANES, d))],
        out_specs=tok,
        compiler_params=_cparams(("parallel",)),
        name="rwkv_post",
    )(y, r, k, v, gg, h, vecs, w_out.astype(BF16), g_post.reshape(1, d), sel, selt)


def _rwkv_mixer(h, g_pre, mu, w_in, w0, w_w2, a0, w_a2, w_g2, k_k, k_a, r_k, gn_w, gn_b, w_out, g_post):
    b, s, d = h.shape
    r, lw, k, v, a, bb, gg = _rw_proj(h, g_pre, mu, w_in, w0, w_w2, a0, w_a2, w_g2, k_k, k_a)
    rhat, y1, g, n = _rw_prep(r, lw, k, v, a, bb)
    y = _rw_scan(rhat, y1, g, n)
    f2 = lambda x: x.reshape(b * s, d)
    return _rw_post(f2(y), f2(r), f2(k), f2(v), f2(gg), f2(h), gn_w, gn_b, r_k, w_out, g_post).reshape(b, s, d)


def kernel(x, ffn1_norm_pre, ffn1_w_gu, ffn1_w_down, ffn1_norm_post, mix_norm_pre, nsa_w_in, nsa_pe_k,
           nsa_w_ck1, nsa_w_ck2, nsa_pe_v, nsa_w_cv1, nsa_w_cv2, nsa_w_out, rwkv_mu, rwkv_w_in, rwkv_w0,
           rwkv_w_w2, rwkv_a0, rwkv_w_a2, rwkv_w_g2, rwkv_k_k, rwkv_k_a, rwkv_r_k, rwkv_gn_w, rwkv_gn_b,
           rwkv_w_out, mix_norm_post, ffn2_norm_pre, ffn2_w_gu, ffn2_w_down, ffn2_norm_post):
    b, s, d = x.shape
    flat = lambda t: t.reshape(b * s, d)
    cube = lambda t: t.reshape(b, s, d)
    h = x
    depth = ffn1_norm_pre.shape[0]
    for i in range(depth):
        h = cube(_ffn(flat(h), ffn1_norm_pre[i], ffn1_w_gu[i], ffn1_w_down[i], ffn1_norm_post[i]))
        j = i // 2
        if i % 2 == 0:
            h = _nsa_mixer(h, mix_norm_pre[i], nsa_w_in[j], nsa_pe_k[j], nsa_w_ck1[j], nsa_w_ck2[j],
                           nsa_pe_v[j], nsa_w_cv1[j], nsa_w_cv2[j], nsa_w_out[j], mix_norm_post[i])
        else:
            h = _rwkv_mixer(h, mix_norm_pre[i], rwkv_mu[j], rwkv_w_in[j], rwkv_w0[j], rwkv_w_w2[j],
                            rwkv_a0[j], rwkv_w_a2[j], rwkv_w_g2[j], rwkv_k_k[j], rwkv_k_a[j], rwkv_r_k[j],
                            rwkv_gn_w[j], rwkv_gn_b[j], rwkv_w_out[j], mix_norm_post[i])
        h = cube(_ffn(flat(h), ffn2_norm_pre[i], ffn2_w_gu[i], ffn2_w_down[i], ffn2_norm_post[i]))
    return h
```

```python
import functools

import numpy as np
import jax
import jax.numpy as jnp
from jax import lax
from jax.experimental import pallas as pl
from jax.experimental.pallas import tpu as pltpu

F32 = jnp.float32
BF16 = jnp.bfloat16

D_MODEL = 1024
D_FF = 2816
HALF_STEP = 0.5
RMS_EPS = 1e-6
MASK_VALUE = -1e30

HD = 64
N_HEADS = 16
N_KV = 4
N_GRP = 4
KV_W = N_KV * HD
CMP_BLOCK = 32
CMP_STRIDE = 16
SEL_BLOCK = 64
N_SELECT = 16
WINDOW = 512
FORCED_SCORE = 1e4

DECAY_LORA = 64
AAA_LORA = 64
GATE_LORA = 160
GN_EPS = 64e-5

LANES = 128
SUBLANES = 8
VMEM_LIMIT_BYTES = 56 * 1024 * 1024

FFN_TM = 512
FFN_CUTS = (0, 1536, 2816)
PROJ_TM = 512
ATT_T = 256
GATE_ROWS = 16
SUM_ROWS = 16
LOG2E = 1.4426950408889634
RW_C = 64
RW_Q = 128
RW_NCH = 16


def _cparams(sem):
    return pltpu.CompilerParams(dimension_semantics=sem, vmem_limit_bytes=VMEM_LIMIT_BYTES)


def _rms(x, g):
    ms = jnp.mean(x * x, axis=-1, keepdims=True)
    return x * lax.rsqrt(ms + RMS_EPS) * g


def _const_spec(shape):
    nd = len(shape)
    return pl.BlockSpec(shape, lambda *_: (0,) * nd, pipeline_mode=pl.Buffered(1))


def _dot(a, b):
    return jnp.dot(a, b, preferred_element_type=F32)


def _dot_nt(a, b):
    return lax.dot_general(a, b, (((1,), (1,)), ((), ())), preferred_element_type=F32)


def _dot_tn(a, b):
    return lax.dot_general(a, b, (((0,), (0,)), ((), ())), preferred_element_type=F32)


def _ffn_body(x_ref, gpre_ref, wgu_ref, wd_ref, gpost_ref, o_ref):
    x = x_ref[...]
    xn = _rms(x, gpre_ref[...]).astype(BF16)
    acc = None
    for lo, hi in zip(FFN_CUTS[:-1], FFN_CUTS[1:]):
        gate = _dot(xn, wgu_ref[:, lo:hi])
        up = _dot(xn, wgu_ref[:, D_FF + lo:D_FF + hi])
        act = (gate * jax.nn.sigmoid(gate) * up).astype(BF16)
        part = _dot(act, wd_ref[lo:hi, :])
        acc = part if acc is None else acc + part
    o_ref[...] = x + HALF_STEP * _rms(acc, gpost_ref[...])


def _ffn(h2, g_pre, w_gu, w_down, g_post):
    t = h2.shape[0]
    tm = min(FFN_TM, t)
    return pl.pallas_call(
        _ffn_body,
        out_shape=jax.ShapeDtypeStruct((t, D_MODEL), F32),
        grid=(t // tm,),
        in_specs=[
            pl.BlockSpec((tm, D_MODEL), lambda i: (i, 0)),
            _const_spec((1, D_MODEL)),
            _const_spec((D_MODEL, 2 * D_FF)),
            _const_spec((D_FF, D_MODEL)),
            _const_spec((1, D_MODEL)),
        ],
        out_specs=pl.BlockSpec((tm, D_MODEL), lambda i: (i, 0)),
        compiler_params=_cparams(("parallel",)),
        name="ffn",
    )(h2, g_pre.reshape(1, D_MODEL), w_gu.astype(BF16), w_down.astype(BF16), g_post.reshape(1, D_MODEL))


def _nsa_proj_body(h_ref, g_ref, wqt_ref, wc_ref, wk_ref, wvt_ref, wgt_ref,
                   qt_ref, kc_ref, vc_ref, k_ref, vt_ref, gt_ref):
    tm = h_ref.shape[1]
    u = _rms(h_ref[0], g_ref[...]).astype(BF16)
    qt_ref[0] = (_dot_nt(wqt_ref[...], u) * (HD ** -0.5 * LOG2E)).astype(BF16)
    c = _dot(u, wc_ref[...])
    for j in range(KV_W // LANES):
        kc_ref[0, j] = c[:, j * LANES:(j + 1) * LANES]
        vc_ref[0, j] = c[:, KV_W + j * LANES:KV_W + (j + 1) * LANES]
    kk = _dot(u, wk_ref[...])
    t0 = pl.program_id(1) * tm
    lane = lax.broadcasted_iota(jnp.int32, (tm, LANES), 1)
    blk = lax.shift_right_logical(t0 + lax.broadcasted_iota(jnp.int32, (tm, LANES), 0), 6)
    onehot = jnp.where(lane - HD == blk, 1.0, 0.0)
    for g in range(N_KV):
        k_ref[0, g, 0] = (kk[:, g * LANES:(g + 1) * LANES] + onehot).astype(BF16)
        k_ref[0, g, 1] = kk[:, (N_KV + g) * LANES:(N_KV + g + 1) * LANES].astype(BF16)
    vt = _dot_nt(wvt_ref[...], u)
    tk = vt_ref.shape[5]
    for j in range(tm // tk):
        cols = slice(j * tk, (j + 1) * tk)
        vt_ref[0, :, 0, j] = vt[:KV_W, cols].reshape(N_KV, HD, tk).astype(BF16)
        vt_ref[0, :, 1, j] = vt[KV_W:, cols].reshape(N_KV, HD, tk).astype(BF16)
    gt_ref[0] = jax.nn.sigmoid(_dot_nt(wgt_ref[...], u)).reshape(N_KV, GATE_ROWS, tm)


def _nsa_proj(h, g_pre, w_in):
    b, s, _ = h.shape
    tm = min(PROJ_TM, s)
    tk = min(ATT_T, s)
    nq = N_HEADS * HD
    assert s // SEL_BLOCK <= HD
    col = lambda i: w_in[:, nq + i * KV_W:nq + (i + 1) * KV_W]
    w_qt = w_in[:, :nq].T.astype(BF16)
    w_c = jnp.concatenate([col(0), col(1)], axis=1).astype(BF16)
    pad_heads = lambda w: jnp.pad(w.reshape(D_MODEL, N_KV, HD), ((0, 0), (0, 0), (0, LANES - HD))).reshape(D_MODEL, N_KV * LANES)
    w_k = jnp.concatenate([pad_heads(col(2)), pad_heads(col(4))], axis=1).astype(BF16)
    w_vt = jnp.concatenate([col(3), col(5)], axis=1).T.astype(BF16)
    w_gl = w_in[:, nq + 6 * KV_W:].reshape(D_MODEL, N_KV, N_GRP * 3)
    w_gt = jnp.pad(w_gl, ((0, 0), (0, 0), (0, GATE_ROWS - N_GRP * 3))).reshape(D_MODEL, N_KV * GATE_ROWS).T.astype(BF16)
    tok = lambda w: pl.BlockSpec((1, tm, w), lambda bi, i: (bi, i, 0))
    cmp_spec = pl.BlockSpec((1, KV_W // LANES, tm, LANES), lambda bi, i: (bi, 0, i, 0))
    key = jax.ShapeDtypeStruct((b, N_KV, 2, s, LANES), BF16)
    key_spec = pl.BlockSpec((1, N_KV, 2, tm, LANES), lambda bi, i: (bi, 0, 0, i, 0))
    valt = jax.ShapeDtypeStruct((b, N_KV, 2, s // tk, HD, tk), BF16)
    valt_spec = pl.BlockSpec((1, N_KV, 2, tm // tk, HD, tk), lambda bi, i: (bi, 0, 0, i, 0, 0))
    return pl.pallas_call(
        _nsa_proj_body,
        out_shape=(
            jax.ShapeDtypeStruct((b, nq, s), BF16),
            jax.ShapeDtypeStruct((b, KV_W // LANES, s, LANES), F32),
            jax.ShapeDtypeStruct((b, KV_W // LANES, s, LANES), F32),
            key, valt,
            jax.ShapeDtypeStruct((b, N_KV, GATE_ROWS, s), F32),
        ),
        grid=(b, s // tm),
        in_specs=[
            tok(D_MODEL),
            _const_spec((1, D_MODEL)),
            _const_spec((nq, D_MODEL)),
            _const_spec((D_MODEL, 2 * KV_W)),
            _const_spec((D_MODEL, 2 * N_KV * LANES)),
            _const_spec((2 * KV_W, D_MODEL)),
            _const_spec((N_KV * GATE_ROWS, D_MODEL)),
        ],
        out_specs=(
            pl.BlockSpec((1, nq, tm), lambda bi, i: (bi, 0, i)),
            cmp_spec, cmp_spec, key_spec, valt_spec,
            pl.BlockSpec((1, N_KV, GATE_ROWS, tm), lambda bi, i: (bi, 0, 0, i)),
        ),
        compiler_params=_cparams(("parallel", "parallel")),
        name="nsa_proj",
    )(h, g_pre.reshape(1, D_MODEL), w_qt, w_c, w_k, w_vt, w_gt)


def _compress_body(x_ref, pe_ref, w1_ref, w2_ref, o_ref, *, transposed):
    half = CMP_BLOCK // 2
    n = x_ref.shape[2] // half
    ha = hb = None
    for l in range(half):
        xl = jnp.concatenate([x_ref[0, j, pl.ds(l, n, stride=half), :] for j in range(x_ref.shape[1])], axis=-1)
        a = _dot((xl + pe_ref[l:l + 1, :]).astype(BF16), w1_ref[l])
        b = _dot((xl + pe_ref[half + l:half + l + 1, :]).astype(BF16), w1_ref[half + l])
        ha = a if ha is None else ha + a
        hb = b if hb is None else hb + b
    hid = ha + pltpu.roll(hb, n - 1, 0)
    hid = (hid * jax.nn.sigmoid(hid)).astype(BF16)
    if transposed:
        o_ref[0] = _dot_nt(w2_ref[...], hid).reshape(N_KV, HD, n).astype(BF16)
    else:
        out = _dot(hid, w2_ref[...]).astype(BF16)
        for g in range(N_KV):
            o_ref[0, g] = out[:, g * HD:(g + 1) * HD]


def _compress(t, pe, w1, w2, transposed):
    b, _, s, _ = t.shape
    rows = s // (CMP_BLOCK // 2)
    eye = jnp.eye(N_KV, dtype=F32)
    wbd = jnp.einsum("ldc,gh->lgdhc", w1, eye).reshape(CMP_BLOCK, KV_W, KV_W).astype(BF16)
    pe_t = jnp.broadcast_to(pe[:, None, :], (CMP_BLOCK, N_KV, HD)).reshape(CMP_BLOCK, KV_W)
    w2bd = jnp.einsum("cd,gh->gchd", w2, eye).reshape(KV_W, KV_W)
    if transposed:
        w2bd = w2bd.T
        out_shape, out_block = (b, N_KV, HD, rows), (1, N_KV, HD, rows)
    else:
        out_shape, out_block = (b, N_KV, rows, HD), (1, N_KV, rows, HD)
    return pl.pallas_call(
        functools.partial(_compress_body, transposed=transposed),
        out_shape=jax.ShapeDtypeStruct(out_shape, BF16),
        grid=(b,),
        in_specs=[
            pl.BlockSpec((1, KV_W // LANES, s, LANES), lambda bi: (bi, 0, 0, 0)),
            _const_spec((CMP_BLOCK, KV_W)),
            _const_spec((CMP_BLOCK, KV_W, KV_W)),
            _const_spec((KV_W, KV_W)),
        ],
        out_specs=pl.BlockSpec(out_block, lambda bi: (bi, 0, 0, 0)),
        compiler_params=_cparams(("parallel",)),
        name="nsa_compress",
    )(t, pe_t, wbd, w2bd.astype(BF16))


def _alibi_slope(h):
    return float(2.0 ** (-8.0 * (h + 1) / N_HEADS))


def _cmp_probs(kc, qh, slope2, end_f, valid):
    s = _dot(kc, qh) + jnp.where(valid, slope2 * end_f, MASK_VALUE)
    return jnp.exp2(s - jnp.max(s, axis=0, keepdims=True)).astype(BF16)


def _cmp_geometry(n_cmp_rows, tq, t0):
    n_io = lax.broadcasted_iota(jnp.int32, (n_cmp_rows, tq), 0)
    t_io = t0 + lax.broadcasted_iota(jnp.int32, (n_cmp_rows, tq), 1)
    end = n_io * CMP_STRIDE + (CMP_BLOCK - 1)
    return end.astype(F32), (t_io >= end) & (n_io < n_cmp_rows - 1)


def _select_body(qt_ref, kc_ref, vct_ref, c2s_ref, chunk_ref, mneg_ref, active_ref, oc_ref, *, n_sel, n_cmp_rows):
    tq = qt_ref.shape[2]
    t0 = pl.program_id(1) * tq
    has_block = t0 + lax.broadcasted_iota(jnp.int32, (1, tq), 1) >= CMP_BLOCK - 1
    v_lo = n_sel + SUM_ROWS
    s_io = lax.broadcasted_iota(jnp.int32, (n_sel, tq), 0)
    tt = t0 + lax.broadcasted_iota(jnp.int32, (n_sel, tq), 1)
    cur = lax.shift_right_logical(tt, 6)
    forced = (s_io == 0) | (s_io == cur) | (s_io == cur - 1)
    visible = s_io * SEL_BLOCK <= tt

    end_f, valid = _cmp_geometry(n_cmp_rows, tq, t0)

    def compressed(g):
        kc = kc_ref[0, g]
        heads = range(g * N_GRP, (g + 1) * N_GRP)
        probs = [_cmp_probs(kc, qt_ref[0, hh * HD:(hh + 1) * HD, :], _alibi_slope(hh) * LOG2E, end_f, valid)
                 for hh in heads]
        lhs = jnp.concatenate([c2s_ref[...], vct_ref[0, g]], axis=0)
        mass = [_dot(lhs, p) for p in probs]
        inv_l = [1.0 / m[n_sel:n_sel + 1] for m in mass]
        for hh, m, il in zip(heads, mass, inv_l):
            oc_ref[0, hh * HD:(hh + 1) * HD, :] = jnp.where(has_block, m[v_lo:v_lo + HD] * il, 0.0).astype(BF16)
        return sum(m[0:n_sel] * il for m, il in zip(mass, inv_l))

    def select(g, imp):
        x = jnp.where(forced, FORCED_SCORE, jnp.where(visible, imp, -1.0))
        cnt = jnp.zeros((n_sel, tq), F32)
        for j in range(n_sel):
            xj = x[j:j + 1, :]
            ge = jnp.where(xj >= x, 1.0, 0.0)
            gt = jnp.where(xj > x, 1.0, 0.0)
            cnt = cnt + jnp.where(s_io > j, ge, gt)
        chosen = cnt < float(N_SELECT)
        mneg_ref[0, g, 0:n_sel, :] = jnp.where(chosen, 0.0, MASK_VALUE).astype(BF16)
        if n_sel < HD:
            mneg_ref[0, g, n_sel:HD, :] = jnp.zeros((HD - n_sel, tq), BF16)
        per_query = _dot(chunk_ref[...], jnp.where(chosen, 1.0, 0.0).astype(BF16))
        active_ref[0, g, 0] = _dot_nt(jnp.ones((SUBLANES, tq), BF16), per_query.astype(BF16))

    imps = [compressed(g) for g in range(N_KV)]
    for g in range(N_KV):
        select(g, imps[g])


def _select(qt, k_cmp, v_cmpt):
    b, nq, s = qt.shape
    tq = min(ATT_T, s)
    n_sel = s // SEL_BLOCK
    rows = k_cmp.shape[2]
    n_cmp = (s - CMP_BLOCK) // CMP_STRIDE + 1
    assert rows == n_cmp + 1
    cs = np.arange(rows) * CMP_STRIDE
    bs = np.arange(n_sel) * SEL_BLOCK
    overlap = np.clip(np.minimum(cs[:, None] + CMP_BLOCK, bs[None, :] + SEL_BLOCK)
                      - np.maximum(cs[:, None], bs[None, :]), 0, None)
    c2s_t = jnp.asarray(np.concatenate([(overlap / CMP_BLOCK).T, np.ones((SUM_ROWS, rows))], axis=0), BF16)
    n_chunks = s // tq
    chunk_of_block = np.arange(n_sel) * SEL_BLOCK // tq
    chunk_mat = jnp.asarray(chunk_of_block[None, :] == np.arange(n_chunks)[:, None], BF16)
    return pl.pallas_call(
        functools.partial(_select_body, n_sel=n_sel, n_cmp_rows=rows),
        out_shape=(jax.ShapeDtypeStruct((b, N_KV, HD, s), BF16),
                   jax.ShapeDtypeStruct((b, N_KV, s // tq, SUBLANES, n_chunks), F32),
                   jax.ShapeDtypeStruct((b, nq, s), BF16)),
        grid=(b, s // tq),
        in_specs=[
            pl.BlockSpec((1, nq, tq), lambda bi, i: (bi, 0, i)),
            pl.BlockSpec((1, N_KV, rows, HD), lambda bi, i: (bi, 0, 0, 0)),
            pl.BlockSpec((1, N_KV, HD, rows), lambda bi, i: (bi, 0, 0, 0)),
            _const_spec((n_sel + SUM_ROWS, rows)),
            _const_spec((n_chunks, n_sel)),
        ],
        out_specs=(pl.BlockSpec((1, N_KV, HD, tq), lambda bi, i: (bi, 0, 0, i)),
                   pl.BlockSpec((1, N_KV, 1, SUBLANES, n_chunks), lambda bi, i: (bi, 0, i, 0, 0)),
                   pl.BlockSpec((1, nq, tq), lambda bi, i: (bi, 0, i))),
        compiler_params=_cparams(("parallel", "parallel")),
        name="nsa_select",
    )(qt, k_cmp, v_cmpt, c2s_t, chunk_mat)


def _attn_body(slope_ref, active_ref, qt_ref, k_ref, vt_ref, oc_ref, mneg_ref,
               gt_ref, o_ref, s_sc, bias_sc, m_sc, acc_sc, todo_ref):
    tq = qt_ref.shape[2]
    tk = tq
    g = pl.program_id(1)
    qi = pl.program_id(2)
    slopes = [slope_ref[g * N_GRP + r] for r in range(N_GRP)]
    q_rows = [qt_ref[0, r * HD:(r + 1) * HD, :] for r in range(N_GRP)]
    qk = [jnp.concatenate([q_rows[r], mneg_ref[0, 0]], axis=0) for r in range(N_GRP)]

    @pl.when(qi == 0)
    def _():
        key_io = lax.broadcasted_iota(jnp.int32, (tk, tq), 0)
        rel = key_io - lax.broadcasted_iota(jnp.int32, (tk, tq), 1)
        key_f = key_io.astype(F32)
        causal = jnp.where(rel <= 0, 0.0, MASK_VALUE)
        oldest = jnp.where(rel > 0, 0.0, MASK_VALUE)
        for r in range(N_GRP):
            base = slopes[r] * key_f
            bias_sc[0, r] = base
            bias_sc[1, r] = base + causal
            bias_sc[2, r] = base + oldest

    ones_rows = jnp.ones((SUM_ROWS, tk), BF16)

    def offsets(c):
        cf = ((c - qi) * tk).astype(F32)
        return [slopes[r] * cf for r in range(N_GRP)]

    tile = (pl.program_id(0) * N_KV + g) * pl.num_programs(2) + qi

    def note_active(c, n):
        todo_ref[n] = c
        return n + active_ref[tile * pl.num_programs(2) + c]

    todo_ref[0] = 0
    n_sel = lax.fori_loop(0, qi, note_active, jnp.int32(0)) + 1
    c_lo_w = jnp.maximum(qi - WINDOW // tk, 0)
    n_items = n_sel + (qi - c_lo_w) + 1

    def item(pos):
        window = (pos >= n_sel).astype(jnp.int32)
        listed = todo_ref[jnp.clip(pos, 0, jnp.maximum(n_sel - 2, 0))]
        c = jnp.where(pos < n_sel - 1, listed, jnp.where(pos == n_sel - 1, qi, c_lo_w + pos - n_sel))
        return window, c

    def score_stage(pos, slot):
        window, c = item(pos)
        kblk = k_ref[0, 0, window, pl.ds(pl.multiple_of(c * tk, tk), tk), :]
        variant = jnp.where(c == qi, 1, jnp.where((window == 1) & (c == qi - WINDOW // tk), 2, 0))
        offs = offsets(c)
        mcs = []
        for r in range(N_GRP):
            s = _dot(kblk, qk[r]) + bias_sc[variant, r]
            s_sc[slot, r] = s
            mcs.append(jnp.max(s, axis=0, keepdims=True) + offs[r])
        return tuple(mcs)

    def value_stage(pos, mcs, slot):
        window, c = item(pos)
        vt = jnp.concatenate([vt_ref[0, 0, window, c], ones_rows], axis=0)
        offs = offsets(c)
        for r in range(N_GRP):
            m = m_sc[window, r, 0:1, :]
            m_new = jnp.maximum(m, mcs[r])
            p = jnp.exp2(s_sc[slot, r] - (m_new - offs[r]))
            acc_sc[window, r] = jnp.exp2(m - m_new) * acc_sc[window, r] + _dot(vt, p.astype(BF16))
            m_sc[window, r, 0:1, :] = m_new

    m_sc[...] = jnp.full(m_sc.shape, MASK_VALUE, F32)
    acc_sc[...] = jnp.zeros(acc_sc.shape, F32)

    def trips(n, first, count, mcs):
        def trip(k, mcs):
            pos = first + n * k
            for i in range(n):
                mcs_next = score_stage(pos + i + 1, (i + 1) % 2)
                value_stage(pos + i, mcs, i % 2)
                mcs = mcs_next
            return mcs
        return lax.fori_loop(0, count, trip, mcs)

    steps = n_items - 1
    mcs = trips(4, 0, steps // 4, score_stage(jnp.int32(0), 0))
    done = (steps // 4) * 4
    mcs = trips(2, done, (steps - done) // 2, mcs)
    done = done + ((steps - done) // 2) * 2
    mcs = trips(1, done, steps - done, mcs)
    value_stage(steps, mcs, steps - done)

    gate = gt_ref[0, 0]
    normed = lambda acc: acc[0:HD] * (1.0 / acc[HD:HD + 1])
    for r in range(N_GRP):
        o_c = oc_ref[0, r * HD:(r + 1) * HD, :].astype(F32)
        o_s = normed(acc_sc[0, r])
        o_w = normed(acc_sc[1, r])
        out = (gate[3 * r:3 * r + 1, :] * o_c + gate[3 * r + 1:3 * r + 2, :] * o_s
               + gate[3 * r + 2:3 * r + 3, :] * o_w)
        o_ref[0, r * HD:(r + 1) * HD, :] = out.astype(BF16)


def _attention(qt, keys, vals_t, oc_t, mneg, active, gates_t):
    b, nq, s = qt.shape
    tq = vals_t.shape[5]
    assert WINDOW % tq == 0 and active.shape == (b, N_KV, s // tq, SUBLANES, s // tq)
    active = (active[:, :, :, 0, :] > 0.0).astype(jnp.int32).reshape(-1)
    gw = N_GRP * HD
    slopes = jnp.asarray([_alibi_slope(h) * LOG2E for h in range(N_HEADS)], F32)
    key_spec = pl.BlockSpec((1, 1, 2, s, LANES), lambda bi, g, i: (bi, g, 0, 0, 0))
    val_spec = pl.BlockSpec((1, 1, 2, s // tq, HD, tq), lambda bi, g, i: (bi, g, 0, 0, 0, 0))
    return pl.pallas_call(
        _attn_body,
        out_shape=jax.ShapeDtypeStruct((b, nq, s), BF16),
        grid=(b, N_KV, s // tq),
        in_specs=[
            pl.BlockSpec(memory_space=pltpu.SMEM),
            pl.BlockSpec(memory_space=pltpu.SMEM),
            pl.BlockSpec((1, gw, tq), lambda bi, g, i: (bi, g, i)),
            key_spec, val_spec,
            pl.BlockSpec((1, gw, tq), lambda bi, g, i: (bi, g, i)),
            pl.BlockSpec((1, 1, HD, tq), lambda bi, g, i: (bi, g, 0, i)),
            pl.BlockSpec((1, 1, GATE_ROWS, tq), lambda bi, g, i: (bi, g, 0, i)),
        ],
        out_specs=pl.BlockSpec((1, gw, tq), lambda bi, g, i: (bi, g, i)),
        scratch_shapes=[
            pltpu.VMEM((2, N_GRP, tq, tq), F32),
            pltpu.VMEM((3, N_GRP, tq, tq), F32),
            pltpu.VMEM((2, N_GRP, SUBLANES, tq), F32),
            pltpu.VMEM((2, N_GRP, HD + SUM_ROWS, tq), F32),
            pltpu.SMEM((s // tq,), jnp.int32),
        ],
        compiler_params=_cparams(("parallel", "parallel", "arbitrary")),
        name="nsa_attention",
    )(slopes, active, qt, keys, vals_t, oc_t, mneg, gates_t)


def _outproj_t_body(yt_ref, h_ref, w_ref, g_ref, o_ref):
    o_ref[0] = h_ref[0] + _rms(_dot_tn(yt_ref[0], w_ref[...]), g_ref[...])


def _outproj_t(yt, h, w_out, g_post):
    b, s, d = h.shape
    tm = min(PROJ_TM, s)
    return pl.pallas_call(
        _outproj_t_body,
        out_shape=jax.ShapeDtypeStruct((b, s, d), F32),
        grid=(b, s // tm),
        in_specs=[
            pl.BlockSpec((1, d, tm), lambda bi, i: (bi, 0, i)),
            pl.BlockSpec((1, tm, d), lambda bi, i: (bi, i, 0)),
            _const_spec((d, d)),
            _const_spec((1, d)),
        ],
        out_specs=pl.BlockSpec((1, tm, d), lambda bi, i: (bi, i, 0)),
        compiler_params=_cparams(("parallel", "parallel")),
        name="outproj_t",
    )(yt, h, w_out.astype(BF16), g_post.reshape(1, d))


def _nsa_mixer(h, g_pre, w_in, pe_k, w_ck1, w_ck2, pe_v, w_cv1, w_cv2, w_out, g_post):
    qt, kc, vc, keys, vals_t, gates_t = _nsa_proj(h, g_pre, w_in)
    k_cmp = _compress(kc, pe_k, w_ck1, w_ck2, transposed=False)
    v_cmpt = _compress(vc, pe_v, w_cv1, w_cv2, transposed=True)
    mneg, active, oc_t = _select(qt, k_cmp, v_cmpt)
    out_t = _attention(qt, keys, vals_t, oc_t, mneg, active, gates_t)
    return _outproj_t(out_t, h, w_out, g_post)


def _head_sum(x, sel_ref):
    return _dot(x.astype(BF16), sel_ref[...])


def _head_expand(x, selt_ref):
    return _dot(x.astype(BF16), selt_ref[...])


def _softplus(x):
    return jnp.maximum(x, 0.0) + jnp.log(1.0 + jnp.exp(-jnp.abs(x)))


def _rw_proj_body(h_ref, hp_ref, g_ref, mu_ref, vec_ref, wr_ref, wk_ref, wv_ref, wd_ref, wa_ref, wg_ref,
                  w2d_ref, w2a_ref, w2g_ref, sel_ref, selt_ref,
                  r_ref, lw_ref, k_ref, v_ref, a_ref, b_ref, gg_ref):
    tm = h_ref.shape[1]
    u = _rms(h_ref[0], g_ref[...])
    prev = _rms(hp_ref[0], g_ref[...])[7:8, :]
    prev = jnp.where(pl.program_id(1) == 0, 0.0, prev)
    row = lax.broadcasted_iota(jnp.int32, (tm, D_MODEL), 0)
    u_prev = jnp.where(row == 0, prev, pltpu.roll(u, 1, 0))
    xx = u_prev - u

    def mix(i):
        return (u + xx * mu_ref[i:i + 1, :]).astype(BF16)

    w0, a0, k_k, k_a = (vec_ref[i:i + 1, :] for i in range(4))
    k = _dot(mix(1), wk_ref[...])
    d1 = jnp.tanh(_dot(mix(3), wd_ref[...])).astype(BF16)
    kk = k * k_k
    norm = jnp.sqrt(_head_sum(kk * kk, sel_ref))
    a1 = _dot(mix(4), wa_ref[...]).astype(BF16)
    r = _dot(mix(0), wr_ref[...])
    w = -_softplus(-(w0 + _dot(d1, w2d_ref[...]))) - 0.5
    kk = kk * _head_expand(1.0 / jnp.maximum(norm, 1e-12), selt_ref)
    g1 = jax.nn.sigmoid(_dot(mix(5), wg_ref[...])).astype(BF16)
    alpha = jax.nn.sigmoid(a0 + _dot(a1, w2a_ref[...]))
    v = _dot(mix(2), wv_ref[...])
    r_ref[0] = r.astype(BF16)
    lw_ref[0] = -jnp.exp(w)
    k_ref[0] = (k * (1.0 + (alpha - 1.0) * k_a)).astype(BF16)
    v_ref[0] = v.astype(BF16)
    a_ref[0] = (-kk).astype(BF16)
    b_ref[0] = (kk * alpha).astype(BF16)
    gg_ref[0] = _dot(g1, w2g_ref[...]).astype(BF16)


def _head_selectors():
    lane_head = np.arange(D_MODEL) // HD
    sel = (lane_head[:, None] == np.arange(LANES)[None, :]).astype(np.float32)
    return jnp.asarray(sel, BF16), jnp.asarray(sel.T, BF16)


def _pad_cols(w, n):
    return jnp.pad(w, ((0, 0), (0, n - w.shape[1])))


def _pad_rows(w, n):
    return jnp.pad(w, ((0, n - w.shape[0]), (0, 0)))


def _rw_proj(h, g_pre, mu, w_in, w0, w_w2, a0, w_a2, w_g2, k_k, k_a):
    b, s, d = h.shape
    tm = min(PROJ_TM, s)
    offs = np.cumsum((0, d, d, d, DECAY_LORA, AAA_LORA, GATE_LORA))
    cols = [w_in[:, int(offs[i]):int(offs[i + 1])] for i in range(6)]
    ld, lg = LANES, 2 * LANES
    wr, wk, wv = (c.astype(BF16) for c in cols[:3])
    wd = _pad_cols(cols[3], ld).astype(BF16)
    wa = _pad_cols(cols[4], ld).astype(BF16)
    wg = _pad_cols(cols[5], lg).astype(BF16)
    w2d = _pad_rows(w_w2, ld).astype(BF16)
    w2a = _pad_rows(w_a2, ld).astype(BF16)
    w2g = _pad_rows(w_g2, lg).astype(BF16)
    mu8 = _pad_rows(mu, 8)
    vecs = _pad_rows(jnp.stack([w0, a0, k_k, k_a]), 8)
    sel, selt = _head_selectors()
    tok = pl.BlockSpec((1, tm, d), lambda bi, i: (bi, i, 0))
    out = lambda dt: jax.ShapeDtypeStruct((b, s, d), dt)
    return pl.pallas_call(
        _rw_proj_body,
        out_shape=(out(BF16), out(F32), out(BF16), out(BF16), out(BF16), out(BF16), out(BF16)),
        grid=(b, s // tm),
        in_specs=[
            tok,
            pl.BlockSpec((1, 8, d), lambda bi, i: (bi, jnp.maximum(i * (tm // 8) - 1, 0), 0)),
            _const_spec((1, d)), _const_spec((8, d)), _const_spec((8, d)),
            _const_spec((d, d)), _const_spec((d, d)), _const_spec((d, d)),
            _const_spec((d, ld)), _const_spec((d, ld)), _const_spec((d, lg)),
            _const_spec((ld, d)), _const_spec((ld, d)), _const_spec((lg, d)),
            _const_spec((d, LANES)), _const_spec((LANES, d)),
        ],
        out_specs=(tok,) * 7,
        compiler_params=_cparams(("parallel", "parallel")),
        name="rwkv_proj",
    )(h, h, g_pre.reshape(1, d), mu8, vecs, wr, wk, wv, wd, wa, wg, w2d, w2a, w2g, sel, selt)


def _rw_prep_body(r_ref, lw_ref, k_ref, v_ref, a_ref, b_ref,
                  rhat_ref, y1_ref, g_ref, n_ref):
    c = RW_C
    q = RW_Q
    nh = q // HD
    nch = r_ref.shape[1] // c
    lane_head = lax.shift_right_logical(lax.broadcasted_iota(jnp.int32, (c, q), 1), 6)
    ri = lax.broadcasted_iota(jnp.int32, (q, q), 0)
    ci = lax.broadcasted_iota(jnp.int32, (q, q), 1)
    same_head = lax.shift_right_logical(ri, 6) == lax.shift_right_logical(ci, 6)
    strict_bd = same_head & ((ci & (c - 1)) < (ri & (c - 1)))
    eye = ri == ci
    t_io = lax.broadcasted_iota(jnp.int32, (c, q), 0)
    j_io = lax.broadcasted_iota(jnp.int32, (c, q), 1) & (c - 1)
    strict_ls = j_io < t_io
    incl_ls = j_io <= t_io
    rows = [slice(i * c, (i + 1) * c) for i in range(nch)]

    def each(f):
        return [f(i) for i in range(nch)]

    def expand(x):
        return jnp.concatenate([jnp.where(lane_head == hh, x, 0.0) for hh in range(nh)], axis=0).astype(BF16)

    def collapse(x):
        out = x[0:c]
        for hh in range(1, nh):
            out = out + x[hh * c:(hh + 1) * c]
        return out

    lw = each(lambda i: lw_ref[0, rows[i], :])

    def cumsum_rows(x):
        shift = 1
        while shift < c:
            x = x + jnp.where(t_io >= shift, pltpu.roll(x, shift, 0), 0.0)
            shift *= 2
        return x

    cum = each(lambda i: cumsum_rows(lw[i]))
    cum_c = each(lambda i: cum[i][c - 1:c, :])
    e_inv = each(lambda i: jnp.exp(-cum[i]))
    e_rem = each(lambda i: jnp.exp(cum_c[i] - cum[i]))
    at = each(lambda i: a_ref[0, rows[i], :] * jnp.exp(cum[i] - lw[i]))
    rt = each(lambda i: r_ref[0, rows[i], :] * jnp.exp(cum[i]))
    x4 = each(lambda i: expand(at[i]))
    b4 = each(lambda i: expand(b_ref[0, rows[i], :] * e_inv[i]))
    k4 = each(lambda i: expand(k_ref[0, rows[i], :] * e_inv[i]))
    v4 = each(lambda i: expand(v_ref[0, rows[i], :]))
    bbar = each(lambda i: (b_ref[0, rows[i], :] * e_rem[i]).astype(BF16))
    kbar = each(lambda i: (k_ref[0, rows[i], :] * e_rem[i]).astype(BF16))

    lbd = each(lambda i: jnp.where(strict_bd, _dot_nt(x4[i], b4[i]), 0.0))
    tbd = each(lambda i: jnp.where(eye, 1.0, lbd[i]))
    lb = each(lambda i: lbd[i].astype(BF16))
    p = each(lambda i: _dot(lb[i], lb[i]))
    n_lvl = int(np.log2(c)) - 1
    for lvl in range(n_lvl):
        pb = each(lambda i: p[i].astype(BF16))
        if lvl + 1 < n_lvl:
            both = each(lambda i: _dot(pb[i], jnp.concatenate([tbd[i].astype(BF16), pb[i]], axis=1)))
            tbd = each(lambda i: tbd[i] + both[i][:, 0:q])
            p = each(lambda i: both[i][:, q:2 * q])
        else:
            tbd = each(lambda i: tbd[i] + _dot(pb[i], tbd[i].astype(BF16)))
    t_ls = each(lambda i: collapse(tbd[i]).astype(BF16))

    w = nh * c
    ar = each(lambda i: _dot_nt(jnp.concatenate([at[i], rt[i]], axis=0).astype(BF16),
                                jnp.concatenate([k4[i], b4[i]], axis=0)))
    a_kk = each(lambda i: jnp.concatenate([jnp.where(strict_ls, ar[i][0:c, 0:w], 0.0),
                                           jnp.where(incl_ls, ar[i][c:2 * c, 0:w], 0.0)], axis=0).astype(BF16))
    a_rb = each(lambda i: jnp.where(incl_ls, ar[i][c:2 * c, w:2 * w], 0.0).astype(BF16))

    gy = each(lambda i: _dot(a_kk[i], v4[i]))
    g1 = each(lambda i: gy[i][0:c])
    ua = each(lambda i: _dot(t_ls[i], jnp.concatenate([expand(g1[i]), x4[i]], axis=1)))
    u0 = each(lambda i: ua[i][:, 0:q])
    ahat = each(lambda i: ua[i][:, q:2 * q])
    ry = each(lambda i: _dot(a_rb[i], jnp.concatenate([expand(ahat[i]), expand(u0[i])], axis=1)))
    rhat = each(lambda i: rt[i] + ry[i][:, 0:q])
    y1 = each(lambda i: gy[i][c:2 * c] + ry[i][:, q:2 * q])

    gm = each(lambda i: _dot_tn(bbar[i], ahat[i].astype(BF16)))
    nm = each(lambda i: _dot_tn(jnp.concatenate([bbar[i], kbar[i]], axis=0),
                                jnp.concatenate([u0[i].astype(BF16), v_ref[0, rows[i], :]], axis=0)))
    for i in range(nch):
        gmi = jnp.where(same_head, gm[i], 0.0) + jnp.where(eye, jnp.exp(cum_c[i]), 0.0)
        rhat_ref[0, rows[i], :] = rhat[i].astype(BF16)
        y1_ref[0, rows[i], :] = y1[i].astype(BF16)
        g_ref[0, rows[i], :] = collapse(gmi).astype(BF16)
        n_ref[0, rows[i], :] = collapse(jnp.where(same_head, nm[i], 0.0)).astype(BF16)


def _rw_prep(r, lw, k, v, a, b):
    bsz, s, d = r.shape
    rows = min(RW_C * RW_NCH, s)
    blk = pl.BlockSpec((1, rows, RW_Q), lambda bi, qi, j: (bi, j, qi))
    return pl.pallas_call(
        _rw_prep_body,
        out_shape=(jax.ShapeDtypeStruct((bsz, s, d), BF16),) * 4,
        grid=(bsz, d // RW_Q, s // rows),
        in_specs=[blk] * 6,
        out_specs=(blk,) * 4,
        compiler_params=_cparams(("parallel", "parallel", "parallel")),
        name="rwkv_prep",
    )(r, lw, k, v, a, b)


def _rw_scan_body(rhat_ref, y1_ref, g_ref, n_ref, y_ref, h_sc):
    c = RW_C
    q = RW_Q
    nb, nq = h_sc.shape[0], h_sc.shape[1]

    @pl.when(pl.program_id(1) == 0)
    def _():
        h_sc[...] = jnp.zeros(h_sc.shape, F32)

    ri = lax.broadcasted_iota(jnp.int32, (q, q), 0)
    ci = lax.broadcasted_iota(jnp.int32, (q, q), 1)
    same_head = lax.shift_right_logical(ri, 6) == lax.shift_right_logical(ci, 6)

    def step(ch, carry):
        rows = pl.ds(pl.multiple_of(ch * c, c), c)
        for bi in range(nb):
            for qi in range(nq):
                lanes = slice(qi * q, (qi + 1) * q)
                hb = h_sc[bi, qi].astype(BF16)
                y = _dot(rhat_ref[bi, rows, lanes], hb) + y1_ref[bi, rows, lanes]
                y_ref[bi, rows, lanes] = y.astype(BF16)
                g_ls = g_ref[bi, rows, lanes]
                n_ls = n_ref[bi, rows, lanes]
                gbd = jnp.where(same_head, jnp.concatenate([g_ls] * (q // c), axis=0), 0.0)
                nbd = jnp.where(same_head, jnp.concatenate([n_ls] * (q // c), axis=0), 0.0)
                h_sc[bi, qi] = _dot(gbd.astype(BF16), hb) + nbd
        return carry

    lax.fori_loop(0, rhat_ref.shape[1] // c, step, 0)


def _rw_scan(rhat, y1, g, n):
    bsz, s, d = rhat.shape
    rows = min(512, s)
    nb = 2 if bsz % 2 == 0 else 1
    blk = pl.BlockSpec((nb, rows, d), lambda bi, j: (bi, j, 0))
    return pl.pallas_call(
        _rw_scan_body,
        out_shape=jax.ShapeDtypeStruct((bsz, s, d), BF16),
        grid=(bsz // nb, s // rows),
        in_specs=[blk] * 4,
        out_specs=blk,
        scratch_shapes=[pltpu.VMEM((nb, d // RW_Q, RW_Q, RW_Q), F32)],
        compiler_params=_cparams(("parallel", "arbitrary")),
        name="rwkv_scan",
    )(rhat, y1, g, n)


def _rw_post_body(y_ref, r_ref, k_ref, v_ref, gg_ref, h_ref, vec_ref, w_ref, gpost_ref, sel_ref, selt_ref, o_ref):
    gn_w, gn_b, r_k = (vec_ref[i:i + 1, :] for i in range(3))
    f32 = lambda ref: ref[...].astype(F32)
    y = f32(y_ref)
    inv_n = 1.0 / HD
    y_sum = _head_sum(y, sel_ref)
    rk_sum = _head_sum(f32(r_ref) * f32(k_ref) * r_k, sel_ref)
    yc = y - _head_expand(y_sum * inv_n, selt_ref)
    bonus = _head_expand(rk_sum, selt_ref) * f32(v_ref)
    var = _head_sum(yc * yc, sel_ref) * inv_n
    yn = yc * _head_expand(lax.rsqrt(var + GN_EPS), selt_ref) * gn_w + gn_b
    z = ((yn + bonus) * f32(gg_ref)).astype(BF16)
    o_ref[...] = h_ref[...] + _rms(_dot(z, w_ref[...]), gpost_ref[...])


def _rw_post(y, r, k, v, gg, h, gn_w, gn_b, r_k, w_out, g_post):
    t, d = h.shape
    tm = min(PROJ_TM, t)
    vecs = _pad_rows(jnp.stack([gn_w, gn_b, r_k.reshape(d)]), 8)
    sel, selt = _head_selectors()
    tok = pl.BlockSpec((tm, d), lambda i: (i, 0))
    return pl.pallas_call(
        _rw_post_body,
        out_shape=jax.ShapeDtypeStruct((t, d), F32),
        grid=(t // tm,),
        in_specs=[tok] * 6 + [_const_spec((8, d)), _const_spec((d, d)), _const_spec((1, d)),
                              _const_spec((d, LANES)), _const_spec((LANES, d))],
        out_specs=tok,
        compiler_params=_cparams(("parallel",)),
        name="rwkv_post",
    )(y, r, k, v, gg, h, vecs, w_out.astype(BF16), g_post.reshape(1, d), sel, selt)


def _rwkv_mixer(h, g_pre, mu, w_in, w0, w_w2, a0, w_a2, w_g2, k_k, k_a, r_k, gn_w, gn_b, w_out, g_post):
    b, s, d = h.shape
    r, lw, k, v, a, bb, gg = _rw_proj(h, g_pre, mu, w_in, w0, w_w2, a0, w_a2, w_g2, k_k, k_a)
    rhat, y1, g, n = _rw_prep(r, lw, k, v, a, bb)
    y = _rw_scan(rhat, y1, g, n)
    f2 = lambda x: x.reshape(b * s, d)
    return _rw_post(f2(y), f2(r), f2(k), f2(v), f2(gg), f2(h), gn_w, gn_b, r_k, w_out, g_post).reshape(b, s, d)


def kernel(x, ffn1_norm_pre, ffn1_w_gu, ffn1_w_down, ffn1_norm_post, mix_norm_pre, nsa_w_in, nsa_pe_k,
           nsa_w_ck1, nsa_w_ck2, nsa_pe_v, nsa_w_cv1, nsa_w_cv2, nsa_w_out, rwkv_mu, rwkv_w_in, rwkv_w0,
           rwkv_w_w2, rwkv_a0, rwkv_w_a2, rwkv_w_g2, rwkv_k_k, rwkv_k_a, rwkv_r_k, rwkv_gn_w, rwkv_gn_b,
           rwkv_w_out, mix_norm_post, ffn2_norm_pre, ffn2_w_gu, ffn2_w_down, ffn2_norm_post):
    b, s, d = x.shape
    flat = lambda t: t.reshape(b * s, d)
    cube = lambda t: t.reshape(b, s, d)
    h = x
    depth = ffn1_norm_pre.shape[0]
    for i in range(depth):
        h = cube(_ffn(flat(h), ffn1_norm_pre[i], ffn1_w_gu[i], ffn1_w_down[i], ffn1_norm_post[i]))
        j = i // 2
        if i % 2 == 0:
            h = _nsa_mixer(h, mix_norm_pre[i], nsa_w_in[j], nsa_pe_k[j], nsa_w_ck1[j], nsa_w_ck2[j],
                           nsa_pe_v[j], nsa_w_cv1[j], nsa_w_cv2[j], nsa_w_out[j], mix_norm_post[i])
        else:
            h = _rwkv_mixer(h, mix_norm_pre[i], rwkv_mu[j], rwkv_w_in[j], rwkv_w0[j], rwkv_w_w2[j],
                            rwkv_a0[j], rwkv_w_a2[j], rwkv_w_g2[j], rwkv_k_k[j], rwkv_k_a[j], rwkv_r_k[j],
                            rwkv_gn_w[j], rwkv_gn_b[j], rwkv_w_out[j], mix_norm_post[i])
        h = cube(_ffn(flat(h), ffn2_norm_pre[i], ffn2_w_gu[i], ffn2_w_down[i], ffn2_norm_post[i]))
    return h
```

```python
import functools

import numpy as np
import jax
import jax.numpy as jnp
from jax import lax
from jax.experimental import pallas as pl
from jax.experimental.pallas import tpu as pltpu

F32 = jnp.float32
BF16 = jnp.bfloat16

D_MODEL = 1024
D_FF = 2816
HALF_STEP = 0.5
RMS_EPS = 1e-6
MASK_VALUE = -1e30

HD = 64
N_HEADS = 16
N_KV = 4
N_GRP = 4
KV_W = N_KV * HD
CMP_BLOCK = 32
CMP_STRIDE = 16
SEL_BLOCK = 64
N_SELECT = 16
WINDOW = 512
FORCED_SCORE = 1e4

DECAY_LORA = 64
AAA_LORA = 64
GATE_LORA = 160
GN_EPS = 64e-5

LANES = 128
SUBLANES = 8
VMEM_LIMIT_BYTES = 56 * 1024 * 1024

FFN_TM = 1024
FFN_SUB = 512
FFN_CUTS = (0, 1536, 2816)
PROJ_TM = 512
ATT_T = 256
GATE_ROWS = 16
SUM_ROWS = 16
LOG2E = 1.4426950408889634
RW_C = 64
RW_Q = 128
RW_NCH = 16


def _cparams(sem):
    return pltpu.CompilerParams(dimension_semantics=sem, vmem_limit_bytes=VMEM_LIMIT_BYTES)


def _rms(x, g):
    ms = jnp.mean(x * x, axis=-1, keepdims=True)
    return x * lax.rsqrt(ms + RMS_EPS) * g


def _const_spec(shape):
    nd = len(shape)
    return pl.BlockSpec(shape, lambda *_: (0,) * nd, pipeline_mode=pl.Buffered(1))


def _dot(a, b):
    return jnp.dot(a, b, preferred_element_type=F32)


def _dot_nt(a, b):
    return lax.dot_general(a, b, (((1,), (1,)), ((), ())), preferred_element_type=F32)


def _dot_tn(a, b):
    return lax.dot_general(a, b, (((0,), (0,)), ((), ())), preferred_element_type=F32)


def _ffn_body(x_ref, gpre_ref, wgu_ref, wd_ref, gpost_ref, o_ref):
    for top in range(0, x_ref.shape[0], FFN_SUB):
        rows = slice(top, top + FFN_SUB)
        x = x_ref[rows, :]
        xn = _rms(x, gpre_ref[...]).astype(BF16)
        acc = None
        for lo, hi in zip(FFN_CUTS[:-1], FFN_CUTS[1:]):
            gate = _dot(xn, wgu_ref[:, lo:hi])
            up = _dot(xn, wgu_ref[:, D_FF + lo:D_FF + hi])
            act = (gate * jax.nn.sigmoid(gate) * up).astype(BF16)
            part = _dot(act, wd_ref[lo:hi, :])
            acc = part if acc is None else acc + part
        o_ref[rows, :] = x + HALF_STEP * _rms(acc, gpost_ref[...])


def _ffn(h2, g_pre, w_gu, w_down, g_post):
    t = h2.shape[0]
    tm = min(FFN_TM, t)
    return pl.pallas_call(
        _ffn_body,
        out_shape=jax.ShapeDtypeStruct((t, D_MODEL), F32),
        grid=(t // tm,),
        in_specs=[
            pl.BlockSpec((tm, D_MODEL), lambda i: (i, 0)),
            _const_spec((1, D_MODEL)),
            _const_spec((D_MODEL, 2 * D_FF)),
            _const_spec((D_FF, D_MODEL)),
            _const_spec((1, D_MODEL)),
        ],
        out_specs=pl.BlockSpec((tm, D_MODEL), lambda i: (i, 0)),
        compiler_params=_cparams(("parallel",)),
        name="ffn",
    )(h2, g_pre.reshape(1, D_MODEL), w_gu.astype(BF16), w_down.astype(BF16), g_post.reshape(1, D_MODEL))


def _nsa_proj_body(h_ref, g_ref, wqt_ref, wc_ref, wk_ref, wvt_ref, wgt_ref,
                   qt_ref, kc_ref, vc_ref, k_ref, vt_ref, gt_ref):
    tm = h_ref.shape[1]
    u = _rms(h_ref[0], g_ref[...]).astype(BF16)
    qt_ref[0] = (_dot_nt(wqt_ref[...], u) * (HD ** -0.5 * LOG2E)).astype(BF16)
    c = _dot(u, wc_ref[...])
    for j in range(KV_W // LANES):
        kc_ref[0, j] = c[:, j * LANES:(j + 1) * LANES]
        vc_ref[0, j] = c[:, KV_W + j * LANES:KV_W + (j + 1) * LANES]
    kk = _dot(u, wk_ref[...])
    t0 = pl.program_id(1) * tm
    lane = lax.broadcasted_iota(jnp.int32, (tm, LANES), 1)
    blk = lax.shift_right_logical(t0 + lax.broadcasted_iota(jnp.int32, (tm, LANES), 0), 6)
    onehot = jnp.where(lane - HD == blk, 1.0, 0.0)
    for g in range(N_KV):
        k_ref[0, g, 0] = (kk[:, g * LANES:(g + 1) * LANES] + onehot).astype(BF16)
        k_ref[0, g, 1] = kk[:, (N_KV + g) * LANES:(N_KV + g + 1) * LANES].astype(BF16)
    vt = _dot_nt(wvt_ref[...], u)
    tk = vt_ref.shape[5]
    for j in range(tm // tk):
        cols = slice(j * tk, (j + 1) * tk)
        vt_ref[0, :, 0, j] = vt[:KV_W, cols].reshape(N_KV, HD, tk).astype(BF16)
        vt_ref[0, :, 1, j] = vt[KV_W:, cols].reshape(N_KV, HD, tk).astype(BF16)
    gt_ref[0] = jax.nn.sigmoid(_dot_nt(wgt_ref[...], u)).reshape(N_KV, GATE_ROWS, tm)


def _nsa_proj(h, g_pre, w_in):
    b, s, _ = h.shape
    tm = min(PROJ_TM, s)
    tk = min(ATT_T, s)
    nq = N_HEADS * HD
    assert s // SEL_BLOCK <= HD
    col = lambda i: w_in[:, nq + i * KV_W:nq + (i + 1) * KV_W]
    w_qt = w_in[:, :nq].T.astype(BF16)
    w_c = jnp.concatenate([col(0), col(1)], axis=1).astype(BF16)
    pad_heads = lambda w: jnp.pad(w.reshape(D_MODEL, N_KV, HD), ((0, 0), (0, 0), (0, LANES - HD))).reshape(D_MODEL, N_KV * LANES)
    w_k = jnp.concatenate([pad_heads(col(2)), pad_heads(col(4))], axis=1).astype(BF16)
    w_vt = jnp.concatenate([col(3), col(5)], axis=1).T.astype(BF16)
    w_gl = w_in[:, nq + 6 * KV_W:].reshape(D_MODEL, N_KV, N_GRP * 3)
    w_gt = jnp.pad(w_gl, ((0, 0), (0, 0), (0, GATE_ROWS - N_GRP * 3))).reshape(D_MODEL, N_KV * GATE_ROWS).T.astype(BF16)
    tok = lambda w: pl.BlockSpec((1, tm, w), lambda bi, i: (bi, i, 0))
    cmp_spec = pl.BlockSpec((1, KV_W // LANES, tm, LANES), lambda bi, i: (bi, 0, i, 0))
    key = jax.ShapeDtypeStruct((b, N_KV, 2, s, LANES), BF16)
    key_spec = pl.BlockSpec((1, N_KV, 2, tm, LANES), lambda bi, i: (bi, 0, 0, i, 0))
    valt = jax.ShapeDtypeStruct((b, N_KV, 2, s // tk, HD, tk), BF16)
    valt_spec = pl.BlockSpec((1, N_KV, 2, tm // tk, HD, tk), lambda bi, i: (bi, 0, 0, i, 0, 0))
    return pl.pallas_call(
        _nsa_proj_body,
        out_shape=(
            jax.ShapeDtypeStruct((b, nq, s), BF16),
            jax.ShapeDtypeStruct((b, KV_W // LANES, s, LANES), F32),
            jax.ShapeDtypeStruct((b, KV_W // LANES, s, LANES), F32),
            key, valt,
            jax.ShapeDtypeStruct((b, N_KV, GATE_ROWS, s), F32),
        ),
        grid=(b, s // tm),
        in_specs=[
            tok(D_MODEL),
            _const_spec((1, D_MODEL)),
            _const_spec((nq, D_MODEL)),
            _const_spec((D_MODEL, 2 * KV_W)),
            _const_spec((D_MODEL, 2 * N_KV * LANES)),
            _const_spec((2 * KV_W, D_MODEL)),
            _const_spec((N_KV * GATE_ROWS, D_MODEL)),
        ],
        out_specs=(
            pl.BlockSpec((1, nq, tm), lambda bi, i: (bi, 0, i)),
            cmp_spec, cmp_spec, key_spec, valt_spec,
            pl.BlockSpec((1, N_KV, GATE_ROWS, tm), lambda bi, i: (bi, 0, 0, i)),
        ),
        compiler_params=_cparams(("parallel", "parallel")),
        name="nsa_proj",
    )(h, g_pre.reshape(1, D_MODEL), w_qt, w_c, w_k, w_vt, w_gt)


def _compress_body(x_ref, pe_ref, w1_ref, w2_ref, o_ref, *, transposed):
    half = CMP_BLOCK // 2
    n = x_ref.shape[2] // half
    ha = hb = None
    for l in range(half):
        xl = jnp.concatenate([x_ref[0, j, pl.ds(l, n, stride=half), :] for j in range(x_ref.shape[1])], axis=-1)
        a = _dot((xl + pe_ref[l:l + 1, :]).astype(BF16), w1_ref[l])
        b = _dot((xl + pe_ref[half + l:half + l + 1, :]).astype(BF16), w1_ref[half + l])
        ha = a if ha is None else ha + a
        hb = b if hb is None else hb + b
    hid = ha + pltpu.roll(hb, n - 1, 0)
    hid = (hid * jax.nn.sigmoid(hid)).astype(BF16)
    if transposed:
        o_ref[0] = _dot_nt(w2_ref[...], hid).reshape(N_KV, HD, n).astype(BF16)
    else:
        out = _dot(hid, w2_ref[...]).astype(BF16)
        for g in range(N_KV):
            o_ref[0, g] = out[:, g * HD:(g + 1) * HD]


def _compress(t, pe, w1, w2, transposed):
    b, _, s, _ = t.shape
    rows = s // (CMP_BLOCK // 2)
    eye = jnp.eye(N_KV, dtype=F32)
    wbd = jnp.einsum("ldc,gh->lgdhc", w1, eye).reshape(CMP_BLOCK, KV_W, KV_W).astype(BF16)
    pe_t = jnp.broadcast_to(pe[:, None, :], (CMP_BLOCK, N_KV, HD)).reshape(CMP_BLOCK, KV_W)
    w2bd = jnp.einsum("cd,gh->gchd", w2, eye).reshape(KV_W, KV_W)
    if transposed:
        w2bd = w2bd.T
        out_shape, out_block = (b, N_KV, HD, rows), (1, N_KV, HD, rows)
    else:
        out_shape, out_block = (b, N_KV, rows, HD), (1, N_KV, rows, HD)
    return pl.pallas_call(
        functools.partial(_compress_body, transposed=transposed),
        out_shape=jax.ShapeDtypeStruct(out_shape, BF16),
        grid=(b,),
        in_specs=[
            pl.BlockSpec((1, KV_W // LANES, s, LANES), lambda bi: (bi, 0, 0, 0)),
            _const_spec((CMP_BLOCK, KV_W)),
            _const_spec((CMP_BLOCK, KV_W, KV_W)),
            _const_spec((KV_W, KV_W)),
        ],
        out_specs=pl.BlockSpec(out_block, lambda bi: (bi, 0, 0, 0)),
        compiler_params=_cparams(("parallel",)),
        name="nsa_compress",
    )(t, pe_t, wbd, w2bd.astype(BF16))


def _alibi_slope(h):
    return float(2.0 ** (-8.0 * (h + 1) / N_HEADS))


def _cmp_probs(kc, qh, slope2, end_f, valid):
    s = _dot(kc, qh) + jnp.where(valid, slope2 * end_f, MASK_VALUE)
    return jnp.exp2(s - jnp.max(s, axis=0, keepdims=True)).astype(BF16)


def _cmp_geometry(n_cmp_rows, tq, t0):
    n_io = lax.broadcasted_iota(jnp.int32, (n_cmp_rows, tq), 0)
    t_io = t0 + lax.broadcasted_iota(jnp.int32, (n_cmp_rows, tq), 1)
    end = n_io * CMP_STRIDE + (CMP_BLOCK - 1)
    return end.astype(F32), (t_io >= end) & (n_io < n_cmp_rows - 1)


def _select_body(qt_ref, kc_ref, vct_ref, c2s_ref, chunk_ref, mneg_ref, active_ref, oc_ref, *, n_sel, n_cmp_rows):
    tq = qt_ref.shape[2]
    t0 = pl.program_id(1) * tq
    has_block = t0 + lax.broadcasted_iota(jnp.int32, (1, tq), 1) >= CMP_BLOCK - 1
    v_lo = n_sel + SUM_ROWS
    s_io = lax.broadcasted_iota(jnp.int32, (n_sel, tq), 0)
    tt = t0 + lax.broadcasted_iota(jnp.int32, (n_sel, tq), 1)
    cur = lax.shift_right_logical(tt, 6)
    forced = (s_io == 0) | (s_io == cur) | (s_io == cur - 1)
    visible = s_io * SEL_BLOCK <= tt

    end_f, valid = _cmp_geometry(n_cmp_rows, tq, t0)

    def compressed(g):
        kc = kc_ref[0, g]
        heads = range(g * N_GRP, (g + 1) * N_GRP)
        probs = [_cmp_probs(kc, qt_ref[0, hh * HD:(hh + 1) * HD, :], _alibi_slope(hh) * LOG2E, end_f, valid)
                 for hh in heads]
        lhs = jnp.concatenate([c2s_ref[...], vct_ref[0, g]], axis=0)
        mass = [_dot(lhs, p) for p in probs]
        inv_l = [1.0 / m[n_sel:n_sel + 1] for m in mass]
        for hh, m, il in zip(heads, mass, inv_l):
            oc_ref[0, hh * HD:(hh + 1) * HD, :] = jnp.where(has_block, m[v_lo:v_lo + HD] * il, 0.0).astype(BF16)
        return sum(m[0:n_sel] * il for m, il in zip(mass, inv_l))

    def select(g, imp):
        x = jnp.where(forced, FORCED_SCORE, jnp.where(visible, imp, -1.0))
        cnt = jnp.zeros((n_sel, tq), F32)
        for j in range(n_sel):
            xj = x[j:j + 1, :]
            ge = jnp.where(xj >= x, 1.0, 0.0)
            gt = jnp.where(xj > x, 1.0, 0.0)
            cnt = cnt + jnp.where(s_io > j, ge, gt)
        chosen = cnt < float(N_SELECT)
        mneg_ref[0, g, 0:n_sel, :] = jnp.where(chosen, 0.0, MASK_VALUE).astype(BF16)
        if n_sel < HD:
            mneg_ref[0, g, n_sel:HD, :] = jnp.zeros((HD - n_sel, tq), BF16)
        per_query = _dot(chunk_ref[...], jnp.where(chosen, 1.0, 0.0).astype(BF16))
        active_ref[0, g, 0] = _dot_nt(jnp.ones((SUBLANES, tq), BF16), per_query.astype(BF16))

    imps = [compressed(g) for g in range(N_KV)]
    for g in range(N_KV):
        select(g, imps[g])


def _select(qt, k_cmp, v_cmpt):
    b, nq, s = qt.shape
    tq = min(ATT_T, s)
    n_sel = s // SEL_BLOCK
    rows = k_cmp.shape[2]
    n_cmp = (s - CMP_BLOCK) // CMP_STRIDE + 1
    assert rows == n_cmp + 1
    cs = np.arange(rows) * CMP_STRIDE
    bs = np.arange(n_sel) * SEL_BLOCK
    overlap = np.clip(np.minimum(cs[:, None] + CMP_BLOCK, bs[None, :] + SEL_BLOCK)
                      - np.maximum(cs[:, None], bs[None, :]), 0, None)
    c2s_t = jnp.asarray(np.concatenate([(overlap / CMP_BLOCK).T, np.ones((SUM_ROWS, rows))], axis=0), BF16)
    n_chunks = s // tq
    chunk_of_block = np.arange(n_sel) * SEL_BLOCK // tq
    chunk_mat = jnp.asarray(chunk_of_block[None, :] == np.arange(n_chunks)[:, None], BF16)
    return pl.pallas_call(
        functools.partial(_select_body, n_sel=n_sel, n_cmp_rows=rows),
        out_shape=(jax.ShapeDtypeStruct((b, N_KV, HD, s), BF16),
                   jax.ShapeDtypeStruct((b, N_KV, s // tq, SUBLANES, n_chunks), F32),
                   jax.ShapeDtypeStruct((b, nq, s), BF16)),
        grid=(b, s // tq),
        in_specs=[
            pl.BlockSpec((1, nq, tq), lambda bi, i: (bi, 0, i)),
            pl.BlockSpec((1, N_KV, rows, HD), lambda bi, i: (bi, 0, 0, 0)),
            pl.BlockSpec((1, N_KV, HD, rows), lambda bi, i: (bi, 0, 0, 0)),
            _const_spec((n_sel + SUM_ROWS, rows)),
            _const_spec((n_chunks, n_sel)),
        ],
        out_specs=(pl.BlockSpec((1, N_KV, HD, tq), lambda bi, i: (bi, 0, 0, i)),
                   pl.BlockSpec((1, N_KV, 1, SUBLANES, n_chunks), lambda bi, i: (bi, 0, i, 0, 0)),
                   pl.BlockSpec((1, nq, tq), lambda bi, i: (bi, 0, i))),
        compiler_params=_cparams(("parallel", "parallel")),
        name="nsa_select",
    )(qt, k_cmp, v_cmpt, c2s_t, chunk_mat)


def _attn_body(slope_ref, active_ref, qt_ref, k_ref, vt_ref, oc_ref, mneg_ref,
               gt_ref, o_ref, s_sc, bias_sc, m_sc, acc_sc, todo_ref):
    tq = qt_ref.shape[2]
    tk = tq
    g = pl.program_id(1)
    qi = pl.program_id(2)
    slopes = [slope_ref[g * N_GRP + r] for r in range(N_GRP)]
    q_rows = [qt_ref[0, r * HD:(r + 1) * HD, :] for r in range(N_GRP)]
    qk = [jnp.concatenate([q_rows[r], mneg_ref[0, 0]], axis=0) for r in range(N_GRP)]

    @pl.when(qi == 0)
    def _():
        key_io = lax.broadcasted_iota(jnp.int32, (tk, tq), 0)
        rel = key_io - lax.broadcasted_iota(jnp.int32, (tk, tq), 1)
        key_f = key_io.astype(F32)
        causal = jnp.where(rel <= 0, 0.0, MASK_VALUE)
        oldest = jnp.where(rel > 0, 0.0, MASK_VALUE)
        for r in range(N_GRP):
            base = slopes[r] * key_f
            bias_sc[0, r] = base
            bias_sc[1, r] = base + causal
            bias_sc[2, r] = base + oldest

    ones_rows = jnp.ones((SUM_ROWS, tk), BF16)

    def offsets(c):
        cf = ((c - qi) * tk).astype(F32)
        return [slopes[r] * cf for r in range(N_GRP)]

    tile = (pl.program_id(0) * N_KV + g) * pl.num_programs(2) + qi

    def note_active(c, n):
        todo_ref[n] = c
        return n + active_ref[tile * pl.num_programs(2) + c]

    todo_ref[0] = 0
    n_sel = lax.fori_loop(0, qi, note_active, jnp.int32(0)) + 1
    c_lo_w = jnp.maximum(qi - WINDOW // tk, 0)
    n_items = n_sel + (qi - c_lo_w) + 1

    def item(pos):
        window = (pos >= n_sel).astype(jnp.int32)
        listed = todo_ref[jnp.clip(pos, 0, jnp.maximum(n_sel - 2, 0))]
        c = jnp.where(pos < n_sel - 1, listed, jnp.where(pos == n_sel - 1, qi, c_lo_w + pos - n_sel))
        return window, c

    def score_stage(pos, slot):
        window, c = item(pos)
        kblk = k_ref[0, 0, window, pl.ds(pl.multiple_of(c * tk, tk), tk), :]
        variant = jnp.where(c == qi, 1, jnp.where((window == 1) & (c == qi - WINDOW // tk), 2, 0))
        offs = offsets(c)
        mcs = []
        for r in range(N_GRP):
            s = _dot(kblk, qk[r]) + bias_sc[variant, r]
            s_sc[slot, r] = s
            mcs.append(jnp.max(s, axis=0, keepdims=True) + offs[r])
        return tuple(mcs)

    def value_stage(pos, mcs, slot):
        window, c = item(pos)
        vt = jnp.concatenate([vt_ref[0, 0, window, c], ones_rows], axis=0)
        offs = offsets(c)
        for r in range(N_GRP):
            m = m_sc[window, r, 0:1, :]
            m_new = jnp.maximum(m, mcs[r])
            p = jnp.exp2(s_sc[slot, r] - (m_new - offs[r]))
            acc_sc[window, r] = jnp.exp2(m - m_new) * acc_sc[window, r] + _dot(vt, p.astype(BF16))
            m_sc[window, r, 0:1, :] = m_new

    m_sc[...] = jnp.full(m_sc.shape, MASK_VALUE, F32)
    acc_sc[...] = jnp.zeros(acc_sc.shape, F32)

    def trips(n, first, count, mcs):
        def trip(k, mcs):
            pos = first + n * k
            for i in range(n):
                mcs_next = score_stage(pos + i + 1, (i + 1) % 2)
                value_stage(pos + i, mcs, i % 2)
                mcs = mcs_next
            return mcs
        return lax.fori_loop(0, count, trip, mcs)

    steps = n_items - 1
    mcs = trips(4, 0, steps // 4, score_stage(jnp.int32(0), 0))
    done = (steps // 4) * 4
    mcs = trips(2, done, (steps - done) // 2, mcs)
    done = done + ((steps - done) // 2) * 2
    mcs = trips(1, done, steps - done, mcs)
    value_stage(steps, mcs, steps - done)

    gate = gt_ref[0, 0]
    normed = lambda acc: acc[0:HD] * (1.0 / acc[HD:HD + 1])
    for r in range(N_GRP):
        o_c = oc_ref[0, r * HD:(r + 1) * HD, :].astype(F32)
        o_s = normed(acc_sc[0, r])
        o_w = normed(acc_sc[1, r])
        out = (gate[3 * r:3 * r + 1, :] * o_c + gate[3 * r + 1:3 * r + 2, :] * o_s
               + gate[3 * r + 2:3 * r + 3, :] * o_w)
        o_ref[0, r * HD:(r + 1) * HD, :] = out.astype(BF16)


def _attention(qt, keys, vals_t, oc_t, mneg, active, gates_t):
    b, nq, s = qt.shape
    tq = vals_t.shape[5]
    assert WINDOW % tq == 0 and active.shape == (b, N_KV, s // tq, SUBLANES, s // tq)
    active = (active[:, :, :, 0, :] > 0.0).astype(jnp.int32).reshape(-1)
    gw = N_GRP * HD
    slopes = jnp.asarray([_alibi_slope(h) * LOG2E for h in range(N_HEADS)], F32)
    key_spec = pl.BlockSpec((1, 1, 2, s, LANES), lambda bi, g, i: (bi, g, 0, 0, 0))
    val_spec = pl.BlockSpec((1, 1, 2, s // tq, HD, tq), lambda bi, g, i: (bi, g, 0, 0, 0, 0))
    return pl.pallas_call(
        _attn_body,
        out_shape=jax.ShapeDtypeStruct((b, nq, s), BF16),
        grid=(b, N_KV, s // tq),
        in_specs=[
            pl.BlockSpec(memory_space=pltpu.SMEM),
            pl.BlockSpec(memory_space=pltpu.SMEM),
            pl.BlockSpec((1, gw, tq), lambda bi, g, i: (bi, g, i)),
            key_spec, val_spec,
            pl.BlockSpec((1, gw, tq), lambda bi, g, i: (bi, g, i)),
            pl.BlockSpec((1, 1, HD, tq), lambda bi, g, i: (bi, g, 0, i)),
            pl.BlockSpec((1, 1, GATE_ROWS, tq), lambda bi, g, i: (bi, g, 0, i)),
        ],
        out_specs=pl.BlockSpec((1, gw, tq), lambda bi, g, i: (bi, g, i)),
        scratch_shapes=[
            pltpu.VMEM((2, N_GRP, tq, tq), F32),
            pltpu.VMEM((3, N_GRP, tq, tq), F32),
            pltpu.VMEM((2, N_GRP, SUBLANES, tq), F32),
            pltpu.VMEM((2, N_GRP, HD + SUM_ROWS, tq), F32),
            pltpu.SMEM((s // tq,), jnp.int32),
        ],
        compiler_params=_cparams(("parallel", "parallel", "arbitrary")),
        name="nsa_attention",
    )(slopes, active, qt, keys, vals_t, oc_t, mneg, gates_t)


def _outproj_t_body(yt_ref, h_ref, w_ref, g_ref, o_ref):
    o_ref[0] = h_ref[0] + _rms(_dot_tn(yt_ref[0], w_ref[...]), g_ref[...])


def _outproj_t(yt, h, w_out, g_post):
    b, s, d = h.shape
    tm = min(PROJ_TM, s)
    return pl.pallas_call(
        _outproj_t_body,
        out_shape=jax.ShapeDtypeStruct((b, s, d), F32),
        grid=(b, s // tm),
        in_specs=[
            pl.BlockSpec((1, d, tm), lambda bi, i: (bi, 0, i)),
            pl.BlockSpec((1, tm, d), lambda bi, i: (bi, i, 0)),
            _const_spec((d, d)),
            _const_spec((1, d)),
        ],
        out_specs=pl.BlockSpec((1, tm, d), lambda bi, i: (bi, i, 0)),
        compiler_params=_cparams(("parallel", "parallel")),
        name="outproj_t",
    )(yt, h, w_out.astype(BF16), g_post.reshape(1, d))


def _nsa_mixer(h, g_pre, w_in, pe_k, w_ck1, w_ck2, pe_v, w_cv1, w_cv2, w_out, g_post):
    qt, kc, vc, keys, vals_t, gates_t = _nsa_proj(h, g_pre, w_in)
    k_cmp = _compress(kc, pe_k, w_ck1, w_ck2, transposed=False)
    v_cmpt = _compress(vc, pe_v, w_cv1, w_cv2, transposed=True)
    mneg, active, oc_t = _select(qt, k_cmp, v_cmpt)
    out_t = _attention(qt, keys, vals_t, oc_t, mneg, active, gates_t)
    return _outproj_t(out_t, h, w_out, g_post)


def _head_sum(x, sel_ref):
    return _dot(x.astype(BF16), sel_ref[...])


def _head_expand(x, selt_ref):
    return _dot(x.astype(BF16), selt_ref[...])


def _softplus(x):
    return jnp.maximum(x, 0.0) + jnp.log(1.0 + jnp.exp(-jnp.abs(x)))


def _rw_proj_body(h_ref, hp_ref, g_ref, mu_ref, vec_ref, wr_ref, wk_ref, wv_ref, wd_ref, wa_ref, wg_ref,
                  w2d_ref, w2a_ref, w2g_ref, sel_ref, selt_ref,
                  r_ref, lw_ref, k_ref, v_ref, a_ref, b_ref, gg_ref):
    tm = h_ref.shape[1]
    u = _rms(h_ref[0], g_ref[...])
    prev = _rms(hp_ref[0], g_ref[...])[7:8, :]
    prev = jnp.where(pl.program_id(1) == 0, 0.0, prev)
    row = lax.broadcasted_iota(jnp.int32, (tm, D_MODEL), 0)
    u_prev = jnp.where(row == 0, prev, pltpu.roll(u, 1, 0))
    xx = u_prev - u

    def mix(i):
        return (u + xx * mu_ref[i:i + 1, :]).astype(BF16)

    w0, a0, k_k, k_a = (vec_ref[i:i + 1, :] for i in range(4))
    k = _dot(mix(1), wk_ref[...])
    d1 = jnp.tanh(_dot(mix(3), wd_ref[...])).astype(BF16)
    kk = k * k_k
    norm = jnp.sqrt(_head_sum(kk * kk, sel_ref))
    a1 = _dot(mix(4), wa_ref[...]).astype(BF16)
    r = _dot(mix(0), wr_ref[...])
    w = -_softplus(-(w0 + _dot(d1, w2d_ref[...]))) - 0.5
    kk = kk * _head_expand(1.0 / jnp.maximum(norm, 1e-12), selt_ref)
    g1 = jax.nn.sigmoid(_dot(mix(5), wg_ref[...])).astype(BF16)
    alpha = jax.nn.sigmoid(a0 + _dot(a1, w2a_ref[...]))
    v = _dot(mix(2), wv_ref[...])
    r_ref[0] = r.astype(BF16)
    lw_ref[0] = -jnp.exp(w)
    k_ref[0] = (k * (1.0 + (alpha - 1.0) * k_a)).astype(BF16)
    v_ref[0] = v.astype(BF16)
    a_ref[0] = (-kk).astype(BF16)
    b_ref[0] = (kk * alpha).astype(BF16)
    gg_ref[0] = _dot(g1, w2g_ref[...]).astype(BF16)


def _head_selectors():
    lane_head = np.arange(D_MODEL) // HD
    sel = (lane_head[:, None] == np.arange(LANES)[None, :]).astype(np.float32)
    return jnp.asarray(sel, BF16), jnp.asarray(sel.T, BF16)


def _pad_cols(w, n):
    return jnp.pad(w, ((0, 0), (0, n - w.shape[1])))


def _pad_rows(w, n):
    return jnp.pad(w, ((0, n - w.shape[0]), (0, 0)))


def _rw_proj(h, g_pre, mu, w_in, w0, w_w2, a0, w_a2, w_g2, k_k, k_a):
    b, s, d = h.shape
    tm = min(PROJ_TM, s)
    offs = np.cumsum((0, d, d, d, DECAY_LORA, AAA_LORA, GATE_LORA))
    cols = [w_in[:, int(offs[i]):int(offs[i + 1])] for i in range(6)]
    ld, lg = LANES, 2 * LANES
    wr, wk, wv = (c.astype(BF16) for c in cols[:3])
    wd = _pad_cols(cols[3], ld).astype(BF16)
    wa = _pad_cols(cols[4], ld).astype(BF16)
    wg = _pad_cols(cols[5], lg).astype(BF16)
    w2d = _pad_rows(w_w2, ld).astype(BF16)
    w2a = _pad_rows(w_a2, ld).astype(BF16)
    w2g = _pad_rows(w_g2, lg).astype(BF16)
    mu8 = _pad_rows(mu, 8)
    vecs = _pad_rows(jnp.stack([w0, a0, k_k, k_a]), 8)
    sel, selt = _head_selectors()
    tok = pl.BlockSpec((1, tm, d), lambda bi, i: (bi, i, 0))
    out = lambda dt: jax.ShapeDtypeStruct((b, s, d), dt)
    return pl.pallas_call(
        _rw_proj_body,
        out_shape=(out(BF16), out(F32), out(BF16), out(BF16), out(BF16), out(BF16), out(BF16)),
        grid=(b, s // tm),
        in_specs=[
            tok,
            pl.BlockSpec((1, 8, d), lambda bi, i: (bi, jnp.maximum(i * (tm // 8) - 1, 0), 0)),
            _const_spec((1, d)), _const_spec((8, d)), _const_spec((8, d)),
            _const_spec((d, d)), _const_spec((d, d)), _const_spec((d, d)),
            _const_spec((d, ld)), _const_spec((d, ld)), _const_spec((d, lg)),
            _const_spec((ld, d)), _const_spec((ld, d)), _const_spec((lg, d)),
            _const_spec((d, LANES)), _const_spec((LANES, d)),
        ],
        out_specs=(tok,) * 7,
        compiler_params=_cparams(("parallel", "parallel")),
        name="rwkv_proj",
    )(h, h, g_pre.reshape(1, d), mu8, vecs, wr, wk, wv, wd, wa, wg, w2d, w2a, w2g, sel, selt)


def _rw_prep_body(r_ref, lw_ref, k_ref, v_ref, a_ref, b_ref,
                  rhat_ref, y1_ref, g_ref, n_ref):
    c = RW_C
    q = RW_Q
    nh = q // HD
    nch = r_ref.shape[1] // c
    lane_head = lax.shift_right_logical(lax.broadcasted_iota(jnp.int32, (c, q), 1), 6)
    ri = lax.broadcasted_iota(jnp.int32, (q, q), 0)
    ci = lax.broadcasted_iota(jnp.int32, (q, q), 1)
    same_head = lax.shift_right_logical(ri, 6) == lax.shift_right_logical(ci, 6)
    strict_bd = same_head & ((ci & (c - 1)) < (ri & (c - 1)))
    eye = ri == ci
    t_io = lax.broadcasted_iota(jnp.int32, (c, q), 0)
    j_io = lax.broadcasted_iota(jnp.int32, (c, q), 1) & (c - 1)
    strict_ls = j_io < t_io
    incl_ls = j_io <= t_io
    rows = [slice(i * c, (i + 1) * c) for i in range(nch)]

    def each(f):
        return [f(i) for i in range(nch)]

    def expand(x):
        return jnp.concatenate([jnp.where(lane_head == hh, x, 0.0) for hh in range(nh)], axis=0).astype(BF16)

    def collapse(x):
        out = x[0:c]
        for hh in range(1, nh):
            out = out + x[hh * c:(hh + 1) * c]
        return out

    lw = each(lambda i: lw_ref[0, rows[i], :])

    def cumsum_rows(x):
        shift = 1
        while shift < c:
            x = x + jnp.where(t_io >= shift, pltpu.roll(x, shift, 0), 0.0)
            shift *= 2
        return x

    cum = each(lambda i: cumsum_rows(lw[i]))
    cum_c = each(lambda i: cum[i][c - 1:c, :])
    e_inv = each(lambda i: jnp.exp(-cum[i]))
    e_rem = each(lambda i: jnp.exp(cum_c[i] - cum[i]))
    at = each(lambda i: a_ref[0, rows[i], :] * jnp.exp(cum[i] - lw[i]))
    rt = each(lambda i: r_ref[0, rows[i], :] * jnp.exp(cum[i]))
    x4 = each(lambda i: expand(at[i]))
    b4 = each(lambda i: expand(b_ref[0, rows[i], :] * e_inv[i]))
    k4 = each(lambda i: expand(k_ref[0, rows[i], :] * e_inv[i]))
    v4 = each(lambda i: expand(v_ref[0, rows[i], :]))
    bbar = each(lambda i: (b_ref[0, rows[i], :] * e_rem[i]).astype(BF16))
    kbar = each(lambda i: (k_ref[0, rows[i], :] * e_rem[i]).astype(BF16))

    lbd = each(lambda i: jnp.where(strict_bd, _dot_nt(x4[i], b4[i]), 0.0))
    tbd = each(lambda i: jnp.where(eye, 1.0, lbd[i]))
    lb = each(lambda i: lbd[i].astype(BF16))
    p = each(lambda i: _dot(lb[i], lb[i]))
    n_lvl = int(np.log2(c)) - 1
    for lvl in range(n_lvl):
        pb = each(lambda i: p[i].astype(BF16))
        if lvl + 1 < n_lvl:
            both = each(lambda i: _dot(pb[i], jnp.concatenate([tbd[i].astype(BF16), pb[i]], axis=1)))
            tbd = each(lambda i: tbd[i] + both[i][:, 0:q])
            p = each(lambda i: both[i][:, q:2 * q])
        else:
            tbd = each(lambda i: tbd[i] + _dot(pb[i], tbd[i].astype(BF16)))
    t_ls = each(lambda i: collapse(tbd[i]).astype(BF16))

    w = nh * c
    ar = each(lambda i: _dot_nt(jnp.concatenate([at[i], rt[i]], axis=0).astype(BF16),
                                jnp.concatenate([k4[i], b4[i]], axis=0)))
    a_kk = each(lambda i: jnp.concatenate([jnp.where(strict_ls, ar[i][0:c, 0:w], 0.0),
                                           jnp.where(incl_ls, ar[i][c:2 * c, 0:w], 0.0)], axis=0).astype(BF16))
    a_rb = each(lambda i: jnp.where(incl_ls, ar[i][c:2 * c, w:2 * w], 0.0).astype(BF16))

    gy = each(lambda i: _dot(a_kk[i], v4[i]))
    g1 = each(lambda i: gy[i][0:c])
    ua = each(lambda i: _dot(t_ls[i], jnp.concatenate([expand(g1[i]), x4[i]], axis=1)))
    u0 = each(lambda i: ua[i][:, 0:q])
    ahat = each(lambda i: ua[i][:, q:2 * q])
    ry = each(lambda i: _dot(a_rb[i], jnp.concatenate([expand(ahat[i]), expand(u0[i])], axis=1)))
    rhat = each(lambda i: rt[i] + ry[i][:, 0:q])
    y1 = each(lambda i: gy[i][c:2 * c] + ry[i][:, q:2 * q])

    gm = each(lambda i: _dot_tn(bbar[i], ahat[i].astype(BF16)))
    nm = each(lambda i: _dot_tn(jnp.concatenate([bbar[i], kbar[i]], axis=0),
                                jnp.concatenate([u0[i].astype(BF16), v_ref[0, rows[i], :]], axis=0)))
    for i in range(nch):
        gmi = jnp.where(same_head, gm[i], 0.0) + jnp.where(eye, jnp.exp(cum_c[i]), 0.0)
        rhat_ref[0, rows[i], :] = rhat[i].astype(BF16)
        y1_ref[0, rows[i], :] = y1[i].astype(BF16)
        g_ref[0, rows[i], :] = collapse(gmi).astype(BF16)
        n_ref[0, rows[i], :] = collapse(jnp.where(same_head, nm[i], 0.0)).astype(BF16)


def _rw_prep(r, lw, k, v, a, b):
    bsz, s, d = r.shape
    rows = min(RW_C * RW_NCH, s)
    blk = pl.BlockSpec((1, rows, RW_Q), lambda bi, qi, j: (bi, j, qi))
    return pl.pallas_call(
        _rw_prep_body,
        out_shape=(jax.ShapeDtypeStruct((bsz, s, d), BF16),) * 4,
        grid=(bsz, d // RW_Q, s // rows),
        in_specs=[blk] * 6,
        out_specs=(blk,) * 4,
        compiler_params=_cparams(("parallel", "parallel", "parallel")),
        name="rwkv_prep",
    )(r, lw, k, v, a, b)


def _rw_scan_body(rhat_ref, y1_ref, g_ref, n_ref, y_ref, h_sc):
    c = RW_C
    q = RW_Q
    nb, nq = h_sc.shape[0], h_sc.shape[1]

    @pl.when(pl.program_id(1) == 0)
    def _():
        h_sc[...] = jnp.zeros(h_sc.shape, F32)

    ri = lax.broadcasted_iota(jnp.int32, (q, q), 0)
    ci = lax.broadcasted_iota(jnp.int32, (q, q), 1)
    same_head = lax.shift_right_logical(ri, 6) == lax.shift_right_logical(ci, 6)

    def step(ch, carry):
        rows = pl.ds(pl.multiple_of(ch * c, c), c)
        for bi in range(nb):
            for qi in range(nq):
                lanes = slice(qi * q, (qi + 1) * q)
                hb = h_sc[bi, qi].astype(BF16)
                y = _dot(rhat_ref[bi, rows, lanes], hb) + y1_ref[bi, rows, lanes]
                y_ref[bi, rows, lanes] = y.astype(BF16)
                g_ls = g_ref[bi, rows, lanes]
                n_ls = n_ref[bi, rows, lanes]
                gbd = jnp.where(same_head, jnp.concatenate([g_ls] * (q // c), axis=0), 0.0)
                nbd = jnp.where(same_head, jnp.concatenate([n_ls] * (q // c), axis=0), 0.0)
                h_sc[bi, qi] = _dot(gbd.astype(BF16), hb) + nbd
        return carry

    lax.fori_loop(0, rhat_ref.shape[1] // c, step, 0)


def _rw_scan(rhat, y1, g, n):
    bsz, s, d = rhat.shape
    rows = min(512, s)
    nb = 2 if bsz % 2 == 0 else 1
    blk = pl.BlockSpec((nb, rows, d), lambda bi, j: (bi, j, 0))
    return pl.pallas_call(
        _rw_scan_body,
        out_shape=jax.ShapeDtypeStruct((bsz, s, d), BF16),
        grid=(bsz // nb, s // rows),
        in_specs=[blk] * 4,
        out_specs=blk,
        scratch_shapes=[pltpu.VMEM((nb, d // RW_Q, RW_Q, RW_Q), F32)],
        compiler_params=_cparams(("parallel", "arbitrary")),
        name="rwkv_scan",
    )(rhat, y1, g, n)


def _rw_post_body(y_ref, r_ref, k_ref, v_ref, gg_ref, h_ref, vec_ref, w_ref, gpost_ref, sel_ref, selt_ref, o_ref):
    gn_w, gn_b, r_k = (vec_ref[i:i + 1, :] for i in range(3))
    f32 = lambda ref: ref[...].astype(F32)
    y = f32(y_ref)
    inv_n = 1.0 / HD
    y_sum = _head_sum(y, sel_ref)
    rk_sum = _head_sum(f32(r_ref) * f32(k_ref) * r_k, sel_ref)
    yc = y - _head_expand(y_sum * inv_n, selt_ref)
    bonus = _head_expand(rk_sum, selt_ref) * f32(v_ref)
    var = _head_sum(yc * yc, sel_ref) * inv_n
    yn = yc * _head_expand(lax.rsqrt(var + GN_EPS), selt_ref) * gn_w + gn_b
    z = ((yn + bonus) * f32(gg_ref)).astype(BF16)
    o_ref[...] = h_ref[...] + _rms(_dot(z, w_ref[...]), gpost_ref[...])


def _rw_post(y, r, k, v, gg, h, gn_w, gn_b, r_k, w_out, g_post):
    t, d = h.shape
    tm = min(PROJ_TM, t)
    vecs = _pad_rows(jnp.stack([gn_w, gn_b, r_k.reshape(d)]), 8)
    sel, selt = _head_selectors()
    tok = pl.BlockSpec((tm, d), lambda i: (i, 0))
    return pl.pallas_call(
        _rw_post_body,
        out_shape=jax.ShapeDtypeStruct((t, d), F32),
        grid=(t // tm,),
        in_specs=[tok] * 6 + [_const_spec((8, d)), _const_spec((d, d)), _const_spec((1, d)),
                              _const_spec((d, LANES)), _const_spec((LANES, d))],
        out_specs=tok,
        compiler_params=_cparams(("parallel",)),
        name="rwkv_post",
    )(y, r, k, v, gg, h, vecs, w_out.astype(BF16), g_post.reshape(1, d), sel, selt)


def _rwkv_mixer(h, g_pre, mu, w_in, w0, w_w2, a0, w_a2, w_g2, k_k, k_a, r_k, gn_w, gn_b, w_out, g_post):
    b, s, d = h.shape
    r, lw, k, v, a, bb, gg = _rw_proj(h, g_pre, mu, w_in, w0, w_w2, a0, w_a2, w_g2, k_k, k_a)
    rhat, y1, g, n = _rw_prep(r, lw, k, v, a, bb)
    y = _rw_scan(rhat, y1, g, n)
    f2 = lambda x: x.reshape(b * s, d)
    return _rw_post(f2(y), f2(r), f2(k), f2(v), f2(gg), f2(h), gn_w, gn_b, r_k, w_out, g_post).reshape(b, s, d)


def kernel(x, ffn1_norm_pre, ffn1_w_gu, ffn1_w_down, ffn1_norm_post, mix_norm_pre, nsa_w_in, nsa_pe_k,
           nsa_w_ck1, nsa_w_ck2, nsa_pe_v, nsa_w_cv1, nsa_w_cv2, nsa_w_out, rwkv_mu, rwkv_w_in, rwkv_w0,
           rwkv_w_w2, rwkv_a0, rwkv_w_a2, rwkv_w_g2, rwkv_k_k, rwkv_k_a, rwkv_r_k, rwkv_gn_w, rwkv_gn_b,
           rwkv_w_out, mix_norm_post, ffn2_norm_pre, ffn2_w_gu, ffn2_w_down, ffn2_norm_post):
    b, s, d = x.shape
    flat = lambda t: t.reshape(b * s, d)
    cube = lambda t: t.reshape(b, s, d)
    h = x
    depth = ffn1_norm_pre.shape[0]
    for i in range(depth):
        h = cube(_ffn(flat(h), ffn1_norm_pre[i], ffn1_w_gu[i], ffn1_w_down[i], ffn1_norm_post[i]))
        j = i // 2
        if i % 2 == 0:
            h = _nsa_mixer(h, mix_norm_pre[i], nsa_w_in[j], nsa_pe_k[j], nsa_w_ck1[j], nsa_w_ck2[j],
                           nsa_pe_v[j], nsa_w_cv1[j], nsa_w_cv2[j], nsa_w_out[j], mix_norm_post[i])
        else:
            h = _rwkv_mixer(h, mix_norm_pre[i], rwkv_mu[j], rwkv_w_in[j], rwkv_w0[j], rwkv_w_w2[j],
                            rwkv_a0[j], rwkv_w_a2[j], rwkv_w_g2[j], rwkv_k_k[j], rwkv_k_a[j], rwkv_r_k[j],
                            rwkv_gn_w[j], rwkv_gn_b[j], rwkv_w_out[j], mix_norm_post[i])
        h = cube(_ffn(flat(h), ffn2_norm_pre[i], ffn2_w_gu[i], ffn2_w_down[i], ffn2_norm_post[i]))
    return h
```
